```python
import math
import jax, jax.numpy as jnp
from jax import lax
import numpy as np

D_MODEL = 1024
BATCH = 8
SEQ = 4096
DEPTH = 2

HEAD_DIM = 64
A_GROUPS = ((128, 1), (512, 4), (2048, 16))
A_HEADS = 4
A_WIDTH = len(A_GROUPS) * A_HEADS * HEAD_DIM
A_OUT = A_HEADS * HEAD_DIM
B_HEADS = 16
B_KV_HEADS = 4
B_Q = B_HEADS * HEAD_DIM
B_KV = B_KV_HEADS * HEAD_DIM
Q_BLOCK = 128
GRID_W = 64
ROPE_THETA = 10000.0
N_BRANCH = 2
N_IN = 3 * A_WIDTH + B_Q + 2 * B_KV + N_BRANCH * D_MODEL
REL_BUCKETS = 32
REL_MAX_DIST = 1024
N_EXPERTS = 32
TOP_K = 4
D_EXPERT = D_MODEL
SWIGLU_LIMIT = 7.0
SWIGLU_ALPHA = 1.702
MOE_BLOCK = 512
N_MOD = 6
EPS = 1e-6
NEG_INF = -1e30

kernel_name = "hybrid_dilated_gqa_moe_encoder"


def _rmsnorm(x, g):
    xf = x.astype(jnp.float32)
    y = xf * lax.rsqrt(jnp.mean(xf * xf, axis=-1, keepdims=True) + EPS)
    return (y * g.astype(jnp.float32)).astype(x.dtype)


def _modulate(h, shift, scale):
    return h * (1.0 + scale[:, None, :]) + shift[:, None, :]


def _t5_bucket(rel):
    nb = REL_BUCKETS // 2
    max_exact = nb // 2
    ret = jnp.where(rel > 0, nb, 0)
    n = jnp.abs(rel)
    nf = jnp.maximum(n, 1).astype(jnp.float32)
    large = max_exact + (jnp.log(nf / max_exact) / math.log(REL_MAX_DIST / max_exact)
                         * (nb - max_exact)).astype(jnp.int32)
    large = jnp.minimum(large, nb - 1)
    return ret + jnp.where(n < max_exact, n, large)


def _dilated_window_attention(q, k, v, bias_tab, dilation, n_side):
    bsz, s, h, dh = q.shape
    L = s // dilation
    nb = n_side
    nblk = -(-L // nb)
    lp = nblk * nb

    def fold(t):
        return t.reshape(bsz, L, dilation, h, dh).swapaxes(1, 2).reshape(bsz * dilation, L, h, dh)

    qf, kf, vf = fold(q), fold(k), fold(v)
    qb = jnp.pad(qf, ((0, 0), (0, lp - L), (0, 0), (0, 0))).reshape(-1, nblk, nb, h, dh)

    def band(t):
        tp = jnp.pad(t, ((0, 0), (nb, lp - L + nb), (0, 0), (0, 0))).reshape(-1, nblk + 2, nb, h, dh)
        return jnp.concatenate([tp[:, :-2], tp[:, 1:-1], tp[:, 2:]], axis=2)

    kb, vb = band(kf), band(vf)
    rel = jnp.arange(3 * nb)[None, :] - nb - jnp.arange(nb)[:, None]
    key_idx = jnp.arange(nblk)[:, None] * nb + jnp.arange(3 * nb)[None, :] - nb
    mask = (jnp.abs(rel) <= n_side)[None] & ((key_idx >= 0) & (key_idx < L))[:, None, :]
    bias = jnp.transpose(bias_tab[_t5_bucket(rel * dilation)], (2, 0, 1)).astype(jnp.float32)

    scores = jnp.einsum('bnqhd,bnkhd->bnhqk', qb, kb).astype(jnp.float32) * (dh ** -0.5) + bias
    scores = jnp.where(mask[None, :, None], scores, NEG_INF)
    m = jnp.max(scores, axis=-1, keepdims=True)
    p = jnp.exp(scores - m)
    den = jnp.sum(p, axis=-1)
    o = jnp.einsum('bnhqk,bnkhd->bnqhd', p.astype(v.dtype), vb)
    o = o / jnp.transpose(den, (0, 1, 3, 2))[..., None].astype(o.dtype)
    lse = jnp.transpose(m[..., 0] + jnp.log(den), (0, 1, 3, 2))
    o = o.reshape(-1, lp, h, dh)[:, :L]
    lse = lse.reshape(-1, lp, h)[:, :L]
    o = o.reshape(bsz, dilation, L, h, dh).swapaxes(1, 2).reshape(bsz, s, h, dh)
    lse = lse.reshape(bsz, dilation, L, h).swapaxes(1, 2).reshape(bsz, s, h)
    return o, lse


def _mixer_dilated(za, rel_bias):
    bsz, s, _ = za.shape
    za = za.reshape(bsz, s, len(A_GROUPS), 3, A_HEADS, HEAD_DIM)
    outs, lses = [], []
    for g, (window, dilation) in enumerate(A_GROUPS):
        n_side = window // (2 * dilation)
        o, lse = _dilated_window_attention(za[:, :, g, 0], za[:, :, g, 1], za[:, :, g, 2],
                                           rel_bias[:, g * A_HEADS:(g + 1) * A_HEADS],
                                           dilation, n_side)
        outs.append(o)
        lses.append(lse)
    alpha = jax.nn.softmax(jnp.stack(lses, axis=0), axis=0)
    o = jnp.einsum('gbsh,gbshd->bshd', alpha.astype(outs[0].dtype), jnp.stack(outs, axis=0))
    return o.reshape(bsz, s, A_OUT)


def _axial_rope_tables(seq):
    n_rows = seq // GRID_W
    row = jnp.repeat(jnp.arange(n_rows), GRID_W).astype(jnp.float32)
    col = (jnp.arange(seq) % GRID_W).astype(jnp.float32)
    half = HEAD_DIM // 2
    inv = ROPE_THETA ** (-jnp.arange(0, half, 2, dtype=jnp.float32) / half)
    ang_r = row[:, None] * inv
    ang_c = col[:, None] * inv
    return jnp.cos(ang_r), jnp.sin(ang_r), jnp.cos(ang_c), jnp.sin(ang_c)


def _rotate(u, cos, sin):
    u1, u2 = jnp.split(u, 2, axis=-1)
    cos = cos[None, :, None, :]
    sin = sin[None, :, None, :]
    return jnp.concatenate([u1 * cos - u2 * sin, u2 * cos + u1 * sin], axis=-1)


def _axial_rope(x, tabs):
    cr, sr, cc, sc = tabs
    half = HEAD_DIM // 2
    xf = x.astype(jnp.float32)
    y = jnp.concatenate([_rotate(xf[..., :half], cr, sr), _rotate(xf[..., half:], cc, sc)], axis=-1)
    return y.astype(x.dtype)


def _mixer_gqa(zq, zk, zv, q_g, k_g, tabs):
    bsz, s, _ = zq.shape
    grp = B_HEADS // B_KV_HEADS
    q = _axial_rope(_rmsnorm(zq.reshape(bsz, s, B_HEADS, HEAD_DIM), q_g), tabs)
    k = _axial_rope(_rmsnorm(zk.reshape(bsz, s, B_KV_HEADS, HEAD_DIM), k_g), tabs)
    v = zv.reshape(bsz, s, B_KV_HEADS, HEAD_DIM)
    qb = q.reshape(bsz, s // Q_BLOCK, Q_BLOCK, B_KV_HEADS, grp, HEAD_DIM).transpose(1, 0, 2, 3, 4, 5)
    scale = HEAD_DIM ** -0.5

    def block(qblk):
        sc = jnp.einsum('bqhgd,bkhd->bhgqk', qblk, k).astype(jnp.float32) * scale
        p = jax.nn.softmax(sc, axis=-1)
        return jnp.einsum('bhgqk,bkhd->bqhgd', p.astype(v.dtype), v)

    o = lax.map(block, qb)
    return o.transpose(1, 0, 2, 3, 4, 5).reshape(bsz, s, B_Q)


def _moe(h, w_router, b_router, w_gu, b_gu, w_dn, b_dn):
    bsz, s, d = h.shape
    t = bsz * s
    ht = h.reshape(t, d)
    logits = (ht @ w_router + b_router).astype(jnp.float32)
    top_v, top_e = lax.top_k(logits, TOP_K)
    gates = jax.nn.softmax(top_v, axis=-1)
    n_assign = t * TOP_K
    e_flat = top_e.reshape(n_assign)
    tok_flat = jnp.repeat(jnp.arange(t, dtype=jnp.int32), TOP_K)
    w_flat = gates.reshape(n_assign)
    order = jnp.argsort(e_flat)
    e_s, tok_s, w_s = e_flat[order], tok_flat[order], w_flat[order]
    sizes = jax.ops.segment_sum(jnp.ones((n_assign,), jnp.int32), e_flat, num_segments=N_EXPERTS)
    starts = jnp.cumsum(sizes) - sizes
    padded = (sizes + MOE_BLOCK - 1) // MOE_BLOCK * MOE_BLOCK
    pad_end = jnp.cumsum(padded)
    pad_start = pad_end - padded
    dest = pad_start[e_s] + (jnp.arange(n_assign, dtype=jnp.int32) - starts[e_s])
    n_blocks = -(-n_assign // MOE_BLOCK) + N_EXPERTS
    cap = n_blocks * MOE_BLOCK
    buf_tok = jnp.zeros((cap,), jnp.int32).at[dest].set(tok_s)
    buf_w = jnp.zeros((cap,), jnp.float32).at[dest].set(w_s)
    blk_e = jnp.minimum(jnp.searchsorted(pad_end, jnp.arange(n_blocks) * MOE_BLOCK, side='right'),
                        N_EXPERTS - 1).astype(jnp.int32)

    def expert_block(args):
        tok_idx, e = args
        xb = ht[tok_idx]
        gu = xb @ w_gu[e] + b_gu[e]
        x_glu, x_lin = jnp.split(gu, 2, axis=-1)
        x_glu = jnp.minimum(x_glu, SWIGLU_LIMIT)
        x_lin = jnp.clip(x_lin, -SWIGLU_LIMIT, SWIGLU_LIMIT)
        act = x_glu * jax.nn.sigmoid(SWIGLU_ALPHA * x_glu) * (x_lin + 1.0)
        return act @ w_dn[e] + b_dn[e]

    out = lax.map(expert_block, (buf_tok.reshape(n_blocks, MOE_BLOCK), blk_e)).reshape(cap, d)
    y = jnp.zeros((t, d), h.dtype).at[buf_tok].add(out * buf_w[:, None].astype(out.dtype))
    return y.reshape(bsz, s, d)


def setup_inputs(seed: int = 0) -> dict:
    key = jax.random.key(seed)
    ks = jax.random.split(key, 22)
    f32 = jnp.float32

    def nrm(k, shape, scale):
        return jax.random.normal(k, shape, f32) * scale

    return {
        "x": nrm(ks[0], (BATCH, SEQ, D_MODEL), 1.0),
        "c": nrm(ks[1], (BATCH, D_MODEL), 1.0),
        "w_ada": nrm(ks[2], (DEPTH, D_MODEL, N_MOD * D_MODEL), 0.5 * D_MODEL ** -0.5),
        "b_ada": nrm(ks[3], (DEPTH, N_MOD * D_MODEL), 0.02),
        "norm1_g": 1.0 + nrm(ks[4], (DEPTH, D_MODEL), 0.01),
        "w_in": nrm(ks[5], (DEPTH, D_MODEL, N_IN), D_MODEL ** -0.5),
        "q_norm_g": 1.0 + nrm(ks[6], (DEPTH, HEAD_DIM), 0.01),
        "k_norm_g": 1.0 + nrm(ks[7], (DEPTH, HEAD_DIM), 0.01),
        "rel_bias": nrm(ks[8], (REL_BUCKETS, len(A_GROUPS) * A_HEADS), 0.5),
        "w_br_a": nrm(ks[9], (DEPTH, A_OUT, D_MODEL), A_OUT ** -0.5),
        "w_br_b": nrm(ks[10], (DEPTH, B_Q, D_MODEL), B_Q ** -0.5),
        "w_out": nrm(ks[11], (DEPTH, D_MODEL, D_MODEL), D_MODEL ** -0.5),
        "norm2_g": 1.0 + nrm(ks[12], (DEPTH, D_MODEL), 0.01),
        "w_router": nrm(ks[13], (DEPTH, D_MODEL, N_EXPERTS), D_MODEL ** -0.5),
        "b_router": nrm(ks[14], (DEPTH, N_EXPERTS), 0.01),
        "w_gate_up": nrm(ks[15], (DEPTH, N_EXPERTS, D_MODEL, 2 * D_EXPERT), D_MODEL ** -0.5),
        "b_gate_up": nrm(ks[16], (DEPTH, N_EXPERTS, 2 * D_EXPERT), 0.01),
        "w_down": nrm(ks[17], (DEPTH, N_EXPERTS, D_EXPERT, D_MODEL), D_EXPERT ** -0.5),
        "b_down": nrm(ks[18], (DEPTH, N_EXPERTS, D_MODEL), 0.01),
        "final_norm_g": 1.0 + nrm(ks[19], (D_MODEL,), 0.01),
    }


def reference(x, c, w_ada, b_ada, norm1_g, w_in, q_norm_g, k_norm_g, rel_bias, w_br_a, w_br_b,
              w_out, norm2_g, w_router, b_router, w_gate_up, b_gate_up, w_down, b_down,
              final_norm_g):
    bsz, s, d = x.shape
    tabs = _axial_rope_tables(s)
    c_act = jax.nn.silu(c)
    cuts = np.cumsum([3 * A_WIDTH, B_Q, B_KV, B_KV]).tolist()
    for l in range(DEPTH):
        mod = c_act @ w_ada[l] + b_ada[l]
        sh1, sc1, g1, sh2, sc2, g2 = jnp.split(mod, N_MOD, axis=-1)
        h = _modulate(_rmsnorm(x, norm1_g[l]), sh1, sc1)
        z = h @ w_in[l]
        za, zq, zk, zv, zg = jnp.split(z, cuts, axis=-1)
        o_a = _mixer_dilated(za, rel_bias)
        o_b = _mixer_gqa(zq, zk, zv, q_norm_g[l], k_norm_g[l], tabs)
        gate = jax.nn.sigmoid(zg.astype(jnp.float32)).astype(x.dtype).reshape(bsz, s, N_BRANCH, d)
        merged = gate[:, :, 0] * (o_a @ w_br_a[l]) + gate[:, :, 1] * (o_b @ w_br_b[l])
        x = x + g1[:, None, :] * (merged @ w_out[l])
        h2 = _modulate(_rmsnorm(x, norm2_g[l]), sh2, sc2)
        x = x + g2[:, None, :] * _moe(h2, w_router[l], b_router[l], w_gate_up[l], b_gate_up[l],
                                      w_down[l], b_down[l])
    return _rmsnorm(x, final_norm_g)
```

```python
import functools
import math

import numpy as np
import jax
import jax.numpy as jnp
from jax import lax
from jax.experimental import pallas as pl
from jax.experimental.pallas import tpu as pltpu

F32 = jnp.float32
BF16 = jnp.bfloat16

HEAD_DIM = 64
A_GROUPS = ((128, 1), (512, 4), (2048, 16))
A_HEADS = 4
A_WIDTH = len(A_GROUPS) * A_HEADS * HEAD_DIM
A_OUT = A_HEADS * HEAD_DIM
A_GROUP_COLS = 3 * A_OUT
B_HEADS = 16
B_KV_HEADS = 4
B_GRP = B_HEADS // B_KV_HEADS
B_Q = B_HEADS * HEAD_DIM
B_KV = B_KV_HEADS * HEAD_DIM
GRID_W = 64
ROPE_THETA = 10000.0
REL_BUCKETS = 32
REL_MAX_DIST = 1024
N_EXPERTS = 32
TOP_K = 4
SWIGLU_LIMIT = 7.0
SWIGLU_ALPHA = 1.702
MOE_BLOCK = 512
N_MOD = 6
EPS = 1e-6
NEG_INF = -1e30
N_SIDE = 64

VMEM_LIMIT = 56 * 1024 * 1024


def _cparams(*sem):
    return pltpu.CompilerParams(dimension_semantics=sem, vmem_limit_bytes=VMEM_LIMIT)


def _ada_kernel(c_ref, w_ref, b_ref, o_ref):
    c = c_ref[...]
    ca = (c * jax.nn.sigmoid(c)).astype(BF16)
    o_ref[...] = jnp.dot(ca, w_ref[...].astype(BF16), preferred_element_type=F32) + b_ref[...]


def _ada_mod(c, w_ada, b_ada):
    depth, d, n = w_ada.shape
    bsz = c.shape[0]
    tn = 1536
    return pl.pallas_call(
        _ada_kernel,
        out_shape=jax.ShapeDtypeStruct((depth, bsz, n), F32),
        grid=(depth, n // tn),
        in_specs=[pl.BlockSpec((bsz, d), lambda l, j: (0, 0)),
                  pl.BlockSpec((None, d, tn), lambda l, j: (l, 0, j)),
                  pl.BlockSpec((None, 1, tn), lambda l, j: (l, 0, j))],
        out_specs=pl.BlockSpec((None, bsz, tn), lambda l, j: (l, 0, j)),
        compiler_params=_cparams("arbitrary", "arbitrary"),
        name="ada_mod",
    )(c, w_ada, b_ada.reshape(depth, 1, n))


def _proj_kernel(x_ref, sh_ref, sc_ref, g_ref, w_ref, aq_ref, bq_ref, ak_ref, bk_ref,
                 za0_ref, za1_ref, za2_ref, q_ref, k_ref, v_ref, gate_ref):
    x = x_ref[...]
    ms = jnp.mean(x * x, axis=-1, keepdims=True)
    h = x * lax.rsqrt(ms + EPS) * g_ref[...]
    h = h * (1.0 + sc_ref[...]) + sh_ref[...]
    hb = h.astype(BF16)

    def mm(lo, hi):
        return jnp.dot(hb, w_ref[:, lo:hi], preferred_element_type=F32)

    o = 0
    for za_ref in (za0_ref, za1_ref, za2_ref):
        za_ref[...] = mm(o, o + A_GROUP_COLS).astype(BF16)
        o += A_GROUP_COLS
    zq = mm(o, o + B_Q); o += B_Q
    zk = mm(o, o + B_KV); o += B_KV
    zv = mm(o, o + B_KV); o += B_KV
    zg = mm(o, o + gate_ref.shape[-1]); o += gate_ref.shape[-1]
    zqp = mm(o, o + B_Q); o += B_Q
    zkp = mm(o, o + B_KV); o += B_KV
    gate_ref[...] = jax.nn.sigmoid(zg).astype(BF16)
    aq, bq, ak, bk = aq_ref[...], bq_ref[...], ak_ref[...], bk_ref[...]
    for hh in range(B_HEADS):
        z = zq[:, hh * HEAD_DIM:(hh + 1) * HEAD_DIM]
        zp = zqp[:, hh * HEAD_DIM:(hh + 1) * HEAD_DIM]
        r = lax.rsqrt(jnp.mean(z * z, axis=-1, keepdims=True) + EPS)
        q_ref[hh] = (r * (z * aq + zp * bq) * (HEAD_DIM ** -0.5)).astype(BF16)
    for hh in range(B_KV_HEADS):
        z = zk[:, hh * HEAD_DIM:(hh + 1) * HEAD_DIM]
        zp = zkp[:, hh * HEAD_DIM:(hh + 1) * HEAD_DIM]
        r = lax.rsqrt(jnp.mean(z * z, axis=-1, keepdims=True) + EPS)
        k_ref[hh] = (r * (z * ak + zp * bk)).astype(BF16)
        v_ref[hh] = zv[:, hh * HEAD_DIM:(hh + 1) * HEAD_DIM].astype(BF16)


def _projection(x2, sh, sc, g, w_ext, tabs, bsz, s):
    t, d = x2.shape
    tm = 256
    tpb = s // tm
    n_ext = w_ext.shape[1]
    n_gate = 2 * d
    row = lambda i: (i, 0)
    per_b = lambda i: (i // tpb, 0, 0)
    tab = lambda i: (i % tpb, 0)
    hm = lambda i: (i // tpb, 0, i % tpb, 0)
    out_shape = (
        [jax.ShapeDtypeStruct((t, A_GROUP_COLS), BF16)] * 3
        + [jax.ShapeDtypeStruct((bsz, B_HEADS, s, HEAD_DIM), BF16),
           jax.ShapeDtypeStruct((bsz, B_KV_HEADS, s, HEAD_DIM), BF16),
           jax.ShapeDtypeStruct((bsz, B_KV_HEADS, s, HEAD_DIM), BF16),
           jax.ShapeDtypeStruct((t, n_gate), BF16)])
    out_specs = (
        [pl.BlockSpec((tm, A_GROUP_COLS), row)] * 3
        + [pl.BlockSpec((None, B_HEADS, tm, HEAD_DIM), hm),
           pl.BlockSpec((None, B_KV_HEADS, tm, HEAD_DIM), hm),
           pl.BlockSpec((None, B_KV_HEADS, tm, HEAD_DIM), hm),
           pl.BlockSpec((tm, n_gate), row)])
    return pl.pallas_call(
        _proj_kernel,
        out_shape=out_shape,
        grid=(t // tm,),
        in_specs=[pl.BlockSpec((tm, d), row),
                  pl.BlockSpec((None, 1, d), per_b),
                  pl.BlockSpec((None, 1, d), per_b),
                  pl.BlockSpec((1, d), lambda i: (0, 0)),
                  pl.BlockSpec((d, n_ext), lambda i: (0, 0))]
                 + [pl.BlockSpec((tm, HEAD_DIM), tab)] * 4,
        out_specs=out_specs,
        compiler_params=_cparams("arbitrary"),
        name="in_proj",
    )(x2, sh, sc, g, w_ext, *tabs)


def _dilated_kernel(q_ref, kp_ref, kc_ref, kn_ref, vp_ref, vc_ref, vn_ref, bias_ref,
                    o_ref, lse_ref, *, tq, seq_len):
    i = pl.program_id(2)
    q = q_ref[...]
    k = jnp.concatenate([kp_ref[...], kc_ref[...], kn_ref[...]], axis=0)
    v = jnp.concatenate([vp_ref[...], vc_ref[...], vn_ref[...]], axis=0)
    nk = k.shape[0]
    kpos = i * tq - N_SIDE + lax.broadcasted_iota(jnp.int32, (tq, nk), 1)
    valid = (kpos >= 0) & (kpos < seq_len)
    for hh in range(A_HEADS):
        sl = slice(hh * HEAD_DIM, (hh + 1) * HEAD_DIM)
        sc = lax.dot_general(q[:, sl], k[:, sl], (((1,), (1,)), ((), ())),
                             preferred_element_type=F32)
        sc = sc * (HEAD_DIM ** -0.5) + bias_ref[hh]
        sc = jnp.where(valid, sc, NEG_INF)
        m = jnp.max(sc, axis=-1, keepdims=True)
        p = jnp.exp(sc - m)
        den = jnp.sum(p, axis=-1, keepdims=True)
        o = jnp.dot(p.astype(BF16), v[:, sl], preferred_element_type=F32) / den
        o_ref[:, sl] = o
        lse_ref[:, sl] = jnp.broadcast_to(m + jnp.log(den), (tq, HEAD_DIM))


def _t5_bucket(rel):
    nb = REL_BUCKETS // 2
    max_exact = nb // 2
    ret = jnp.where(rel > 0, nb, 0)
    n = jnp.abs(rel)
    nf = jnp.maximum(n, 1).astype(F32)
    large = max_exact + (jnp.log(nf / max_exact) / math.log(REL_MAX_DIST / max_exact)
                         * (nb - max_exact)).astype(jnp.int32)
    large = jnp.minimum(large, nb - 1)
    return ret + jnp.where(n < max_exact, n, large)


def _band_bias(rel_bias_g, dilation, tq):
    nk = tq + 2 * N_SIDE
    rel = jnp.arange(nk)[None, :] - N_SIDE - jnp.arange(tq)[:, None]
    bias = jnp.transpose(rel_bias_g[_t5_bucket(rel * dilation)], (2, 0, 1)).astype(F32)
    return jnp.where((jnp.abs(rel) <= N_SIDE)[None], bias, NEG_INF)


def _dilated_group(za, rel_bias_g, dilation, bsz, s):
    ll = s // dilation
    tq = 128
    assert ll % tq == 0 and tq == 2 * N_SIDE
    zv = za.reshape(bsz, ll, dilation * A_GROUP_COLS)
    nhalf = ll // N_SIDE
    cur = lambda c: (lambda b, r, i: (b, i, 3 * r + c))
    prv = lambda c: (lambda b, r, i: (b, jnp.maximum(2 * i - 1, 0), 3 * r + c))
    nxt = lambda c: (lambda b, r, i: (b, jnp.minimum(2 * i + 2, nhalf - 1), 3 * r + c))
    full = pl.BlockSpec((None, tq, A_OUT), cur(0))
    half = lambda f: pl.BlockSpec((None, N_SIDE, A_OUT), f)
    bias = _band_bias(rel_bias_g, dilation, tq)
    out_sd = jax.ShapeDtypeStruct((bsz, ll, dilation * A_OUT), F32)
    out_spec = pl.BlockSpec((None, tq, A_OUT), lambda b, r, i: (b, i, r))
    o, lse = pl.pallas_call(
        functools.partial(_dilated_kernel, tq=tq, seq_len=ll),
        out_shape=[out_sd, out_sd],
        grid=(bsz, dilation, ll // tq),
        in_specs=[full,
                  half(prv(1)), pl.BlockSpec((None, tq, A_OUT), cur(1)), half(nxt(1)),
                  half(prv(2)), pl.BlockSpec((None, tq, A_OUT), cur(2)), half(nxt(2)),
                  pl.BlockSpec(bias.shape, lambda b, r, i: (0, 0, 0))],
        out_specs=[out_spec, out_spec],
        compiler_params=_cparams("arbitrary", "arbitrary", "arbitrary"),
        name=f"dilated_attn_d{dilation}",
    )(zv, zv, zv, zv, zv, zv, zv, bias)
    return o.reshape(bsz * s, A_OUT), lse.reshape(bsz * s, A_OUT)


def _gqa_kernel(q_ref, k_ref, v_ref, o_ref):
    k = k_ref[...]
    v = v_ref[...]
    for hh in range(B_GRP):
        sc = lax.dot_general(q_ref[hh], k, (((1,), (1,)), ((), ())), preferred_element_type=F32)
        m = jnp.max(sc, axis=-1, keepdims=True)
        p = jnp.exp(sc - m)
        den = jnp.sum(p, axis=-1, keepdims=True)
        o = jnp.dot(p.astype(BF16), v, preferred_element_type=F32) / den
        o_ref[:, hh * HEAD_DIM:(hh + 1) * HEAD_DIM] = o.astype(BF16)


def _gqa_attention(q, k, v):
    bsz, _, s, _ = q.shape
    tq = 256
    return pl.pallas_call(
        _gqa_kernel,
        out_shape=jax.ShapeDtypeStruct((bsz, s, B_Q), BF16),
        grid=(bsz, B_KV_HEADS, s // tq),
        in_specs=[pl.BlockSpec((None, B_GRP, tq, HEAD_DIM), lambda b, h, i: (b, h, i, 0)),
                  pl.BlockSpec((None, None, s, HEAD_DIM), lambda b, h, i: (b, h, 0, 0)),
                  pl.BlockSpec((None, None, s, HEAD_DIM), lambda b, h, i: (b, h, 0, 0))],
        out_specs=pl.BlockSpec((None, tq, B_GRP * HEAD_DIM), lambda b, h, i: (b, i, h)),
        compiler_params=_cparams("arbitrary", "arbitrary", "arbitrary"),
        name="gqa_attn",
    )(q, k, v)


def _merge_kernel(o0_ref, o1_ref, o2_ref, l0_ref, l1_ref, l2_ref, ob_ref, gate_ref, x_ref, g1_ref,
                  wa_ref, wb_ref, wo_ref, xo_ref):
    d = x_ref.shape[-1]
    l0, l1, l2 = l0_ref[...], l1_ref[...], l2_ref[...]
    m = jnp.maximum(jnp.maximum(l0, l1), l2)
    e0, e1, e2 = jnp.exp(l0 - m), jnp.exp(l1 - m), jnp.exp(l2 - m)
    tot = e0 + e1 + e2
    oa = (e0 / tot) * o0_ref[...] + (e1 / tot) * o1_ref[...] + (e2 / tot) * o2_ref[...]
    ya = jnp.dot(oa.astype(BF16), wa_ref[...], preferred_element_type=F32)
    yb = jnp.dot(ob_ref[...], wb_ref[...], preferred_element_type=F32)
    merged = gate_ref[:, :d].astype(F32) * ya + gate_ref[:, d:].astype(F32) * yb
    y = jnp.dot(merged.astype(BF16), wo_ref[...], preferred_element_type=F32)
    xo_ref[...] = x_ref[...] + g1_ref[...] * y


def _merge(os_, ls_, ob, gate, x2, g1, wa, wb, wo, s):
    t, d = x2.shape
    tm = 256
    tpb = s // tm
    row = lambda i: (i, 0)
    const = lambda i: (0, 0)
    a_spec = pl.BlockSpec((tm, A_OUT), row)
    return pl.pallas_call(
        _merge_kernel,
        out_shape=jax.ShapeDtypeStruct((t, d), F32),
        grid=(t // tm,),
        in_specs=[a_spec] * 6
                 + [pl.BlockSpec((tm, B_Q), row),
                    pl.BlockSpec((tm, 2 * d), row),
                    pl.BlockSpec((tm, d), row),
                    pl.BlockSpec((None, 1, d), lambda i: (i // tpb, 0, 0)),
                    pl.BlockSpec(wa.shape, const),
                    pl.BlockSpec(wb.shape, const),
                    pl.BlockSpec(wo.shape, const)],
        out_specs=pl.BlockSpec((tm, d), row),
        compiler_params=_cparams("arbitrary"),
        name="branch_merge",
    )(*os_, *ls_, ob, gate, x2, g1, wa, wb, wo)


def _router_kernel(x_ref, sh_ref, sc_ref, g_ref, whi_ref, wlo_ref, b_ref, h_ref, e_ref, w_ref):
    x = x_ref[...]
    ms = jnp.mean(x * x, axis=-1, keepdims=True)
    h = x * lax.rsqrt(ms + EPS) * g_ref[...]
    h = h * (1.0 + sc_ref[...]) + sh_ref[...]
    h_ref[...] = h
    hhi = h.astype(BF16)
    hlo = (h - hhi.astype(F32)).astype(BF16)
    nt = (((1,), (1,)), ((), ()))
    dotf = lambda a, b: lax.dot_general(a, b, nt, preferred_element_type=F32)
    logits = dotf(whi_ref[...], hhi) + (dotf(whi_ref[...], hlo) + dotf(wlo_ref[...], hhi)) + b_ref[...]
    ne, tm = logits.shape
    iota = lax.broadcasted_iota(jnp.int32, (ne, tm), 0).astype(F32)
    vals, idxs = [], []
    cur = logits
    for _ in range(TOP_K):
        m = jnp.max(cur, axis=0, keepdims=True)
        idx = jnp.min(jnp.where(cur == m, iota, float(ne)), axis=0, keepdims=True)
        vals.append(m)
        idxs.append(idx)
        cur = jnp.where(iota == idx, -jnp.inf, cur)
    tv = jnp.concatenate(vals, axis=0)
    ex = jnp.exp(tv - tv[0:1])
    w_ref[...] = ex / jnp.sum(ex, axis=0, keepdims=True)
    e_ref[...] = jnp.concatenate(idxs, axis=0).astype(jnp.int32)


def _router(x2, sh, sc, g, w_router, b_router, s):
    t, d = x2.shape
    tm = 256
    tpb = s // tm
    ne = w_router.shape[1]
    wt = w_router.T
    whi = wt.astype(BF16)
    wlo = (wt - whi.astype(F32)).astype(BF16)
    per_b = lambda i: (i // tpb, 0, 0)
    const = lambda i: (0, 0)
    return pl.pallas_call(
        _router_kernel,
        out_shape=[jax.ShapeDtypeStruct((t, d), F32),
                   jax.ShapeDtypeStruct((TOP_K, t), jnp.int32),
                   jax.ShapeDtypeStruct((TOP_K, t), F32)],
        grid=(t // tm,),
        in_specs=[pl.BlockSpec((tm, d), lambda i: (i, 0)),
                  pl.BlockSpec((None, 1, d), per_b),
                  pl.BlockSpec((None, 1, d), per_b),
                  pl.BlockSpec((1, d), const),
                  pl.BlockSpec((ne, d), const),
                  pl.BlockSpec((ne, d), const),
                  pl.BlockSpec((ne, 1), const)],
        out_specs=[pl.BlockSpec((tm, d), lambda i: (i, 0)),
                   pl.BlockSpec((TOP_K, tm), lambda i: (0, i)),
                   pl.BlockSpec((TOP_K, tm), lambda i: (0, i))],
        compiler_params=_cparams("arbitrary"),
        name="moe_router",
    )(x2, sh, sc, g, whi, wlo, b_router.reshape(ne, 1))


def _expert_kernel(blk_e_ref, n_used_ref, tok_ref, h_hbm, wgu_ref, bgu_ref, wdn_ref, bdn_ref,
                   o_ref, xbuf, sem):
    j = pl.program_id(0)
    nrow = xbuf.shape[0]
    d = xbuf.shape[1]

    @pl.when(j < n_used_ref[0])
    def _():
        def issue(r, carry):
            pltpu.make_async_copy(h_hbm.at[pl.ds(tok_ref[0, 0, r], 1), :],
                                  xbuf.at[pl.ds(r, 1), :], sem).start()
            return carry
        lax.fori_loop(0, nrow, issue, 0)
        pltpu.make_async_copy(h_hbm.at[pl.ds(0, nrow), :], xbuf, sem).wait()
        xb = xbuf[...].astype(BF16)
        gu = jnp.dot(xb, wgu_ref[...], preferred_element_type=F32) + bgu_ref[...]
        x_glu = jnp.minimum(gu[:, :d], SWIGLU_LIMIT)
        x_lin = jnp.clip(gu[:, d:], -SWIGLU_LIMIT, SWIGLU_LIMIT)
        act = x_glu * jax.nn.sigmoid(SWIGLU_ALPHA * x_glu) * (x_lin + 1.0)
        o_ref[...] = jnp.dot(act.astype(BF16), wdn_ref[...], preferred_element_type=F32) + bdn_ref[...]

    @pl.when(j >= n_used_ref[0])
    def _():
        o_ref[...] = jnp.zeros_like(o_ref)


def _experts(h2, buf_tok, blk_e, n_used, wgu, bgu, wdn, bdn):
    t, d = h2.shape
    n_blocks = blk_e.shape[0]
    ne = wgu.shape[0]
    grid_spec = pltpu.PrefetchScalarGridSpec(
        num_scalar_prefetch=2,
        grid=(n_blocks,),
        in_specs=[pl.BlockSpec((1, 1, MOE_BLOCK), lambda j, be, nu: (j, 0, 0),
                               memory_space=pltpu.SMEM),
                  pl.BlockSpec(memory_space=pl.ANY),
                  pl.BlockSpec((None, d, 2 * d), lambda j, be, nu: (be[j], 0, 0)),
                  pl.BlockSpec((None, 1, 2 * d), lambda j, be, nu: (be[j], 0, 0)),
                  pl.BlockSpec((None, d, d), lambda j, be, nu: (be[j], 0, 0)),
                  pl.BlockSpec((None, 1, d), lambda j, be, nu: (be[j], 0, 0))],
        out_specs=pl.BlockSpec((MOE_BLOCK, d), lambda j, be, nu: (j, 0)),
        scratch_shapes=[pltpu.VMEM((MOE_BLOCK, d), F32), pltpu.SemaphoreType.DMA],
    )
    return pl.pallas_call(
        _expert_kernel,
        out_shape=jax.ShapeDtypeStruct((n_blocks * MOE_BLOCK, d), F32),
        grid_spec=grid_spec,
        compiler_params=_cparams("arbitrary"),
        name="moe_experts",
    )(blk_e, n_used, buf_tok.reshape(n_blocks, 1, MOE_BLOCK), h2,
      wgu, bgu.reshape(ne, 1, 2 * d), wdn, bdn.reshape(ne, 1, d))


def _combine_kernel(dest_ref, out_hbm, w_ref, x_ref, g2_ref, fg_ref, xo_ref, ybuf, sem, *, final):
    tm = x_ref.shape[0]

    def issue(r, carry):
        for kk in range(TOP_K):
            pltpu.make_async_copy(out_hbm.at[pl.ds(dest_ref[0, 0, r * TOP_K + kk], 1), :],
                                  ybuf.at[kk, pl.ds(r, 1), :], sem).start()
        return carry
    lax.fori_loop(0, tm, issue, 0)
    for kk in range(TOP_K):
        pltpu.make_async_copy(out_hbm.at[pl.ds(0, tm), :], ybuf.at[kk], sem).wait()
    w = w_ref[...]
    y = w[:, 0:1] * ybuf[0]
    for kk in range(1, TOP_K):
        y = y + w[:, kk:kk + 1] * ybuf[kk]
    xn = x_ref[...] + g2_ref[...] * y
    if final:
        ms = jnp.mean(xn * xn, axis=-1, keepdims=True)
        xn = xn * lax.rsqrt(ms + EPS) * fg_ref[...]
    xo_ref[...] = xn


def _combine(dest, out_sorted, gates_t, x2, g2, final_g, s, final):
    t, d = x2.shape
    tm = 128
    tpb = s // tm
    return pl.pallas_call(
        functools.partial(_combine_kernel, final=final),
        out_shape=jax.ShapeDtypeStruct((t, d), F32),
        grid=(t // tm,),
        in_specs=[pl.BlockSpec((1, 1, TOP_K * tm), lambda i: (i, 0, 0), memory_space=pltpu.SMEM),
                  pl.BlockSpec(memory_space=pl.ANY),
                  pl.BlockSpec((tm, TOP_K), lambda i: (i, 0)),
                  pl.BlockSpec((tm, d), lambda i: (i, 0)),
                  pl.BlockSpec((None, 1, d), lambda i: (i // tpb, 0, 0)),
                  pl.BlockSpec((1, d), lambda i: (0, 0))],
        out_specs=pl.BlockSpec((tm, d), lambda i: (i, 0)),
        scratch_shapes=[pltpu.VMEM((TOP_K, tm, d), F32), pltpu.SemaphoreType.DMA],
        compiler_params=_cparams("arbitrary"),
        name="moe_combine",
    )(dest.reshape(t // tm, 1, TOP_K * tm), out_sorted, gates_t, x2, g2, final_g)


def _moe_plan(top_e):
    t = top_e.shape[1]
    n_assign = t * TOP_K
    e_flat = top_e.T.reshape(n_assign)
    onehot = (e_flat[:, None] == jnp.arange(N_EXPERTS, dtype=jnp.int32)[None, :]).astype(jnp.int32)
    csum = jnp.cumsum(onehot, axis=0)
    sizes = csum[-1]
    pos = jnp.take_along_axis(csum - onehot, e_flat[:, None], axis=1)[:, 0]
    padded = (sizes + MOE_BLOCK - 1) // MOE_BLOCK * MOE_BLOCK
    pad_end = jnp.cumsum(padded)
    pad_start = pad_end - padded
    dest = pad_start[e_flat] + pos
    n_blocks = -(-n_assign // MOE_BLOCK) + N_EXPERTS
    tok_flat = jnp.repeat(jnp.arange(t, dtype=jnp.int32), TOP_K)
    buf_tok = jnp.zeros((n_blocks * MOE_BLOCK,), jnp.int32).at[dest].set(tok_flat)
    blk_e = jnp.minimum(jnp.searchsorted(pad_end, jnp.arange(n_blocks) * MOE_BLOCK, side='right'),
                        N_EXPERTS - 1).astype(jnp.int32)
    n_used = (pad_end[-1] // MOE_BLOCK).astype(jnp.int32).reshape(1)
    return dest.astype(jnp.int32), buf_tok, blk_e, n_used


def _rope_tables(s, gain):
    n_rows = s // GRID_W
    row = jnp.repeat(jnp.arange(n_rows), GRID_W).astype(F32)
    col = (jnp.arange(s) % GRID_W).astype(F32)
    half = HEAD_DIM // 2
    inv = ROPE_THETA ** (-jnp.arange(0, half, 2, dtype=F32) / half)
    ang_r = row[:, None] * inv
    ang_c = col[:, None] * inv
    cos = jnp.concatenate([jnp.cos(ang_r)] * 2 + [jnp.cos(ang_c)] * 2, axis=-1)
    sin = jnp.concatenate([-jnp.sin(ang_r), jnp.sin(ang_r), -jnp.sin(ang_c), jnp.sin(ang_c)], axis=-1)
    gain = gain.astype(F32)
    return cos * gain[None, :], sin * gain[_PARTNER][None, :]


_q = HEAD_DIM // 4
_PARTNER = np.concatenate([np.arange(_q, 2 * _q), np.arange(0, _q),
                           np.arange(3 * _q, 4 * _q), np.arange(2 * _q, 3 * _q)])


def _partner_cols(w, n_heads):
    idx = (np.arange(n_heads)[:, None] * HEAD_DIM + _PARTNER[None, :]).reshape(-1)
    return w[:, idx]


def kernel(x, c, w_ada, b_ada, norm1_g, w_in, q_norm_g, k_norm_g, rel_bias, w_br_a, w_br_b, w_out,
           norm2_g, w_router, b_router, w_gate_up, b_gate_up, w_down, b_down, final_norm_g):
    bsz, s, d = x.shape
    depth = w_ada.shape[0]
    t = bsz * s
    mod = _ada_mod(c, w_ada, b_ada)
    x2 = x.reshape(t, d)
    q_off = 3 * A_WIDTH
    k_off = q_off + B_Q
    for l in range(depth):
        sh1, sc1, g1, sh2, sc2, g2 = [mod[l, :, i * d:(i + 1) * d].reshape(bsz, 1, d)
                                      for i in range(N_MOD)]
        w = w_in[l]
        w_ext = jnp.concatenate([w, _partner_cols(w[:, q_off:q_off + B_Q], B_HEADS),
                                 _partner_cols(w[:, k_off:k_off + B_KV], B_KV_HEADS)],
                                axis=1).astype(BF16)
        tabs = _rope_tables(s, q_norm_g[l]) + _rope_tables(s, k_norm_g[l])
        za0, za1, za2, q, k, v, gate = _projection(x2, sh1, sc1, norm1_g[l].reshape(1, d), w_ext,
                                                   tabs, bsz, s)
        os_, ls_ = [], []
        for g, (za, (_, dilation)) in enumerate(zip((za0, za1, za2), A_GROUPS)):
            o, lse = _dilated_group(za, rel_bias[:, g * A_HEADS:(g + 1) * A_HEADS], dilation, bsz, s)
            os_.append(o)
            ls_.append(lse)
        ob = _gqa_attention(q, k, v).reshape(t, B_Q)
        x2 = _merge(os_, ls_, ob, gate, x2, g1, w_br_a[l].astype(BF16), w_br_b[l].astype(BF16),
                    w_out[l].astype(BF16), s)
        h2, top_e, gates = _router(x2, sh2, sc2, norm2_g[l].reshape(1, d), w_router[l], b_router[l], s)
        dest, buf_tok, blk_e, n_used = _moe_plan(top_e)
        out_sorted = _experts(h2, buf_tok, blk_e, n_used, w_gate_up[l].astype(BF16), b_gate_up[l],
                              w_down[l].astype(BF16), b_down[l])
        x2 = _combine(dest, out_sorted, gates.T, x2, g2, final_norm_g.reshape(1, d), s,
                      final=(l == depth - 1))
    return x2.reshape(bsz, s, d)
```

```python
import functools
import math

import numpy as np
import jax
import jax.numpy as jnp
from jax import lax
from jax.experimental import pallas as pl
from jax.experimental.pallas import tpu as pltpu

F32 = jnp.float32
BF16 = jnp.bfloat16

HEAD_DIM = 64
A_GROUPS = ((128, 1), (512, 4), (2048, 16))
A_HEADS = 4
A_WIDTH = len(A_GROUPS) * A_HEADS * HEAD_DIM
A_OUT = A_HEADS * HEAD_DIM
A_GROUP_COLS = 3 * A_OUT
B_HEADS = 16
B_KV_HEADS = 4
B_GRP = B_HEADS // B_KV_HEADS
B_Q = B_HEADS * HEAD_DIM
B_KV = B_KV_HEADS * HEAD_DIM
GRID_W = 64
ROPE_THETA = 10000.0
REL_BUCKETS = 32
REL_MAX_DIST = 1024
N_EXPERTS = 32
TOP_K = 4
SWIGLU_LIMIT = 7.0
SWIGLU_ALPHA = 1.702
MOE_BLOCK = 512
N_MOD = 6
EPS = 1e-6
NEG_INF = -1e30
LOG2E = math.log2(math.e)
N_SIDE = 64

VMEM_LIMIT = 56 * 1024 * 1024


def _cparams(*sem):
    return pltpu.CompilerParams(dimension_semantics=sem, vmem_limit_bytes=VMEM_LIMIT)


def _ada_kernel(c_ref, w_ref, b_ref, o_ref):
    c = c_ref[...]
    ca = (c * jax.nn.sigmoid(c)).astype(BF16)
    o_ref[...] = jnp.dot(ca, w_ref[...].astype(BF16), preferred_element_type=F32) + b_ref[...]


def _ada_mod(c, w_ada, b_ada):
    depth, d, n = w_ada.shape
    bsz = c.shape[0]
    tn = 1536
    return pl.pallas_call(
        _ada_kernel,
        out_shape=jax.ShapeDtypeStruct((depth, bsz, n), F32),
        grid=(depth, n // tn),
        in_specs=[pl.BlockSpec((bsz, d), lambda l, j: (0, 0)),
                  pl.BlockSpec((None, d, tn), lambda l, j: (l, 0, j)),
                  pl.BlockSpec((None, 1, tn), lambda l, j: (l, 0, j))],
        out_specs=pl.BlockSpec((None, bsz, tn), lambda l, j: (l, 0, j)),
        compiler_params=_cparams("arbitrary", "arbitrary"),
        name="ada_mod",
    )(c, w_ada, b_ada.reshape(depth, 1, n))


def _proj_kernel(x_ref, sh_ref, sc_ref, g_ref, w_ref, aq_ref, bq_ref, ak_ref, bk_ref,
                 za0_ref, za1_ref, za2_ref, q_ref, k_ref, v_ref, gate_ref):
    x = x_ref[...]
    ms = jnp.mean(x * x, axis=-1, keepdims=True)
    h = x * lax.rsqrt(ms + EPS) * g_ref[...]
    h = h * (1.0 + sc_ref[...]) + sh_ref[...]
    hb = h.astype(BF16)

    def mm(lo, hi):
        return jnp.dot(hb, w_ref[:, lo:hi], preferred_element_type=F32)

    o = 0
    for za_ref in (za0_ref, za1_ref, za2_ref):
        za_ref[...] = mm(o, o + A_GROUP_COLS).astype(BF16)
        o += A_GROUP_COLS
    zq = mm(o, o + B_Q); o += B_Q
    zk = mm(o, o + B_KV); o += B_KV
    zv = mm(o, o + B_KV); o += B_KV
    zg = mm(o, o + gate_ref.shape[-1]); o += gate_ref.shape[-1]
    zqp = mm(o, o + B_Q); o += B_Q
    zkp = mm(o, o + B_KV); o += B_KV
    gate_ref[...] = jax.nn.sigmoid(zg).astype(BF16)
    aq, bq, ak, bk = aq_ref[...], bq_ref[...], ak_ref[...], bk_ref[...]
    for hh in range(B_HEADS):
        z = zq[:, hh * HEAD_DIM:(hh + 1) * HEAD_DIM]
        zp = zqp[:, hh * HEAD_DIM:(hh + 1) * HEAD_DIM]
        r = lax.rsqrt(jnp.mean(z * z, axis=-1, keepdims=True) + EPS)
        q_ref[hh] = (r * (z * aq + zp * bq) * (LOG2E * HEAD_DIM ** -0.5)).astype(BF16)
    for hh in range(B_KV_HEADS):
        z = zk[:, hh * HEAD_DIM:(hh + 1) * HEAD_DIM]
        zp = zkp[:, hh * HEAD_DIM:(hh + 1) * HEAD_DIM]
        r = lax.rsqrt(jnp.mean(z * z, axis=-1, keepdims=True) + EPS)
        k_ref[hh] = (r * (z * ak + zp * bk)).astype(BF16)
        v_ref[hh, :, :HEAD_DIM] = zv[:, hh * HEAD_DIM:(hh + 1) * HEAD_DIM].astype(BF16)
        lane = lax.broadcasted_iota(jnp.int32, (zv.shape[0], V_PAD - HEAD_DIM), 1)
        v_ref[hh, :, HEAD_DIM:] = jnp.where(lane == 0, 1.0, 0.0).astype(BF16)


def _projection(x2, sh, sc, g, w_ext, tabs, bsz, s):
    t, d = x2.shape
    tm = 256
    tpb = s // tm
    n_ext = w_ext.shape[1]
    n_gate = 2 * d
    row = lambda i: (i, 0)
    per_b = lambda i: (i // tpb, 0, 0)
    tab = lambda i: (i % tpb, 0)
    hm = lambda i: (i // tpb, 0, i % tpb, 0)
    out_shape = (
        [jax.ShapeDtypeStruct((t, A_GROUP_COLS), BF16)] * 3
        + [jax.ShapeDtypeStruct((bsz, B_HEADS, s, HEAD_DIM), BF16),
           jax.ShapeDtypeStruct((bsz, B_KV_HEADS, s, HEAD_DIM), BF16),
           jax.ShapeDtypeStruct((bsz, B_KV_HEADS, s, V_PAD), BF16),
           jax.ShapeDtypeStruct((t, n_gate), BF16)])
    out_specs = (
        [pl.BlockSpec((tm, A_GROUP_COLS), row)] * 3
        + [pl.BlockSpec((None, B_HEADS, tm, HEAD_DIM), hm),
           pl.BlockSpec((None, B_KV_HEADS, tm, HEAD_DIM), hm),
           pl.BlockSpec((None, B_KV_HEADS, tm, V_PAD), hm),
           pl.BlockSpec((tm, n_gate), row)])
    return pl.pallas_call(
        _proj_kernel,
        out_shape=out_shape,
        grid=(t // tm,),
        in_specs=[pl.BlockSpec((tm, d), row),
                  pl.BlockSpec((None, 1, d), per_b),
                  pl.BlockSpec((None, 1, d), per_b),
                  pl.BlockSpec((1, d), lambda i: (0, 0)),
                  pl.BlockSpec((d, n_ext), lambda i: (0, 0))]
                 + [pl.BlockSpec((tm, HEAD_DIM), tab)] * 4,
        out_specs=out_specs,
        compiler_params=_cparams("arbitrary"),
        name="in_proj",
    )(x2, sh, sc, g, w_ext, *tabs)


def _dilated_kernel(q_ref, kp_ref, kc_ref, kn_ref, vp_ref, vc_ref, vn_ref, bias_ref,
                    o_ref, lse_ref, *, tq, seq_len):
    i = pl.program_id(2)
    q = q_ref[...]
    k = jnp.concatenate([kp_ref[...], kc_ref[...], kn_ref[...]], axis=0)
    v = jnp.concatenate([vp_ref[...], vc_ref[...], vn_ref[...]], axis=0)
    nk = k.shape[0]
    kpos = i * tq - N_SIDE + lax.broadcasted_iota(jnp.int32, (tq, nk), 1)
    valid = (kpos >= 0) & (kpos < seq_len)
    for hh in range(A_HEADS):
        sl = slice(hh * HEAD_DIM, (hh + 1) * HEAD_DIM)
        sc = lax.dot_general(q[:, sl], k[:, sl], (((1,), (1,)), ((), ())),
                             preferred_element_type=F32)
        sc = sc * (HEAD_DIM ** -0.5) + bias_ref[hh]
        sc = jnp.where(valid, sc, NEG_INF)
        m = jnp.max(sc, axis=-1, keepdims=True)
        p = jnp.exp(sc - m)
        den = jnp.sum(p, axis=-1, keepdims=True)
        o = jnp.dot(p.astype(BF16), v[:, sl], preferred_element_type=F32) / den
        o_ref[:, sl] = o
        lse_ref[:, sl] = jnp.broadcast_to(m + jnp.log(den), (tq, HEAD_DIM))


def _t5_bucket(rel):
    nb = REL_BUCKETS // 2
    max_exact = nb // 2
    ret = jnp.where(rel > 0, nb, 0)
    n = jnp.abs(rel)
    nf = jnp.maximum(n, 1).astype(F32)
    large = max_exact + (jnp.log(nf / max_exact) / math.log(REL_MAX_DIST / max_exact)
                         * (nb - max_exact)).astype(jnp.int32)
    large = jnp.minimum(large, nb - 1)
    return ret + jnp.where(n < max_exact, n, large)


def _band_bias(rel_bias_g, dilation, tq):
    nk = tq + 2 * N_SIDE
    rel = jnp.arange(nk)[None, :] - N_SIDE - jnp.arange(tq)[:, None]
    bucket = _t5_bucket(rel * dilation)[None]
    bias = jnp.full((A_HEADS,) + rel.shape, NEG_INF, F32)
    for b in range(REL_BUCKETS):
        bias = jnp.where(bucket == b, rel_bias_g[b].astype(F32)[:, None, None], bias)
    return jnp.where((jnp.abs(rel) <= N_SIDE)[None], bias, NEG_INF)


def _dilated_group(za, rel_bias_g, dilation, bsz, s):
    ll = s // dilation
    tq = 128
    assert ll % tq == 0 and tq == 2 * N_SIDE
    zv = za.reshape(bsz, ll, dilation * A_GROUP_COLS)
    nhalf = ll // N_SIDE
    cur = lambda c: (lambda b, r, i: (b, i, 3 * r + c))
    prv = lambda c: (lambda b, r, i: (b, jnp.maximum(2 * i - 1, 0), 3 * r + c))
    nxt = lambda c: (lambda b, r, i: (b, jnp.minimum(2 * i + 2, nhalf - 1), 3 * r + c))
    full = pl.BlockSpec((None, tq, A_OUT), cur(0))
    half = lambda f: pl.BlockSpec((None, N_SIDE, A_OUT), f)
    bias = _band_bias(rel_bias_g, dilation, tq)
    out_sd = jax.ShapeDtypeStruct((bsz, ll, dilation * A_OUT), F32)
    out_spec = pl.BlockSpec((None, tq, A_OUT), lambda b, r, i: (b, i, r))
    o, lse = pl.pallas_call(
        functools.partial(_dilated_kernel, tq=tq, seq_len=ll),
        out_shape=[out_sd, out_sd],
        grid=(bsz, dilation, ll // tq),
        in_specs=[full,
                  half(prv(1)), pl.BlockSpec((None, tq, A_OUT), cur(1)), half(nxt(1)),
                  half(prv(2)), pl.BlockSpec((None, tq, A_OUT), cur(2)), half(nxt(2)),
                  pl.BlockSpec(bias.shape, lambda b, r, i: (0, 0, 0))],
        out_specs=[out_spec, out_spec],
        compiler_params=_cparams("arbitrary", "arbitrary", "arbitrary"),
        name=f"dilated_attn_d{dilation}",
    )(zv, zv, zv, zv, zv, zv, zv, bias)
    return o.reshape(bsz * s, A_OUT), lse.reshape(bsz * s, A_OUT)


GQA_NO_SHIFT_MAX_LOG2 = 80.0
GQA_KEY_CHUNK = 512
V_PAD = 128


def _gqa_kernel_noshift(q_ref, k_ref, v_ref, o_ref):
    grp, tq, dh = q_ref.shape
    q = q_ref[...].reshape(grp * tq, dh)
    nt = (((1,), (1,)), ((), ()))
    acc = jnp.zeros((grp * tq, V_PAD), F32)
    for c in range(k_ref.shape[0] // GQA_KEY_CHUNK):
        rows = slice(c * GQA_KEY_CHUNK, (c + 1) * GQA_KEY_CHUNK)
        p = jnp.exp2(lax.dot_general(q, k_ref[rows, :], nt, preferred_element_type=F32))
        acc = acc + jnp.dot(p.astype(BF16), v_ref[rows, :], preferred_element_type=F32)
    o = acc[:, :dh] / acc[:, dh:dh + 1]
    for hh in range(grp):
        o_ref[:, hh * dh:(hh + 1) * dh] = o[hh * tq:(hh + 1) * tq].astype(BF16)


def _gqa_kernel_rowmax(q_ref, k_ref, v_ref, o_ref):
    k = k_ref[...]
    v = v_ref[:, :HEAD_DIM]
    for hh in range(q_ref.shape[0]):
        sc = lax.dot_general(q_ref[hh], k, (((1,), (1,)), ((), ())), preferred_element_type=F32)
        m = jnp.max(sc, axis=-1, keepdims=True)
        p = jnp.exp2(sc - m)
        den = jnp.sum(p, axis=-1, keepdims=True)
        o = jnp.dot(p.astype(BF16), v, preferred_element_type=F32) / den
        o_ref[:, hh * HEAD_DIM:(hh + 1) * HEAD_DIM] = o.astype(BF16)


def _gqa_call(body, name, tq, q, k, v):
    bsz, _, s, _ = q.shape
    return pl.pallas_call(
        body,
        out_shape=jax.ShapeDtypeStruct((bsz, s, B_Q), BF16),
        grid=(bsz, B_KV_HEADS, s // tq),
        in_specs=[pl.BlockSpec((None, B_GRP, tq, HEAD_DIM), lambda b, h, i: (b, h, i, 0)),
                  pl.BlockSpec((None, None, s, HEAD_DIM), lambda b, h, i: (b, h, 0, 0)),
                  pl.BlockSpec((None, None, s, V_PAD), lambda b, h, i: (b, h, 0, 0))],
        out_specs=pl.BlockSpec((None, tq, B_GRP * HEAD_DIM), lambda b, h, i: (b, i, h)),
        compiler_params=_cparams("arbitrary", "arbitrary", "arbitrary"),
        name=name,
    )(q, k, v)


def _gqa_attention(q, k, v, q_gain, k_gain):
    bound = (HEAD_DIM ** 0.5) * LOG2E * 1.01 * jnp.max(jnp.abs(q_gain)) * jnp.max(jnp.abs(k_gain))
    return lax.cond(bound <= GQA_NO_SHIFT_MAX_LOG2,
                    functools.partial(_gqa_call, _gqa_kernel_noshift, "gqa_attn", 512),
                    functools.partial(_gqa_call, _gqa_kernel_rowmax, "gqa_attn_rowmax", 256),
                    q, k, v)


def _merge_kernel(o0_ref, o1_ref, o2_ref, l0_ref, l1_ref, l2_ref, ob_ref, gate_ref, x_ref, g1_ref,
                  wa_ref, wb_ref, wo_ref, xo_ref):
    d = x_ref.shape[-1]
    l0, l1, l2 = l0_ref[...], l1_ref[...], l2_ref[...]
    m = jnp.maximum(jnp.maximum(l0, l1), l2)
    e0, e1, e2 = jnp.exp(l0 - m), jnp.exp(l1 - m), jnp.exp(l2 - m)
    tot = e0 + e1 + e2
    oa = (e0 / tot) * o0_ref[...] + (e1 / tot) * o1_ref[...] + (e2 / tot) * o2_ref[...]
    ya = jnp.dot(oa.astype(BF16), wa_ref[...], preferred_element_type=F32)
    yb = jnp.dot(ob_ref[...], wb_ref[...], preferred_element_type=F32)
    merged = gate_ref[:, :d].astype(F32) * ya + gate_ref[:, d:].astype(F32) * yb
    y = jnp.dot(merged.astype(BF16), wo_ref[...], preferred_element_type=F32)
    xo_ref[...] = x_ref[...] + g1_ref[...] * y


def _merge(os_, ls_, ob, gate, x2, g1, wa, wb, wo, s):
    t, d = x2.shape
    tm = 256
    tpb = s // tm
    row = lambda i: (i, 0)
    const = lambda i: (0, 0)
    a_spec = pl.BlockSpec((tm, A_OUT), row)
    return pl.pallas_call(
        _merge_kernel,
        out_shape=jax.ShapeDtypeStruct((t, d), F32),
        grid=(t // tm,),
        in_specs=[a_spec] * 6
                 + [pl.BlockSpec((tm, B_Q), row),
                    pl.BlockSpec((tm, 2 * d), row),
                    pl.BlockSpec((tm, d), row),
                    pl.BlockSpec((None, 1, d), lambda i: (i // tpb, 0, 0)),
                    pl.BlockSpec(wa.shape, const),
                    pl.BlockSpec(wb.shape, const),
                    pl.BlockSpec(wo.shape, const)],
        out_specs=pl.BlockSpec((tm, d), row),
        compiler_params=_cparams("arbitrary"),
        name="branch_merge",
    )(*os_, *ls_, ob, gate, x2, g1, wa, wb, wo)


def _router_kernel(x_ref, sh_ref, sc_ref, g_ref, whi_ref, wlo_ref, b_ref, h_ref, e_ref, w_ref):
    x = x_ref[...]
    ms = jnp.mean(x * x, axis=-1, keepdims=True)
    h = x * lax.rsqrt(ms + EPS) * g_ref[...]
    h = h * (1.0 + sc_ref[...]) + sh_ref[...]
    h_ref[...] = h
    hhi = h.astype(BF16)
    hlo = (h - hhi.astype(F32)).astype(BF16)
    nt = (((1,), (1,)), ((), ()))
    dotf = lambda a, b: lax.dot_general(a, b, nt, preferred_element_type=F32)
    logits = dotf(whi_ref[...], hhi) + (dotf(whi_ref[...], hlo) + dotf(wlo_ref[...], hhi)) + b_ref[...]
    ne, tm = logits.shape
    iota = lax.broadcasted_iota(jnp.int32, (ne, tm), 0).astype(F32)
    vals, idxs = [], []
    cur = logits
    for _ in range(TOP_K):
        m = jnp.max(cur, axis=0, keepdims=True)
        idx = jnp.min(jnp.where(cur == m, iota, float(ne)), axis=0, keepdims=True)
        vals.append(m)
        idxs.append(idx)
        cur = jnp.where(iota == idx, -jnp.inf, cur)
    tv = jnp.concatenate(vals, axis=0)
    ex = jnp.exp(tv - tv[0:1])
    w_ref[...] = ex / jnp.sum(ex, axis=0, keepdims=True)
    e_ref[...] = jnp.concatenate(idxs, axis=0).astype(jnp.int32)


def _router(x2, sh, sc, g, w_router, b_router, s):
    t, d = x2.shape
    tm = 256
    tpb = s // tm
    ne = w_router.shape[1]
    wt = w_router.T
    whi = wt.astype(BF16)
    wlo = (wt - whi.astype(F32)).astype(BF16)
    per_b = lambda i: (i // tpb, 0, 0)
    const = lambda i: (0, 0)
    return pl.pallas_call(
        _router_kernel,
        out_shape=[jax.ShapeDtypeStruct((t, d), F32),
                   jax.ShapeDtypeStruct((TOP_K, t), jnp.int32),
                   jax.ShapeDtypeStruct((TOP_K, t), F32)],
        grid=(t // tm,),
        in_specs=[pl.BlockSpec((tm, d), lambda i: (i, 0)),
                  pl.BlockSpec((None, 1, d), per_b),
                  pl.BlockSpec((None, 1, d), per_b),
                  pl.BlockSpec((1, d), const),
                  pl.BlockSpec((ne, d), const),
                  pl.BlockSpec((ne, d), const),
                  pl.BlockSpec((ne, 1), const)],
        out_specs=[pl.BlockSpec((tm, d), lambda i: (i, 0)),
                   pl.BlockSpec((TOP_K, tm), lambda i: (0, i)),
                   pl.BlockSpec((TOP_K, tm), lambda i: (0, i))],
        compiler_params=_cparams("arbitrary"),
        name="moe_router",
    )(x2, sh, sc, g, whi, wlo, b_router.reshape(ne, 1))


def _expert_kernel(blk_e_ref, n_used_ref, tok_ref, h_hbm, wgu_ref, bgu_ref, wdn_ref, bdn_ref,
                   o_ref, xbuf, sem):
    j = pl.program_id(0)
    nrow = xbuf.shape[0]
    d = xbuf.shape[1]

    @pl.when(j < n_used_ref[0])
    def _():
        def issue(r, carry):
            pltpu.make_async_copy(h_hbm.at[pl.ds(tok_ref[0, 0, r], 1), :],
                                  xbuf.at[pl.ds(r, 1), :], sem).start()
            return carry
        lax.fori_loop(0, nrow, issue, 0)
        pltpu.make_async_copy(h_hbm.at[pl.ds(0, nrow), :], xbuf, sem).wait()
        xb = xbuf[...].astype(BF16)
        gu = jnp.dot(xb, wgu_ref[...], preferred_element_type=F32) + bgu_ref[...]
        x_glu = jnp.minimum(gu[:, :d], SWIGLU_LIMIT)
        x_lin = jnp.clip(gu[:, d:], -SWIGLU_LIMIT, SWIGLU_LIMIT)
        act = x_glu * jax.nn.sigmoid(SWIGLU_ALPHA * x_glu) * (x_lin + 1.0)
        o_ref[...] = jnp.dot(act.astype(BF16), wdn_ref[...], preferred_element_type=F32) + bdn_ref[...]

    @pl.when(j >= n_used_ref[0])
    def _():
        o_ref[...] = jnp.zeros_like(o_ref)


def _experts(h2, buf_tok, blk_e, n_used, wgu, bgu, wdn, bdn):
    t, d = h2.shape
    n_blocks = blk_e.shape[0]
    ne = wgu.shape[0]
    grid_spec = pltpu.PrefetchScalarGridSpec(
        num_scalar_prefetch=2,
        grid=(n_blocks,),
        in_specs=[pl.BlockSpec((1, 1, MOE_BLOCK), lambda j, be, nu: (j, 0, 0),
                               memory_space=pltpu.SMEM),
                  pl.BlockSpec(memory_space=pl.ANY),
                  pl.BlockSpec((None, d, 2 * d), lambda j, be, nu: (be[j], 0, 0)),
                  pl.BlockSpec((None, 1, 2 * d), lambda j, be, nu: (be[j], 0, 0)),
                  pl.BlockSpec((None, d, d), lambda j, be, nu: (be[j], 0, 0)),
                  pl.BlockSpec((None, 1, d), lambda j, be, nu: (be[j], 0, 0))],
        out_specs=pl.BlockSpec((MOE_BLOCK, d), lambda j, be, nu: (j, 0)),
        scratch_shapes=[pltpu.VMEM((MOE_BLOCK, d), F32), pltpu.SemaphoreType.DMA],
    )
    return pl.pallas_call(
        _expert_kernel,
        out_shape=jax.ShapeDtypeStruct((n_blocks * MOE_BLOCK, d), F32),
        grid_spec=grid_spec,
        compiler_params=_cparams("arbitrary"),
        name="moe_experts",
    )(blk_e, n_used, buf_tok.reshape(n_blocks, 1, MOE_BLOCK), h2,
      wgu, bgu.reshape(ne, 1, 2 * d), wdn, bdn.reshape(ne, 1, d))


def _combine_kernel(dest_ref, out_hbm, w_ref, x_ref, g2_ref, fg_ref, xo_ref, ybuf, sem, *, final):
    tm = x_ref.shape[0]

    def issue(r, carry):
        for kk in range(TOP_K):
            pltpu.make_async_copy(out_hbm.at[pl.ds(dest_ref[0, 0, r * TOP_K + kk], 1), :],
                                  ybuf.at[kk, pl.ds(r, 1), :], sem).start()
        return carry
    lax.fori_loop(0, tm, issue, 0)
    for kk in range(TOP_K):
        pltpu.make_async_copy(out_hbm.at[pl.ds(0, tm), :], ybuf.at[kk], sem).wait()
    w = w_ref[...]
    y = w[:, 0:1] * ybuf[0]
    for kk in range(1, TOP_K):
        y = y + w[:, kk:kk + 1] * ybuf[kk]
    xn = x_ref[...] + g2_ref[...] * y
    if final:
        ms = jnp.mean(xn * xn, axis=-1, keepdims=True)
        xn = xn * lax.rsqrt(ms + EPS) * fg_ref[...]
    xo_ref[...] = xn


def _combine(dest, out_sorted, gates_t, x2, g2, final_g, s, final):
    t, d = x2.shape
    tm = 128
    tpb = s // tm
    return pl.pallas_call(
        functools.partial(_combine_kernel, final=final),
        out_shape=jax.ShapeDtypeStruct((t, d), F32),
        grid=(t // tm,),
        in_specs=[pl.BlockSpec((1, 1, TOP_K * tm), lambda i: (i, 0, 0), memory_space=pltpu.SMEM),
                  pl.BlockSpec(memory_space=pl.ANY),
                  pl.BlockSpec((tm, TOP_K), lambda i: (i, 0)),
                  pl.BlockSpec((tm, d), lambda i: (i, 0)),
                  pl.BlockSpec((None, 1, d), lambda i: (i // tpb, 0, 0)),
                  pl.BlockSpec((1, d), lambda i: (0, 0))],
        out_specs=pl.BlockSpec((tm, d), lambda i: (i, 0)),
        scratch_shapes=[pltpu.VMEM((TOP_K, tm, d), F32), pltpu.SemaphoreType.DMA],
        compiler_params=_cparams("arbitrary"),
        name="moe_combine",
    )(dest.reshape(t // tm, 1, TOP_K * tm), out_sorted, gates_t, x2, g2, final_g)


def _moe_plan(top_e):
    t = top_e.shape[1]
    n_assign = t * TOP_K
    e_flat = top_e.T.reshape(n_assign)
    onehot = (e_flat[:, None] == jnp.arange(N_EXPERTS, dtype=jnp.int32)[None, :]).astype(jnp.int32)
    csum = jnp.cumsum(onehot, axis=0)
    sizes = csum[-1]
    pos = jnp.take_along_axis(csum - onehot, e_flat[:, None], axis=1)[:, 0]
    padded = (sizes + MOE_BLOCK - 1) // MOE_BLOCK * MOE_BLOCK
    pad_end = jnp.cumsum(padded)
    pad_start = pad_end - padded
    dest = pad_start[e_flat] + pos
    n_blocks = -(-n_assign // MOE_BLOCK) + N_EXPERTS
    tok_flat = jnp.repeat(jnp.arange(t, dtype=jnp.int32), TOP_K)
    buf_tok = jnp.zeros((n_blocks * MOE_BLOCK,), jnp.int32).at[dest].set(tok_flat)
    blk_e = jnp.minimum(jnp.searchsorted(pad_end, jnp.arange(n_blocks) * MOE_BLOCK, side='right'),
                        N_EXPERTS - 1).astype(jnp.int32)
    n_used = (pad_end[-1] // MOE_BLOCK).astype(jnp.int32).reshape(1)
    return dest.astype(jnp.int32), buf_tok, blk_e, n_used


def _rope_tables(s, gain):
    n_rows = s // GRID_W
    row = jnp.repeat(jnp.arange(n_rows), GRID_W).astype(F32)
    col = (jnp.arange(s) % GRID_W).astype(F32)
    half = HEAD_DIM // 2
    inv = ROPE_THETA ** (-jnp.arange(0, half, 2, dtype=F32) / half)
    ang_r = row[:, None] * inv
    ang_c = col[:, None] * inv
    cos = jnp.concatenate([jnp.cos(ang_r)] * 2 + [jnp.cos(ang_c)] * 2, axis=-1)
    sin = jnp.concatenate([-jnp.sin(ang_r), jnp.sin(ang_r), -jnp.sin(ang_c), jnp.sin(ang_c)], axis=-1)
    gain = gain.astype(F32)
    return cos * gain[None, :], sin * gain[_PARTNER][None, :]


_q = HEAD_DIM // 4
_PARTNER = np.concatenate([np.arange(_q, 2 * _q), np.arange(0, _q),
                           np.arange(3 * _q, 4 * _q), np.arange(2 * _q, 3 * _q)])


def _partner_cols(w, n_heads):
    idx = (np.arange(n_heads)[:, None] * HEAD_DIM + _PARTNER[None, :]).reshape(-1)
    return w[:, idx]


def kernel(x, c, w_ada, b_ada, norm1_g, w_in, q_norm_g, k_norm_g, rel_bias, w_br_a, w_br_b, w_out,
           norm2_g, w_router, b_router, w_gate_up, b_gate_up, w_down, b_down, final_norm_g):
    bsz, s, d = x.shape
    depth = w_ada.shape[0]
    t = bsz * s
    mod = _ada_mod(c, w_ada, b_ada)
    x2 = x.reshape(t, d)
    q_off = 3 * A_WIDTH
    k_off = q_off + B_Q
    for l in range(depth):
        sh1, sc1, g1, sh2, sc2, g2 = [mod[l, :, i * d:(i + 1) * d].reshape(bsz, 1, d)
                                      for i in range(N_MOD)]
        w = w_in[l]
        w_ext = jnp.concatenate([w, _partner_cols(w[:, q_off:q_off + B_Q], B_HEADS),
                                 _partner_cols(w[:, k_off:k_off + B_KV], B_KV_HEADS)],
                                axis=1).astype(BF16)
        tabs = _rope_tables(s, q_norm_g[l]) + _rope_tables(s, k_norm_g[l])
        za0, za1, za2, q, k, v, gate = _projection(x2, sh1, sc1, norm1_g[l].reshape(1, d), w_ext,
                                                   tabs, bsz, s)
        os_, ls_ = [], []
        for g, (za, (_, dilation)) in enumerate(zip((za0, za1, za2), A_GROUPS)):
            o, lse = _dilated_group(za, rel_bias[:, g * A_HEADS:(g + 1) * A_HEADS], dilation, bsz, s)
            os_.append(o)
            ls_.append(lse)
        ob = _gqa_attention(q, k, v, q_norm_g[l], k_norm_g[l]).reshape(t, B_Q)
        x2 = _merge(os_, ls_, ob, gate, x2, g1, w_br_a[l].astype(BF16), w_br_b[l].astype(BF16),
                    w_out[l].astype(BF16), s)
        h2, top_e, gates = _router(x2, sh2, sc2, norm2_g[l].reshape(1, d), w_router[l], b_router[l], s)
        dest, buf_tok, blk_e, n_used = _moe_plan(top_e)
        out_sorted = _experts(h2, buf_tok, blk_e, n_used, w_gate_up[l].astype(BF16), b_gate_up[l],
                              w_down[l].astype(BF16), b_down[l])
        x2 = _combine(dest, out_sorted, gates.T, x2, g2, final_norm_g.reshape(1, d), s,
                      final=(l == depth - 1))
    return x2.reshape(bsz, s, d)
```

```python
import functools
import math

import numpy as np
import jax
import jax.numpy as jnp
from jax import lax
from jax.experimental import pallas as pl
from jax.experimental.pallas import tpu as pltpu

F32 = jnp.float32
BF16 = jnp.bfloat16

HEAD_DIM = 64
A_GROUPS = ((128, 1), (512, 4), (2048, 16))
A_HEADS = 4
A_WIDTH = len(A_GROUPS) * A_HEADS * HEAD_DIM
A_OUT = A_HEADS * HEAD_DIM
A_GROUP_COLS = 3 * A_OUT
B_HEADS = 16
B_KV_HEADS = 4
B_GRP = B_HEADS // B_KV_HEADS
B_Q = B_HEADS * HEAD_DIM
B_KV = B_KV_HEADS * HEAD_DIM
GRID_W = 64
ROPE_THETA = 10000.0
REL_BUCKETS = 32
REL_MAX_DIST = 1024
N_EXPERTS = 32
TOP_K = 4
SWIGLU_LIMIT = 7.0
SWIGLU_ALPHA = 1.702
MOE_BLOCK = 512
N_MOD = 6
EPS = 1e-6
NEG_INF = -1e30
LOG2E = math.log2(math.e)
N_SIDE = 64

VMEM_LIMIT = 56 * 1024 * 1024


def _cparams(*sem):
    return pltpu.CompilerParams(dimension_semantics=sem, vmem_limit_bytes=VMEM_LIMIT)


def _ada_kernel(c_ref, w_ref, b_ref, o_ref):
    c = c_ref[...]
    ca = (c * jax.nn.sigmoid(c)).astype(BF16)
    o_ref[...] = jnp.dot(ca, w_ref[...].astype(BF16), preferred_element_type=F32) + b_ref[...]


def _ada_mod(c, w_ada, b_ada):
    depth, d, n = w_ada.shape
    bsz = c.shape[0]
    tn = 1536
    return pl.pallas_call(
        _ada_kernel,
        out_shape=jax.ShapeDtypeStruct((depth, bsz, n), F32),
        grid=(depth, n // tn),
        in_specs=[pl.BlockSpec((bsz, d), lambda l, j: (0, 0)),
                  pl.BlockSpec((None, d, tn), lambda l, j: (l, 0, j)),
                  pl.BlockSpec((None, 1, tn), lambda l, j: (l, 0, j))],
        out_specs=pl.BlockSpec((None, bsz, tn), lambda l, j: (l, 0, j)),
        compiler_params=_cparams("arbitrary", "arbitrary"),
        name="ada_mod",
    )(c, w_ada, b_ada.reshape(depth, 1, n))


def _proj_kernel(x_ref, sh_ref, sc_ref, g_ref, w_ref, aq_ref, bq_ref, ak_ref, bk_ref,
                 za0_ref, za1_ref, za2_ref, q_ref, k_ref, v_ref, gate_ref):
    x = x_ref[...]
    ms = jnp.mean(x * x, axis=-1, keepdims=True)
    h = x * lax.rsqrt(ms + EPS) * g_ref[...]
    h = h * (1.0 + sc_ref[...]) + sh_ref[...]
    hb = h.astype(BF16)

    def mm(lo, hi):
        return jnp.dot(hb, w_ref[:, lo:hi], preferred_element_type=F32)

    o = 0
    for za_ref in (za0_ref, za1_ref, za2_ref):
        za_ref[...] = mm(o, o + A_GROUP_COLS).astype(BF16)
        o += A_GROUP_COLS
    zq = mm(o, o + B_Q); o += B_Q
    zk = mm(o, o + B_KV); o += B_KV
    zv = mm(o, o + B_KV); o += B_KV
    zg = mm(o, o + gate_ref.shape[-1]); o += gate_ref.shape[-1]
    zqp = mm(o, o + B_Q); o += B_Q
    zkp = mm(o, o + B_KV); o += B_KV
    gate_ref[...] = jax.nn.sigmoid(zg).astype(BF16)
    aq, bq, ak, bk = aq_ref[...], bq_ref[...], ak_ref[...], bk_ref[...]
    for hh in range(B_HEADS):
        z = zq[:, hh * HEAD_DIM:(hh + 1) * HEAD_DIM]
        zp = zqp[:, hh * HEAD_DIM:(hh + 1) * HEAD_DIM]
        r = lax.rsqrt(jnp.mean(z * z, axis=-1, keepdims=True) + EPS)
        q_ref[hh] = (r * (z * aq + zp * bq) * (LOG2E * HEAD_DIM ** -0.5)).astype(BF16)
    for hh in range(B_KV_HEADS):
        z = zk[:, hh * HEAD_DIM:(hh + 1) * HEAD_DIM]
        zp = zkp[:, hh * HEAD_DIM:(hh + 1) * HEAD_DIM]
        r = lax.rsqrt(jnp.mean(z * z, axis=-1, keepdims=True) + EPS)
        k_ref[hh] = (r * (z * ak + zp * bk)).astype(BF16)
        v_ref[hh, :, :HEAD_DIM] = zv[:, hh * HEAD_DIM:(hh + 1) * HEAD_DIM].astype(BF16)
        lane = lax.broadcasted_iota(jnp.int32, (zv.shape[0], V_PAD - HEAD_DIM), 1)
        v_ref[hh, :, HEAD_DIM:] = jnp.where(lane == 0, 1.0, 0.0).astype(BF16)


def _projection(x2, sh, sc, g, w_ext, tabs, bsz, s):
    t, d = x2.shape
    tm = 256
    tpb = s // tm
    n_ext = w_ext.shape[1]
    n_gate = 2 * d
    row = lambda i: (i, 0)
    per_b = lambda i: (i // tpb, 0, 0)
    tab = lambda i: (i % tpb, 0)
    hm = lambda i: (i // tpb, 0, i % tpb, 0)
    out_shape = (
        [jax.ShapeDtypeStruct((t, A_GROUP_COLS), BF16)] * 3
        + [jax.ShapeDtypeStruct((bsz, B_HEADS, s, HEAD_DIM), BF16),
           jax.ShapeDtypeStruct((bsz, B_KV_HEADS, s, HEAD_DIM), BF16),
           jax.ShapeDtypeStruct((bsz, B_KV_HEADS, s, V_PAD), BF16),
           jax.ShapeDtypeStruct((t, n_gate), BF16)])
    out_specs = (
        [pl.BlockSpec((tm, A_GROUP_COLS), row)] * 3
        + [pl.BlockSpec((None, B_HEADS, tm, HEAD_DIM), hm),
           pl.BlockSpec((None, B_KV_HEADS, tm, HEAD_DIM), hm),
           pl.BlockSpec((None, B_KV_HEADS, tm, V_PAD), hm),
           pl.BlockSpec((tm, n_gate), row)])
    return pl.pallas_call(
        _proj_kernel,
        out_shape=out_shape,
        grid=(t // tm,),
        in_specs=[pl.BlockSpec((tm, d), row),
                  pl.BlockSpec((None, 1, d), per_b),
                  pl.BlockSpec((None, 1, d), per_b),
                  pl.BlockSpec((1, d), lambda i: (0, 0)),
                  pl.BlockSpec((d, n_ext), lambda i: (0, 0))]
                 + [pl.BlockSpec((tm, HEAD_DIM), tab)] * 4,
        out_specs=out_specs,
        compiler_params=_cparams("arbitrary"),
        name="in_proj",
    )(x2, sh, sc, g, w_ext, *tabs)


def _dilated_kernel(q_ref, kp_ref, kc_ref, kn_ref, vp_ref, vc_ref, vn_ref, bias_ref,
                    o_ref, lse_ref, *, tq, seq_len):
    i = pl.program_id(2)
    q = q_ref[...]
    k = jnp.concatenate([kp_ref[...], kc_ref[...], kn_ref[...]], axis=0)
    v = jnp.concatenate([vp_ref[...], vc_ref[...], vn_ref[...]], axis=0)
    nk = k.shape[0]
    kpos = i * tq - N_SIDE + lax.broadcasted_iota(jnp.int32, (tq, nk), 1)
    valid = (kpos >= 0) & (kpos < seq_len)
    for hh in range(A_HEADS):
        sl = slice(hh * HEAD_DIM, (hh + 1) * HEAD_DIM)
        sc = lax.dot_general(q[:, sl], k[:, sl], (((1,), (1,)), ((), ())),
                             preferred_element_type=F32)
        sc = sc * (HEAD_DIM ** -0.5) + bias_ref[hh]
        sc = jnp.where(valid, sc, NEG_INF)
        m = jnp.max(sc, axis=-1, keepdims=True)
        p = jnp.exp(sc - m)
        den = jnp.sum(p, axis=-1, keepdims=True)
        o = jnp.dot(p.astype(BF16), v[:, sl], preferred_element_type=F32) / den
        o_ref[:, sl] = o
        lse_ref[:, sl] = jnp.broadcast_to(m + jnp.log(den), (tq, HEAD_DIM))


def _t5_bucket(rel):
    nb = REL_BUCKETS // 2
    max_exact = nb // 2
    ret = jnp.where(rel > 0, nb, 0)
    n = jnp.abs(rel)
    nf = jnp.maximum(n, 1).astype(F32)
    large = max_exact + (jnp.log(nf / max_exact) / math.log(REL_MAX_DIST / max_exact)
                         * (nb - max_exact)).astype(jnp.int32)
    large = jnp.minimum(large, nb - 1)
    return ret + jnp.where(n < max_exact, n, large)


def _band_bias(rel_bias_g, dilation, tq):
    nk = tq + 2 * N_SIDE
    rel = jnp.arange(nk)[None, :] - N_SIDE - jnp.arange(tq)[:, None]
    bucket = _t5_bucket(rel * dilation)[None]
    bias = jnp.full((A_HEADS,) + rel.shape, NEG_INF, F32)
    for b in range(REL_BUCKETS):
        bias = jnp.where(bucket == b, rel_bias_g[b].astype(F32)[:, None, None], bias)
    return jnp.where((jnp.abs(rel) <= N_SIDE)[None], bias, NEG_INF)


def _dilated_group(za, rel_bias_g, dilation, bsz, s):
    ll = s // dilation
    tq = 128
    assert ll % tq == 0 and tq == 2 * N_SIDE
    zv = za.reshape(bsz, ll, dilation * A_GROUP_COLS)
    nhalf = ll // N_SIDE
    cur = lambda c: (lambda b, r, i: (b, i, 3 * r + c))
    prv = lambda c: (lambda b, r, i: (b, jnp.maximum(2 * i - 1, 0), 3 * r + c))
    nxt = lambda c: (lambda b, r, i: (b, jnp.minimum(2 * i + 2, nhalf - 1), 3 * r + c))
    full = pl.BlockSpec((None, tq, A_OUT), cur(0))
    half = lambda f: pl.BlockSpec((None, N_SIDE, A_OUT), f)
    bias = _band_bias(rel_bias_g, dilation, tq)
    out_sd = jax.ShapeDtypeStruct((bsz, ll, dilation * A_OUT), F32)
    out_spec = pl.BlockSpec((None, tq, A_OUT), lambda b, r, i: (b, i, r))
    o, lse = pl.pallas_call(
        functools.partial(_dilated_kernel, tq=tq, seq_len=ll),
        out_shape=[out_sd, out_sd],
        grid=(bsz, dilation, ll // tq),
        in_specs=[full,
                  half(prv(1)), pl.BlockSpec((None, tq, A_OUT), cur(1)), half(nxt(1)),
                  half(prv(2)), pl.BlockSpec((None, tq, A_OUT), cur(2)), half(nxt(2)),
                  pl.BlockSpec(bias.shape, lambda b, r, i: (0, 0, 0))],
        out_specs=[out_spec, out_spec],
        compiler_params=_cparams("arbitrary", "arbitrary", "arbitrary"),
        name=f"dilated_attn_d{dilation}",
    )(zv, zv, zv, zv, zv, zv, zv, bias)
    return o.reshape(bsz * s, A_OUT), lse.reshape(bsz * s, A_OUT)


GQA_NO_SHIFT_MAX_LOG2 = 80.0
GQA_KEY_CHUNK = 512
V_PAD = 128


def _gqa_kernel_noshift(q_ref, k_ref, v_ref, o_ref):
    grp, tq, dh = q_ref.shape
    q = q_ref[...].reshape(grp * tq, dh)
    nt = (((1,), (1,)), ((), ()))
    acc = jnp.zeros((grp * tq, V_PAD), F32)
    for c in range(k_ref.shape[0] // GQA_KEY_CHUNK):
        rows = slice(c * GQA_KEY_CHUNK, (c + 1) * GQA_KEY_CHUNK)
        p = jnp.exp2(lax.dot_general(q, k_ref[rows, :], nt, preferred_element_type=F32))
        acc = acc + jnp.dot(p.astype(BF16), v_ref[rows, :], preferred_element_type=F32)
    o = acc[:, :dh] / acc[:, dh:dh + 1]
    for hh in range(grp):
        o_ref[:, hh * dh:(hh + 1) * dh] = o[hh * tq:(hh + 1) * tq].astype(BF16)


def _gqa_kernel_rowmax(q_ref, k_ref, v_ref, o_ref):
    k = k_ref[...]
    v = v_ref[:, :HEAD_DIM]
    for hh in range(q_ref.shape[0]):
        sc = lax.dot_general(q_ref[hh], k, (((1,), (1,)), ((), ())), preferred_element_type=F32)
        m = jnp.max(sc, axis=-1, keepdims=True)
        p = jnp.exp2(sc - m)
        den = jnp.sum(p, axis=-1, keepdims=True)
        o = jnp.dot(p.astype(BF16), v, preferred_element_type=F32) / den
        o_ref[:, hh * HEAD_DIM:(hh + 1) * HEAD_DIM] = o.astype(BF16)


def _gqa_call(body, name, tq, q, k, v):
    bsz, _, s, _ = q.shape
    return pl.pallas_call(
        body,
        out_shape=jax.ShapeDtypeStruct((bsz, s, B_Q), BF16),
        grid=(bsz, B_KV_HEADS, s // tq),
        in_specs=[pl.BlockSpec((None, B_GRP, tq, HEAD_DIM), lambda b, h, i: (b, h, i, 0)),
                  pl.BlockSpec((None, None, s, HEAD_DIM), lambda b, h, i: (b, h, 0, 0)),
                  pl.BlockSpec((None, None, s, V_PAD), lambda b, h, i: (b, h, 0, 0))],
        out_specs=pl.BlockSpec((None, tq, B_GRP * HEAD_DIM), lambda b, h, i: (b, i, h)),
        compiler_params=_cparams("arbitrary", "arbitrary", "arbitrary"),
        name=name,
    )(q, k, v)


def _gqa_attention(q, k, v, q_gain, k_gain):
    bound = (HEAD_DIM ** 0.5) * LOG2E * 1.01 * jnp.max(jnp.abs(q_gain)) * jnp.max(jnp.abs(k_gain))
    return lax.cond(bound <= GQA_NO_SHIFT_MAX_LOG2,
                    functools.partial(_gqa_call, _gqa_kernel_noshift, "gqa_attn", 512),
                    functools.partial(_gqa_call, _gqa_kernel_rowmax, "gqa_attn_rowmax", 256),
                    q, k, v)


def _merge_kernel(o0_ref, o1_ref, o2_ref, l0_ref, l1_ref, l2_ref, ob_ref, gate_ref, x_ref, g1_ref,
                  wa_ref, wb_ref, wo_ref, xo_ref):
    d = x_ref.shape[-1]
    l0, l1, l2 = l0_ref[...], l1_ref[...], l2_ref[...]
    m = jnp.maximum(jnp.maximum(l0, l1), l2)
    e0, e1, e2 = jnp.exp(l0 - m), jnp.exp(l1 - m), jnp.exp(l2 - m)
    tot = e0 + e1 + e2
    oa = (e0 / tot) * o0_ref[...] + (e1 / tot) * o1_ref[...] + (e2 / tot) * o2_ref[...]
    ya = jnp.dot(oa.astype(BF16), wa_ref[...], preferred_element_type=F32)
    yb = jnp.dot(ob_ref[...], wb_ref[...], preferred_element_type=F32)
    merged = gate_ref[:, :d].astype(F32) * ya + gate_ref[:, d:].astype(F32) * yb
    y = jnp.dot(merged.astype(BF16), wo_ref[...], preferred_element_type=F32)
    xo_ref[...] = x_ref[...] + g1_ref[...] * y


def _merge(os_, ls_, ob, gate, x2, g1, wa, wb, wo, s):
    t, d = x2.shape
    tm = 256
    tpb = s // tm
    row = lambda i: (i, 0)
    const = lambda i: (0, 0)
    a_spec = pl.BlockSpec((tm, A_OUT), row)
    return pl.pallas_call(
        _merge_kernel,
        out_shape=jax.ShapeDtypeStruct((t, d), F32),
        grid=(t // tm,),
        in_specs=[a_spec] * 6
                 + [pl.BlockSpec((tm, B_Q), row),
                    pl.BlockSpec((tm, 2 * d), row),
                    pl.BlockSpec((tm, d), row),
                    pl.BlockSpec((None, 1, d), lambda i: (i // tpb, 0, 0)),
                    pl.BlockSpec(wa.shape, const),
                    pl.BlockSpec(wb.shape, const),
                    pl.BlockSpec(wo.shape, const)],
        out_specs=pl.BlockSpec((tm, d), row),
        compiler_params=_cparams("arbitrary"),
        name="branch_merge",
    )(*os_, *ls_, ob, gate, x2, g1, wa, wb, wo)


def _router_kernel(x_ref, sh_ref, sc_ref, g_ref, whi_ref, wlo_ref, b_ref,
                   h_ref, e_ref, w_ref, pos_ref, cnt_ref, run_ref):
    @pl.when(pl.program_id(0) == 0)
    def _():
        run_ref[...] = jnp.zeros_like(run_ref)

    x = x_ref[...]
    ms = jnp.mean(x * x, axis=-1, keepdims=True)
    h = x * lax.rsqrt(ms + EPS) * g_ref[...]
    h = h * (1.0 + sc_ref[...]) + sh_ref[...]
    h_ref[...] = h
    hhi = h.astype(BF16)
    hlo = (h - hhi.astype(F32)).astype(BF16)
    nt = (((1,), (1,)), ((), ()))
    dotf = lambda a, b: lax.dot_general(a, b, nt, preferred_element_type=F32)
    logits = dotf(whi_ref[...], hhi) + (dotf(whi_ref[...], hlo) + dotf(wlo_ref[...], hhi)) + b_ref[...]
    ne, tm = logits.shape
    iota = lax.broadcasted_iota(jnp.int32, (ne, tm), 0).astype(F32)
    vals, idxs = [], []
    cur = logits
    for _ in range(TOP_K):
        m = jnp.max(cur, axis=0, keepdims=True)
        idx = jnp.min(jnp.where(cur == m, iota, float(ne)), axis=0, keepdims=True)
        vals.append(m)
        idxs.append(idx)
        cur = jnp.where(iota == idx, -jnp.inf, cur)
    tv = jnp.concatenate(vals, axis=0)
    ex = jnp.exp(tv - tv[0:1])
    w_ref[...] = ex / jnp.sum(ex, axis=0, keepdims=True)
    e_ref[...] = jnp.concatenate(idxs, axis=0).astype(jnp.int32)
    onehot = jnp.zeros((ne, tm), F32)
    for idx in idxs:
        onehot = onehot + jnp.where(iota == idx, 1.0, 0.0)
    earlier = (lax.broadcasted_iota(jnp.int32, (tm, tm), 0)
               < lax.broadcasted_iota(jnp.int32, (tm, tm), 1))
    rank = jnp.dot(onehot.astype(BF16), jnp.where(earlier, 1.0, 0.0).astype(BF16),
                   preferred_element_type=F32) + run_ref[...]
    pos_ref[...] = jnp.concatenate(
        [jnp.sum(jnp.where(iota == idx, rank, 0.0), axis=0, keepdims=True) for idx in idxs],
        axis=0).astype(jnp.int32)
    run_ref[...] = run_ref[...] + jnp.sum(onehot, axis=1, keepdims=True)
    cnt_ref[...] = run_ref[...]


def _router(x2, sh, sc, g, w_router, b_router, s):
    t, d = x2.shape
    tm = 256
    tpb = s // tm
    ne = w_router.shape[1]
    wt = w_router.T
    whi = wt.astype(BF16)
    wlo = (wt - whi.astype(F32)).astype(BF16)
    per_b = lambda i: (i // tpb, 0, 0)
    const = lambda i: (0, 0)
    return pl.pallas_call(
        _router_kernel,
        out_shape=[jax.ShapeDtypeStruct((t, d), F32),
                   jax.ShapeDtypeStruct((TOP_K, t), jnp.int32),
                   jax.ShapeDtypeStruct((TOP_K, t), F32),
                   jax.ShapeDtypeStruct((TOP_K, t), jnp.int32),
                   jax.ShapeDtypeStruct((ne, 1), F32)],
        grid=(t // tm,),
        in_specs=[pl.BlockSpec((tm, d), lambda i: (i, 0)),
                  pl.BlockSpec((None, 1, d), per_b),
                  pl.BlockSpec((None, 1, d), per_b),
                  pl.BlockSpec((1, d), const),
                  pl.BlockSpec((ne, d), const),
                  pl.BlockSpec((ne, d), const),
                  pl.BlockSpec((ne, 1), const)],
        out_specs=[pl.BlockSpec((tm, d), lambda i: (i, 0)),
                   pl.BlockSpec((TOP_K, tm), lambda i: (0, i)),
                   pl.BlockSpec((TOP_K, tm), lambda i: (0, i)),
                   pl.BlockSpec((TOP_K, tm), lambda i: (0, i)),
                   pl.BlockSpec((ne, 1), const)],
        scratch_shapes=[pltpu.VMEM((ne, 1), F32)],
        compiler_params=_cparams("arbitrary"),
        name="moe_router",
    )(x2, sh, sc, g, whi, wlo, b_router.reshape(ne, 1))


def _dispatch_kernel(dest_ref, h_ref, xs_in_hbm, xs_hbm, sem):
    del xs_in_hbm
    tm = h_ref.shape[0]

    def issue(r, carry):
        for kk in range(TOP_K):
            pltpu.make_async_copy(h_ref.at[pl.ds(r, 1), :],
                                  xs_hbm.at[pl.ds(dest_ref[0, 0, r * TOP_K + kk], 1), :], sem).start()
        return carry
    lax.fori_loop(0, tm, issue, 0)
    for kk in range(TOP_K):
        pltpu.make_async_copy(h_ref, xs_hbm.at[pl.ds(0, tm), :], sem).wait()


def _dispatch(dest_tk, h2, n_slots):
    t, d = h2.shape
    tm = 256
    xs0 = jnp.zeros((n_slots, d), F32)
    return pl.pallas_call(
        _dispatch_kernel,
        out_shape=jax.ShapeDtypeStruct((n_slots, d), F32),
        grid=(t // tm,),
        in_specs=[pl.BlockSpec((1, 1, TOP_K * tm), lambda i: (i, 0, 0), memory_space=pltpu.SMEM),
                  pl.BlockSpec((tm, d), lambda i: (i, 0)),
                  pl.BlockSpec(memory_space=pl.ANY)],
        out_specs=pl.BlockSpec(memory_space=pl.ANY),
        scratch_shapes=[pltpu.SemaphoreType.DMA],
        input_output_aliases={2: 0},
        compiler_params=_cparams("arbitrary"),
        name="moe_dispatch",
    )(dest_tk.reshape(t // tm, 1, TOP_K * tm), h2, xs0)


def _expert_kernel(blk_e_ref, n_used_ref, x_ref, wgu_ref, bgu_ref, wdn_ref, bdn_ref, o_ref):
    j = pl.program_id(0)
    d = x_ref.shape[1]

    @pl.when(j < n_used_ref[0])
    def _():
        xb = x_ref[...].astype(BF16)
        gu = jnp.dot(xb, wgu_ref[...], preferred_element_type=F32) + bgu_ref[...]
        x_glu = jnp.minimum(gu[:, :d], SWIGLU_LIMIT)
        x_lin = jnp.clip(gu[:, d:], -SWIGLU_LIMIT, SWIGLU_LIMIT)
        act = x_glu * jax.nn.sigmoid(SWIGLU_ALPHA * x_glu) * (x_lin + 1.0)
        o_ref[...] = jnp.dot(act.astype(BF16), wdn_ref[...], preferred_element_type=F32) + bdn_ref[...]

    @pl.when(j >= n_used_ref[0])
    def _():
        o_ref[...] = jnp.zeros_like(o_ref)


def _experts(xs, blk_e, n_used, wgu, bgu, wdn, bdn):
    d = xs.shape[1]
    n_blocks = blk_e.shape[0]
    ne = wgu.shape[0]
    grid_spec = pltpu.PrefetchScalarGridSpec(
        num_scalar_prefetch=2,
        grid=(n_blocks,),
        in_specs=[pl.BlockSpec((MOE_BLOCK, d), lambda j, be, nu: (jnp.minimum(j, nu[0] - 1), 0)),
                  pl.BlockSpec((None, d, 2 * d), lambda j, be, nu: (be[j], 0, 0)),
                  pl.BlockSpec((None, 1, 2 * d), lambda j, be, nu: (be[j], 0, 0)),
                  pl.BlockSpec((None, d, d), lambda j, be, nu: (be[j], 0, 0)),
                  pl.BlockSpec((None, 1, d), lambda j, be, nu: (be[j], 0, 0))],
        out_specs=pl.BlockSpec((MOE_BLOCK, d), lambda j, be, nu: (j, 0)),
    )
    return pl.pallas_call(
        _expert_kernel,
        out_shape=jax.ShapeDtypeStruct((n_blocks * MOE_BLOCK, d), F32),
        grid_spec=grid_spec,
        compiler_params=_cparams("arbitrary"),
        name="moe_experts",
    )(blk_e, n_used, xs, wgu, bgu.reshape(ne, 1, 2 * d), wdn, bdn.reshape(ne, 1, d))


def _combine_kernel(dest_ref, out_hbm, w_ref, x_ref, g2_ref, fg_ref, xo_ref, ybuf, sem, *, final):
    tm = x_ref.shape[0]

    def issue(r, carry):
        for kk in range(TOP_K):
            pltpu.make_async_copy(out_hbm.at[pl.ds(dest_ref[0, 0, r * TOP_K + kk], 1), :],
                                  ybuf.at[kk, pl.ds(r, 1), :], sem).start()
        return carry
    lax.fori_loop(0, tm, issue, 0)
    for kk in range(TOP_K):
        pltpu.make_async_copy(out_hbm.at[pl.ds(0, tm), :], ybuf.at[kk], sem).wait()
    w = w_ref[...]
    y = w[:, 0:1] * ybuf[0]
    for kk in range(1, TOP_K):
        y = y + w[:, kk:kk + 1] * ybuf[kk]
    xn = x_ref[...] + g2_ref[...] * y
    if final:
        ms = jnp.mean(xn * xn, axis=-1, keepdims=True)
        xn = xn * lax.rsqrt(ms + EPS) * fg_ref[...]
    xo_ref[...] = xn


def _combine(dest, out_sorted, gates_t, x2, g2, final_g, s, final):
    t, d = x2.shape
    tm = 128
    tpb = s // tm
    return pl.pallas_call(
        functools.partial(_combine_kernel, final=final),
        out_shape=jax.ShapeDtypeStruct((t, d), F32),
        grid=(t // tm,),
        in_specs=[pl.BlockSpec((1, 1, TOP_K * tm), lambda i: (i, 0, 0), memory_space=pltpu.SMEM),
                  pl.BlockSpec(memory_space=pl.ANY),
                  pl.BlockSpec((tm, TOP_K), lambda i: (i, 0)),
                  pl.BlockSpec((tm, d), lambda i: (i, 0)),
                  pl.BlockSpec((None, 1, d), lambda i: (i // tpb, 0, 0)),
                  pl.BlockSpec((1, d), lambda i: (0, 0))],
        out_specs=pl.BlockSpec((tm, d), lambda i: (i, 0)),
        scratch_shapes=[pltpu.VMEM((TOP_K, tm, d), F32), pltpu.SemaphoreType.DMA],
        compiler_params=_cparams("arbitrary"),
        name="moe_combine",
    )(dest.reshape(t // tm, 1, TOP_K * tm), out_sorted, gates_t, x2, g2, final_g)


def _moe_plan(top_e, pos, counts):
    t = top_e.shape[1]
    sizes = counts[:, 0].astype(jnp.int32)
    padded = (sizes + MOE_BLOCK - 1) // MOE_BLOCK * MOE_BLOCK
    pad_end = jnp.cumsum(padded)
    pad_start = pad_end - padded
    start_of = jnp.zeros_like(top_e)
    for e in range(N_EXPERTS):
        start_of = jnp.where(top_e == e, pad_start[e], start_of)
    dest = (start_of + pos).T.reshape(t * TOP_K)
    n_blocks = -(-t * TOP_K // MOE_BLOCK) + N_EXPERTS
    blk_start = jnp.arange(n_blocks, dtype=jnp.int32) * MOE_BLOCK
    blk_e = jnp.minimum(jnp.sum((pad_end[None, :] <= blk_start[:, None]).astype(jnp.int32), axis=1),
                        N_EXPERTS - 1).astype(jnp.int32)
    n_used = (pad_end[-1] // MOE_BLOCK).astype(jnp.int32).reshape(1)
    return dest, blk_e, n_used, n_blocks


def _rope_tables(s, gain):
    n_rows = s // GRID_W
    row = jnp.repeat(jnp.arange(n_rows), GRID_W).astype(F32)
    col = (jnp.arange(s) % GRID_W).astype(F32)
    half = HEAD_DIM // 2
    inv = ROPE_THETA ** (-jnp.arange(0, half, 2, dtype=F32) / half)
    ang_r = row[:, None] * inv
    ang_c = col[:, None] * inv
    cos = jnp.concatenate([jnp.cos(ang_r)] * 2 + [jnp.cos(ang_c)] * 2, axis=-1)
    sin = jnp.concatenate([-jnp.sin(ang_r), jnp.sin(ang_r), -jnp.sin(ang_c), jnp.sin(ang_c)], axis=-1)
    gain = gain.astype(F32)
    return cos * gain[None, :], sin * gain[_PARTNER][None, :]


_q = HEAD_DIM // 4
_PARTNER = np.concatenate([np.arange(_q, 2 * _q), np.arange(0, _q),
                           np.arange(3 * _q, 4 * _q), np.arange(2 * _q, 3 * _q)])


def _partner_cols(w, n_heads):
    idx = (np.arange(n_heads)[:, None] * HEAD_DIM + _PARTNER[None, :]).reshape(-1)
    return w[:, idx]


def kernel(x, c, w_ada, b_ada, norm1_g, w_in, q_norm_g, k_norm_g, rel_bias, w_br_a, w_br_b, w_out,
           norm2_g, w_router, b_router, w_gate_up, b_gate_up, w_down, b_down, final_norm_g):
    bsz, s, d = x.shape
    depth = w_ada.shape[0]
    t = bsz * s
    mod = _ada_mod(c, w_ada, b_ada)
    x2 = x.reshape(t, d)
    q_off = 3 * A_WIDTH
    k_off = q_off + B_Q
    for l in range(depth):
        sh1, sc1, g1, sh2, sc2, g2 = [mod[l, :, i * d:(i + 1) * d].reshape(bsz, 1, d)
                                      for i in range(N_MOD)]
        w = w_in[l]
        w_ext = jnp.concatenate([w, _partner_cols(w[:, q_off:q_off + B_Q], B_HEADS),
                                 _partner_cols(w[:, k_off:k_off + B_KV], B_KV_HEADS)],
                                axis=1).astype(BF16)
        tabs = _rope_tables(s, q_norm_g[l]) + _rope_tables(s, k_norm_g[l])
        za0, za1, za2, q, k, v, gate = _projection(x2, sh1, sc1, norm1_g[l].reshape(1, d), w_ext,
                                                   tabs, bsz, s)
        os_, ls_ = [], []
        for g, (za, (_, dilation)) in enumerate(zip((za0, za1, za2), A_GROUPS)):
            o, lse = _dilated_group(za, rel_bias[:, g * A_HEADS:(g + 1) * A_HEADS], dilation, bsz, s)
            os_.append(o)
            ls_.append(lse)
        ob = _gqa_attention(q, k, v, q_norm_g[l], k_norm_g[l]).reshape(t, B_Q)
        x2 = _merge(os_, ls_, ob, gate, x2, g1, w_br_a[l].astype(BF16), w_br_b[l].astype(BF16),
                    w_out[l].astype(BF16), s)
        h2, top_e, gates, pos, counts = _router(x2, sh2, sc2, norm2_g[l].reshape(1, d), w_router[l],
                                                b_router[l], s)
        dest, blk_e, n_used, n_blocks = _moe_plan(top_e, pos, counts)
        xs = _dispatch(dest, h2, n_blocks * MOE_BLOCK)
        out_sorted = _experts(xs, blk_e, n_used, w_gate_up[l].astype(BF16), b_gate_up[l],
                              w_down[l].astype(BF16), b_down[l])
        x2 = _combine(dest, out_sorted, gates.T, x2, g2, final_norm_g.reshape(1, d), s,
                      final=(l == depth - 1))
    return x2.reshape(bsz, s, d)
```

```python
import functools
import math

import numpy as np
import jax
import jax.numpy as jnp
from jax import lax
from jax.experimental import pallas as pl
from jax.experimental.pallas import tpu as pltpu
from jax.experimental.pallas import tpu_sc as plsc

F32 = jnp.float32
BF16 = jnp.bfloat16

HEAD_DIM = 64
A_GROUPS = ((128, 1), (512, 4), (2048, 16))
A_HEADS = 4
A_WIDTH = len(A_GROUPS) * A_HEADS * HEAD_DIM
A_OUT = A_HEADS * HEAD_DIM
A_GROUP_COLS = 3 * A_OUT
B_HEADS = 16
B_KV_HEADS = 4
B_GRP = B_HEADS // B_KV_HEADS
B_Q = B_HEADS * HEAD_DIM
B_KV = B_KV_HEADS * HEAD_DIM
GRID_W = 64
ROPE_THETA = 10000.0
REL_BUCKETS = 32
REL_MAX_DIST = 1024
N_EXPERTS = 32
TOP_K = 4
SWIGLU_LIMIT = 7.0
SWIGLU_ALPHA = 1.702
MOE_BLOCK = 512
N_MOD = 6
EPS = 1e-6
NEG_INF = -1e30
LOG2E = math.log2(math.e)
N_SIDE = 64

VMEM_LIMIT = 56 * 1024 * 1024


def _cparams(*sem):
    return pltpu.CompilerParams(dimension_semantics=sem, vmem_limit_bytes=VMEM_LIMIT)


def _ada_kernel(c_ref, w_ref, b_ref, o_ref):
    c = c_ref[...]
    ca = (c * jax.nn.sigmoid(c)).astype(BF16)
    o_ref[...] = jnp.dot(ca, w_ref[...].astype(BF16), preferred_element_type=F32) + b_ref[...]


def _ada_mod(c, w_ada, b_ada):
    depth, d, n = w_ada.shape
    bsz = c.shape[0]
    tn = 1536
    return pl.pallas_call(
        _ada_kernel,
        out_shape=jax.ShapeDtypeStruct((depth, bsz, n), F32),
        grid=(depth, n // tn),
        in_specs=[pl.BlockSpec((bsz, d), lambda l, j: (0, 0)),
                  pl.BlockSpec((None, d, tn), lambda l, j: (l, 0, j)),
                  pl.BlockSpec((None, 1, tn), lambda l, j: (l, 0, j))],
        out_specs=pl.BlockSpec((None, bsz, tn), lambda l, j: (l, 0, j)),
        compiler_params=_cparams("arbitrary", "arbitrary"),
        name="ada_mod",
    )(c, w_ada, b_ada.reshape(depth, 1, n))


def _proj_kernel(x_ref, sh_ref, sc_ref, g_ref, w_ref, aq_ref, bq_ref, ak_ref, bk_ref,
                 za0_ref, za1_ref, za2_ref, q_ref, k_ref, v_ref, gate_ref):
    x = x_ref[...]
    ms = jnp.mean(x * x, axis=-1, keepdims=True)
    h = x * lax.rsqrt(ms + EPS) * g_ref[...]
    h = h * (1.0 + sc_ref[...]) + sh_ref[...]
    hb = h.astype(BF16)

    def mm(lo, hi):
        return jnp.dot(hb, w_ref[:, lo:hi], preferred_element_type=F32)

    o = 0
    for za_ref in (za0_ref, za1_ref, za2_ref):
        za_ref[...] = mm(o, o + A_GROUP_COLS).astype(BF16)
        o += A_GROUP_COLS
    zq = mm(o, o + B_Q); o += B_Q
    zk = mm(o, o + B_KV); o += B_KV
    zv = mm(o, o + B_KV); o += B_KV
    zg = mm(o, o + gate_ref.shape[-1]); o += gate_ref.shape[-1]
    zqp = mm(o, o + B_Q); o += B_Q
    zkp = mm(o, o + B_KV); o += B_KV
    gate_ref[...] = jax.nn.sigmoid(zg).astype(BF16)
    aq, bq, ak, bk = aq_ref[...], bq_ref[...], ak_ref[...], bk_ref[...]
    for hh in range(B_HEADS):
        z = zq[:, hh * HEAD_DIM:(hh + 1) * HEAD_DIM]
        zp = zqp[:, hh * HEAD_DIM:(hh + 1) * HEAD_DIM]
        r = lax.rsqrt(jnp.mean(z * z, axis=-1, keepdims=True) + EPS)
        q_ref[hh] = (r * (z * aq + zp * bq) * (LOG2E * HEAD_DIM ** -0.5)).astype(BF16)
    for hh in range(B_KV_HEADS):
        z = zk[:, hh * HEAD_DIM:(hh + 1) * HEAD_DIM]
        zp = zkp[:, hh * HEAD_DIM:(hh + 1) * HEAD_DIM]
        r = lax.rsqrt(jnp.mean(z * z, axis=-1, keepdims=True) + EPS)
        k_ref[hh] = (r * (z * ak + zp * bk)).astype(BF16)
        v_ref[hh, :, :HEAD_DIM] = zv[:, hh * HEAD_DIM:(hh + 1) * HEAD_DIM].astype(BF16)
        lane = lax.broadcasted_iota(jnp.int32, (zv.shape[0], V_PAD - HEAD_DIM), 1)
        v_ref[hh, :, HEAD_DIM:] = jnp.where(lane == 0, 1.0, 0.0).astype(BF16)


def _projection(x2, sh, sc, g, w_ext, tabs, bsz, s):
    t, d = x2.shape
    tm = 256
    tpb = s // tm
    n_ext = w_ext.shape[1]
    n_gate = 2 * d
    row = lambda i: (i, 0)
    per_b = lambda i: (i // tpb, 0, 0)
    tab = lambda i: (i % tpb, 0)
    hm = lambda i: (i // tpb, 0, i % tpb, 0)
    out_shape = (
        [jax.ShapeDtypeStruct((t, A_GROUP_COLS), BF16)] * 3
        + [jax.ShapeDtypeStruct((bsz, B_HEADS, s, HEAD_DIM), BF16),
           jax.ShapeDtypeStruct((bsz, B_KV_HEADS, s, HEAD_DIM), BF16),
           jax.ShapeDtypeStruct((bsz, B_KV_HEADS, s, V_PAD), BF16),
           jax.ShapeDtypeStruct((t, n_gate), BF16)])
    out_specs = (
        [pl.BlockSpec((tm, A_GROUP_COLS), row)] * 3
        + [pl.BlockSpec((None, B_HEADS, tm, HEAD_DIM), hm),
           pl.BlockSpec((None, B_KV_HEADS, tm, HEAD_DIM), hm),
           pl.BlockSpec((None, B_KV_HEADS, tm, V_PAD), hm),
           pl.BlockSpec((tm, n_gate), row)])
    return pl.pallas_call(
        _proj_kernel,
        out_shape=out_shape,
        grid=(t // tm,),
        in_specs=[pl.BlockSpec((tm, d), row),
                  pl.BlockSpec((None, 1, d), per_b),
                  pl.BlockSpec((None, 1, d), per_b),
                  pl.BlockSpec((1, d), lambda i: (0, 0)),
                  pl.BlockSpec((d, n_ext), lambda i: (0, 0))]
                 + [pl.BlockSpec((tm, HEAD_DIM), tab)] * 4,
        out_specs=out_specs,
        compiler_params=_cparams("arbitrary"),
        name="in_proj",
    )(x2, sh, sc, g, w_ext, *tabs)


def _dilated_kernel(q_ref, kp_ref, kc_ref, kn_ref, vp_ref, vc_ref, vn_ref, bias_ref,
                    o_ref, lse_ref, *, tq, seq_len):
    i = pl.program_id(2)
    q = q_ref[...]
    k = jnp.concatenate([kp_ref[...], kc_ref[...], kn_ref[...]], axis=0)
    v = jnp.concatenate([vp_ref[...], vc_ref[...], vn_ref[...]], axis=0)
    nk = k.shape[0]
    kpos = i * tq - N_SIDE + lax.broadcasted_iota(jnp.int32, (tq, nk), 1)
    valid = (kpos >= 0) & (kpos < seq_len)
    for hh in range(A_HEADS):
        sl = slice(hh * HEAD_DIM, (hh + 1) * HEAD_DIM)
        sc = lax.dot_general(q[:, sl], k[:, sl], (((1,), (1,)), ((), ())),
                             preferred_element_type=F32)
        sc = sc * (HEAD_DIM ** -0.5) + bias_ref[hh]
        sc = jnp.where(valid, sc, NEG_INF)
        m = jnp.max(sc, axis=-1, keepdims=True)
        p = jnp.exp(sc - m)
        den = jnp.sum(p, axis=-1, keepdims=True)
        o = jnp.dot(p.astype(BF16), v[:, sl], preferred_element_type=F32) / den
        o_ref[:, sl] = o
        lse_ref[:, sl] = jnp.broadcast_to(m + jnp.log(den), (tq, HEAD_DIM))


def _t5_bucket(rel):
    nb = REL_BUCKETS // 2
    max_exact = nb // 2
    ret = jnp.where(rel > 0, nb, 0)
    n = jnp.abs(rel)
    nf = jnp.maximum(n, 1).astype(F32)
    large = max_exact + (jnp.log(nf / max_exact) / math.log(REL_MAX_DIST / max_exact)
                         * (nb - max_exact)).astype(jnp.int32)
    large = jnp.minimum(large, nb - 1)
    return ret + jnp.where(n < max_exact, n, large)


def _band_bias(rel_bias_g, dilation, tq):
    nk = tq + 2 * N_SIDE
    rel = jnp.arange(nk)[None, :] - N_SIDE - jnp.arange(tq)[:, None]
    bucket = _t5_bucket(rel * dilation)[None]
    bias = jnp.full((A_HEADS,) + rel.shape, NEG_INF, F32)
    for b in range(REL_BUCKETS):
        bias = jnp.where(bucket == b, rel_bias_g[b].astype(F32)[:, None, None], bias)
    return jnp.where((jnp.abs(rel) <= N_SIDE)[None], bias, NEG_INF)


def _dilated_group(za, rel_bias_g, dilation, bsz, s):
    ll = s // dilation
    tq = 128
    assert ll % tq == 0 and tq == 2 * N_SIDE
    zv = za.reshape(bsz, ll, dilation * A_GROUP_COLS)
    nhalf = ll // N_SIDE
    cur = lambda c: (lambda b, r, i: (b, i, 3 * r + c))
    prv = lambda c: (lambda b, r, i: (b, jnp.maximum(2 * i - 1, 0), 3 * r + c))
    nxt = lambda c: (lambda b, r, i: (b, jnp.minimum(2 * i + 2, nhalf - 1), 3 * r + c))
    full = pl.BlockSpec((None, tq, A_OUT), cur(0))
    half = lambda f: pl.BlockSpec((None, N_SIDE, A_OUT), f)
    bias = _band_bias(rel_bias_g, dilation, tq)
    out_sd = jax.ShapeDtypeStruct((bsz, ll, dilation * A_OUT), F32)
    out_spec = pl.BlockSpec((None, tq, A_OUT), lambda b, r, i: (b, i, r))
    o, lse = pl.pallas_call(
        functools.partial(_dilated_kernel, tq=tq, seq_len=ll),
        out_shape=[out_sd, out_sd],
        grid=(bsz, dilation, ll // tq),
        in_specs=[full,
                  half(prv(1)), pl.BlockSpec((None, tq, A_OUT), cur(1)), half(nxt(1)),
                  half(prv(2)), pl.BlockSpec((None, tq, A_OUT), cur(2)), half(nxt(2)),
                  pl.BlockSpec(bias.shape, lambda b, r, i: (0, 0, 0))],
        out_specs=[out_spec, out_spec],
        compiler_params=_cparams("arbitrary", "arbitrary", "arbitrary"),
        name=f"dilated_attn_d{dilation}",
    )(zv, zv, zv, zv, zv, zv, zv, bias)
    return o.reshape(bsz * s, A_OUT), lse.reshape(bsz * s, A_OUT)


GQA_NO_SHIFT_MAX_LOG2 = 80.0
GQA_KEY_CHUNK = 512
V_PAD = 128


def _gqa_kernel_noshift(q_ref, k_ref, v_ref, o_ref):
    grp, tq, dh = q_ref.shape
    q = q_ref[...].reshape(grp * tq, dh)
    nt = (((1,), (1,)), ((), ()))
    acc = jnp.zeros((grp * tq, V_PAD), F32)
    for c in range(k_ref.shape[0] // GQA_KEY_CHUNK):
        rows = slice(c * GQA_KEY_CHUNK, (c + 1) * GQA_KEY_CHUNK)
        p = jnp.exp2(lax.dot_general(q, k_ref[rows, :], nt, preferred_element_type=F32))
        acc = acc + jnp.dot(p.astype(BF16), v_ref[rows, :], preferred_element_type=F32)
    o = acc[:, :dh] / acc[:, dh:dh + 1]
    for hh in range(grp):
        o_ref[:, hh * dh:(hh + 1) * dh] = o[hh * tq:(hh + 1) * tq].astype(BF16)


def _gqa_kernel_rowmax(q_ref, k_ref, v_ref, o_ref):
    k = k_ref[...]
    v = v_ref[:, :HEAD_DIM]
    for hh in range(q_ref.shape[0]):
        sc = lax.dot_general(q_ref[hh], k, (((1,), (1,)), ((), ())), preferred_element_type=F32)
        m = jnp.max(sc, axis=-1, keepdims=True)
        p = jnp.exp2(sc - m)
        den = jnp.sum(p, axis=-1, keepdims=True)
        o = jnp.dot(p.astype(BF16), v, preferred_element_type=F32) / den
        o_ref[:, hh * HEAD_DIM:(hh + 1) * HEAD_DIM] = o.astype(BF16)


def _gqa_call(body, name, tq, q, k, v):
    bsz, _, s, _ = q.shape
    return pl.pallas_call(
        body,
        out_shape=jax.ShapeDtypeStruct((bsz, s, B_Q), BF16),
        grid=(bsz, B_KV_HEADS, s // tq),
        in_specs=[pl.BlockSpec((None, B_GRP, tq, HEAD_DIM), lambda b, h, i: (b, h, i, 0)),
                  pl.BlockSpec((None, None, s, HEAD_DIM), lambda b, h, i: (b, h, 0, 0)),
                  pl.BlockSpec((None, None, s, V_PAD), lambda b, h, i: (b, h, 0, 0))],
        out_specs=pl.BlockSpec((None, tq, B_GRP * HEAD_DIM), lambda b, h, i: (b, i, h)),
        compiler_params=_cparams("arbitrary", "arbitrary", "arbitrary"),
        name=name,
    )(q, k, v)


def _gqa_attention(q, k, v, q_gain, k_gain):
    bound = (HEAD_DIM ** 0.5) * LOG2E * 1.01 * jnp.max(jnp.abs(q_gain)) * jnp.max(jnp.abs(k_gain))
    return lax.cond(bound <= GQA_NO_SHIFT_MAX_LOG2,
                    functools.partial(_gqa_call, _gqa_kernel_noshift, "gqa_attn", 512),
                    functools.partial(_gqa_call, _gqa_kernel_rowmax, "gqa_attn_rowmax", 256),
                    q, k, v)


def _merge_kernel(o0_ref, o1_ref, o2_ref, l0_ref, l1_ref, l2_ref, ob_ref, gate_ref, x_ref, g1_ref,
                  wa_ref, wb_ref, wo_ref, xo_ref):
    d = x_ref.shape[-1]
    l0, l1, l2 = l0_ref[...], l1_ref[...], l2_ref[...]
    m = jnp.maximum(jnp.maximum(l0, l1), l2)
    e0, e1, e2 = jnp.exp(l0 - m), jnp.exp(l1 - m), jnp.exp(l2 - m)
    tot = e0 + e1 + e2
    oa = (e0 / tot) * o0_ref[...] + (e1 / tot) * o1_ref[...] + (e2 / tot) * o2_ref[...]
    ya = jnp.dot(oa.astype(BF16), wa_ref[...], preferred_element_type=F32)
    yb = jnp.dot(ob_ref[...], wb_ref[...], preferred_element_type=F32)
    merged = gate_ref[:, :d].astype(F32) * ya + gate_ref[:, d:].astype(F32) * yb
    y = jnp.dot(merged.astype(BF16), wo_ref[...], preferred_element_type=F32)
    xo_ref[...] = x_ref[...] + g1_ref[...] * y


def _merge(os_, ls_, ob, gate, x2, g1, wa, wb, wo, s):
    t, d = x2.shape
    tm = 256
    tpb = s // tm
    row = lambda i: (i, 0)
    const = lambda i: (0, 0)
    a_spec = pl.BlockSpec((tm, A_OUT), row)
    return pl.pallas_call(
        _merge_kernel,
        out_shape=jax.ShapeDtypeStruct((t, d), F32),
        grid=(t // tm,),
        in_specs=[a_spec] * 6
                 + [pl.BlockSpec((tm, B_Q), row),
                    pl.BlockSpec((tm, 2 * d), row),
                    pl.BlockSpec((tm, d), row),
                    pl.BlockSpec((None, 1, d), lambda i: (i // tpb, 0, 0)),
                    pl.BlockSpec(wa.shape, const),
                    pl.BlockSpec(wb.shape, const),
                    pl.BlockSpec(wo.shape, const)],
        out_specs=pl.BlockSpec((tm, d), row),
        compiler_params=_cparams("arbitrary"),
        name="branch_merge",
    )(*os_, *ls_, ob, gate, x2, g1, wa, wb, wo)


def _router_kernel(x_ref, sh_ref, sc_ref, g_ref, whi_ref, wlo_ref, b_ref,
                   h_ref, e_ref, w_ref, pos_ref, cnt_ref, run_ref):
    @pl.when(pl.program_id(0) == 0)
    def _():
        run_ref[...] = jnp.zeros_like(run_ref)

    x = x_ref[...]
    ms = jnp.mean(x * x, axis=-1, keepdims=True)
    h = x * lax.rsqrt(ms + EPS) * g_ref[...]
    h = h * (1.0 + sc_ref[...]) + sh_ref[...]
    h_ref[...] = h
    hhi = h.astype(BF16)
    hlo = (h - hhi.astype(F32)).astype(BF16)
    nt = (((1,), (1,)), ((), ()))
    dotf = lambda a, b: lax.dot_general(a, b, nt, preferred_element_type=F32)
    logits = dotf(whi_ref[...], hhi) + (dotf(whi_ref[...], hlo) + dotf(wlo_ref[...], hhi)) + b_ref[...]
    ne, tm = logits.shape
    iota = lax.broadcasted_iota(jnp.int32, (ne, tm), 0).astype(F32)
    vals, idxs = [], []
    cur = logits
    for _ in range(TOP_K):
        m = jnp.max(cur, axis=0, keepdims=True)
        idx = jnp.min(jnp.where(cur == m, iota, float(ne)), axis=0, keepdims=True)
        vals.append(m)
        idxs.append(idx)
        cur = jnp.where(iota == idx, -jnp.inf, cur)
    tv = jnp.concatenate(vals, axis=0)
    ex = jnp.exp(tv - tv[0:1])
    w_ref[...] = ex / jnp.sum(ex, axis=0, keepdims=True)
    e_ref[...] = jnp.concatenate(idxs, axis=0).astype(jnp.int32)
    onehot = jnp.zeros((ne, tm), F32)
    for idx in idxs:
        onehot = onehot + jnp.where(iota == idx, 1.0, 0.0)
    earlier = (lax.broadcasted_iota(jnp.int32, (tm, tm), 0)
               < lax.broadcasted_iota(jnp.int32, (tm, tm), 1))
    rank = jnp.dot(onehot.astype(BF16), jnp.where(earlier, 1.0, 0.0).astype(BF16),
                   preferred_element_type=F32) + run_ref[...]
    pos_ref[...] = jnp.concatenate(
        [jnp.sum(jnp.where(iota == idx, rank, 0.0), axis=0, keepdims=True) for idx in idxs],
        axis=0).astype(jnp.int32)
    run_ref[...] = run_ref[...] + jnp.sum(onehot, axis=1, keepdims=True)
    cnt_ref[...] = run_ref[...]


def _router(x2, sh, sc, g, w_router, b_router, s):
    t, d = x2.shape
    tm = 256
    tpb = s // tm
    ne = w_router.shape[1]
    wt = w_router.T
    whi = wt.astype(BF16)
    wlo = (wt - whi.astype(F32)).astype(BF16)
    per_b = lambda i: (i // tpb, 0, 0)
    const = lambda i: (0, 0)
    return pl.pallas_call(
        _router_kernel,
        out_shape=[jax.ShapeDtypeStruct((t, d), F32),
                   jax.ShapeDtypeStruct((TOP_K, t), jnp.int32),
                   jax.ShapeDtypeStruct((TOP_K, t), F32),
                   jax.ShapeDtypeStruct((TOP_K, t), jnp.int32),
                   jax.ShapeDtypeStruct((ne, 1), F32)],
        grid=(t // tm,),
        in_specs=[pl.BlockSpec((tm, d), lambda i: (i, 0)),
                  pl.BlockSpec((None, 1, d), per_b),
                  pl.BlockSpec((None, 1, d), per_b),
                  pl.BlockSpec((1, d), const),
                  pl.BlockSpec((ne, d), const),
                  pl.BlockSpec((ne, d), const),
                  pl.BlockSpec((ne, 1), const)],
        out_specs=[pl.BlockSpec((tm, d), lambda i: (i, 0)),
                   pl.BlockSpec((TOP_K, tm), lambda i: (0, i)),
                   pl.BlockSpec((TOP_K, tm), lambda i: (0, i)),
                   pl.BlockSpec((TOP_K, tm), lambda i: (0, i)),
                   pl.BlockSpec((ne, 1), const)],
        scratch_shapes=[pltpu.VMEM((ne, 1), F32)],
        compiler_params=_cparams("arbitrary"),
        name="moe_router",
    )(x2, sh, sc, g, whi, wlo, b_router.reshape(ne, 1))


def _dispatch_kernel(dest_ref, h_ref, xs_in_hbm, xs_hbm, sem):
    del xs_in_hbm
    tm = h_ref.shape[0]

    def issue(r, carry):
        for kk in range(TOP_K):
            pltpu.make_async_copy(h_ref.at[pl.ds(r, 1), :],
                                  xs_hbm.at[pl.ds(dest_ref[0, 0, r * TOP_K + kk], 1), :], sem).start()
        return carry
    lax.fori_loop(0, tm, issue, 0)
    for kk in range(TOP_K):
        pltpu.make_async_copy(h_ref, xs_hbm.at[pl.ds(0, tm), :], sem).wait()


def _dispatch(dest_tk, h2, n_slots):
    t, d = h2.shape
    tm = 256
    xs0 = jnp.zeros((n_slots, d), F32)
    return pl.pallas_call(
        _dispatch_kernel,
        out_shape=jax.ShapeDtypeStruct((n_slots, d), F32),
        grid=(t // tm,),
        in_specs=[pl.BlockSpec((1, 1, TOP_K * tm), lambda i: (i, 0, 0), memory_space=pltpu.SMEM),
                  pl.BlockSpec((tm, d), lambda i: (i, 0)),
                  pl.BlockSpec(memory_space=pl.ANY)],
        out_specs=pl.BlockSpec(memory_space=pl.ANY),
        scratch_shapes=[pltpu.SemaphoreType.DMA],
        input_output_aliases={2: 0},
        compiler_params=_cparams("arbitrary"),
        name="moe_dispatch",
    )(dest_tk.reshape(t // tm, 1, TOP_K * tm), h2, xs0)


def _expert_kernel(blk_e_ref, n_used_ref, x_ref, wgu_ref, bgu_ref, wdn_ref, bdn_ref, o_ref):
    j = pl.program_id(0)
    d = x_ref.shape[1]

    @pl.when(j < n_used_ref[0])
    def _():
        xb = x_ref[...].astype(BF16)
        gu = jnp.dot(xb, wgu_ref[...], preferred_element_type=F32) + bgu_ref[...]
        x_glu = jnp.minimum(gu[:, :d], SWIGLU_LIMIT)
        x_lin = jnp.clip(gu[:, d:], -SWIGLU_LIMIT, SWIGLU_LIMIT)
        act = x_glu * jax.nn.sigmoid(SWIGLU_ALPHA * x_glu) * (x_lin + 1.0)
        o_ref[...] = jnp.dot(act.astype(BF16), wdn_ref[...], preferred_element_type=F32) + bdn_ref[...]

    @pl.when(j >= n_used_ref[0])
    def _():
        o_ref[...] = jnp.zeros_like(o_ref)


def _experts(xs, blk_e, n_used, wgu, bgu, wdn, bdn):
    d = xs.shape[1]
    n_blocks = blk_e.shape[0]
    ne = wgu.shape[0]
    grid_spec = pltpu.PrefetchScalarGridSpec(
        num_scalar_prefetch=2,
        grid=(n_blocks,),
        in_specs=[pl.BlockSpec((MOE_BLOCK, d), lambda j, be, nu: (jnp.minimum(j, nu[0] - 1), 0)),
                  pl.BlockSpec((None, d, 2 * d), lambda j, be, nu: (be[j], 0, 0)),
                  pl.BlockSpec((None, 1, 2 * d), lambda j, be, nu: (be[j], 0, 0)),
                  pl.BlockSpec((None, d, d), lambda j, be, nu: (be[j], 0, 0)),
                  pl.BlockSpec((None, 1, d), lambda j, be, nu: (be[j], 0, 0))],
        out_specs=pl.BlockSpec((MOE_BLOCK, d), lambda j, be, nu: (j, 0)),
    )
    return pl.pallas_call(
        _expert_kernel,
        out_shape=jax.ShapeDtypeStruct((n_blocks * MOE_BLOCK, d), F32),
        grid_spec=grid_spec,
        compiler_params=_cparams("arbitrary"),
        name="moe_experts",
    )(blk_e, n_used, xs, wgu, bgu.reshape(ne, 1, 2 * d), wdn, bdn.reshape(ne, 1, d))


SC_CORES = 2
SC_SUBCORES = 16
SC_GATHER_ROWS = 32


def _sc_gather_rows(table, idx):
    n_out = idx.shape[0]
    d = table.shape[1]
    n_workers = SC_CORES * SC_SUBCORES
    per_w = n_out // n_workers
    n_chunks = per_w // SC_GATHER_ROWS
    assert per_w * n_workers == n_out and n_chunks * SC_GATHER_ROWS == per_w and n_chunks % 2 == 0
    mesh = plsc.VectorSubcoreMesh(core_axis_name="c", subcore_axis_name="s")

    @functools.partial(
        pl.kernel, mesh=mesh,
        out_type=jax.ShapeDtypeStruct((n_out, d), table.dtype),
        scratch_types=[pltpu.VMEM((per_w,), jnp.int32),
                       pltpu.VMEM((SC_GATHER_ROWS, d), table.dtype),
                       pltpu.VMEM((SC_GATHER_ROWS, d), table.dtype),
                       pltpu.SemaphoreType.DMA, pltpu.SemaphoreType.DMA],
        name="sc_gather_rows",
    )
    def gather_kernel(table_hbm, idx_hbm, out_hbm, idx_v, rows_a, rows_b, sem_a, sem_b):
        base = (lax.axis_index("s") * SC_CORES + lax.axis_index("c")) * per_w
        pltpu.sync_copy(idx_hbm.at[pl.ds(base, per_w)], idx_v)

        def gather(j, buf, sem):
            rows = idx_v.at[pl.ds(j * SC_GATHER_ROWS, SC_GATHER_ROWS)]
            return pltpu.make_async_copy(table_hbm.at[rows], buf, sem)

        def write_back(j, buf):
            pltpu.sync_copy(buf, out_hbm.at[pl.ds(base + j * SC_GATHER_ROWS, SC_GATHER_ROWS)])

        gather(0, rows_a, sem_a).start()

        @pl.loop(0, n_chunks, step=2)
        def _(j):
            gather(j + 1, rows_b, sem_b).start()
            gather(j, rows_a, sem_a).wait()
            write_back(j, rows_a)

            @pl.when(j + 2 < n_chunks)
            def _():
                gather(j + 2, rows_a, sem_a).start()
            gather(j + 1, rows_b, sem_b).wait()
            write_back(j + 1, rows_b)

    return gather_kernel(table, idx)


def _combine_kernel(y0_ref, y1_ref, y2_ref, y3_ref, w_ref, x_ref, g2_ref, fg_ref, xo_ref, *, final):
    w = w_ref[...]
    y = w[:, 0:1] * y0_ref[...]
    for kk, y_ref in enumerate((y1_ref, y2_ref, y3_ref), start=1):
        y = y + w[:, kk:kk + 1] * y_ref[...]
    xn = x_ref[...] + g2_ref[...] * y
    if final:
        ms = jnp.mean(xn * xn, axis=-1, keepdims=True)
        xn = xn * lax.rsqrt(ms + EPS) * fg_ref[...]
    xo_ref[...] = xn


def _combine(dest_kt, out_sorted, gates_t, x2, g2, final_g, s, final):
    t, d = x2.shape
    tm = 256
    tpb = s // tm
    nt = t // tm
    yg = _sc_gather_rows(out_sorted, dest_kt)
    plane = lambda kk: pl.BlockSpec((tm, d), lambda i: (kk * nt + i, 0))
    return pl.pallas_call(
        functools.partial(_combine_kernel, final=final),
        out_shape=jax.ShapeDtypeStruct((t, d), F32),
        grid=(nt,),
        in_specs=[plane(0), plane(1), plane(2), plane(3),
                  pl.BlockSpec((tm, TOP_K), lambda i: (i, 0)),
                  pl.BlockSpec((tm, d), lambda i: (i, 0)),
                  pl.BlockSpec((None, 1, d), lambda i: (i // tpb, 0, 0)),
                  pl.BlockSpec((1, d), lambda i: (0, 0))],
        out_specs=pl.BlockSpec((tm, d), lambda i: (i, 0)),
        compiler_params=_cparams("arbitrary"),
        name="moe_combine",
    )(yg, yg, yg, yg, gates_t, x2, g2, final_g)


def _moe_plan(top_e, pos, counts):
    t = top_e.shape[1]
    sizes = counts[:, 0].astype(jnp.int32)
    padded = (sizes + MOE_BLOCK - 1) // MOE_BLOCK * MOE_BLOCK
    pad_end = jnp.cumsum(padded)
    pad_start = pad_end - padded
    start_of = jnp.zeros_like(top_e)
    for e in range(N_EXPERTS):
        start_of = jnp.where(top_e == e, pad_start[e], start_of)
    dest_kt = (start_of + pos).reshape(TOP_K * t)
    dest_tk = (start_of + pos).T.reshape(t * TOP_K)
    n_blocks = -(-t * TOP_K // MOE_BLOCK) + N_EXPERTS
    blk_start = jnp.arange(n_blocks, dtype=jnp.int32) * MOE_BLOCK
    blk_e = jnp.minimum(jnp.sum((pad_end[None, :] <= blk_start[:, None]).astype(jnp.int32), axis=1),
                        N_EXPERTS - 1).astype(jnp.int32)
    n_used = (pad_end[-1] // MOE_BLOCK).astype(jnp.int32).reshape(1)
    return dest_tk, dest_kt, blk_e, n_used, n_blocks


def _rope_tables(s, gain):
    n_rows = s // GRID_W
    row = jnp.repeat(jnp.arange(n_rows), GRID_W).astype(F32)
    col = (jnp.arange(s) % GRID_W).astype(F32)
    half = HEAD_DIM // 2
    inv = ROPE_THETA ** (-jnp.arange(0, half, 2, dtype=F32) / half)
    ang_r = row[:, None] * inv
    ang_c = col[:, None] * inv
    cos = jnp.concatenate([jnp.cos(ang_r)] * 2 + [jnp.cos(ang_c)] * 2, axis=-1)
    sin = jnp.concatenate([-jnp.sin(ang_r), jnp.sin(ang_r), -jnp.sin(ang_c), jnp.sin(ang_c)], axis=-1)
    gain = gain.astype(F32)
    return cos * gain[None, :], sin * gain[_PARTNER][None, :]


_q = HEAD_DIM // 4
_PARTNER = np.concatenate([np.arange(_q, 2 * _q), np.arange(0, _q),
                           np.arange(3 * _q, 4 * _q), np.arange(2 * _q, 3 * _q)])


def _partner_cols(w, n_heads):
    idx = (np.arange(n_heads)[:, None] * HEAD_DIM + _PARTNER[None, :]).reshape(-1)
    return w[:, idx]


def kernel(x, c, w_ada, b_ada, norm1_g, w_in, q_norm_g, k_norm_g, rel_bias, w_br_a, w_br_b, w_out,
           norm2_g, w_router, b_router, w_gate_up, b_gate_up, w_down, b_down, final_norm_g):
    bsz, s, d = x.shape
    depth = w_ada.shape[0]
    t = bsz * s
    mod = _ada_mod(c, w_ada, b_ada)
    x2 = x.reshape(t, d)
    q_off = 3 * A_WIDTH
    k_off = q_off + B_Q
    for l in range(depth):
        sh1, sc1, g1, sh2, sc2, g2 = [mod[l, :, i * d:(i + 1) * d].reshape(bsz, 1, d)
                                      for i in range(N_MOD)]
        w = w_in[l]
        w_ext = jnp.concatenate([w, _partner_cols(w[:, q_off:q_off + B_Q], B_HEADS),
                                 _partner_cols(w[:, k_off:k_off + B_KV], B_KV_HEADS)],
                                axis=1).astype(BF16)
        tabs = _rope_tables(s, q_norm_g[l]) + _rope_tables(s, k_norm_g[l])
        za0, za1, za2, q, k, v, gate = _projection(x2, sh1, sc1, norm1_g[l].reshape(1, d), w_ext,
                                                   tabs, bsz, s)
        os_, ls_ = [], []
        for g, (za, (_, dilation)) in enumerate(zip((za0, za1, za2), A_GROUPS)):
            o, lse = _dilated_group(za, rel_bias[:, g * A_HEADS:(g + 1) * A_HEADS], dilation, bsz, s)
            os_.append(o)
            ls_.append(lse)
        ob = _gqa_attention(q, k, v, q_norm_g[l], k_norm_g[l]).reshape(t, B_Q)
        x2 = _merge(os_, ls_, ob, gate, x2, g1, w_br_a[l].astype(BF16), w_br_b[l].astype(BF16),
                    w_out[l].astype(BF16), s)
        h2, top_e, gates, pos, counts = _router(x2, sh2, sc2, norm2_g[l].reshape(1, d), w_router[l],
                                                b_router[l], s)
        dest_tk, dest_kt, blk_e, n_used, n_blocks = _moe_plan(top_e, pos, counts)
        xs = _dispatch(dest_tk, h2, n_blocks * MOE_BLOCK)
        out_sorted = _experts(xs, blk_e, n_used, w_gate_up[l].astype(BF16), b_gate_up[l],
                              w_down[l].astype(BF16), b_down[l])
        x2 = _combine(dest_kt, out_sorted, gates.T, x2, g2, final_norm_g.reshape(1, d), s,
                      final=(l == depth - 1))
    return x2.reshape(bsz, s, d)
```

```python
import functools
import math

import numpy as np
import jax
import jax.numpy as jnp
from jax import lax
from jax.experimental import pallas as pl
from jax.experimental.pallas import tpu as pltpu
from jax.experimental.pallas import tpu_sc as plsc

F32 = jnp.float32
BF16 = jnp.bfloat16

HEAD_DIM = 64
A_GROUPS = ((128, 1), (512, 4), (2048, 16))
A_HEADS = 4
A_WIDTH = len(A_GROUPS) * A_HEADS * HEAD_DIM
A_OUT = A_HEADS * HEAD_DIM
A_GROUP_COLS = 3 * A_OUT
B_HEADS = 16
B_KV_HEADS = 4
B_GRP = B_HEADS // B_KV_HEADS
B_Q = B_HEADS * HEAD_DIM
B_KV = B_KV_HEADS * HEAD_DIM
GRID_W = 64
ROPE_THETA = 10000.0
REL_BUCKETS = 32
REL_MAX_DIST = 1024
N_EXPERTS = 32
TOP_K = 4
SWIGLU_LIMIT = 7.0
SWIGLU_ALPHA = 1.702
MOE_BLOCK = 512
N_MOD = 6
EPS = 1e-6
NEG_INF = -1e30
LOG2E = math.log2(math.e)
N_SIDE = 64

VMEM_LIMIT = 56 * 1024 * 1024


def _cparams(*sem):
    return pltpu.CompilerParams(dimension_semantics=sem, vmem_limit_bytes=VMEM_LIMIT)


def _ada_kernel(c_ref, w_ref, b_ref, o_ref):
    c = c_ref[...]
    ca = (c * jax.nn.sigmoid(c)).astype(BF16)
    o_ref[...] = jnp.dot(ca, w_ref[...].astype(BF16), preferred_element_type=F32) + b_ref[...]


def _ada_mod(c, w_ada, b_ada):
    depth, d, n = w_ada.shape
    bsz = c.shape[0]
    tn = 1536
    return pl.pallas_call(
        _ada_kernel,
        out_shape=jax.ShapeDtypeStruct((depth, bsz, n), F32),
        grid=(depth, n // tn),
        in_specs=[pl.BlockSpec((bsz, d), lambda l, j: (0, 0)),
                  pl.BlockSpec((None, d, tn), lambda l, j: (l, 0, j)),
                  pl.BlockSpec((None, 1, tn), lambda l, j: (l, 0, j))],
        out_specs=pl.BlockSpec((None, bsz, tn), lambda l, j: (l, 0, j)),
        compiler_params=_cparams("arbitrary", "arbitrary"),
        name="ada_mod",
    )(c, w_ada, b_ada.reshape(depth, 1, n))


def _proj_kernel(x_ref, sh_ref, sc_ref, g_ref, w_ref, aq_ref, bq_ref, ak_ref, bk_ref,
                 za0_ref, za1_ref, za2_ref, q_ref, k_ref, v_ref, gate_ref):
    x = x_ref[...]
    ms = jnp.mean(x * x, axis=-1, keepdims=True)
    h = x * lax.rsqrt(ms + EPS) * g_ref[...]
    h = h * (1.0 + sc_ref[...]) + sh_ref[...]
    hb = h.astype(BF16)

    def mm(lo, hi):
        return jnp.dot(hb, w_ref[:, lo:hi], preferred_element_type=F32)

    o = 0
    for za_ref in (za0_ref, za1_ref, za2_ref):
        za_ref[...] = mm(o, o + A_GROUP_COLS).astype(BF16)
        o += A_GROUP_COLS
    zq = mm(o, o + B_Q); o += B_Q
    zk = mm(o, o + B_KV); o += B_KV
    zv = mm(o, o + B_KV); o += B_KV
    zg = mm(o, o + gate_ref.shape[-1]); o += gate_ref.shape[-1]
    zqp = mm(o, o + B_Q); o += B_Q
    zkp = mm(o, o + B_KV); o += B_KV
    gate_ref[...] = jax.nn.sigmoid(zg).astype(BF16)
    aq, bq, ak, bk = aq_ref[...], bq_ref[...], ak_ref[...], bk_ref[...]
    for hh in range(B_HEADS):
        z = zq[:, hh * HEAD_DIM:(hh + 1) * HEAD_DIM]
        zp = zqp[:, hh * HEAD_DIM:(hh + 1) * HEAD_DIM]
        r = lax.rsqrt(jnp.mean(z * z, axis=-1, keepdims=True) + EPS)
        q_ref[hh] = (r * (z * aq + zp * bq) * (LOG2E * HEAD_DIM ** -0.5)).astype(BF16)
    for hh in range(B_KV_HEADS):
        z = zk[:, hh * HEAD_DIM:(hh + 1) * HEAD_DIM]
        zp = zkp[:, hh * HEAD_DIM:(hh + 1) * HEAD_DIM]
        r = lax.rsqrt(jnp.mean(z * z, axis=-1, keepdims=True) + EPS)
        k_ref[hh] = (r * (z * ak + zp * bk)).astype(BF16)
        v_ref[hh, :, :HEAD_DIM] = zv[:, hh * HEAD_DIM:(hh + 1) * HEAD_DIM].astype(BF16)
        lane = lax.broadcasted_iota(jnp.int32, (zv.shape[0], V_PAD - HEAD_DIM), 1)
        v_ref[hh, :, HEAD_DIM:] = jnp.where(lane == 0, 1.0, 0.0).astype(BF16)


def _projection(x2, sh, sc, g, w_ext, tabs, bsz, s):
    t, d = x2.shape
    tm = 256
    tpb = s // tm
    n_ext = w_ext.shape[1]
    n_gate = 2 * d
    row = lambda i: (i, 0)
    per_b = lambda i: (i // tpb, 0, 0)
    tab = lambda i: (i % tpb, 0)
    hm = lambda i: (i // tpb, 0, i % tpb, 0)
    out_shape = (
        [jax.ShapeDtypeStruct((t, A_GROUP_COLS), BF16)] * 3
        + [jax.ShapeDtypeStruct((bsz, B_HEADS, s, HEAD_DIM), BF16),
           jax.ShapeDtypeStruct((bsz, B_KV_HEADS, s, HEAD_DIM), BF16),
           jax.ShapeDtypeStruct((bsz, B_KV_HEADS, s, V_PAD), BF16),
           jax.ShapeDtypeStruct((t, n_gate), BF16)])
    out_specs = (
        [pl.BlockSpec((tm, A_GROUP_COLS), row)] * 3
        + [pl.BlockSpec((None, B_HEADS, tm, HEAD_DIM), hm),
           pl.BlockSpec((None, B_KV_HEADS, tm, HEAD_DIM), hm),
           pl.BlockSpec((None, B_KV_HEADS, tm, V_PAD), hm),
           pl.BlockSpec((tm, n_gate), row)])
    return pl.pallas_call(
        _proj_kernel,
        out_shape=out_shape,
        grid=(t // tm,),
        in_specs=[pl.BlockSpec((tm, d), row),
                  pl.BlockSpec((None, 1, d), per_b),
                  pl.BlockSpec((None, 1, d), per_b),
                  pl.BlockSpec((1, d), lambda i: (0, 0)),
                  pl.BlockSpec((d, n_ext), lambda i: (0, 0))]
                 + [pl.BlockSpec((tm, HEAD_DIM), tab)] * 4,
        out_specs=out_specs,
        compiler_params=_cparams("arbitrary"),
        name="in_proj",
    )(x2, sh, sc, g, w_ext, *tabs)


def _dilated_kernel(q_ref, kp_ref, kc_ref, kn_ref, vp_ref, vc_ref, vn_ref, bias_ref,
                    o_ref, lse_ref, *, tq, seq_len):
    i = pl.program_id(2)
    q = q_ref[...]
    k = jnp.concatenate([kp_ref[...], kc_ref[...], kn_ref[...]], axis=0)
    v = jnp.concatenate([vp_ref[...], vc_ref[...], vn_ref[...]], axis=0)
    nk = k.shape[0]
    kpos = i * tq - N_SIDE + lax.broadcasted_iota(jnp.int32, (tq, nk), 1)
    valid = (kpos >= 0) & (kpos < seq_len)
    for hh in range(A_HEADS):
        sl = slice(hh * HEAD_DIM, (hh + 1) * HEAD_DIM)
        sc = lax.dot_general(q[:, sl], k[:, sl], (((1,), (1,)), ((), ())),
                             preferred_element_type=F32)
        sc = sc * (HEAD_DIM ** -0.5) + bias_ref[hh]
        sc = jnp.where(valid, sc, NEG_INF)
        m = jnp.max(sc, axis=-1, keepdims=True)
        p = jnp.exp(sc - m)
        den = jnp.sum(p, axis=-1, keepdims=True)
        o = jnp.dot(p.astype(BF16), v[:, sl], preferred_element_type=F32) / den
        o_ref[:, sl] = o
        lse_ref[:, sl] = jnp.broadcast_to(m + jnp.log(den), (tq, HEAD_DIM))


def _t5_bucket(rel):
    nb = REL_BUCKETS // 2
    max_exact = nb // 2
    ret = jnp.where(rel > 0, nb, 0)
    n = jnp.abs(rel)
    nf = jnp.maximum(n, 1).astype(F32)
    large = max_exact + (jnp.log(nf / max_exact) / math.log(REL_MAX_DIST / max_exact)
                         * (nb - max_exact)).astype(jnp.int32)
    large = jnp.minimum(large, nb - 1)
    return ret + jnp.where(n < max_exact, n, large)


def _band_bias(rel_bias_g, dilation, tq):
    nk = tq + 2 * N_SIDE
    rel = jnp.arange(nk)[None, :] - N_SIDE - jnp.arange(tq)[:, None]
    bucket = _t5_bucket(rel * dilation)[None]
    bias = jnp.full((A_HEADS,) + rel.shape, NEG_INF, F32)
    for b in range(REL_BUCKETS):
        bias = jnp.where(bucket == b, rel_bias_g[b].astype(F32)[:, None, None], bias)
    return jnp.where((jnp.abs(rel) <= N_SIDE)[None], bias, NEG_INF)


def _dilated_group(za, rel_bias_g, dilation, bsz, s):
    ll = s // dilation
    tq = 128
    assert ll % tq == 0 and tq == 2 * N_SIDE
    zv = za.reshape(bsz, ll, dilation * A_GROUP_COLS)
    nhalf = ll // N_SIDE
    cur = lambda c: (lambda b, r, i: (b, i, 3 * r + c))
    prv = lambda c: (lambda b, r, i: (b, jnp.maximum(2 * i - 1, 0), 3 * r + c))
    nxt = lambda c: (lambda b, r, i: (b, jnp.minimum(2 * i + 2, nhalf - 1), 3 * r + c))
    full = pl.BlockSpec((None, tq, A_OUT), cur(0))
    half = lambda f: pl.BlockSpec((None, N_SIDE, A_OUT), f)
    bias = _band_bias(rel_bias_g, dilation, tq)
    out_sd = jax.ShapeDtypeStruct((bsz, ll, dilation * A_OUT), F32)
    out_spec = pl.BlockSpec((None, tq, A_OUT), lambda b, r, i: (b, i, r))
    o, lse = pl.pallas_call(
        functools.partial(_dilated_kernel, tq=tq, seq_len=ll),
        out_shape=[out_sd, out_sd],
        grid=(bsz, dilation, ll // tq),
        in_specs=[full,
                  half(prv(1)), pl.BlockSpec((None, tq, A_OUT), cur(1)), half(nxt(1)),
                  half(prv(2)), pl.BlockSpec((None, tq, A_OUT), cur(2)), half(nxt(2)),
                  pl.BlockSpec(bias.shape, lambda b, r, i: (0, 0, 0))],
        out_specs=[out_spec, out_spec],
        compiler_params=_cparams("arbitrary", "arbitrary", "arbitrary"),
        name=f"dilated_attn_d{dilation}",
    )(zv, zv, zv, zv, zv, zv, zv, bias)
    return o.reshape(bsz * s, A_OUT), lse.reshape(bsz * s, A_OUT)


GQA_NO_SHIFT_MAX_LOG2 = 80.0
GQA_KEY_CHUNK = 512
V_PAD = 128


def _gqa_kernel_noshift(q_ref, k_ref, v_ref, o_ref):
    grp, tq, dh = q_ref.shape
    q = q_ref[...].reshape(grp * tq, dh)
    nt = (((1,), (1,)), ((), ()))
    acc = jnp.zeros((grp * tq, V_PAD), F32)
    for c in range(k_ref.shape[0] // GQA_KEY_CHUNK):
        rows = slice(c * GQA_KEY_CHUNK, (c + 1) * GQA_KEY_CHUNK)
        p = jnp.exp2(lax.dot_general(q, k_ref[rows, :], nt, preferred_element_type=F32))
        acc = acc + jnp.dot(p.astype(BF16), v_ref[rows, :], preferred_element_type=F32)
    o = acc[:, :dh] / acc[:, dh:dh + 1]
    for hh in range(grp):
        o_ref[:, hh * dh:(hh + 1) * dh] = o[hh * tq:(hh + 1) * tq].astype(BF16)


def _gqa_kernel_rowmax(q_ref, k_ref, v_ref, o_ref):
    k = k_ref[...]
    v = v_ref[:, :HEAD_DIM]
    for hh in range(q_ref.shape[0]):
        sc = lax.dot_general(q_ref[hh], k, (((1,), (1,)), ((), ())), preferred_element_type=F32)
        m = jnp.max(sc, axis=-1, keepdims=True)
        p = jnp.exp2(sc - m)
        den = jnp.sum(p, axis=-1, keepdims=True)
        o = jnp.dot(p.astype(BF16), v, preferred_element_type=F32) / den
        o_ref[:, hh * HEAD_DIM:(hh + 1) * HEAD_DIM] = o.astype(BF16)


def _gqa_call(body, name, tq, q, k, v):
    bsz, _, s, _ = q.shape
    return pl.pallas_call(
        body,
        out_shape=jax.ShapeDtypeStruct((bsz, s, B_Q), BF16),
        grid=(bsz, B_KV_HEADS, s // tq),
        in_specs=[pl.BlockSpec((None, B_GRP, tq, HEAD_DIM), lambda b, h, i: (b, h, i, 0)),
                  pl.BlockSpec((None, None, s, HEAD_DIM), lambda b, h, i: (b, h, 0, 0)),
                  pl.BlockSpec((None, None, s, V_PAD), lambda b, h, i: (b, h, 0, 0))],
        out_specs=pl.BlockSpec((None, tq, B_GRP * HEAD_DIM), lambda b, h, i: (b, i, h)),
        compiler_params=_cparams("arbitrary", "arbitrary", "arbitrary"),
        name=name,
    )(q, k, v)


def _gqa_attention(q, k, v, q_gain, k_gain):
    bound = (HEAD_DIM ** 0.5) * LOG2E * 1.01 * jnp.max(jnp.abs(q_gain)) * jnp.max(jnp.abs(k_gain))
    return lax.cond(bound <= GQA_NO_SHIFT_MAX_LOG2,
                    functools.partial(_gqa_call, _gqa_kernel_noshift, "gqa_attn", 512),
                    functools.partial(_gqa_call, _gqa_kernel_rowmax, "gqa_attn_rowmax", 256),
                    q, k, v)


def _merge_kernel(o0_ref, o1_ref, o2_ref, l0_ref, l1_ref, l2_ref, ob_ref, gate_ref, x_ref, g1_ref,
                  wa_ref, wb_ref, wo_ref, xo_ref):
    d = x_ref.shape[-1]
    l0, l1, l2 = l0_ref[...], l1_ref[...], l2_ref[...]
    m = jnp.maximum(jnp.maximum(l0, l1), l2)
    e0, e1, e2 = jnp.exp(l0 - m), jnp.exp(l1 - m), jnp.exp(l2 - m)
    tot = e0 + e1 + e2
    oa = (e0 / tot) * o0_ref[...] + (e1 / tot) * o1_ref[...] + (e2 / tot) * o2_ref[...]
    ya = jnp.dot(oa.astype(BF16), wa_ref[...], preferred_element_type=F32)
    yb = jnp.dot(ob_ref[...], wb_ref[...], preferred_element_type=F32)
    merged = gate_ref[:, :d].astype(F32) * ya + gate_ref[:, d:].astype(F32) * yb
    y = jnp.dot(merged.astype(BF16), wo_ref[...], preferred_element_type=F32)
    xo_ref[...] = x_ref[...] + g1_ref[...] * y


def _merge(os_, ls_, ob, gate, x2, g1, wa, wb, wo, s):
    t, d = x2.shape
    tm = 256
    tpb = s // tm
    row = lambda i: (i, 0)
    const = lambda i: (0, 0)
    a_spec = pl.BlockSpec((tm, A_OUT), row)
    return pl.pallas_call(
        _merge_kernel,
        out_shape=jax.ShapeDtypeStruct((t, d), F32),
        grid=(t // tm,),
        in_specs=[a_spec] * 6
                 + [pl.BlockSpec((tm, B_Q), row),
                    pl.BlockSpec((tm, 2 * d), row),
                    pl.BlockSpec((tm, d), row),
                    pl.BlockSpec((None, 1, d), lambda i: (i // tpb, 0, 0)),
                    pl.BlockSpec(wa.shape, const),
                    pl.BlockSpec(wb.shape, const),
                    pl.BlockSpec(wo.shape, const)],
        out_specs=pl.BlockSpec((tm, d), row),
        compiler_params=_cparams("arbitrary"),
        name="branch_merge",
    )(*os_, *ls_, ob, gate, x2, g1, wa, wb, wo)


def _router_kernel(x_ref, sh_ref, sc_ref, g_ref, whi_ref, wlo_ref, b_ref,
                   h_ref, e_ref, w_ref, pos_ref, cnt_ref, run_ref):
    @pl.when(pl.program_id(0) == 0)
    def _():
        run_ref[...] = jnp.zeros_like(run_ref)

    x = x_ref[...]
    ms = jnp.mean(x * x, axis=-1, keepdims=True)
    h = x * lax.rsqrt(ms + EPS) * g_ref[...]
    h = h * (1.0 + sc_ref[...]) + sh_ref[...]
    h_ref[...] = h
    hhi = h.astype(BF16)
    hlo = (h - hhi.astype(F32)).astype(BF16)
    nt = (((1,), (1,)), ((), ()))
    dotf = lambda a, b: lax.dot_general(a, b, nt, preferred_element_type=F32)
    logits = dotf(whi_ref[...], hhi) + (dotf(whi_ref[...], hlo) + dotf(wlo_ref[...], hhi)) + b_ref[...]
    ne, tm = logits.shape
    iota = lax.broadcasted_iota(jnp.int32, (ne, tm), 0).astype(F32)
    vals, idxs = [], []
    cur = logits
    for _ in range(TOP_K):
        m = jnp.max(cur, axis=0, keepdims=True)
        idx = jnp.min(jnp.where(cur == m, iota, float(ne)), axis=0, keepdims=True)
        vals.append(m)
        idxs.append(idx)
        cur = jnp.where(iota == idx, -jnp.inf, cur)
    tv = jnp.concatenate(vals, axis=0)
    ex = jnp.exp(tv - tv[0:1])
    w_ref[...] = ex / jnp.sum(ex, axis=0, keepdims=True)
    e_ref[...] = jnp.concatenate(idxs, axis=0).astype(jnp.int32)
    onehot = jnp.zeros((ne, tm), F32)
    for idx in idxs:
        onehot = onehot + jnp.where(iota == idx, 1.0, 0.0)
    earlier = (lax.broadcasted_iota(jnp.int32, (tm, tm), 0)
               < lax.broadcasted_iota(jnp.int32, (tm, tm), 1))
    rank = jnp.dot(onehot.astype(BF16), jnp.where(earlier, 1.0, 0.0).astype(BF16),
                   preferred_element_type=F32) + run_ref[...]
    pos_ref[...] = jnp.concatenate(
        [jnp.sum(jnp.where(iota == idx, rank, 0.0), axis=0, keepdims=True) for idx in idxs],
        axis=0).astype(jnp.int32)
    run_ref[...] = run_ref[...] + jnp.sum(onehot, axis=1, keepdims=True)
    cnt_ref[...] = run_ref[...]


def _router(x2, sh, sc, g, w_router, b_router, s):
    t, d = x2.shape
    tm = 256
    tpb = s // tm
    ne = w_router.shape[1]
    wt = w_router.T
    whi = wt.astype(BF16)
    wlo = (wt - whi.astype(F32)).astype(BF16)
    per_b = lambda i: (i // tpb, 0, 0)
    const = lambda i: (0, 0)
    return pl.pallas_call(
        _router_kernel,
        out_shape=[jax.ShapeDtypeStruct((t, d), F32),
                   jax.ShapeDtypeStruct((TOP_K, t), jnp.int32),
                   jax.ShapeDtypeStruct((TOP_K, t), F32),
                   jax.ShapeDtypeStruct((TOP_K, t), jnp.int32),
                   jax.ShapeDtypeStruct((ne, 1), F32)],
        grid=(t // tm,),
        in_specs=[pl.BlockSpec((tm, d), lambda i: (i, 0)),
                  pl.BlockSpec((None, 1, d), per_b),
                  pl.BlockSpec((None, 1, d), per_b),
                  pl.BlockSpec((1, d), const),
                  pl.BlockSpec((ne, d), const),
                  pl.BlockSpec((ne, d), const),
                  pl.BlockSpec((ne, 1), const)],
        out_specs=[pl.BlockSpec((tm, d), lambda i: (i, 0)),
                   pl.BlockSpec((TOP_K, tm), lambda i: (0, i)),
                   pl.BlockSpec((TOP_K, tm), lambda i: (0, i)),
                   pl.BlockSpec((TOP_K, tm), lambda i: (0, i)),
                   pl.BlockSpec((ne, 1), const)],
        scratch_shapes=[pltpu.VMEM((ne, 1), F32)],
        compiler_params=_cparams("arbitrary"),
        name="moe_router",
    )(x2, sh, sc, g, whi, wlo, b_router.reshape(ne, 1))


SC_CORES = 2
SC_SUBCORES = 16
SC_ROWS = 32


def _sc_scatter_rows(rows, dest_kt, n_slots):
    t, d = rows.shape
    n_workers = SC_CORES * SC_SUBCORES
    per_w = t // n_workers
    n_chunks = per_w // SC_ROWS
    assert per_w * n_workers == t and n_chunks * SC_ROWS == per_w and n_chunks % 2 == 0
    idx = dest_kt.reshape(TOP_K, n_workers, n_chunks, SC_ROWS).transpose(1, 2, 0, 3)
    idx = idx.reshape(n_workers * n_chunks * TOP_K, SC_ROWS)
    lists_per_w = n_chunks * TOP_K
    mesh = plsc.VectorSubcoreMesh(core_axis_name="c", subcore_axis_name="s")

    @functools.partial(
        pl.kernel, mesh=mesh,
        out_type=jax.ShapeDtypeStruct((n_slots, d), rows.dtype),
        scratch_types=[pltpu.VMEM((lists_per_w, SC_ROWS), jnp.int32),
                       pltpu.VMEM((SC_ROWS, d), rows.dtype),
                       pltpu.VMEM((SC_ROWS, d), rows.dtype)] + [pltpu.SemaphoreType.DMA] * 4,
        name="sc_scatter_rows",
    )
    def scatter_kernel(rows_hbm, idx_hbm, out_hbm, idx_v, buf_a, buf_b, ld_a, ld_b, st_a, st_b):
        wid = lax.axis_index("s") * SC_CORES + lax.axis_index("c")
        pltpu.sync_copy(idx_hbm.at[pl.ds(wid * lists_per_w, lists_per_w)], idx_v)

        def load(j, buf, sem):
            return pltpu.make_async_copy(rows_hbm.at[pl.ds(wid * per_w + j * SC_ROWS, SC_ROWS)], buf, sem)

        def scatter_all(j, buf, sem):
            copies = [pltpu.make_async_copy(buf, out_hbm.at[idx_v.at[j * TOP_K + kk]], sem)
                      for kk in range(TOP_K)]
            for cp in copies:
                cp.start()
            for cp in copies:
                cp.wait()

        load(0, buf_a, ld_a).start()

        @pl.loop(0, n_chunks, step=2)
        def _(j):
            load(j + 1, buf_b, ld_b).start()
            load(j, buf_a, ld_a).wait()
            scatter_all(j, buf_a, st_a)

            @pl.when(j + 2 < n_chunks)
            def _():
                load(j + 2, buf_a, ld_a).start()
            load(j + 1, buf_b, ld_b).wait()
            scatter_all(j + 1, buf_b, st_b)

    return scatter_kernel(rows, idx)


def _expert_kernel(blk_e_ref, n_used_ref, n_valid_ref, x_ref, wgu_ref, bgu_ref, wdn_ref, bdn_ref, o_ref):
    j = pl.program_id(0)
    d = x_ref.shape[1]

    @pl.when(j < n_used_ref[0])
    def _():
        row = lax.broadcasted_iota(jnp.int32, x_ref.shape, 0)
        xb = jnp.where(row < n_valid_ref[j], x_ref[...], 0.0).astype(BF16)
        gu = jnp.dot(xb, wgu_ref[...], preferred_element_type=F32) + bgu_ref[...]
        x_glu = jnp.minimum(gu[:, :d], SWIGLU_LIMIT)
        x_lin = jnp.clip(gu[:, d:], -SWIGLU_LIMIT, SWIGLU_LIMIT)
        act = x_glu * jax.nn.sigmoid(SWIGLU_ALPHA * x_glu) * (x_lin + 1.0)
        o_ref[...] = jnp.dot(act.astype(BF16), wdn_ref[...], preferred_element_type=F32) + bdn_ref[...]

    @pl.when(j >= n_used_ref[0])
    def _():
        o_ref[...] = jnp.zeros_like(o_ref)


def _experts(xs, blk_e, n_used, n_valid, wgu, bgu, wdn, bdn):
    d = xs.shape[1]
    n_blocks = blk_e.shape[0]
    ne = wgu.shape[0]
    by_expert = lambda j, be, nu, nv: (be[j], 0, 0)
    grid_spec = pltpu.PrefetchScalarGridSpec(
        num_scalar_prefetch=3,
        grid=(n_blocks,),
        in_specs=[pl.BlockSpec((MOE_BLOCK, d), lambda j, be, nu, nv: (jnp.minimum(j, nu[0] - 1), 0)),
                  pl.BlockSpec((None, d, 2 * d), by_expert),
                  pl.BlockSpec((None, 1, 2 * d), by_expert),
                  pl.BlockSpec((None, d, d), by_expert),
                  pl.BlockSpec((None, 1, d), by_expert)],
        out_specs=pl.BlockSpec((MOE_BLOCK, d), lambda j, be, nu, nv: (j, 0)),
    )
    return pl.pallas_call(
        _expert_kernel,
        out_shape=jax.ShapeDtypeStruct((n_blocks * MOE_BLOCK, d), F32),
        grid_spec=grid_spec,
        compiler_params=_cparams("arbitrary"),
        name="moe_experts",
    )(blk_e, n_used, n_valid, xs, wgu, bgu.reshape(ne, 1, 2 * d), wdn, bdn.reshape(ne, 1, d))


def _sc_gather_rows(table, idx):
    n_out = idx.shape[0]
    d = table.shape[1]
    n_workers = SC_CORES * SC_SUBCORES
    per_w = n_out // n_workers
    n_chunks = per_w // SC_ROWS
    assert per_w * n_workers == n_out and n_chunks * SC_ROWS == per_w and n_chunks % 2 == 0
    mesh = plsc.VectorSubcoreMesh(core_axis_name="c", subcore_axis_name="s")

    @functools.partial(
        pl.kernel, mesh=mesh,
        out_type=jax.ShapeDtypeStruct((n_out, d), table.dtype),
        scratch_types=[pltpu.VMEM((per_w,), jnp.int32),
                       pltpu.VMEM((SC_ROWS, d), table.dtype),
                       pltpu.VMEM((SC_ROWS, d), table.dtype),
                       pltpu.SemaphoreType.DMA, pltpu.SemaphoreType.DMA],
        name="sc_gather_rows",
    )
    def gather_kernel(table_hbm, idx_hbm, out_hbm, idx_v, rows_a, rows_b, sem_a, sem_b):
        base = (lax.axis_index("s") * SC_CORES + lax.axis_index("c")) * per_w
        pltpu.sync_copy(idx_hbm.at[pl.ds(base, per_w)], idx_v)

        def gather(j, buf, sem):
            rows = idx_v.at[pl.ds(j * SC_ROWS, SC_ROWS)]
            return pltpu.make_async_copy(table_hbm.at[rows], buf, sem)

        def write_back(j, buf):
            pltpu.sync_copy(buf, out_hbm.at[pl.ds(base + j * SC_ROWS, SC_ROWS)])

        gather(0, rows_a, sem_a).start()

        @pl.loop(0, n_chunks, step=2)
        def _(j):
            gather(j + 1, rows_b, sem_b).start()
            gather(j, rows_a, sem_a).wait()
            write_back(j, rows_a)

            @pl.when(j + 2 < n_chunks)
            def _():
                gather(j + 2, rows_a, sem_a).start()
            gather(j + 1, rows_b, sem_b).wait()
            write_back(j + 1, rows_b)

    return gather_kernel(table, idx)


def _combine_kernel(y0_ref, y1_ref, y2_ref, y3_ref, w_ref, x_ref, g2_ref, fg_ref, xo_ref, *, final):
    w = w_ref[...]
    y = w[:, 0:1] * y0_ref[...]
    for kk, y_ref in enumerate((y1_ref, y2_ref, y3_ref), start=1):
        y = y + w[:, kk:kk + 1] * y_ref[...]
    xn = x_ref[...] + g2_ref[...] * y
    if final:
        ms = jnp.mean(xn * xn, axis=-1, keepdims=True)
        xn = xn * lax.rsqrt(ms + EPS) * fg_ref[...]
    xo_ref[...] = xn


def _combine(dest_kt, out_sorted, gates_t, x2, g2, final_g, s, final):
    t, d = x2.shape
    tm = 256
    tpb = s // tm
    nt = t // tm
    yg = _sc_gather_rows(out_sorted, dest_kt)
    plane = lambda kk: pl.BlockSpec((tm, d), lambda i: (kk * nt + i, 0))
    return pl.pallas_call(
        functools.partial(_combine_kernel, final=final),
        out_shape=jax.ShapeDtypeStruct((t, d), F32),
        grid=(nt,),
        in_specs=[plane(0), plane(1), plane(2), plane(3),
                  pl.BlockSpec((tm, TOP_K), lambda i: (i, 0)),
                  pl.BlockSpec((tm, d), lambda i: (i, 0)),
                  pl.BlockSpec((None, 1, d), lambda i: (i // tpb, 0, 0)),
                  pl.BlockSpec((1, d), lambda i: (0, 0))],
        out_specs=pl.BlockSpec((tm, d), lambda i: (i, 0)),
        compiler_params=_cparams("arbitrary"),
        name="moe_combine",
    )(yg, yg, yg, yg, gates_t, x2, g2, final_g)


def _moe_plan(top_e, pos, counts):
    t = top_e.shape[1]
    sizes = counts[:, 0].astype(jnp.int32)
    padded = (sizes + MOE_BLOCK - 1) // MOE_BLOCK * MOE_BLOCK
    pad_end = jnp.cumsum(padded)
    pad_start = pad_end - padded
    start_of = jnp.zeros_like(top_e)
    for e in range(N_EXPERTS):
        start_of = jnp.where(top_e == e, pad_start[e], start_of)
    dest_kt = (start_of + pos).reshape(TOP_K * t)
    n_blocks = -(-t * TOP_K // MOE_BLOCK) + N_EXPERTS
    blk_start = jnp.arange(n_blocks, dtype=jnp.int32) * MOE_BLOCK
    blk_e = jnp.minimum(jnp.sum((pad_end[None, :] <= blk_start[:, None]).astype(jnp.int32), axis=1),
                        N_EXPERTS - 1).astype(jnp.int32)
    n_used = (pad_end[-1] // MOE_BLOCK).astype(jnp.int32).reshape(1)
    filled_end = pad_start + sizes
    n_valid = jnp.zeros((n_blocks,), jnp.int32)
    for e in range(N_EXPERTS):
        n_valid = jnp.where(blk_e == e, jnp.clip(filled_end[e] - blk_start, 0, MOE_BLOCK), n_valid)
    return dest_kt, blk_e, n_used, n_valid, n_blocks


def _rope_tables(s, gain):
    n_rows = s // GRID_W
    row = jnp.repeat(jnp.arange(n_rows), GRID_W).astype(F32)
    col = (jnp.arange(s) % GRID_W).astype(F32)
    half = HEAD_DIM // 2
    inv = ROPE_THETA ** (-jnp.arange(0, half, 2, dtype=F32) / half)
    ang_r = row[:, None] * inv
    ang_c = col[:, None] * inv
    cos = jnp.concatenate([jnp.cos(ang_r)] * 2 + [jnp.cos(ang_c)] * 2, axis=-1)
    sin = jnp.concatenate([-jnp.sin(ang_r), jnp.sin(ang_r), -jnp.sin(ang_c), jnp.sin(ang_c)], axis=-1)
    gain = gain.astype(F32)
    return cos * gain[None, :], sin * gain[_PARTNER][None, :]


_q = HEAD_DIM // 4
_PARTNER = np.concatenate([np.arange(_q, 2 * _q), np.arange(0, _q),
                           np.arange(3 * _q, 4 * _q), np.arange(2 * _q, 3 * _q)])


def _partner_cols(w, n_heads):
    idx = (np.arange(n_heads)[:, None] * HEAD_DIM + _PARTNER[None, :]).reshape(-1)
    return w[:, idx]


def kernel(x, c, w_ada, b_ada, norm1_g, w_in, q_norm_g, k_norm_g, rel_bias, w_br_a, w_br_b, w_out,
           norm2_g, w_router, b_router, w_gate_up, b_gate_up, w_down, b_down, final_norm_g):
    bsz, s, d = x.shape
    depth = w_ada.shape[0]
    t = bsz * s
    mod = _ada_mod(c, w_ada, b_ada)
    x2 = x.reshape(t, d)
    q_off = 3 * A_WIDTH
    k_off = q_off + B_Q
    for l in range(depth):
        sh1, sc1, g1, sh2, sc2, g2 = [mod[l, :, i * d:(i + 1) * d].reshape(bsz, 1, d)
                                      for i in range(N_MOD)]
        w = w_in[l]
        w_ext = jnp.concatenate([w, _partner_cols(w[:, q_off:q_off + B_Q], B_HEADS),
                                 _partner_cols(w[:, k_off:k_off + B_KV], B_KV_HEADS)],
                                axis=1).astype(BF16)
        tabs = _rope_tables(s, q_norm_g[l]) + _rope_tables(s, k_norm_g[l])
        za0, za1, za2, q, k, v, gate = _projection(x2, sh1, sc1, norm1_g[l].reshape(1, d), w_ext,
                                                   tabs, bsz, s)
        os_, ls_ = [], []
        for g, (za, (_, dilation)) in enumerate(zip((za0, za1, za2), A_GROUPS)):
            o, lse = _dilated_group(za, rel_bias[:, g * A_HEADS:(g + 1) * A_HEADS], dilation, bsz, s)
            os_.append(o)
            ls_.append(lse)
        ob = _gqa_attention(q, k, v, q_norm_g[l], k_norm_g[l]).reshape(t, B_Q)
        x2 = _merge(os_, ls_, ob, gate, x2, g1, w_br_a[l].astype(BF16), w_br_b[l].astype(BF16),
                    w_out[l].astype(BF16), s)
        h2, top_e, gates, pos, counts = _router(x2, sh2, sc2, norm2_g[l].reshape(1, d), w_router[l],
                                                b_router[l], s)
        dest_kt, blk_e, n_used, n_valid, n_blocks = _moe_plan(top_e, pos, counts)
        xs = _sc_scatter_rows(h2, dest_kt, n_blocks * MOE_BLOCK)
        out_sorted = _experts(xs, blk_e, n_used, n_valid, w_gate_up[l].astype(BF16), b_gate_up[l],
                              w_down[l].astype(BF16), b_down[l])
        x2 = _combine(dest_kt, out_sorted, gates.T, x2, g2, final_norm_g.reshape(1, d), s,
                      final=(l == depth - 1))
    return x2.reshape(bsz, s, d)
```

```python
import functools
import math

import numpy as np
import jax
import jax.numpy as jnp
from jax import lax
from jax.experimental import pallas as pl
from jax.experimental.pallas import tpu as pltpu
from jax.experimental.pallas import tpu_sc as plsc

F32 = jnp.float32
BF16 = jnp.bfloat16

HEAD_DIM = 64
A_GROUPS = ((128, 1), (512, 4), (2048, 16))
A_HEADS = 4
A_WIDTH = len(A_GROUPS) * A_HEADS * HEAD_DIM
A_OUT = A_HEADS * HEAD_DIM
A_GROUP_COLS = 3 * A_OUT
B_HEADS = 16
B_KV_HEADS = 4
B_GRP = B_HEADS // B_KV_HEADS
B_Q = B_HEADS * HEAD_DIM
B_KV = B_KV_HEADS * HEAD_DIM
GRID_W = 64
ROPE_THETA = 10000.0
REL_BUCKETS = 32
REL_MAX_DIST = 1024
N_EXPERTS = 32
TOP_K = 4
SWIGLU_LIMIT = 7.0
SWIGLU_ALPHA = 1.702
MOE_BLOCK = 512
N_MOD = 6
EPS = 1e-6
NEG_INF = -1e30
LOG2E = math.log2(math.e)
N_SIDE = 64
DIL_SUB = 2 * N_SIDE
DIL_STEP_ROWS = 512

VMEM_LIMIT = 56 * 1024 * 1024


def _cparams(*sem):
    return pltpu.CompilerParams(dimension_semantics=sem, vmem_limit_bytes=VMEM_LIMIT)


def _ada_kernel(c_ref, w_ref, b_ref, o_ref):
    c = c_ref[...]
    ca = (c * jax.nn.sigmoid(c)).astype(BF16)
    o_ref[...] = jnp.dot(ca, w_ref[...].astype(BF16), preferred_element_type=F32) + b_ref[...]


def _ada_mod(c, w_ada, b_ada):
    depth, d, n = w_ada.shape
    bsz = c.shape[0]
    tn = 1536
    return pl.pallas_call(
        _ada_kernel,
        out_shape=jax.ShapeDtypeStruct((depth, bsz, n), F32),
        grid=(depth, n // tn),
        in_specs=[pl.BlockSpec((bsz, d), lambda l, j: (0, 0)),
                  pl.BlockSpec((None, d, tn), lambda l, j: (l, 0, j)),
                  pl.BlockSpec((None, 1, tn), lambda l, j: (l, 0, j))],
        out_specs=pl.BlockSpec((None, bsz, tn), lambda l, j: (l, 0, j)),
        compiler_params=_cparams("arbitrary", "arbitrary"),
        name="ada_mod",
    )(c, w_ada, b_ada.reshape(depth, 1, n))


def _proj_kernel(x_ref, sh_ref, sc_ref, g_ref, w_ref, aq_ref, bq_ref, ak_ref, bk_ref,
                 za0_ref, za1_ref, za2_ref, q_ref, k_ref, v_ref, gate_ref):
    x = x_ref[...]
    ms = jnp.mean(x * x, axis=-1, keepdims=True)
    h = x * lax.rsqrt(ms + EPS) * g_ref[...]
    h = h * (1.0 + sc_ref[...]) + sh_ref[...]
    hb = h.astype(BF16)

    def mm(lo, hi):
        return jnp.dot(hb, w_ref[:, lo:hi], preferred_element_type=F32)

    o = 0
    for za_ref in (za0_ref, za1_ref, za2_ref):
        za_ref[...] = mm(o, o + A_GROUP_COLS).astype(BF16)
        o += A_GROUP_COLS
    zq = mm(o, o + B_Q); o += B_Q
    zk = mm(o, o + B_KV); o += B_KV
    zv = mm(o, o + B_KV); o += B_KV
    zg = mm(o, o + gate_ref.shape[-1]); o += gate_ref.shape[-1]
    zqp = mm(o, o + B_Q); o += B_Q
    zkp = mm(o, o + B_KV); o += B_KV
    gate_ref[...] = jax.nn.sigmoid(zg).astype(BF16)
    aq, bq, ak, bk = aq_ref[...], bq_ref[...], ak_ref[...], bk_ref[...]
    for hh in range(B_HEADS):
        z = zq[:, hh * HEAD_DIM:(hh + 1) * HEAD_DIM]
        zp = zqp[:, hh * HEAD_DIM:(hh + 1) * HEAD_DIM]
        r = lax.rsqrt(jnp.mean(z * z, axis=-1, keepdims=True) + EPS)
        q_ref[hh] = (r * (z * aq + zp * bq) * (LOG2E * HEAD_DIM ** -0.5)).astype(BF16)
    for hh in range(B_KV_HEADS):
        z = zk[:, hh * HEAD_DIM:(hh + 1) * HEAD_DIM]
        zp = zkp[:, hh * HEAD_DIM:(hh + 1) * HEAD_DIM]
        r = lax.rsqrt(jnp.mean(z * z, axis=-1, keepdims=True) + EPS)
        k_ref[hh] = (r * (z * ak + zp * bk)).astype(BF16)
        v_ref[hh, :, :HEAD_DIM] = zv[:, hh * HEAD_DIM:(hh + 1) * HEAD_DIM].astype(BF16)
        lane = lax.broadcasted_iota(jnp.int32, (zv.shape[0], V_PAD - HEAD_DIM), 1)
        v_ref[hh, :, HEAD_DIM:] = jnp.where(lane == 0, 1.0, 0.0).astype(BF16)


def _projection(x2, sh, sc, g, w_ext, tabs, bsz, s):
    t, d = x2.shape
    tm = 256
    tpb = s // tm
    n_ext = w_ext.shape[1]
    n_gate = 2 * d
    row = lambda i: (i, 0)
    per_b = lambda i: (i // tpb, 0, 0)
    tab = lambda i: (i % tpb, 0)
    hm = lambda i: (i // tpb, 0, i % tpb, 0)
    out_shape = (
        [jax.ShapeDtypeStruct((t, A_GROUP_COLS), BF16)] * 3
        + [jax.ShapeDtypeStruct((bsz, B_HEADS, s, HEAD_DIM), BF16),
           jax.ShapeDtypeStruct((bsz, B_KV_HEADS, s, HEAD_DIM), BF16),
           jax.ShapeDtypeStruct((bsz, B_KV_HEADS, s, V_PAD), BF16),
           jax.ShapeDtypeStruct((t, n_gate), BF16)])
    out_specs = (
        [pl.BlockSpec((tm, A_GROUP_COLS), row)] * 3
        + [pl.BlockSpec((None, B_HEADS, tm, HEAD_DIM), hm),
           pl.BlockSpec((None, B_KV_HEADS, tm, HEAD_DIM), hm),
           pl.BlockSpec((None, B_KV_HEADS, tm, V_PAD), hm),
           pl.BlockSpec((tm, n_gate), row)])
    return pl.pallas_call(
        _proj_kernel,
        out_shape=out_shape,
        grid=(t // tm,),
        in_specs=[pl.BlockSpec((tm, d), row),
                  pl.BlockSpec((None, 1, d), per_b),
                  pl.BlockSpec((None, 1, d), per_b),
                  pl.BlockSpec((1, d), lambda i: (0, 0)),
                  pl.BlockSpec((d, n_ext), lambda i: (0, 0))]
                 + [pl.BlockSpec((tm, HEAD_DIM), tab)] * 4,
        out_specs=out_specs,
        compiler_params=_cparams("arbitrary"),
        name="in_proj",
    )(x2, sh, sc, g, w_ext, *tabs)


def _dilated_kernel(q_ref, kp_ref, kc_ref, kn_ref, vp_ref, vc_ref, vn_ref, bias_ref,
                    o_ref, lse_ref, *, tq, seq_len):
    i = pl.program_id(2)
    k = jnp.concatenate([kp_ref[...], kc_ref[...], kn_ref[...]], axis=0)
    v = jnp.concatenate([vp_ref[...], vc_ref[...], vn_ref[...]], axis=0)
    nk = DIL_SUB + 2 * N_SIDE
    col = lax.broadcasted_iota(jnp.int32, (DIL_SUB, nk), 1)
    for sb in range(tq // DIL_SUB):
        rows = slice(sb * DIL_SUB, (sb + 1) * DIL_SUB)
        q = q_ref[rows, :]
        kb = k[sb * DIL_SUB:sb * DIL_SUB + nk]
        vb = v[sb * DIL_SUB:sb * DIL_SUB + nk]
        kpos = i * tq + sb * DIL_SUB - N_SIDE + col
        valid = (kpos >= 0) & (kpos < seq_len)
        for hh in range(A_HEADS):
            sl = slice(hh * HEAD_DIM, (hh + 1) * HEAD_DIM)
            sc = lax.dot_general(q[:, sl], kb[:, sl], (((1,), (1,)), ((), ())),
                                 preferred_element_type=F32)
            sc = sc * (HEAD_DIM ** -0.5) + bias_ref[hh]
            sc = jnp.where(valid, sc, NEG_INF)
            m = jnp.max(sc, axis=-1, keepdims=True)
            p = jnp.exp(sc - m)
            den = jnp.sum(p, axis=-1, keepdims=True)
            o = jnp.dot(p.astype(BF16), vb[:, sl], preferred_element_type=F32) / den
            o_ref[rows, sl] = o
            lse_ref[rows, sl] = jnp.broadcast_to(m + jnp.log(den), (DIL_SUB, HEAD_DIM))


def _t5_bucket(rel):
    nb = REL_BUCKETS // 2
    max_exact = nb // 2
    ret = jnp.where(rel > 0, nb, 0)
    n = jnp.abs(rel)
    nf = jnp.maximum(n, 1).astype(F32)
    large = max_exact + (jnp.log(nf / max_exact) / math.log(REL_MAX_DIST / max_exact)
                         * (nb - max_exact)).astype(jnp.int32)
    large = jnp.minimum(large, nb - 1)
    return ret + jnp.where(n < max_exact, n, large)


def _band_bias(rel_bias_g, dilation, tq):
    nk = tq + 2 * N_SIDE
    rel = jnp.arange(nk)[None, :] - N_SIDE - jnp.arange(tq)[:, None]
    bucket = _t5_bucket(rel * dilation)[None]
    bias = jnp.full((A_HEADS,) + rel.shape, NEG_INF, F32)
    for b in range(REL_BUCKETS):
        bias = jnp.where(bucket == b, rel_bias_g[b].astype(F32)[:, None, None], bias)
    return jnp.where((jnp.abs(rel) <= N_SIDE)[None], bias, NEG_INF)


def _dilated_group(za, rel_bias_g, dilation, bsz, s):
    ll = s // dilation
    tq = min(DIL_STEP_ROWS, ll)
    assert ll % tq == 0 and tq % DIL_SUB == 0
    zv = za.reshape(bsz, ll, dilation * A_GROUP_COLS)
    nhalf = ll // N_SIDE
    per = tq // N_SIDE
    cur = lambda c: (lambda b, r, i: (b, i, 3 * r + c))
    prv = lambda c: (lambda b, r, i: (b, jnp.maximum(per * i - 1, 0), 3 * r + c))
    nxt = lambda c: (lambda b, r, i: (b, jnp.minimum(per * (i + 1), nhalf - 1), 3 * r + c))
    full = pl.BlockSpec((None, tq, A_OUT), cur(0))
    half = lambda f: pl.BlockSpec((None, N_SIDE, A_OUT), f)
    bias = _band_bias(rel_bias_g, dilation, DIL_SUB)
    out_sd = jax.ShapeDtypeStruct((bsz, ll, dilation * A_OUT), F32)
    out_spec = pl.BlockSpec((None, tq, A_OUT), lambda b, r, i: (b, i, r))
    o, lse = pl.pallas_call(
        functools.partial(_dilated_kernel, tq=tq, seq_len=ll),
        out_shape=[out_sd, out_sd],
        grid=(bsz, dilation, ll // tq),
        in_specs=[full,
                  half(prv(1)), pl.BlockSpec((None, tq, A_OUT), cur(1)), half(nxt(1)),
                  half(prv(2)), pl.BlockSpec((None, tq, A_OUT), cur(2)), half(nxt(2)),
                  pl.BlockSpec(bias.shape, lambda b, r, i: (0, 0, 0))],
        out_specs=[out_spec, out_spec],
        compiler_params=_cparams("arbitrary", "arbitrary", "arbitrary"),
        name=f"dilated_attn_d{dilation}",
    )(zv, zv, zv, zv, zv, zv, zv, bias)
    return o.reshape(bsz * s, A_OUT), lse.reshape(bsz * s, A_OUT)


GQA_NO_SHIFT_MAX_LOG2 = 80.0
GQA_KEY_CHUNK = 512
V_PAD = 128


def _gqa_kernel_noshift(q_ref, k_ref, v_ref, o_ref):
    grp, tq, dh = q_ref.shape
    q = q_ref[...].reshape(grp * tq, dh)
    nt = (((1,), (1,)), ((), ()))
    acc = jnp.zeros((grp * tq, V_PAD), F32)
    for c in range(k_ref.shape[0] // GQA_KEY_CHUNK):
        rows = slice(c * GQA_KEY_CHUNK, (c + 1) * GQA_KEY_CHUNK)
        p = jnp.exp2(lax.dot_general(q, k_ref[rows, :], nt, preferred_element_type=F32))
        acc = acc + jnp.dot(p.astype(BF16), v_ref[rows, :], preferred_element_type=F32)
    o = acc[:, :dh] / acc[:, dh:dh + 1]
    for hh in range(grp):
        o_ref[:, hh * dh:(hh + 1) * dh] = o[hh * tq:(hh + 1) * tq].astype(BF16)


def _gqa_kernel_rowmax(q_ref, k_ref, v_ref, o_ref):
    k = k_ref[...]
    v = v_ref[:, :HEAD_DIM]
    for hh in range(q_ref.shape[0]):
        sc = lax.dot_general(q_ref[hh], k, (((1,), (1,)), ((), ())), preferred_element_type=F32)
        m = jnp.max(sc, axis=-1, keepdims=True)
        p = jnp.exp2(sc - m)
        den = jnp.sum(p, axis=-1, keepdims=True)
        o = jnp.dot(p.astype(BF16), v, preferred_element_type=F32) / den
        o_ref[:, hh * HEAD_DIM:(hh + 1) * HEAD_DIM] = o.astype(BF16)


def _gqa_call(body, name, tq, q, k, v):
    bsz, _, s, _ = q.shape
    return pl.pallas_call(
        body,
        out_shape=jax.ShapeDtypeStruct((bsz, s, B_Q), BF16),
        grid=(bsz, B_KV_HEADS, s // tq),
        in_specs=[pl.BlockSpec((None, B_GRP, tq, HEAD_DIM), lambda b, h, i: (b, h, i, 0)),
                  pl.BlockSpec((None, None, s, HEAD_DIM), lambda b, h, i: (b, h, 0, 0)),
                  pl.BlockSpec((None, None, s, V_PAD), lambda b, h, i: (b, h, 0, 0))],
        out_specs=pl.BlockSpec((None, tq, B_GRP * HEAD_DIM), lambda b, h, i: (b, i, h)),
        compiler_params=_cparams("arbitrary", "arbitrary", "arbitrary"),
        name=name,
    )(q, k, v)


def _gqa_attention(q, k, v, q_gain, k_gain):
    bound = (HEAD_DIM ** 0.5) * LOG2E * 1.01 * jnp.max(jnp.abs(q_gain)) * jnp.max(jnp.abs(k_gain))
    return lax.cond(bound <= GQA_NO_SHIFT_MAX_LOG2,
                    functools.partial(_gqa_call, _gqa_kernel_noshift, "gqa_attn", 512),
                    functools.partial(_gqa_call, _gqa_kernel_rowmax, "gqa_attn_rowmax", 256),
                    q, k, v)


def _merge_kernel(o0_ref, o1_ref, o2_ref, l0_ref, l1_ref, l2_ref, ob_ref, gate_ref, x_ref, g1_ref,
                  wa_ref, wb_ref, wo_ref, xo_ref):
    d = x_ref.shape[-1]
    l0, l1, l2 = l0_ref[...], l1_ref[...], l2_ref[...]
    m = jnp.maximum(jnp.maximum(l0, l1), l2)
    e0, e1, e2 = jnp.exp(l0 - m), jnp.exp(l1 - m), jnp.exp(l2 - m)
    tot = e0 + e1 + e2
    oa = (e0 / tot) * o0_ref[...] + (e1 / tot) * o1_ref[...] + (e2 / tot) * o2_ref[...]
    ya = jnp.dot(oa.astype(BF16), wa_ref[...], preferred_element_type=F32)
    yb = jnp.dot(ob_ref[...], wb_ref[...], preferred_element_type=F32)
    merged = gate_ref[:, :d].astype(F32) * ya + gate_ref[:, d:].astype(F32) * yb
    y = jnp.dot(merged.astype(BF16), wo_ref[...], preferred_element_type=F32)
    xo_ref[...] = x_ref[...] + g1_ref[...] * y


def _merge(os_, ls_, ob, gate, x2, g1, wa, wb, wo, s):
    t, d = x2.shape
    tm = 256
    tpb = s // tm
    row = lambda i: (i, 0)
    const = lambda i: (0, 0)
    a_spec = pl.BlockSpec((tm, A_OUT), row)
    return pl.pallas_call(
        _merge_kernel,
        out_shape=jax.ShapeDtypeStruct((t, d), F32),
        grid=(t // tm,),
        in_specs=[a_spec] * 6
                 + [pl.BlockSpec((tm, B_Q), row),
                    pl.BlockSpec((tm, 2 * d), row),
                    pl.BlockSpec((tm, d), row),
                    pl.BlockSpec((None, 1, d), lambda i: (i // tpb, 0, 0)),
                    pl.BlockSpec(wa.shape, const),
                    pl.BlockSpec(wb.shape, const),
                    pl.BlockSpec(wo.shape, const)],
        out_specs=pl.BlockSpec((tm, d), row),
        compiler_params=_cparams("arbitrary"),
        name="branch_merge",
    )(*os_, *ls_, ob, gate, x2, g1, wa, wb, wo)


def _router_kernel(x_ref, sh_ref, sc_ref, g_ref, whi_ref, wlo_ref, b_ref,
                   h_ref, e_ref, w_ref, pos_ref, cnt_ref, run_ref):
    @pl.when(pl.program_id(0) == 0)
    def _():
        run_ref[...] = jnp.zeros_like(run_ref)

    x = x_ref[...]
    ms = jnp.mean(x * x, axis=-1, keepdims=True)
    h = x * lax.rsqrt(ms + EPS) * g_ref[...]
    h = h * (1.0 + sc_ref[...]) + sh_ref[...]
    h_ref[...] = h
    hhi = h.astype(BF16)
    hlo = (h - hhi.astype(F32)).astype(BF16)
    nt = (((1,), (1,)), ((), ()))
    dotf = lambda a, b: lax.dot_general(a, b, nt, preferred_element_type=F32)
    logits = dotf(whi_ref[...], hhi) + (dotf(whi_ref[...], hlo) + dotf(wlo_ref[...], hhi)) + b_ref[...]
    ne, tm = logits.shape
    iota = lax.broadcasted_iota(jnp.int32, (ne, tm), 0).astype(F32)
    vals, idxs = [], []
    cur = logits
    for _ in range(TOP_K):
        m = jnp.max(cur, axis=0, keepdims=True)
        idx = jnp.min(jnp.where(cur == m, iota, float(ne)), axis=0, keepdims=True)
        vals.append(m)
        idxs.append(idx)
        cur = jnp.where(iota == idx, -jnp.inf, cur)
    tv = jnp.concatenate(vals, axis=0)
    ex = jnp.exp(tv - tv[0:1])
    w_ref[...] = ex / jnp.sum(ex, axis=0, keepdims=True)
    e_ref[...] = jnp.concatenate(idxs, axis=0).astype(jnp.int32)
    onehot = jnp.zeros((ne, tm), F32)
    for idx in idxs:
        onehot = onehot + jnp.where(iota == idx, 1.0, 0.0)
    earlier = (lax.broadcasted_iota(jnp.int32, (tm, tm), 0)
               < lax.broadcasted_iota(jnp.int32, (tm, tm), 1))
    rank = jnp.dot(onehot.astype(BF16), jnp.where(earlier, 1.0, 0.0).astype(BF16),
                   preferred_element_type=F32) + run_ref[...]
    pos_ref[...] = jnp.concatenate(
        [jnp.sum(jnp.where(iota == idx, rank, 0.0), axis=0, keepdims=True) for idx in idxs],
        axis=0).astype(jnp.int32)
    run_ref[...] = run_ref[...] + jnp.sum(onehot, axis=1, keepdims=True)
    cnt_ref[...] = run_ref[...]


def _router(x2, sh, sc, g, w_router, b_router, s):
    t, d = x2.shape
    tm = 256
    tpb = s // tm
    ne = w_router.shape[1]
    wt = w_router.T
    whi = wt.astype(BF16)
    wlo = (wt - whi.astype(F32)).astype(BF16)
    per_b = lambda i: (i // tpb, 0, 0)
    const = lambda i: (0, 0)
    return pl.pallas_call(
        _router_kernel,
        out_shape=[jax.ShapeDtypeStruct((t, d), F32),
                   jax.ShapeDtypeStruct((TOP_K, t), jnp.int32),
                   jax.ShapeDtypeStruct((TOP_K, t), F32),
                   jax.ShapeDtypeStruct((TOP_K, t), jnp.int32),
                   jax.ShapeDtypeStruct((ne, 1), F32)],
        grid=(t // tm,),
        in_specs=[pl.BlockSpec((tm, d), lambda i: (i, 0)),
                  pl.BlockSpec((None, 1, d), per_b),
                  pl.BlockSpec((None, 1, d), per_b),
                  pl.BlockSpec((1, d), const),
                  pl.BlockSpec((ne, d), const),
                  pl.BlockSpec((ne, d), const),
                  pl.BlockSpec((ne, 1), const)],
        out_specs=[pl.BlockSpec((tm, d), lambda i: (i, 0)),
                   pl.BlockSpec((TOP_K, tm), lambda i: (0, i)),
                   pl.BlockSpec((TOP_K, tm), lambda i: (0, i)),
                   pl.BlockSpec((TOP_K, tm), lambda i: (0, i)),
                   pl.BlockSpec((ne, 1), const)],
        scratch_shapes=[pltpu.VMEM((ne, 1), F32)],
        compiler_params=_cparams("arbitrary"),
        name="moe_router",
    )(x2, sh, sc, g, whi, wlo, b_router.reshape(ne, 1))


SC_CORES = 2
SC_SUBCORES = 16
SC_ROWS = 32


def _sc_scatter_rows(rows, dest_kt, n_slots):
    t, d = rows.shape
    n_workers = SC_CORES * SC_SUBCORES
    per_w = t // n_workers
    n_chunks = per_w // SC_ROWS
    assert per_w * n_workers == t and n_chunks * SC_ROWS == per_w and n_chunks % 2 == 0
    idx = dest_kt.reshape(TOP_K, n_workers, n_chunks, SC_ROWS).transpose(1, 2, 0, 3)
    idx = idx.reshape(n_workers * n_chunks * TOP_K, SC_ROWS)
    lists_per_w = n_chunks * TOP_K
    mesh = plsc.VectorSubcoreMesh(core_axis_name="c", subcore_axis_name="s")

    @functools.partial(
        pl.kernel, mesh=mesh,
        out_type=jax.ShapeDtypeStruct((n_slots, d), rows.dtype),
        scratch_types=[pltpu.VMEM((lists_per_w, SC_ROWS), jnp.int32),
                       pltpu.VMEM((SC_ROWS, d), rows.dtype),
                       pltpu.VMEM((SC_ROWS, d), rows.dtype)] + [pltpu.SemaphoreType.DMA] * 4,
        name="sc_scatter_rows",
    )
    def scatter_kernel(rows_hbm, idx_hbm, out_hbm, idx_v, buf_a, buf_b, ld_a, ld_b, st_a, st_b):
        wid = lax.axis_index("s") * SC_CORES + lax.axis_index("c")
        pltpu.sync_copy(idx_hbm.at[pl.ds(wid * lists_per_w, lists_per_w)], idx_v)

        def load(j, buf, sem):
            return pltpu.make_async_copy(rows_hbm.at[pl.ds(wid * per_w + j * SC_ROWS, SC_ROWS)], buf, sem)

        def scatter_all(j, buf, sem):
            copies = [pltpu.make_async_copy(buf, out_hbm.at[idx_v.at[j * TOP_K + kk]], sem)
                      for kk in range(TOP_K)]
            for cp in copies:
                cp.start()
            for cp in copies:
                cp.wait()

        load(0, buf_a, ld_a).start()

        @pl.loop(0, n_chunks, step=2)
        def _(j):
            load(j + 1, buf_b, ld_b).start()
            load(j, buf_a, ld_a).wait()
            scatter_all(j, buf_a, st_a)

            @pl.when(j + 2 < n_chunks)
            def _():
                load(j + 2, buf_a, ld_a).start()
            load(j + 1, buf_b, ld_b).wait()
            scatter_all(j + 1, buf_b, st_b)

    return scatter_kernel(rows, idx)


def _expert_kernel(blk_e_ref, n_used_ref, n_valid_ref, x_ref, wgu_ref, bgu_ref, wdn_ref, bdn_ref, o_ref,
                   wgu_bf, wdn_bf):
    j = pl.program_id(0)
    d = x_ref.shape[1]

    @pl.when(j < n_used_ref[0])
    def _():
        @pl.when((j == 0) | (blk_e_ref[j] != blk_e_ref[jnp.maximum(j - 1, 0)]))
        def _():
            wgu_bf[...] = wgu_ref[...].astype(BF16)
            wdn_bf[...] = wdn_ref[...].astype(BF16)

        row = lax.broadcasted_iota(jnp.int32, x_ref.shape, 0)
        xb = jnp.where(row < n_valid_ref[j], x_ref[...], 0.0).astype(BF16)
        gu = jnp.dot(xb, wgu_bf[...], preferred_element_type=F32) + bgu_ref[...]
        x_glu = jnp.minimum(gu[:, :d], SWIGLU_LIMIT)
        x_lin = jnp.clip(gu[:, d:], -SWIGLU_LIMIT, SWIGLU_LIMIT)
        act = x_glu * jax.nn.sigmoid(SWIGLU_ALPHA * x_glu) * (x_lin + 1.0)
        o_ref[...] = jnp.dot(act.astype(BF16), wdn_bf[...], preferred_element_type=F32) + bdn_ref[...]

    @pl.when(j >= n_used_ref[0])
    def _():
        o_ref[...] = jnp.zeros_like(o_ref)


def _experts(xs, blk_e, n_used, n_valid, wgu, bgu, wdn, bdn):
    d = xs.shape[1]
    n_blocks = blk_e.shape[0]
    ne = wgu.shape[0]
    by_expert = lambda j, be, nu, nv: (be[j], 0, 0)
    grid_spec = pltpu.PrefetchScalarGridSpec(
        num_scalar_prefetch=3,
        grid=(n_blocks,),
        in_specs=[pl.BlockSpec((MOE_BLOCK, d), lambda j, be, nu, nv: (jnp.minimum(j, nu[0] - 1), 0)),
                  pl.BlockSpec((None, d, 2 * d), by_expert),
                  pl.BlockSpec((None, 1, 2 * d), by_expert),
                  pl.BlockSpec((None, d, d), by_expert),
                  pl.BlockSpec((None, 1, d), by_expert)],
        out_specs=pl.BlockSpec((MOE_BLOCK, d), lambda j, be, nu, nv: (j, 0)),
        scratch_shapes=[pltpu.VMEM((d, 2 * d), BF16), pltpu.VMEM((d, d), BF16)],
    )
    return pl.pallas_call(
        _expert_kernel,
        out_shape=jax.ShapeDtypeStruct((n_blocks * MOE_BLOCK, d), F32),
        grid_spec=grid_spec,
        compiler_params=_cparams("arbitrary"),
        name="moe_experts",
    )(blk_e, n_used, n_valid, xs, wgu, bgu.reshape(ne, 1, 2 * d), wdn, bdn.reshape(ne, 1, d))


def _sc_gather_rows(table, idx):
    n_out = idx.shape[0]
    d = table.shape[1]
    n_workers = SC_CORES * SC_SUBCORES
    per_w = n_out // n_workers
    n_chunks = per_w // SC_ROWS
    assert per_w * n_workers == n_out and n_chunks * SC_ROWS == per_w and n_chunks % 2 == 0
    mesh = plsc.VectorSubcoreMesh(core_axis_name="c", subcore_axis_name="s")

    @functools.partial(
        pl.kernel, mesh=mesh,
        out_type=jax.ShapeDtypeStruct((n_out, d), table.dtype),
        scratch_types=[pltpu.VMEM((per_w,), jnp.int32),
                       pltpu.VMEM((SC_ROWS, d), table.dtype),
                       pltpu.VMEM((SC_ROWS, d), table.dtype),
                       pltpu.SemaphoreType.DMA, pltpu.SemaphoreType.DMA],
        name="sc_gather_rows",
    )
    def gather_kernel(table_hbm, idx_hbm, out_hbm, idx_v, rows_a, rows_b, sem_a, sem_b):
        base = (lax.axis_index("s") * SC_CORES + lax.axis_index("c")) * per_w
        pltpu.sync_copy(idx_hbm.at[pl.ds(base, per_w)], idx_v)

        def gather(j, buf, sem):
            rows = idx_v.at[pl.ds(j * SC_ROWS, SC_ROWS)]
            return pltpu.make_async_copy(table_hbm.at[rows], buf, sem)

        def write_back(j, buf):
            pltpu.sync_copy(buf, out_hbm.at[pl.ds(base + j * SC_ROWS, SC_ROWS)])

        gather(0, rows_a, sem_a).start()

        @pl.loop(0, n_chunks, step=2)
        def _(j):
            gather(j + 1, rows_b, sem_b).start()
            gather(j, rows_a, sem_a).wait()
            write_back(j, rows_a)

            @pl.when(j + 2 < n_chunks)
            def _():
                gather(j + 2, rows_a, sem_a).start()
            gather(j + 1, rows_b, sem_b).wait()
            write_back(j + 1, rows_b)

    return gather_kernel(table, idx)


def _combine_kernel(y0_ref, y1_ref, y2_ref, y3_ref, w_ref, x_ref, g2_ref, fg_ref, xo_ref, *, final):
    w = w_ref[...]
    y = w[:, 0:1] * y0_ref[...]
    for kk, y_ref in enumerate((y1_ref, y2_ref, y3_ref), start=1):
        y = y + w[:, kk:kk + 1] * y_ref[...]
    xn = x_ref[...] + g2_ref[...] * y
    if final:
        ms = jnp.mean(xn * xn, axis=-1, keepdims=True)
        xn = xn * lax.rsqrt(ms + EPS) * fg_ref[...]
    xo_ref[...] = xn


def _combine(dest_kt, out_sorted, gates_t, x2, g2, final_g, s, final):
    t, d = x2.shape
    tm = 256
    tpb = s // tm
    nt = t // tm
    yg = _sc_gather_rows(out_sorted, dest_kt)
    plane = lambda kk: pl.BlockSpec((tm, d), lambda i: (kk * nt + i, 0))
    return pl.pallas_call(
        functools.partial(_combine_kernel, final=final),
        out_shape=jax.ShapeDtypeStruct((t, d), F32),
        grid=(nt,),
        in_specs=[plane(0), plane(1), plane(2), plane(3),
                  pl.BlockSpec((tm, TOP_K), lambda i: (i, 0)),
                  pl.BlockSpec((tm, d), lambda i: (i, 0)),
                  pl.BlockSpec((None, 1, d), lambda i: (i // tpb, 0, 0)),
                  pl.BlockSpec((1, d), lambda i: (0, 0))],
        out_specs=pl.BlockSpec((tm, d), lambda i: (i, 0)),
        compiler_params=_cparams("arbitrary"),
        name="moe_combine",
    )(yg, yg, yg, yg, gates_t, x2, g2, final_g)


def _moe_plan(top_e, pos, counts):
    t = top_e.shape[1]
    sizes = counts[:, 0].astype(jnp.int32)
    padded = (sizes + MOE_BLOCK - 1) // MOE_BLOCK * MOE_BLOCK
    pad_end = jnp.cumsum(padded)
    pad_start = pad_end - padded
    start_of = jnp.zeros_like(top_e)
    for e in range(N_EXPERTS):
        start_of = jnp.where(top_e == e, pad_start[e], start_of)
    dest_kt = (start_of + pos).reshape(TOP_K * t)
    n_blocks = -(-t * TOP_K // MOE_BLOCK) + N_EXPERTS
    blk_start = jnp.arange(n_blocks, dtype=jnp.int32) * MOE_BLOCK
    blk_e = jnp.minimum(jnp.sum((pad_end[None, :] <= blk_start[:, None]).astype(jnp.int32), axis=1),
                        N_EXPERTS - 1).astype(jnp.int32)
    n_used = (pad_end[-1] // MOE_BLOCK).astype(jnp.int32).reshape(1)
    filled_end = pad_start + sizes
    n_valid = jnp.zeros((n_blocks,), jnp.int32)
    for e in range(N_EXPERTS):
        n_valid = jnp.where(blk_e == e, jnp.clip(filled_end[e] - blk_start, 0, MOE_BLOCK), n_valid)
    return dest_kt, blk_e, n_used, n_valid, n_blocks


def _rope_tables(s, gain):
    n_rows = s // GRID_W
    row = jnp.repeat(jnp.arange(n_rows), GRID_W).astype(F32)
    col = (jnp.arange(s) % GRID_W).astype(F32)
    half = HEAD_DIM // 2
    inv = ROPE_THETA ** (-jnp.arange(0, half, 2, dtype=F32) / half)
    ang_r = row[:, None] * inv
    ang_c = col[:, None] * inv
    cos = jnp.concatenate([jnp.cos(ang_r)] * 2 + [jnp.cos(ang_c)] * 2, axis=-1)
    sin = jnp.concatenate([-jnp.sin(ang_r), jnp.sin(ang_r), -jnp.sin(ang_c), jnp.sin(ang_c)], axis=-1)
    gain = gain.astype(F32)
    return cos * gain[None, :], sin * gain[_PARTNER][None, :]


_q = HEAD_DIM // 4
_PARTNER = np.concatenate([np.arange(_q, 2 * _q), np.arange(0, _q),
                           np.arange(3 * _q, 4 * _q), np.arange(2 * _q, 3 * _q)])


def _partner_cols(w, n_heads):
    idx = (np.arange(n_heads)[:, None] * HEAD_DIM + _PARTNER[None, :]).reshape(-1)
    return w[:, idx]


def kernel(x, c, w_ada, b_ada, norm1_g, w_in, q_norm_g, k_norm_g, rel_bias, w_br_a, w_br_b, w_out,
           norm2_g, w_router, b_router, w_gate_up, b_gate_up, w_down, b_down, final_norm_g):
    bsz, s, d = x.shape
    depth = w_ada.shape[0]
    t = bsz * s
    mod = _ada_mod(c, w_ada, b_ada)
    x2 = x.reshape(t, d)
    q_off = 3 * A_WIDTH
    k_off = q_off + B_Q
    for l in range(depth):
        sh1, sc1, g1, sh2, sc2, g2 = [mod[l, :, i * d:(i + 1) * d].reshape(bsz, 1, d)
                                      for i in range(N_MOD)]
        w = w_in[l]
        w_ext = jnp.concatenate([w, _partner_cols(w[:, q_off:q_off + B_Q], B_HEADS),
                                 _partner_cols(w[:, k_off:k_off + B_KV], B_KV_HEADS)],
                                axis=1).astype(BF16)
        tabs = _rope_tables(s, q_norm_g[l]) + _rope_tables(s, k_norm_g[l])
        za0, za1, za2, q, k, v, gate = _projection(x2, sh1, sc1, norm1_g[l].reshape(1, d), w_ext,
                                                   tabs, bsz, s)
        os_, ls_ = [], []
        for g, (za, (_, dilation)) in enumerate(zip((za0, za1, za2), A_GROUPS)):
            o, lse = _dilated_group(za, rel_bias[:, g * A_HEADS:(g + 1) * A_HEADS], dilation, bsz, s)
            os_.append(o)
            ls_.append(lse)
        ob = _gqa_attention(q, k, v, q_norm_g[l], k_norm_g[l]).reshape(t, B_Q)
        x2 = _merge(os_, ls_, ob, gate, x2, g1, w_br_a[l].astype(BF16), w_br_b[l].astype(BF16),
                    w_out[l].astype(BF16), s)
        h2, top_e, gates, pos, counts = _router(x2, sh2, sc2, norm2_g[l].reshape(1, d), w_router[l],
                                                b_router[l], s)
        dest_kt, blk_e, n_used, n_valid, n_blocks = _moe_plan(top_e, pos, counts)
        xs = _sc_scatter_rows(h2, dest_kt, n_blocks * MOE_BLOCK)
        out_sorted = _experts(xs, blk_e, n_used, n_valid, w_gate_up[l], b_gate_up[l], w_down[l], b_down[l])
        x2 = _combine(dest_kt, out_sorted, gates.T, x2, g2, final_norm_g.reshape(1, d), s,
                      final=(l == depth - 1))
    return x2.reshape(bsz, s, d)
```

```python
import functools
import math

import numpy as np
import jax
import jax.numpy as jnp
from jax import lax
from jax.experimental import pallas as pl
from jax.experimental.pallas import tpu as pltpu
from jax.experimental.pallas import tpu_sc as plsc

F32 = jnp.float32
BF16 = jnp.bfloat16

HEAD_DIM = 64
PAIR = 2 * HEAD_DIM
A_GROUPS = ((128, 1), (512, 4), (2048, 16))
A_HEADS = 4
A_WIDTH = len(A_GROUPS) * A_HEADS * HEAD_DIM
A_OUT = A_HEADS * HEAD_DIM
A_GROUP_COLS = 3 * A_OUT
B_HEADS = 16
B_KV_HEADS = 4
B_GRP = B_HEADS // B_KV_HEADS
B_Q = B_HEADS * HEAD_DIM
B_KV = B_KV_HEADS * HEAD_DIM
GRID_W = 64
ROPE_THETA = 10000.0
REL_BUCKETS = 32
REL_MAX_DIST = 1024
N_EXPERTS = 32
TOP_K = 4
SWIGLU_LIMIT = 7.0
SWIGLU_ALPHA = 1.702
MOE_BLOCK = 512
N_MOD = 6
EPS = 1e-6
NEG_INF = -1e30
LOG2E = math.log2(math.e)
N_SIDE = 64
DIL_SUB = 2 * N_SIDE
DIL_STEP_ROWS = 512

VMEM_LIMIT = 56 * 1024 * 1024


def _cparams(*sem):
    return pltpu.CompilerParams(dimension_semantics=sem, vmem_limit_bytes=VMEM_LIMIT)


def _ada_kernel(c_ref, w_ref, b_ref, o_ref):
    c = c_ref[...]
    ca = (c * jax.nn.sigmoid(c)).astype(BF16)
    o_ref[...] = jnp.dot(ca, w_ref[...].astype(BF16), preferred_element_type=F32) + b_ref[...]


def _ada_mod(c, w_ada, b_ada):
    depth, d, n = w_ada.shape
    bsz = c.shape[0]
    tn = 1536
    return pl.pallas_call(
        _ada_kernel,
        out_shape=jax.ShapeDtypeStruct((depth, bsz, n), F32),
        grid=(depth, n // tn),
        in_specs=[pl.BlockSpec((bsz, d), lambda l, j: (0, 0)),
                  pl.BlockSpec((None, d, tn), lambda l, j: (l, 0, j)),
                  pl.BlockSpec((None, 1, tn), lambda l, j: (l, 0, j))],
        out_specs=pl.BlockSpec((None, bsz, tn), lambda l, j: (l, 0, j)),
        compiler_params=_cparams("arbitrary", "arbitrary"),
        name="ada_mod",
    )(c, w_ada, b_ada.reshape(depth, 1, n))


def _proj_kernel(x_ref, sh_ref, sc_ref, g_ref, w_ref, aq_ref, bq_ref, ak_ref, bk_ref,
                 za0_ref, za1_ref, za2_ref, q_ref, k_ref, v_ref, gate_ref):
    x = x_ref[...]
    ms = jnp.mean(x * x, axis=-1, keepdims=True)
    h = x * lax.rsqrt(ms + EPS) * g_ref[...]
    h = h * (1.0 + sc_ref[...]) + sh_ref[...]
    hb = h.astype(BF16)

    def mm(lo, hi):
        return jnp.dot(hb, w_ref[:, lo:hi], preferred_element_type=F32)

    tm = x.shape[0]
    o = 0
    zq = mm(o, o + B_Q); o += B_Q
    zqp = mm(o, o + B_Q); o += B_Q
    zk = mm(o, o + B_KV); o += B_KV
    zkp = mm(o, o + B_KV); o += B_KV
    zv = mm(o, o + B_KV); o += B_KV
    lane = lax.broadcasted_iota(jnp.int32, (tm, PAIR), 1)
    first = lane < HEAD_DIM

    def pair_norm_rope(z, zp, a, b):
        zz = z * z
        ss = jnp.where(first,
                       jnp.sum(jnp.where(first, zz, 0.0), axis=-1, keepdims=True),
                       jnp.sum(jnp.where(first, 0.0, zz), axis=-1, keepdims=True))
        return lax.rsqrt(ss * (1.0 / HEAD_DIM) + EPS) * (z * a + zp * b)

    aq, bq, ak, bk = aq_ref[...], bq_ref[...], ak_ref[...], bk_ref[...]
    for j in range(B_HEADS // 2):
        cols = slice(j * PAIR, (j + 1) * PAIR)
        q_ref[j] = (pair_norm_rope(zq[:, cols], zqp[:, cols], aq, bq)
                    * (LOG2E * HEAD_DIM ** -0.5)).astype(BF16)
    ones_col = jnp.where(lane == HEAD_DIM, 1.0, 0.0)
    for j in range(B_KV_HEADS // 2):
        cols = slice(j * PAIR, (j + 1) * PAIR)
        kk = pair_norm_rope(zk[:, cols], zkp[:, cols], ak, bk)
        kk_sw = pltpu.roll(kk, HEAD_DIM, axis=1)
        k_ref[2 * j] = jnp.where(first, kk, kk_sw).astype(BF16)
        k_ref[2 * j + 1] = jnp.where(first, kk_sw, kk).astype(BF16)
        vv = zv[:, cols]
        v_ref[2 * j] = jnp.where(first, vv, ones_col).astype(BF16)
        v_ref[2 * j + 1] = jnp.where(first, pltpu.roll(vv, HEAD_DIM, axis=1), ones_col).astype(BF16)
    for za_ref in (za0_ref, za1_ref, za2_ref):
        za_ref[...] = mm(o, o + A_GROUP_COLS).astype(BF16)
        o += A_GROUP_COLS
    gate_ref[...] = jax.nn.sigmoid(mm(o, o + gate_ref.shape[-1])).astype(BF16)


def _projection(x2, sh, sc, g, w_ext, tabs, bsz, s):
    t, d = x2.shape
    tm = 256
    tpb = s // tm
    n_ext = w_ext.shape[1]
    n_gate = 2 * d
    row = lambda i: (i, 0)
    per_b = lambda i: (i // tpb, 0, 0)
    tab = lambda i: (i % tpb, 0)
    hm = lambda i: (i // tpb, 0, i % tpb, 0)
    out_shape = (
        [jax.ShapeDtypeStruct((t, A_GROUP_COLS), BF16)] * 3
        + [jax.ShapeDtypeStruct((bsz, B_HEADS // 2, s, PAIR), BF16),
           jax.ShapeDtypeStruct((bsz, B_KV_HEADS, s, PAIR), BF16),
           jax.ShapeDtypeStruct((bsz, B_KV_HEADS, s, PAIR), BF16),
           jax.ShapeDtypeStruct((t, n_gate), BF16)])
    out_specs = (
        [pl.BlockSpec((tm, A_GROUP_COLS), row)] * 3
        + [pl.BlockSpec((None, B_HEADS // 2, tm, PAIR), hm),
           pl.BlockSpec((None, B_KV_HEADS, tm, PAIR), hm),
           pl.BlockSpec((None, B_KV_HEADS, tm, PAIR), hm),
           pl.BlockSpec((tm, n_gate), row)])
    return pl.pallas_call(
        _proj_kernel,
        out_shape=out_shape,
        grid=(t // tm,),
        in_specs=[pl.BlockSpec((tm, d), row),
                  pl.BlockSpec((None, 1, d), per_b),
                  pl.BlockSpec((None, 1, d), per_b),
                  pl.BlockSpec((1, d), lambda i: (0, 0)),
                  pl.BlockSpec((d, n_ext), lambda i: (0, 0))]
                 + [pl.BlockSpec((tm, PAIR), tab)] * 4,
        out_specs=out_specs,
        compiler_params=_cparams("arbitrary"),
        name="in_proj",
    )(x2, sh, sc, g, w_ext, *tabs)


def _dilated_kernel(q_ref, kp_ref, kc_ref, kn_ref, vp_ref, vc_ref, vn_ref, bias_ref,
                    o_ref, lse_ref, *, tq, seq_len):
    i = pl.program_id(2)
    k = jnp.concatenate([kp_ref[...], kc_ref[...], kn_ref[...]], axis=0)
    v = jnp.concatenate([vp_ref[...], vc_ref[...], vn_ref[...]], axis=0)
    nk = DIL_SUB + 2 * N_SIDE
    col = lax.broadcasted_iota(jnp.int32, (DIL_SUB, nk), 1)
    for sb in range(tq // DIL_SUB):
        rows = slice(sb * DIL_SUB, (sb + 1) * DIL_SUB)
        q = q_ref[rows, :]
        kb = k[sb * DIL_SUB:sb * DIL_SUB + nk]
        vb = v[sb * DIL_SUB:sb * DIL_SUB + nk]
        kpos = i * tq + sb * DIL_SUB - N_SIDE + col
        valid = (kpos >= 0) & (kpos < seq_len)
        for hh in range(A_HEADS):
            sl = slice(hh * HEAD_DIM, (hh + 1) * HEAD_DIM)
            sc = lax.dot_general(q[:, sl], kb[:, sl], (((1,), (1,)), ((), ())),
                                 preferred_element_type=F32)
            sc = sc * (HEAD_DIM ** -0.5) + bias_ref[hh]
            sc = jnp.where(valid, sc, NEG_INF)
            m = jnp.max(sc, axis=-1, keepdims=True)
            p = jnp.exp(sc - m)
            den = jnp.sum(p, axis=-1, keepdims=True)
            o = jnp.dot(p.astype(BF16), vb[:, sl], preferred_element_type=F32) / den
            o_ref[rows, sl] = o
            lse_ref[rows, sl] = jnp.broadcast_to(m + jnp.log(den), (DIL_SUB, HEAD_DIM))


def _t5_bucket(rel):
    nb = REL_BUCKETS // 2
    max_exact = nb // 2
    ret = jnp.where(rel > 0, nb, 0)
    n = jnp.abs(rel)
    nf = jnp.maximum(n, 1).astype(F32)
    large = max_exact + (jnp.log(nf / max_exact) / math.log(REL_MAX_DIST / max_exact)
                         * (nb - max_exact)).astype(jnp.int32)
    large = jnp.minimum(large, nb - 1)
    return ret + jnp.where(n < max_exact, n, large)


def _band_bias(rel_bias_g, dilation, tq):
    nk = tq + 2 * N_SIDE
    rel = jnp.arange(nk)[None, :] - N_SIDE - jnp.arange(tq)[:, None]
    bucket = _t5_bucket(rel * dilation)[None]
    bias = jnp.full((A_HEADS,) + rel.shape, NEG_INF, F32)
    for b in range(REL_BUCKETS):
        bias = jnp.where(bucket == b, rel_bias_g[b].astype(F32)[:, None, None], bias)
    return jnp.where((jnp.abs(rel) <= N_SIDE)[None], bias, NEG_INF)


def _dilated_group(za, rel_bias_g, dilation, bsz, s):
    ll = s // dilation
    tq = min(DIL_STEP_ROWS, ll)
    assert ll % tq == 0 and tq % DIL_SUB == 0
    zv = za.reshape(bsz, ll, dilation * A_GROUP_COLS)
    nhalf = ll // N_SIDE
    per = tq // N_SIDE
    cur = lambda c: (lambda b, r, i: (b, i, 3 * r + c))
    prv = lambda c: (lambda b, r, i: (b, jnp.maximum(per * i - 1, 0), 3 * r + c))
    nxt = lambda c: (lambda b, r, i: (b, jnp.minimum(per * (i + 1), nhalf - 1), 3 * r + c))
    full = pl.BlockSpec((None, tq, A_OUT), cur(0))
    half = lambda f: pl.BlockSpec((None, N_SIDE, A_OUT), f)
    bias = _band_bias(rel_bias_g, dilation, DIL_SUB)
    out_sd = jax.ShapeDtypeStruct((bsz, ll, dilation * A_OUT), F32)
    out_spec = pl.BlockSpec((None, tq, A_OUT), lambda b, r, i: (b, i, r))
    o, lse = pl.pallas_call(
        functools.partial(_dilated_kernel, tq=tq, seq_len=ll),
        out_shape=[out_sd, out_sd],
        grid=(bsz, dilation, ll // tq),
        in_specs=[full,
                  half(prv(1)), pl.BlockSpec((None, tq, A_OUT), cur(1)), half(nxt(1)),
                  half(prv(2)), pl.BlockSpec((None, tq, A_OUT), cur(2)), half(nxt(2)),
                  pl.BlockSpec(bias.shape, lambda b, r, i: (0, 0, 0))],
        out_specs=[out_spec, out_spec],
        compiler_params=_cparams("arbitrary", "arbitrary", "arbitrary"),
        name=f"dilated_attn_d{dilation}",
    )(zv, zv, zv, zv, zv, zv, zv, bias)
    return o.reshape(bsz * s, A_OUT), lse.reshape(bsz * s, A_OUT)


GQA_NO_SHIFT_MAX_LOG2 = 80.0
GQA_KEY_CHUNK = 512


def _stacked_heads(q_ref):
    first = lax.broadcasted_iota(jnp.int32, q_ref.shape[1:], 1) < HEAD_DIM
    blocks = []
    for j in range(q_ref.shape[0]):
        qp = q_ref[j]
        blocks += [jnp.where(first, qp, jnp.zeros_like(qp)), jnp.where(first, jnp.zeros_like(qp), qp)]
    return jnp.concatenate(blocks, axis=0)


def _gqa_kernel_noshift(q_ref, k_ref, v_ref, o_ref):
    tq = q_ref.shape[1]
    q = _stacked_heads(q_ref)
    nt = (((1,), (1,)), ((), ()))
    acc = jnp.zeros((q.shape[0], PAIR), F32)
    for c in range(k_ref.shape[0] // GQA_KEY_CHUNK):
        rows = slice(c * GQA_KEY_CHUNK, (c + 1) * GQA_KEY_CHUNK)
        p = jnp.exp2(lax.dot_general(q, k_ref[rows, :], nt, preferred_element_type=F32))
        acc = acc + jnp.dot(p.astype(BF16), v_ref[rows, :], preferred_element_type=F32)
    o = acc[:, :HEAD_DIM] / acc[:, HEAD_DIM:HEAD_DIM + 1]
    for hh in range(q.shape[0] // tq):
        o_ref[:, hh * HEAD_DIM:(hh + 1) * HEAD_DIM] = o[hh * tq:(hh + 1) * tq].astype(BF16)


def _gqa_kernel_rowmax(q_ref, k_ref, v_ref, o_ref):
    tq = q_ref.shape[1]
    q = _stacked_heads(q_ref)
    k = k_ref[...]
    v = v_ref[:, :HEAD_DIM]
    for hh in range(q.shape[0] // tq):
        sc = lax.dot_general(q[hh * tq:(hh + 1) * tq], k, (((1,), (1,)), ((), ())),
                             preferred_element_type=F32)
        m = jnp.max(sc, axis=-1, keepdims=True)
        p = jnp.exp2(sc - m)
        den = jnp.sum(p, axis=-1, keepdims=True)
        o = jnp.dot(p.astype(BF16), v, preferred_element_type=F32) / den
        o_ref[:, hh * HEAD_DIM:(hh + 1) * HEAD_DIM] = o.astype(BF16)


def _gqa_call(body, name, tq, q, k, v):
    bsz, _, s, _ = q.shape
    return pl.pallas_call(
        body,
        out_shape=jax.ShapeDtypeStruct((bsz, s, B_Q), BF16),
        grid=(bsz, B_KV_HEADS, s // tq),
        in_specs=[pl.BlockSpec((None, B_GRP // 2, tq, PAIR), lambda b, h, i: (b, h, i, 0)),
                  pl.BlockSpec((None, None, s, PAIR), lambda b, h, i: (b, h, 0, 0)),
                  pl.BlockSpec((None, None, s, PAIR), lambda b, h, i: (b, h, 0, 0))],
        out_specs=pl.BlockSpec((None, tq, B_GRP * HEAD_DIM), lambda b, h, i: (b, i, h)),
        compiler_params=_cparams("arbitrary", "arbitrary", "arbitrary"),
        name=name,
    )(q, k, v)


def _gqa_attention(q, k, v, q_gain, k_gain):
    bound = (HEAD_DIM ** 0.5) * LOG2E * 1.01 * jnp.max(jnp.abs(q_gain)) * jnp.max(jnp.abs(k_gain))
    return lax.cond(bound <= GQA_NO_SHIFT_MAX_LOG2,
                    functools.partial(_gqa_call, _gqa_kernel_noshift, "gqa_attn", 512),
                    functools.partial(_gqa_call, _gqa_kernel_rowmax, "gqa_attn_rowmax", 256),
                    q, k, v)


def _merge_kernel(o0_ref, o1_ref, o2_ref, l0_ref, l1_ref, l2_ref, ob_ref, gate_ref, x_ref, g1_ref,
                  wa_ref, wb_ref, wo_ref, xo_ref):
    d = x_ref.shape[-1]
    l0, l1, l2 = l0_ref[...], l1_ref[...], l2_ref[...]
    m = jnp.maximum(jnp.maximum(l0, l1), l2)
    e0, e1, e2 = jnp.exp(l0 - m), jnp.exp(l1 - m), jnp.exp(l2 - m)
    tot = e0 + e1 + e2
    oa = (e0 / tot) * o0_ref[...] + (e1 / tot) * o1_ref[...] + (e2 / tot) * o2_ref[...]
    ya = jnp.dot(oa.astype(BF16), wa_ref[...], preferred_element_type=F32)
    yb = jnp.dot(ob_ref[...], wb_ref[...], preferred_element_type=F32)
    merged = gate_ref[:, :d].astype(F32) * ya + gate_ref[:, d:].astype(F32) * yb
    y = jnp.dot(merged.astype(BF16), wo_ref[...], preferred_element_type=F32)
    xo_ref[...] = x_ref[...] + g1_ref[...] * y


def _merge(os_, ls_, ob, gate, x2, g1, wa, wb, wo, s):
    t, d = x2.shape
    tm = 256
    tpb = s // tm
    row = lambda i: (i, 0)
    const = lambda i: (0, 0)
    a_spec = pl.BlockSpec((tm, A_OUT), row)
    return pl.pallas_call(
        _merge_kernel,
        out_shape=jax.ShapeDtypeStruct((t, d), F32),
        grid=(t // tm,),
        in_specs=[a_spec] * 6
                 + [pl.BlockSpec((tm, B_Q), row),
                    pl.BlockSpec((tm, 2 * d), row),
                    pl.BlockSpec((tm, d), row),
                    pl.BlockSpec((None, 1, d), lambda i: (i // tpb, 0, 0)),
                    pl.BlockSpec(wa.shape, const),
                    pl.BlockSpec(wb.shape, const),
                    pl.BlockSpec(wo.shape, const)],
        out_specs=pl.BlockSpec((tm, d), row),
        compiler_params=_cparams("arbitrary"),
        name="branch_merge",
    )(*os_, *ls_, ob, gate, x2, g1, wa, wb, wo)


def _router_kernel(x_ref, sh_ref, sc_ref, g_ref, whi_ref, wlo_ref, b_ref,
                   h_ref, e_ref, w_ref, pos_ref, cnt_ref, run_ref):
    @pl.when(pl.program_id(0) == 0)
    def _():
        run_ref[...] = jnp.zeros_like(run_ref)

    x = x_ref[...]
    ms = jnp.mean(x * x, axis=-1, keepdims=True)
    h = x * lax.rsqrt(ms + EPS) * g_ref[...]
    h = h * (1.0 + sc_ref[...]) + sh_ref[...]
    h_ref[...] = h
    hhi = h.astype(BF16)
    hlo = (h - hhi.astype(F32)).astype(BF16)
    nt = (((1,), (1,)), ((), ()))
    dotf = lambda a, b: lax.dot_general(a, b, nt, preferred_element_type=F32)
    logits = dotf(whi_ref[...], hhi) + (dotf(whi_ref[...], hlo) + dotf(wlo_ref[...], hhi)) + b_ref[...]
    ne, tm = logits.shape
    iota = lax.broadcasted_iota(jnp.int32, (ne, tm), 0).astype(F32)
    vals, idxs = [], []
    cur = logits
    for _ in range(TOP_K):
        m = jnp.max(cur, axis=0, keepdims=True)
        idx = jnp.min(jnp.where(cur == m, iota, float(ne)), axis=0, keepdims=True)
        vals.append(m)
        idxs.append(idx)
        cur = jnp.where(iota == idx, -jnp.inf, cur)
    tv = jnp.concatenate(vals, axis=0)
    ex = jnp.exp(tv - tv[0:1])
    w_ref[...] = ex / jnp.sum(ex, axis=0, keepdims=True)
    e_ref[...] = jnp.concatenate(idxs, axis=0).astype(jnp.int32)
    onehot = jnp.zeros((ne, tm), F32)
    for idx in idxs:
        onehot = onehot + jnp.where(iota == idx, 1.0, 0.0)
    earlier = (lax.broadcasted_iota(jnp.int32, (tm, tm), 0)
               < lax.broadcasted_iota(jnp.int32, (tm, tm), 1))
    rank = jnp.dot(onehot.astype(BF16), jnp.where(earlier, 1.0, 0.0).astype(BF16),
                   preferred_element_type=F32) + run_ref[...]
    pos_ref[...] = jnp.concatenate(
        [jnp.sum(jnp.where(iota == idx, rank, 0.0), axis=0, keepdims=True) for idx in idxs],
        axis=0).astype(jnp.int32)
    run_ref[...] = run_ref[...] + jnp.sum(onehot, axis=1, keepdims=True)
    cnt_ref[...] = run_ref[...]


def _router(x2, sh, sc, g, w_router, b_router, s):
    t, d = x2.shape
    tm = 256
    tpb = s // tm
    ne = w_router.shape[1]
    wt = w_router.T
    whi = wt.astype(BF16)
    wlo = (wt - whi.astype(F32)).astype(BF16)
    per_b = lambda i: (i // tpb, 0, 0)
    const = lambda i: (0, 0)
    return pl.pallas_call(
        _router_kernel,
        out_shape=[jax.ShapeDtypeStruct((t, d), F32),
                   jax.ShapeDtypeStruct((TOP_K, t), jnp.int32),
                   jax.ShapeDtypeStruct((TOP_K, t), F32),
                   jax.ShapeDtypeStruct((TOP_K, t), jnp.int32),
                   jax.ShapeDtypeStruct((ne, 1), F32)],
        grid=(t // tm,),
        in_specs=[pl.BlockSpec((tm, d), lambda i: (i, 0)),
                  pl.BlockSpec((None, 1, d), per_b),
                  pl.BlockSpec((None, 1, d), per_b),
                  pl.BlockSpec((1, d), const),
                  pl.BlockSpec((ne, d), const),
                  pl.BlockSpec((ne, d), const),
                  pl.BlockSpec((ne, 1), const)],
        out_specs=[pl.BlockSpec((tm, d), lambda i: (i, 0)),
                   pl.BlockSpec((TOP_K, tm), lambda i: (0, i)),
                   pl.BlockSpec((TOP_K, tm), lambda i: (0, i)),
                   pl.BlockSpec((TOP_K, tm), lambda i: (0, i)),
                   pl.BlockSpec((ne, 1), const)],
        scratch_shapes=[pltpu.VMEM((ne, 1), F32)],
        compiler_params=_cparams("arbitrary"),
        name="moe_router",
    )(x2, sh, sc, g, whi, wlo, b_router.reshape(ne, 1))


SC_CORES = 2
SC_SUBCORES = 16
SC_ROWS = 32


def _sc_scatter_rows(rows, dest_kt, n_slots):
    t, d = rows.shape
    n_workers = SC_CORES * SC_SUBCORES
    per_w = t // n_workers
    n_chunks = per_w // SC_ROWS
    assert per_w * n_workers == t and n_chunks * SC_ROWS == per_w and n_chunks % 2 == 0
    idx = dest_kt.reshape(TOP_K, n_workers, n_chunks, SC_ROWS).transpose(1, 2, 0, 3)
    idx = idx.reshape(n_workers * n_chunks * TOP_K, SC_ROWS)
    lists_per_w = n_chunks * TOP_K
    mesh = plsc.VectorSubcoreMesh(core_axis_name="c", subcore_axis_name="s")

    @functools.partial(
        pl.kernel, mesh=mesh,
        out_type=jax.ShapeDtypeStruct((n_slots, d), rows.dtype),
        scratch_types=[pltpu.VMEM((lists_per_w, SC_ROWS), jnp.int32),
                       pltpu.VMEM((SC_ROWS, d), rows.dtype),
                       pltpu.VMEM((SC_ROWS, d), rows.dtype)] + [pltpu.SemaphoreType.DMA] * 4,
        name="sc_scatter_rows",
    )
    def scatter_kernel(rows_hbm, idx_hbm, out_hbm, idx_v, buf_a, buf_b, ld_a, ld_b, st_a, st_b):
        wid = lax.axis_index("s") * SC_CORES + lax.axis_index("c")
        pltpu.sync_copy(idx_hbm.at[pl.ds(wid * lists_per_w, lists_per_w)], idx_v)

        def load(j, buf, sem):
            return pltpu.make_async_copy(rows_hbm.at[pl.ds(wid * per_w + j * SC_ROWS, SC_ROWS)], buf, sem)

        def scatter_all(j, buf, sem):
            copies = [pltpu.make_async_copy(buf, out_hbm.at[idx_v.at[j * TOP_K + kk]], sem)
                      for kk in range(TOP_K)]
            for cp in copies:
                cp.start()
            for cp in copies:
                cp.wait()

        load(0, buf_a, ld_a).start()

        @pl.loop(0, n_chunks, step=2)
        def _(j):
            load(j + 1, buf_b, ld_b).start()
            load(j, buf_a, ld_a).wait()
            scatter_all(j, buf_a, st_a)

            @pl.when(j + 2 < n_chunks)
            def _():
                load(j + 2, buf_a, ld_a).start()
            load(j + 1, buf_b, ld_b).wait()
            scatter_all(j + 1, buf_b, st_b)

    return scatter_kernel(rows, idx)


def _expert_kernel(blk_e_ref, n_used_ref, n_valid_ref, x_ref, wgu_ref, bgu_ref, wdn_ref, bdn_ref, o_ref,
                   wgu_bf, wdn_bf):
    j = pl.program_id(0)
    d = x_ref.shape[1]

    @pl.when(j < n_used_ref[0])
    def _():
        @pl.when((j == 0) | (blk_e_ref[j] != blk_e_ref[jnp.maximum(j - 1, 0)]))
        def _():
            wgu_bf[...] = wgu_ref[...].astype(BF16)
            wdn_bf[...] = wdn_ref[...].astype(BF16)

        row = lax.broadcasted_iota(jnp.int32, x_ref.shape, 0)
        xb = jnp.where(row < n_valid_ref[j], x_ref[...], 0.0).astype(BF16)
        gu = jnp.dot(xb, wgu_bf[...], preferred_element_type=F32) + bgu_ref[...]
        x_glu = jnp.minimum(gu[:, :d], SWIGLU_LIMIT)
        x_lin = jnp.clip(gu[:, d:], -SWIGLU_LIMIT, SWIGLU_LIMIT)
        act = x_glu * jax.nn.sigmoid(SWIGLU_ALPHA * x_glu) * (x_lin + 1.0)
        o_ref[...] = jnp.dot(act.astype(BF16), wdn_bf[...], preferred_element_type=F32) + bdn_ref[...]

    @pl.when(j >= n_used_ref[0])
    def _():
        o_ref[...] = jnp.zeros_like(o_ref)


def _experts(xs, blk_e, n_used, n_valid, wgu, bgu, wdn, bdn, layer):
    d = xs.shape[1]
    n_blocks = blk_e.shape[0]
    depth, ne = wgu.shape[:2]
    by_expert = lambda j, be, nu, nv: (layer, be[j], 0, 0)
    grid_spec = pltpu.PrefetchScalarGridSpec(
        num_scalar_prefetch=3,
        grid=(n_blocks,),
        in_specs=[pl.BlockSpec((MOE_BLOCK, d), lambda j, be, nu, nv: (jnp.minimum(j, nu[0] - 1), 0)),
                  pl.BlockSpec((None, None, d, 2 * d), by_expert),
                  pl.BlockSpec((None, None, 1, 2 * d), by_expert),
                  pl.BlockSpec((None, None, d, d), by_expert),
                  pl.BlockSpec((None, None, 1, d), by_expert)],
        out_specs=pl.BlockSpec((MOE_BLOCK, d), lambda j, be, nu, nv: (j, 0)),
        scratch_shapes=[pltpu.VMEM((d, 2 * d), BF16), pltpu.VMEM((d, d), BF16)],
    )
    return pl.pallas_call(
        _expert_kernel,
        out_shape=jax.ShapeDtypeStruct((n_blocks * MOE_BLOCK, d), F32),
        grid_spec=grid_spec,
        compiler_params=_cparams("arbitrary"),
        name="moe_experts",
    )(blk_e, n_used, n_valid, xs, wgu, bgu.reshape(depth, ne, 1, 2 * d), wdn, bdn.reshape(depth, ne, 1, d))


def _sc_gather_rows(table, idx):
    n_out = idx.shape[0]
    d = table.shape[1]
    n_workers = SC_CORES * SC_SUBCORES
    per_w = n_out // n_workers
    n_chunks = per_w // SC_ROWS
    assert per_w * n_workers == n_out and n_chunks * SC_ROWS == per_w and n_chunks % 2 == 0
    mesh = plsc.VectorSubcoreMesh(core_axis_name="c", subcore_axis_name="s")

    @functools.partial(
        pl.kernel, mesh=mesh,
        out_type=jax.ShapeDtypeStruct((n_out, d), table.dtype),
        scratch_types=[pltpu.VMEM((per_w,), jnp.int32),
                       pltpu.VMEM((SC_ROWS, d), table.dtype),
                       pltpu.VMEM((SC_ROWS, d), table.dtype),
                       pltpu.SemaphoreType.DMA, pltpu.SemaphoreType.DMA],
        name="sc_gather_rows",
    )
    def gather_kernel(table_hbm, idx_hbm, out_hbm, idx_v, rows_a, rows_b, sem_a, sem_b):
        base = (lax.axis_index("s") * SC_CORES + lax.axis_index("c")) * per_w
        pltpu.sync_copy(idx_hbm.at[pl.ds(base, per_w)], idx_v)

        def gather(j, buf, sem):
            rows = idx_v.at[pl.ds(j * SC_ROWS, SC_ROWS)]
            return pltpu.make_async_copy(table_hbm.at[rows], buf, sem)

        def write_back(j, buf):
            pltpu.sync_copy(buf, out_hbm.at[pl.ds(base + j * SC_ROWS, SC_ROWS)])

        gather(0, rows_a, sem_a).start()

        @pl.loop(0, n_chunks, step=2)
        def _(j):
            gather(j + 1, rows_b, sem_b).start()
            gather(j, rows_a, sem_a).wait()
            write_back(j, rows_a)

            @pl.when(j + 2 < n_chunks)
            def _():
                gather(j + 2, rows_a, sem_a).start()
            gather(j + 1, rows_b, sem_b).wait()
            write_back(j + 1, rows_b)

    return gather_kernel(table, idx)


def _combine_kernel(y0_ref, y1_ref, y2_ref, y3_ref, w_ref, x_ref, g2_ref, fg_ref, xo_ref, *, final):
    w = w_ref[...]
    y = w[:, 0:1] * y0_ref[...]
    for kk, y_ref in enumerate((y1_ref, y2_ref, y3_ref), start=1):
        y = y + w[:, kk:kk + 1] * y_ref[...]
    xn = x_ref[...] + g2_ref[...] * y
    if final:
        ms = jnp.mean(xn * xn, axis=-1, keepdims=True)
        xn = xn * lax.rsqrt(ms + EPS) * fg_ref[...]
    xo_ref[...] = xn


def _combine(dest_kt, out_sorted, gates_t, x2, g2, final_g, s, final):
    t, d = x2.shape
    tm = 256
    tpb = s // tm
    nt = t // tm
    yg = _sc_gather_rows(out_sorted, dest_kt)
    plane = lambda kk: pl.BlockSpec((tm, d), lambda i: (kk * nt + i, 0))
    return pl.pallas_call(
        functools.partial(_combine_kernel, final=final),
        out_shape=jax.ShapeDtypeStruct((t, d), F32),
        grid=(nt,),
        in_specs=[plane(0), plane(1), plane(2), plane(3),
                  pl.BlockSpec((tm, TOP_K), lambda i: (i, 0)),
                  pl.BlockSpec((tm, d), lambda i: (i, 0)),
                  pl.BlockSpec((None, 1, d), lambda i: (i // tpb, 0, 0)),
                  pl.BlockSpec((1, d), lambda i: (0, 0))],
        out_specs=pl.BlockSpec((tm, d), lambda i: (i, 0)),
        compiler_params=_cparams("arbitrary"),
        name="moe_combine",
    )(yg, yg, yg, yg, gates_t, x2, g2, final_g)


def _moe_plan(top_e, pos, counts):
    t = top_e.shape[1]
    sizes = counts[:, 0].astype(jnp.int32)
    padded = (sizes + MOE_BLOCK - 1) // MOE_BLOCK * MOE_BLOCK
    pad_end = jnp.cumsum(padded)
    pad_start = pad_end - padded
    start_of = jnp.zeros_like(top_e)
    for e in range(N_EXPERTS):
        start_of = jnp.where(top_e == e, pad_start[e], start_of)
    dest_kt = (start_of + pos).reshape(TOP_K * t)
    n_blocks = -(-t * TOP_K // MOE_BLOCK) + N_EXPERTS
    blk_start = jnp.arange(n_blocks, dtype=jnp.int32) * MOE_BLOCK
    blk_e = jnp.minimum(jnp.sum((pad_end[None, :] <= blk_start[:, None]).astype(jnp.int32), axis=1),
                        N_EXPERTS - 1).astype(jnp.int32)
    n_used = (pad_end[-1] // MOE_BLOCK).astype(jnp.int32).reshape(1)
    filled_end = pad_start + sizes
    n_valid = jnp.zeros((n_blocks,), jnp.int32)
    for e in range(N_EXPERTS):
        n_valid = jnp.where(blk_e == e, jnp.clip(filled_end[e] - blk_start, 0, MOE_BLOCK), n_valid)
    return dest_kt, blk_e, n_used, n_valid, n_blocks


def _rope_tables(s, gain):
    n_rows = s // GRID_W
    row = jnp.repeat(jnp.arange(n_rows), GRID_W).astype(F32)
    col = (jnp.arange(s) % GRID_W).astype(F32)
    half = HEAD_DIM // 2
    inv = ROPE_THETA ** (-jnp.arange(0, half, 2, dtype=F32) / half)
    ang_r = row[:, None] * inv
    ang_c = col[:, None] * inv
    cos = jnp.concatenate([jnp.cos(ang_r)] * 2 + [jnp.cos(ang_c)] * 2, axis=-1)
    sin = jnp.concatenate([-jnp.sin(ang_r), jnp.sin(ang_r), -jnp.sin(ang_c), jnp.sin(ang_c)], axis=-1)
    gain = gain.astype(F32)
    return cos * gain[None, :], sin * gain[_PARTNER][None, :]


_q = HEAD_DIM // 4
_PARTNER = np.concatenate([np.arange(_q, 2 * _q), np.arange(0, _q),
                           np.arange(3 * _q, 4 * _q), np.arange(2 * _q, 3 * _q)])


def _partner_cols(w, n_heads):
    idx = (np.arange(n_heads)[:, None] * HEAD_DIM + _PARTNER[None, :]).reshape(-1)
    return w[:, idx]


def kernel(x, c, w_ada, b_ada, norm1_g, w_in, q_norm_g, k_norm_g, rel_bias, w_br_a, w_br_b, w_out,
           norm2_g, w_router, b_router, w_gate_up, b_gate_up, w_down, b_down, final_norm_g):
    bsz, s, d = x.shape
    depth = w_ada.shape[0]
    t = bsz * s
    mod = _ada_mod(c, w_ada, b_ada)
    x2 = x.reshape(t, d)
    q_off = 3 * A_WIDTH
    k_off = q_off + B_Q
    for l in range(depth):
        sh1, sc1, g1, sh2, sc2, g2 = [mod[l, :, i * d:(i + 1) * d].reshape(bsz, 1, d)
                                      for i in range(N_MOD)]
        w = w_in[l]
        wq, wk = w[:, q_off:q_off + B_Q], w[:, k_off:k_off + B_KV]
        w_ext = jnp.concatenate([wq, _partner_cols(wq, B_HEADS), wk, _partner_cols(wk, B_KV_HEADS),
                                 w[:, k_off + B_KV:k_off + 2 * B_KV], w[:, :q_off],
                                 w[:, k_off + 2 * B_KV:]], axis=1).astype(BF16)
        tabs = [jnp.tile(tb, (1, 2)) for tb in
                _rope_tables(s, q_norm_g[l]) + _rope_tables(s, k_norm_g[l])]
        za0, za1, za2, q, k, v, gate = _projection(x2, sh1, sc1, norm1_g[l].reshape(1, d), w_ext,
                                                   tabs, bsz, s)
        os_, ls_ = [], []
        for g, (za, (_, dilation)) in enumerate(zip((za0, za1, za2), A_GROUPS)):
            o, lse = _dilated_group(za, rel_bias[:, g * A_HEADS:(g + 1) * A_HEADS], dilation, bsz, s)
            os_.append(o)
            ls_.append(lse)
        ob = _gqa_attention(q, k, v, q_norm_g[l], k_norm_g[l]).reshape(t, B_Q)
        x2 = _merge(os_, ls_, ob, gate, x2, g1, w_br_a[l].astype(BF16), w_br_b[l].astype(BF16),
                    w_out[l].astype(BF16), s)
        h2, top_e, gates, pos, counts = _router(x2, sh2, sc2, norm2_g[l].reshape(1, d), w_router[l],
                                                b_router[l], s)
        dest_kt, blk_e, n_used, n_valid, n_blocks = _moe_plan(top_e, pos, counts)
        xs = _sc_scatter_rows(h2, dest_kt, n_blocks * MOE_BLOCK)
        out_sorted = _experts(xs, blk_e, n_used, n_valid, w_gate_up, b_gate_up, w_down, b_down, l)
        x2 = _combine(dest_kt, out_sorted, gates.T, x2, g2, final_norm_g.reshape(1, d), s,
                      final=(l == depth - 1))
    return x2.reshape(bsz, s, d)
```

```python
import functools
import math

import numpy as np
import jax
import jax.numpy as jnp
from jax import lax
from jax.experimental import pallas as pl
from jax.experimental.pallas import tpu as pltpu
from jax.experimental.pallas import tpu_sc as plsc

F32 = jnp.float32
BF16 = jnp.bfloat16

HEAD_DIM = 64
LANE = 128
PAIR = 2 * HEAD_DIM
A_GROUPS = ((128, 1), (512, 4), (2048, 16))
A_HEADS = 4
A_WIDTH = len(A_GROUPS) * A_HEADS * HEAD_DIM
A_OUT = A_HEADS * HEAD_DIM
A_GROUP_COLS = 3 * A_OUT
B_HEADS = 16
B_KV_HEADS = 4
B_GRP = B_HEADS // B_KV_HEADS
B_Q = B_HEADS * HEAD_DIM
B_KV = B_KV_HEADS * HEAD_DIM
GRID_W = 64
ROPE_THETA = 10000.0
REL_BUCKETS = 32
REL_MAX_DIST = 1024
N_EXPERTS = 32
TOP_K = 4
SWIGLU_LIMIT = 7.0
SWIGLU_ALPHA = 1.702
MOE_BLOCK = 512
EXPERT_HIDDEN_CHUNK = 256
N_MOD = 6
EPS = 1e-6
NEG_INF = -1e30
LOG2E = math.log2(math.e)
N_SIDE = 64
DIL_SUB = 2 * N_SIDE
DIL_STEP_ROWS = 512

VMEM_LIMIT = 56 * 1024 * 1024


def _cparams(*sem):
    return pltpu.CompilerParams(dimension_semantics=sem, vmem_limit_bytes=VMEM_LIMIT)


def _ada_kernel(c_ref, w_ref, b_ref, o_ref):
    c = c_ref[...]
    ca = (c * jax.nn.sigmoid(c)).astype(BF16)
    o_ref[...] = jnp.dot(ca, w_ref[...].astype(BF16), preferred_element_type=F32) + b_ref[...]


def _ada_mod(c, w_ada, b_ada):
    depth, d, n = w_ada.shape
    bsz = c.shape[0]
    tn = 1536
    return pl.pallas_call(
        _ada_kernel,
        out_shape=jax.ShapeDtypeStruct((depth, bsz, n), F32),
        grid=(depth, n // tn),
        in_specs=[pl.BlockSpec((bsz, d), lambda l, j: (0, 0)),
                  pl.BlockSpec((None, d, tn), lambda l, j: (l, 0, j)),
                  pl.BlockSpec((None, 1, tn), lambda l, j: (l, 0, j))],
        out_specs=pl.BlockSpec((None, bsz, tn), lambda l, j: (l, 0, j)),
        compiler_params=_cparams("arbitrary", "arbitrary"),
        name="ada_mod",
    )(c, w_ada, b_ada.reshape(depth, 1, n))


def _proj_kernel(x_ref, sh_ref, sc_ref, g_ref, w_ref, aq_ref, bq_ref, ak_ref, bk_ref,
                 za0_ref, za1_ref, za2_ref, q_ref, k_ref, v_ref, gate_ref, fold_ref):
    x = x_ref[...]
    ms = jnp.mean(x * x, axis=-1, keepdims=True)
    h = x * lax.rsqrt(ms + EPS) * g_ref[...]
    h = h * (1.0 + sc_ref[...]) + sh_ref[...]
    hb = h.astype(BF16)

    def mm(lo, hi):
        return jnp.dot(hb, w_ref[:, lo:hi], preferred_element_type=F32)

    tm = x.shape[0]
    o = 0
    zq = mm(o, o + B_Q); o += B_Q
    zqp = mm(o, o + B_Q); o += B_Q
    zk = mm(o, o + B_KV); o += B_KV
    zkp = mm(o, o + B_KV); o += B_KV
    zv = mm(o, o + B_KV); o += B_KV
    lane = lax.broadcasted_iota(jnp.int32, (tm, PAIR), 1)
    first = lane < HEAD_DIM

    def pair_norm_rope(z, zp, a, b):
        zz = z * z
        ss = jnp.where(first,
                       jnp.sum(jnp.where(first, zz, 0.0), axis=-1, keepdims=True),
                       jnp.sum(jnp.where(first, 0.0, zz), axis=-1, keepdims=True))
        return lax.rsqrt(ss * (1.0 / HEAD_DIM) + EPS) * (z * a + zp * b)

    aq, bq, ak, bk = aq_ref[...], bq_ref[...], ak_ref[...], bk_ref[...]
    for j in range(B_HEADS // 2):
        cols = slice(j * PAIR, (j + 1) * PAIR)
        q_ref[j] = (pair_norm_rope(zq[:, cols], zqp[:, cols], aq, bq)
                    * (LOG2E * HEAD_DIM ** -0.5)).astype(BF16)
    ones_col = jnp.where(lane == HEAD_DIM, 1.0, 0.0)
    for j in range(B_KV_HEADS // 2):
        cols = slice(j * PAIR, (j + 1) * PAIR)
        kk = pair_norm_rope(zk[:, cols], zkp[:, cols], ak, bk)
        kk_sw = pltpu.roll(kk, HEAD_DIM, axis=1)
        k_ref[2 * j] = jnp.where(first, kk, kk_sw).astype(BF16)
        k_ref[2 * j + 1] = jnp.where(first, kk_sw, kk).astype(BF16)
        vv = zv[:, cols]
        v_ref[2 * j] = jnp.where(first, vv, ones_col).astype(BF16)
        v_ref[2 * j + 1] = jnp.where(first, pltpu.roll(vv, HEAD_DIM, axis=1), ones_col).astype(BF16)
    for za_ref, (_, dil) in zip((za0_ref, za1_ref, za2_ref), A_GROUPS):
        z = mm(o, o + A_GROUP_COLS)
        o += A_GROUP_COLS
        if dil == 1:
            za_ref[...] = z.astype(BF16)
        else:
            for c in range(A_GROUP_COLS // LANE):
                fold_ref[c] = z[:, c * LANE:(c + 1) * LANE]
            for r in range(dil):
                for c in range(A_GROUP_COLS // LANE):
                    col = r * A_GROUP_COLS + c * LANE
                    za_ref[:, col:col + LANE] = (
                        fold_ref[c, pl.ds(r, tm // dil, stride=dil), :].astype(BF16))
    gate_ref[...] = jax.nn.sigmoid(mm(o, o + gate_ref.shape[-1])).astype(BF16)


def _projection(x2, sh, sc, g, w_ext, tabs, bsz, s):
    t, d = x2.shape
    tm = 256
    tpb = s // tm
    n_ext = w_ext.shape[1]
    n_gate = 2 * d
    row = lambda i: (i, 0)
    per_b = lambda i: (i // tpb, 0, 0)
    tab = lambda i: (i % tpb, 0)
    hm = lambda i: (i // tpb, 0, i % tpb, 0)
    out_shape = (
        [jax.ShapeDtypeStruct((bsz, s // dil, dil * A_GROUP_COLS), BF16) for _, dil in A_GROUPS]
        + [jax.ShapeDtypeStruct((bsz, B_HEADS // 2, s, PAIR), BF16),
           jax.ShapeDtypeStruct((bsz, B_KV_HEADS, s, PAIR), BF16),
           jax.ShapeDtypeStruct((bsz, B_KV_HEADS, s, PAIR), BF16),
           jax.ShapeDtypeStruct((t, n_gate), BF16)])
    out_specs = (
        [pl.BlockSpec((None, tm // dil, dil * A_GROUP_COLS), lambda i: (i // tpb, i % tpb, 0))
         for _, dil in A_GROUPS]
        + [pl.BlockSpec((None, B_HEADS // 2, tm, PAIR), hm),
           pl.BlockSpec((None, B_KV_HEADS, tm, PAIR), hm),
           pl.BlockSpec((None, B_KV_HEADS, tm, PAIR), hm),
           pl.BlockSpec((tm, n_gate), row)])
    return pl.pallas_call(
        _proj_kernel,
        out_shape=out_shape,
        grid=(t // tm,),
        in_specs=[pl.BlockSpec((tm, d), row),
                  pl.BlockSpec((None, 1, d), per_b),
                  pl.BlockSpec((None, 1, d), per_b),
                  pl.BlockSpec((1, d), lambda i: (0, 0)),
                  pl.BlockSpec((d, n_ext), lambda i: (0, 0))]
                 + [pl.BlockSpec((tm, PAIR), tab)] * 4,
        out_specs=out_specs,
        scratch_shapes=[pltpu.VMEM((A_GROUP_COLS // LANE, tm, LANE), F32)],
        compiler_params=_cparams("arbitrary"),
        name="in_proj",
    )(x2, sh, sc, g, w_ext, *tabs)


def _dilated_kernel(q_ref, kp_ref, kc_ref, kn_ref, vp_ref, vc_ref, vn_ref, bias_ref,
                    o_ref, lse_ref, *, tq, seq_len):
    i = pl.program_id(2)
    k = jnp.concatenate([kp_ref[...], kc_ref[...], kn_ref[...]], axis=0)
    v = jnp.concatenate([vp_ref[...], vc_ref[...], vn_ref[...]], axis=0)
    nk = DIL_SUB + 2 * N_SIDE
    col = lax.broadcasted_iota(jnp.int32, (DIL_SUB, nk), 1)
    for sb in range(tq // DIL_SUB):
        rows = slice(sb * DIL_SUB, (sb + 1) * DIL_SUB)
        q = q_ref[rows, :]
        kb = k[sb * DIL_SUB:sb * DIL_SUB + nk]
        vb = v[sb * DIL_SUB:sb * DIL_SUB + nk]
        kpos = i * tq + sb * DIL_SUB - N_SIDE + col
        valid = (kpos >= 0) & (kpos < seq_len)
        for hh in range(A_HEADS):
            sl = slice(hh * HEAD_DIM, (hh + 1) * HEAD_DIM)
            sc = lax.dot_general(q[:, sl], kb[:, sl], (((1,), (1,)), ((), ())),
                                 preferred_element_type=F32)
            sc = sc * (HEAD_DIM ** -0.5) + bias_ref[hh]
            sc = jnp.where(valid, sc, NEG_INF)
            m = jnp.max(sc, axis=-1, keepdims=True)
            p = jnp.exp(sc - m)
            den = jnp.sum(p, axis=-1, keepdims=True)
            o = jnp.dot(p.astype(BF16), vb[:, sl], preferred_element_type=F32) / den
            o_ref[rows, sl] = o
            lse_ref[rows, sl] = jnp.broadcast_to(m + jnp.log(den), (DIL_SUB, HEAD_DIM))


def _t5_bucket(rel):
    nb = REL_BUCKETS // 2
    max_exact = nb // 2
    ret = jnp.where(rel > 0, nb, 0)
    n = jnp.abs(rel)
    nf = jnp.maximum(n, 1).astype(F32)
    large = max_exact + (jnp.log(nf / max_exact) / math.log(REL_MAX_DIST / max_exact)
                         * (nb - max_exact)).astype(jnp.int32)
    large = jnp.minimum(large, nb - 1)
    return ret + jnp.where(n < max_exact, n, large)


def _band_bias(rel_bias_g, dilation, tq):
    nk = tq + 2 * N_SIDE
    rel = jnp.arange(nk)[None, :] - N_SIDE - jnp.arange(tq)[:, None]
    bucket = _t5_bucket(rel * dilation)[None]
    bias = jnp.full((A_HEADS,) + rel.shape, NEG_INF, F32)
    for b in range(REL_BUCKETS):
        bias = jnp.where(bucket == b, rel_bias_g[b].astype(F32)[:, None, None], bias)
    return jnp.where((jnp.abs(rel) <= N_SIDE)[None], bias, NEG_INF)


def _dilated_group(za, rel_bias_g, dilation, bsz, s):
    ll = s // dilation
    tq = min(DIL_STEP_ROWS, ll)
    assert ll % tq == 0 and tq % DIL_SUB == 0
    zv = za
    nhalf = ll // N_SIDE
    per = tq // N_SIDE
    cur = lambda c: (lambda b, r, i: (b, i, 3 * r + c))
    prv = lambda c: (lambda b, r, i: (b, jnp.maximum(per * i - 1, 0), 3 * r + c))
    nxt = lambda c: (lambda b, r, i: (b, jnp.minimum(per * (i + 1), nhalf - 1), 3 * r + c))
    full = pl.BlockSpec((None, tq, A_OUT), cur(0))
    half = lambda f: pl.BlockSpec((None, N_SIDE, A_OUT), f)
    bias = _band_bias(rel_bias_g, dilation, DIL_SUB)
    out_sd = jax.ShapeDtypeStruct((bsz, ll, dilation * A_OUT), F32)
    out_spec = pl.BlockSpec((None, tq, A_OUT), lambda b, r, i: (b, i, r))
    o, lse = pl.pallas_call(
        functools.partial(_dilated_kernel, tq=tq, seq_len=ll),
        out_shape=[out_sd, out_sd],
        grid=(bsz, dilation, ll // tq),
        in_specs=[full,
                  half(prv(1)), pl.BlockSpec((None, tq, A_OUT), cur(1)), half(nxt(1)),
                  half(prv(2)), pl.BlockSpec((None, tq, A_OUT), cur(2)), half(nxt(2)),
                  pl.BlockSpec(bias.shape, lambda b, r, i: (0, 0, 0))],
        out_specs=[out_spec, out_spec],
        compiler_params=_cparams("arbitrary", "arbitrary", "arbitrary"),
        name=f"dilated_attn_d{dilation}",
    )(zv, zv, zv, zv, zv, zv, zv, bias)
    return o, lse


GQA_NO_SHIFT_MAX_LOG2 = 80.0
GQA_KEY_CHUNK = 512


def _stacked_heads(q_ref):
    first = lax.broadcasted_iota(jnp.int32, q_ref.shape[1:], 1) < HEAD_DIM
    blocks = []
    for j in range(q_ref.shape[0]):
        qp = q_ref[j]
        blocks += [jnp.where(first, qp, jnp.zeros_like(qp)), jnp.where(first, jnp.zeros_like(qp), qp)]
    return jnp.concatenate(blocks, axis=0)


def _gqa_kernel_noshift(q_ref, k_ref, v_ref, o_ref):
    tq = q_ref.shape[1]
    q = _stacked_heads(q_ref)
    nt = (((1,), (1,)), ((), ()))
    acc = jnp.zeros((q.shape[0], PAIR), F32)
    for c in range(k_ref.shape[0] // GQA_KEY_CHUNK):
        rows = slice(c * GQA_KEY_CHUNK, (c + 1) * GQA_KEY_CHUNK)
        p = jnp.exp2(lax.dot_general(q, k_ref[rows, :], nt, preferred_element_type=F32))
        acc = acc + jnp.dot(p.astype(BF16), v_ref[rows, :], preferred_element_type=F32)
    o = acc[:, :HEAD_DIM] / acc[:, HEAD_DIM:HEAD_DIM + 1]
    for hh in range(q.shape[0] // tq):
        o_ref[:, hh * HEAD_DIM:(hh + 1) * HEAD_DIM] = o[hh * tq:(hh + 1) * tq].astype(BF16)


def _gqa_kernel_rowmax(q_ref, k_ref, v_ref, o_ref):
    tq = q_ref.shape[1]
    q = _stacked_heads(q_ref)
    k = k_ref[...]
    v = v_ref[:, :HEAD_DIM]
    for hh in range(q.shape[0] // tq):
        sc = lax.dot_general(q[hh * tq:(hh + 1) * tq], k, (((1,), (1,)), ((), ())),
                             preferred_element_type=F32)
        m = jnp.max(sc, axis=-1, keepdims=True)
        p = jnp.exp2(sc - m)
        den = jnp.sum(p, axis=-1, keepdims=True)
        o = jnp.dot(p.astype(BF16), v, preferred_element_type=F32) / den
        o_ref[:, hh * HEAD_DIM:(hh + 1) * HEAD_DIM] = o.astype(BF16)


def _gqa_call(body, name, tq, q, k, v):
    bsz, _, s, _ = q.shape
    return pl.pallas_call(
        body,
        out_shape=jax.ShapeDtypeStruct((bsz, s, B_Q), BF16),
        grid=(bsz, B_KV_HEADS, s // tq),
        in_specs=[pl.BlockSpec((None, B_GRP // 2, tq, PAIR), lambda b, h, i: (b, h, i, 0)),
                  pl.BlockSpec((None, None, s, PAIR), lambda b, h, i: (b, h, 0, 0)),
                  pl.BlockSpec((None, None, s, PAIR), lambda b, h, i: (b, h, 0, 0))],
        out_specs=pl.BlockSpec((None, tq, B_GRP * HEAD_DIM), lambda b, h, i: (b, i, h)),
        compiler_params=_cparams("arbitrary", "arbitrary", "arbitrary"),
        name=name,
    )(q, k, v)


def _gqa_attention(q, k, v, q_gain, k_gain):
    bound = (HEAD_DIM ** 0.5) * LOG2E * 1.01 * jnp.max(jnp.abs(q_gain)) * jnp.max(jnp.abs(k_gain))
    return lax.cond(bound <= GQA_NO_SHIFT_MAX_LOG2,
                    functools.partial(_gqa_call, _gqa_kernel_noshift, "gqa_attn", 512),
                    functools.partial(_gqa_call, _gqa_kernel_rowmax, "gqa_attn_rowmax", 256),
                    q, k, v)


def _merge_kernel(o0_ref, o1_ref, o2_ref, l0_ref, l1_ref, l2_ref, ob_ref, gate_ref, x_ref, g1_ref,
                  wa_ref, wb_ref, wo_ref, xo_ref, *unfold_refs):
    tm, d = x_ref.shape
    scratch = list(unfold_refs)

    def token_order(ref, dil):
        if dil == 1:
            return ref[...]
        buf = scratch.pop()
        for r in range(dil):
            for c in range(A_OUT // LANE):
                col = r * A_OUT + c * LANE
                buf[c, pl.ds(r, tm // dil, stride=dil), :] = ref[:, col:col + LANE]
        return jnp.concatenate([buf[c] for c in range(A_OUT // LANE)], axis=1)

    dils = [dil for _, dil in A_GROUPS]
    o0, o1, o2 = [token_order(r, dl) for r, dl in zip((o0_ref, o1_ref, o2_ref), dils)]
    l0, l1, l2 = [token_order(r, dl) for r, dl in zip((l0_ref, l1_ref, l2_ref), dils)]
    m = jnp.maximum(jnp.maximum(l0, l1), l2)
    e0, e1, e2 = jnp.exp(l0 - m), jnp.exp(l1 - m), jnp.exp(l2 - m)
    tot = e0 + e1 + e2
    oa = (e0 / tot) * o0 + (e1 / tot) * o1 + (e2 / tot) * o2
    ya = jnp.dot(oa.astype(BF16), wa_ref[...], preferred_element_type=F32)
    yb = jnp.dot(ob_ref[...], wb_ref[...], preferred_element_type=F32)
    merged = gate_ref[:, :d].astype(F32) * ya + gate_ref[:, d:].astype(F32) * yb
    y = jnp.dot(merged.astype(BF16), wo_ref[...], preferred_element_type=F32)
    xo_ref[...] = x_ref[...] + g1_ref[...] * y


def _merge(os_, ls_, ob, gate, x2, g1, wa, wb, wo, s):
    t, d = x2.shape
    tm = 256
    tpb = s // tm
    row = lambda i: (i, 0)
    const = lambda i: (0, 0)
    a_specs = [pl.BlockSpec((None, tm // dil, dil * A_OUT), lambda i: (i // tpb, i % tpb, 0))
               for _, dil in A_GROUPS]
    n_unfold = 2 * sum(1 for _, dil in A_GROUPS if dil > 1)
    return pl.pallas_call(
        _merge_kernel,
        out_shape=jax.ShapeDtypeStruct((t, d), F32),
        grid=(t // tm,),
        scratch_shapes=[pltpu.VMEM((A_OUT // LANE, tm, LANE), F32)] * n_unfold,
        in_specs=a_specs * 2
                 + [pl.BlockSpec((tm, B_Q), row),
                    pl.BlockSpec((tm, 2 * d), row),
                    pl.BlockSpec((tm, d), row),
                    pl.BlockSpec((None, 1, d), lambda i: (i // tpb, 0, 0)),
                    pl.BlockSpec(wa.shape, const),
                    pl.BlockSpec(wb.shape, const),
                    pl.BlockSpec(wo.shape, const)],
        out_specs=pl.BlockSpec((tm, d), row),
        compiler_params=_cparams("arbitrary"),
        name="branch_merge",
    )(*os_, *ls_, ob, gate, x2, g1, wa, wb, wo)


def _router_kernel(x_ref, sh_ref, sc_ref, g_ref, whi_ref, wlo_ref, b_ref,
                   h_ref, e_ref, w_ref, pos_ref, cnt_ref, run_ref):
    @pl.when(pl.program_id(0) == 0)
    def _():
        run_ref[...] = jnp.zeros_like(run_ref)

    x = x_ref[...]
    ms = jnp.mean(x * x, axis=-1, keepdims=True)
    h = x * lax.rsqrt(ms + EPS) * g_ref[...]
    h = h * (1.0 + sc_ref[...]) + sh_ref[...]
    h_ref[...] = h
    hhi = h.astype(BF16)
    hlo = (h - hhi.astype(F32)).astype(BF16)
    nt = (((1,), (1,)), ((), ()))
    dotf = lambda a, b: lax.dot_general(a, b, nt, preferred_element_type=F32)
    logits = dotf(whi_ref[...], hhi) + (dotf(whi_ref[...], hlo) + dotf(wlo_ref[...], hhi)) + b_ref[...]
    ne, tm = logits.shape
    iota = lax.broadcasted_iota(jnp.int32, (ne, tm), 0).astype(F32)
    vals, idxs = [], []
    cur = logits
    for _ in range(TOP_K):
        m = jnp.max(cur, axis=0, keepdims=True)
        idx = jnp.min(jnp.where(cur == m, iota, float(ne)), axis=0, keepdims=True)
        vals.append(m)
        idxs.append(idx)
        cur = jnp.where(iota == idx, -jnp.inf, cur)
    tv = jnp.concatenate(vals, axis=0)
    ex = jnp.exp(tv - tv[0:1])
    w_ref[...] = ex / jnp.sum(ex, axis=0, keepdims=True)
    e_ref[...] = jnp.concatenate(idxs, axis=0).astype(jnp.int32)
    onehot = jnp.zeros((ne, tm), F32)
    for idx in idxs:
        onehot = onehot + jnp.where(iota == idx, 1.0, 0.0)
    earlier = (lax.broadcasted_iota(jnp.int32, (tm, tm), 0)
               < lax.broadcasted_iota(jnp.int32, (tm, tm), 1))
    rank = jnp.dot(onehot.astype(BF16), jnp.where(earlier, 1.0, 0.0).astype(BF16),
                   preferred_element_type=F32) + run_ref[...]
    pos_ref[...] = jnp.concatenate(
        [jnp.sum(jnp.where(iota == idx, rank, 0.0), axis=0, keepdims=True) for idx in idxs],
        axis=0).astype(jnp.int32)
    run_ref[...] = run_ref[...] + jnp.sum(onehot, axis=1, keepdims=True)
    cnt_ref[...] = run_ref[...]


def _router(x2, sh, sc, g, w_router, b_router, s):
    t, d = x2.shape
    tm = 256
    tpb = s // tm
    ne = w_router.shape[1]
    wt = w_router.T
    whi = wt.astype(BF16)
    wlo = (wt - whi.astype(F32)).astype(BF16)
    per_b = lambda i: (i // tpb, 0, 0)
    const = lambda i: (0, 0)
    return pl.pallas_call(
        _router_kernel,
        out_shape=[jax.ShapeDtypeStruct((t, d), F32),
                   jax.ShapeDtypeStruct((TOP_K, t), jnp.int32),
                   jax.ShapeDtypeStruct((TOP_K, t), F32),
                   jax.ShapeDtypeStruct((TOP_K, t), jnp.int32),
                   jax.ShapeDtypeStruct((ne, 1), F32)],
        grid=(t // tm,),
        in_specs=[pl.BlockSpec((tm, d), lambda i: (i, 0)),
                  pl.BlockSpec((None, 1, d), per_b),
                  pl.BlockSpec((None, 1, d), per_b),
                  pl.BlockSpec((1, d), const),
                  pl.BlockSpec((ne, d), const),
                  pl.BlockSpec((ne, d), const),
                  pl.BlockSpec((ne, 1), const)],
        out_specs=[pl.BlockSpec((tm, d), lambda i: (i, 0)),
                   pl.BlockSpec((TOP_K, tm), lambda i: (0, i)),
                   pl.BlockSpec((TOP_K, tm), lambda i: (0, i)),
                   pl.BlockSpec((TOP_K, tm), lambda i: (0, i)),
                   pl.BlockSpec((ne, 1), const)],
        scratch_shapes=[pltpu.VMEM((ne, 1), F32)],
        compiler_params=_cparams("arbitrary"),
        name="moe_router",
    )(x2, sh, sc, g, whi, wlo, b_router.reshape(ne, 1))


SC_CORES = 2
SC_SUBCORES = 16
SC_ROWS = 32


def _sc_scatter_rows(rows, dest_kt, n_slots):
    t, d = rows.shape
    n_workers = SC_CORES * SC_SUBCORES
    per_w = t // n_workers
    n_chunks = per_w // SC_ROWS
    assert per_w * n_workers == t and n_chunks * SC_ROWS == per_w and n_chunks % 2 == 0
    idx = dest_kt.reshape(TOP_K, n_workers, n_chunks, SC_ROWS).transpose(1, 2, 0, 3)
    idx = idx.reshape(n_workers * n_chunks * TOP_K, SC_ROWS)
    lists_per_w = n_chunks * TOP_K
    mesh = plsc.VectorSubcoreMesh(core_axis_name="c", subcore_axis_name="s")

    @functools.partial(
        pl.kernel, mesh=mesh,
        out_type=jax.ShapeDtypeStruct((n_slots, d), rows.dtype),
        scratch_types=[pltpu.VMEM((lists_per_w, SC_ROWS), jnp.int32),
                       pltpu.VMEM((SC_ROWS, d), rows.dtype),
                       pltpu.VMEM((SC_ROWS, d), rows.dtype)] + [pltpu.SemaphoreType.DMA] * 4,
        name="sc_scatter_rows",
    )
    def scatter_kernel(rows_hbm, idx_hbm, out_hbm, idx_v, buf_a, buf_b, ld_a, ld_b, st_a, st_b):
        wid = lax.axis_index("s") * SC_CORES + lax.axis_index("c")
        pltpu.sync_copy(idx_hbm.at[pl.ds(wid * lists_per_w, lists_per_w)], idx_v)

        def load(j, buf, sem):
            return pltpu.make_async_copy(rows_hbm.at[pl.ds(wid * per_w + j * SC_ROWS, SC_ROWS)], buf, sem)

        def scatter_all(j, buf, sem):
            copies = [pltpu.make_async_copy(buf, out_hbm.at[idx_v.at[j * TOP_K + kk]], sem)
                      for kk in range(TOP_K)]
            for cp in copies:
                cp.start()
            for cp in copies:
                cp.wait()

        load(0, buf_a, ld_a).start()

        @pl.loop(0, n_chunks, step=2)
        def _(j):
            load(j + 1, buf_b, ld_b).start()
            load(j, buf_a, ld_a).wait()
            scatter_all(j, buf_a, st_a)

            @pl.when(j + 2 < n_chunks)
            def _():
                load(j + 2, buf_a, ld_a).start()
            load(j + 1, buf_b, ld_b).wait()
            scatter_all(j + 1, buf_b, st_b)

    return scatter_kernel(rows, idx)


def _expert_kernel(blk_e_ref, n_used_ref, n_valid_ref, x_ref, wgu_ref, bgu_ref, wdn_ref, bdn_ref, o_ref,
                   wgu_bf, wdn_bf):
    j = pl.program_id(0)
    d = x_ref.shape[1]

    @pl.when(j < n_used_ref[0])
    def _():
        @pl.when((j == 0) | (blk_e_ref[j] != blk_e_ref[jnp.maximum(j - 1, 0)]))
        def _():
            wgu_bf[...] = wgu_ref[...].astype(BF16)
            wdn_bf[...] = wdn_ref[...].astype(BF16)

        row = lax.broadcasted_iota(jnp.int32, x_ref.shape, 0)
        xb = jnp.where(row < n_valid_ref[j], x_ref[...], 0.0).astype(BF16)
        out = None
        for c in range(d // EXPERT_HIDDEN_CHUNK):
            lo, hi = c * EXPERT_HIDDEN_CHUNK, (c + 1) * EXPERT_HIDDEN_CHUNK
            x_glu = jnp.dot(xb, wgu_bf[:, lo:hi], preferred_element_type=F32) + bgu_ref[:, lo:hi]
            x_lin = (jnp.dot(xb, wgu_bf[:, d + lo:d + hi], preferred_element_type=F32)
                     + bgu_ref[:, d + lo:d + hi])
            x_glu = jnp.minimum(x_glu, SWIGLU_LIMIT)
            x_lin = jnp.clip(x_lin, -SWIGLU_LIMIT, SWIGLU_LIMIT)
            act = x_glu * jax.nn.sigmoid(SWIGLU_ALPHA * x_glu) * (x_lin + 1.0)
            part = jnp.dot(act.astype(BF16), wdn_bf[lo:hi, :], preferred_element_type=F32)
            out = part if out is None else out + part
        o_ref[...] = out + bdn_ref[...]

    @pl.when(j >= n_used_ref[0])
    def _():
        o_ref[...] = jnp.zeros_like(o_ref)


def _experts(xs, blk_e, n_used, n_valid, wgu, bgu, wdn, bdn, layer):
    d = xs.shape[1]
    n_blocks = blk_e.shape[0]
    depth, ne = wgu.shape[:2]
    by_expert = lambda j, be, nu, nv: (layer, be[j], 0, 0)
    grid_spec = pltpu.PrefetchScalarGridSpec(
        num_scalar_prefetch=3,
        grid=(n_blocks,),
        in_specs=[pl.BlockSpec((MOE_BLOCK, d), lambda j, be, nu, nv: (jnp.minimum(j, nu[0] - 1), 0)),
                  pl.BlockSpec((None, None, d, 2 * d), by_expert),
                  pl.BlockSpec((None, None, 1, 2 * d), by_expert),
                  pl.BlockSpec((None, None, d, d), by_expert),
                  pl.BlockSpec((None, None, 1, d), by_expert)],
        out_specs=pl.BlockSpec((MOE_BLOCK, d), lambda j, be, nu, nv: (j, 0)),
        scratch_shapes=[pltpu.VMEM((d, 2 * d), BF16), pltpu.VMEM((d, d), BF16)],
    )
    return pl.pallas_call(
        _expert_kernel,
        out_shape=jax.ShapeDtypeStruct((n_blocks * MOE_BLOCK, d), F32),
        grid_spec=grid_spec,
        compiler_params=_cparams("arbitrary"),
        name="moe_experts",
    )(blk_e, n_used, n_valid, xs, wgu, bgu.reshape(depth, ne, 1, 2 * d), wdn, bdn.reshape(depth, ne, 1, d))


def _sc_gather_rows(table, idx):
    n_out = idx.shape[0]
    d = table.shape[1]
    n_workers = SC_CORES * SC_SUBCORES
    per_w = n_out // n_workers
    n_chunks = per_w // SC_ROWS
    assert per_w * n_workers == n_out and n_chunks * SC_ROWS == per_w and n_chunks % 2 == 0
    mesh = plsc.VectorSubcoreMesh(core_axis_name="c", subcore_axis_name="s")

    @functools.partial(
        pl.kernel, mesh=mesh,
        out_type=jax.ShapeDtypeStruct((n_out, d), table.dtype),
        scratch_types=[pltpu.VMEM((per_w,), jnp.int32),
                       pltpu.VMEM((SC_ROWS, d), table.dtype),
                       pltpu.VMEM((SC_ROWS, d), table.dtype),
                       pltpu.SemaphoreType.DMA, pltpu.SemaphoreType.DMA],
        name="sc_gather_rows",
    )
    def gather_kernel(table_hbm, idx_hbm, out_hbm, idx_v, rows_a, rows_b, sem_a, sem_b):
        base = (lax.axis_index("s") * SC_CORES + lax.axis_index("c")) * per_w
        pltpu.sync_copy(idx_hbm.at[pl.ds(base, per_w)], idx_v)

        def gather(j, buf, sem):
            rows = idx_v.at[pl.ds(j * SC_ROWS, SC_ROWS)]
            return pltpu.make_async_copy(table_hbm.at[rows], buf, sem)

        def write_back(j, buf):
            pltpu.sync_copy(buf, out_hbm.at[pl.ds(base + j * SC_ROWS, SC_ROWS)])

        gather(0, rows_a, sem_a).start()

        @pl.loop(0, n_chunks, step=2)
        def _(j):
            gather(j + 1, rows_b, sem_b).start()
            gather(j, rows_a, sem_a).wait()
            write_back(j, rows_a)

            @pl.when(j + 2 < n_chunks)
            def _():
                gather(j + 2, rows_a, sem_a).start()
            gather(j + 1, rows_b, sem_b).wait()
            write_back(j + 1, rows_b)

    return gather_kernel(table, idx)


def _combine_kernel(y0_ref, y1_ref, y2_ref, y3_ref, w_ref, x_ref, g2_ref, fg_ref, xo_ref, *, final):
    w = w_ref[...]
    y = w[:, 0:1] * y0_ref[...]
    for kk, y_ref in enumerate((y1_ref, y2_ref, y3_ref), start=1):
        y = y + w[:, kk:kk + 1] * y_ref[...]
    xn = x_ref[...] + g2_ref[...] * y
    if final:
        ms = jnp.mean(xn * xn, axis=-1, keepdims=True)
        xn = xn * lax.rsqrt(ms + EPS) * fg_ref[...]
    xo_ref[...] = xn


def _combine(dest_kt, out_sorted, gates_t, x2, g2, final_g, s, final):
    t, d = x2.shape
    tm = 256
    tpb = s // tm
    nt = t // tm
    yg = _sc_gather_rows(out_sorted, dest_kt)
    plane = lambda kk: pl.BlockSpec((tm, d), lambda i: (kk * nt + i, 0))
    return pl.pallas_call(
        functools.partial(_combine_kernel, final=final),
        out_shape=jax.ShapeDtypeStruct((t, d), F32),
        grid=(nt,),
        in_specs=[plane(0), plane(1), plane(2), plane(3),
                  pl.BlockSpec((tm, TOP_K), lambda i: (i, 0)),
                  pl.BlockSpec((tm, d), lambda i: (i, 0)),
                  pl.BlockSpec((None, 1, d), lambda i: (i // tpb, 0, 0)),
                  pl.BlockSpec((1, d), lambda i: (0, 0))],
        out_specs=pl.BlockSpec((tm, d), lambda i: (i, 0)),
        compiler_params=_cparams("arbitrary"),
        name="moe_combine",
    )(yg, yg, yg, yg, gates_t, x2, g2, final_g)


def _moe_plan(top_e, pos, counts):
    t = top_e.shape[1]
    sizes = counts[:, 0].astype(jnp.int32)
    padded = (sizes + MOE_BLOCK - 1) // MOE_BLOCK * MOE_BLOCK
    pad_end = jnp.cumsum(padded)
    pad_start = pad_end - padded
    start_of = jnp.zeros_like(top_e)
    for e in range(N_EXPERTS):
        start_of = jnp.where(top_e == e, pad_start[e], start_of)
    dest_kt = (start_of + pos).reshape(TOP_K * t)
    n_blocks = -(-t * TOP_K // MOE_BLOCK) + N_EXPERTS
    blk_start = jnp.arange(n_blocks, dtype=jnp.int32) * MOE_BLOCK
    blk_e = jnp.minimum(jnp.sum((pad_end[None, :] <= blk_start[:, None]).astype(jnp.int32), axis=1),
                        N_EXPERTS - 1).astype(jnp.int32)
    n_used = (pad_end[-1] // MOE_BLOCK).astype(jnp.int32).reshape(1)
    filled_end = pad_start + sizes
    n_valid = jnp.zeros((n_blocks,), jnp.int32)
    for e in range(N_EXPERTS):
        n_valid = jnp.where(blk_e == e, jnp.clip(filled_end[e] - blk_start, 0, MOE_BLOCK), n_valid)
    return dest_kt, blk_e, n_used, n_valid, n_blocks


def _rope_tables(s, gain):
    n_rows = s // GRID_W
    row = jnp.repeat(jnp.arange(n_rows), GRID_W).astype(F32)
    col = (jnp.arange(s) % GRID_W).astype(F32)
    half = HEAD_DIM // 2
    inv = ROPE_THETA ** (-jnp.arange(0, half, 2, dtype=F32) / half)
    ang_r = row[:, None] * inv
    ang_c = col[:, None] * inv
    cos = jnp.concatenate([jnp.cos(ang_r)] * 2 + [jnp.cos(ang_c)] * 2, axis=-1)
    sin = jnp.concatenate([-jnp.sin(ang_r), jnp.sin(ang_r), -jnp.sin(ang_c), jnp.sin(ang_c)], axis=-1)
    gain = gain.astype(F32)
    return cos * gain[None, :], sin * gain[_PARTNER][None, :]


_q = HEAD_DIM // 4
_PARTNER = np.concatenate([np.arange(_q, 2 * _q), np.arange(0, _q),
                           np.arange(3 * _q, 4 * _q), np.arange(2 * _q, 3 * _q)])


def _partner_cols(w, n_heads):
    idx = (np.arange(n_heads)[:, None] * HEAD_DIM + _PARTNER[None, :]).reshape(-1)
    return w[:, idx]


def kernel(x, c, w_ada, b_ada, norm1_g, w_in, q_norm_g, k_norm_g, rel_bias, w_br_a, w_br_b, w_out,
           norm2_g, w_router, b_router, w_gate_up, b_gate_up, w_down, b_down, final_norm_g):
    bsz, s, d = x.shape
    depth = w_ada.shape[0]
    t = bsz * s
    mod = _ada_mod(c, w_ada, b_ada)
    x2 = x.reshape(t, d)
    q_off = 3 * A_WIDTH
    k_off = q_off + B_Q
    for l in range(depth):
        sh1, sc1, g1, sh2, sc2, g2 = [mod[l, :, i * d:(i + 1) * d].reshape(bsz, 1, d)
                                      for i in range(N_MOD)]
        w = w_in[l]
        wq, wk = w[:, q_off:q_off + B_Q], w[:, k_off:k_off + B_KV]
        w_ext = jnp.concatenate([wq, _partner_cols(wq, B_HEADS), wk, _partner_cols(wk, B_KV_HEADS),
                                 w[:, k_off + B_KV:k_off + 2 * B_KV], w[:, :q_off],
                                 w[:, k_off + 2 * B_KV:]], axis=1).astype(BF16)
        tabs = [jnp.tile(tb, (1, 2)) for tb in
                _rope_tables(s, q_norm_g[l]) + _rope_tables(s, k_norm_g[l])]
        za0, za1, za2, q, k, v, gate = _projection(x2, sh1, sc1, norm1_g[l].reshape(1, d), w_ext,
                                                   tabs, bsz, s)
        os_, ls_ = [], []
        for g, (za, (_, dilation)) in enumerate(zip((za0, za1, za2), A_GROUPS)):
            o, lse = _dilated_group(za, rel_bias[:, g * A_HEADS:(g + 1) * A_HEADS], dilation, bsz, s)
            os_.append(o)
            ls_.append(lse)
        ob = _gqa_attention(q, k, v, q_norm_g[l], k_norm_g[l]).reshape(t, B_Q)
        x2 = _merge(os_, ls_, ob, gate, x2, g1, w_br_a[l].astype(BF16), w_br_b[l].astype(BF16),
                    w_out[l].astype(BF16), s)
        h2, top_e, gates, pos, counts = _router(x2, sh2, sc2, norm2_g[l].reshape(1, d), w_router[l],
                                                b_router[l], s)
        dest_kt, blk_e, n_used, n_valid, n_blocks = _moe_plan(top_e, pos, counts)
        xs = _sc_scatter_rows(h2, dest_kt, n_blocks * MOE_BLOCK)
        out_sorted = _experts(xs, blk_e, n_used, n_valid, w_gate_up, b_gate_up, w_down, b_down, l)
        x2 = _combine(dest_kt, out_sorted, gates.T, x2, g2, final_norm_g.reshape(1, d), s,
                      final=(l == depth - 1))
    return x2.reshape(bsz, s, d)
```

```python
import functools
import math

import numpy as np
import jax
import jax.numpy as jnp
from jax import lax
from jax.experimental import pallas as pl
from jax.experimental.pallas import tpu as pltpu
from jax.experimental.pallas import tpu_sc as plsc

F32 = jnp.float32
BF16 = jnp.bfloat16

HEAD_DIM = 64
LANE = 128
PAIR = 2 * HEAD_DIM
A_GROUPS = ((128, 1), (512, 4), (2048, 16))
A_HEADS = 4
A_WIDTH = len(A_GROUPS) * A_HEADS * HEAD_DIM
A_OUT = A_HEADS * HEAD_DIM
A_GROUP_COLS = 3 * A_OUT
B_HEADS = 16
B_KV_HEADS = 4
B_GRP = B_HEADS // B_KV_HEADS
B_Q = B_HEADS * HEAD_DIM
B_KV = B_KV_HEADS * HEAD_DIM
GRID_W = 64
ROPE_THETA = 10000.0
REL_BUCKETS = 32
REL_MAX_DIST = 1024
N_EXPERTS = 32
TOP_K = 4
SWIGLU_LIMIT = 7.0
SWIGLU_ALPHA = 1.702
MOE_BLOCK = 512
N_MOD = 6
N_STREAMS = 2
EPS = 1e-6
NEG_INF = -1e30
LOG2E = math.log2(math.e)
N_SIDE = 64
DIL_SUB = 2 * N_SIDE
DIL_STEP_ROWS = 512

VMEM_LIMIT = 56 * 1024 * 1024


def _cparams(*sem):
    return pltpu.CompilerParams(dimension_semantics=sem, vmem_limit_bytes=VMEM_LIMIT)


def _ada_kernel(c_ref, w_ref, b_ref, o_ref):
    c = c_ref[...]
    ca = (c * jax.nn.sigmoid(c)).astype(BF16)
    o_ref[...] = jnp.dot(ca, w_ref[...].astype(BF16), preferred_element_type=F32) + b_ref[...]


def _ada_mod(c, w_ada, b_ada):
    depth, d, n = w_ada.shape
    bsz = c.shape[0]
    tn = 1536
    return pl.pallas_call(
        _ada_kernel,
        out_shape=jax.ShapeDtypeStruct((depth, bsz, n), F32),
        grid=(depth, n // tn),
        in_specs=[pl.BlockSpec((bsz, d), lambda l, j: (0, 0)),
                  pl.BlockSpec((None, d, tn), lambda l, j: (l, 0, j)),
                  pl.BlockSpec((None, 1, tn), lambda l, j: (l, 0, j))],
        out_specs=pl.BlockSpec((None, bsz, tn), lambda l, j: (l, 0, j)),
        compiler_params=_cparams("arbitrary", "arbitrary"),
        name="ada_mod",
    )(c, w_ada, b_ada.reshape(depth, 1, n))


def _proj_kernel(x_ref, sh_ref, sc_ref, g_ref, w_ref, aq_ref, bq_ref, ak_ref, bk_ref,
                 za0_ref, za1_ref, za2_ref, q_ref, k_ref, v_ref, gate_ref, fold_ref):
    x = x_ref[...]
    ms = jnp.mean(x * x, axis=-1, keepdims=True)
    h = x * lax.rsqrt(ms + EPS) * g_ref[...]
    h = h * (1.0 + sc_ref[...]) + sh_ref[...]
    hb = h.astype(BF16)

    def mm(lo, hi):
        return jnp.dot(hb, w_ref[:, lo:hi], preferred_element_type=F32)

    tm = x.shape[0]
    o = 0
    zq = mm(o, o + B_Q); o += B_Q
    zqp = mm(o, o + B_Q); o += B_Q
    zk = mm(o, o + B_KV); o += B_KV
    zkp = mm(o, o + B_KV); o += B_KV
    zv = mm(o, o + B_KV); o += B_KV
    lane = lax.broadcasted_iota(jnp.int32, (tm, PAIR), 1)
    first = lane < HEAD_DIM

    def pair_norm_rope(z, zp, a, b):
        zz = z * z
        ss = jnp.where(first,
                       jnp.sum(jnp.where(first, zz, 0.0), axis=-1, keepdims=True),
                       jnp.sum(jnp.where(first, 0.0, zz), axis=-1, keepdims=True))
        return lax.rsqrt(ss * (1.0 / HEAD_DIM) + EPS) * (z * a + zp * b)

    aq, bq, ak, bk = aq_ref[...], bq_ref[...], ak_ref[...], bk_ref[...]
    for j in range(B_HEADS // 2):
        cols = slice(j * PAIR, (j + 1) * PAIR)
        q_ref[j] = (pair_norm_rope(zq[:, cols], zqp[:, cols], aq, bq)
                    * (LOG2E * HEAD_DIM ** -0.5)).astype(BF16)
    ones_col = jnp.where(lane == HEAD_DIM, 1.0, 0.0)
    for j in range(B_KV_HEADS // 2):
        cols = slice(j * PAIR, (j + 1) * PAIR)
        kk = pair_norm_rope(zk[:, cols], zkp[:, cols], ak, bk)
        kk_sw = pltpu.roll(kk, HEAD_DIM, axis=1)
        k_ref[2 * j] = jnp.where(first, kk, kk_sw).astype(BF16)
        k_ref[2 * j + 1] = jnp.where(first, kk_sw, kk).astype(BF16)
        vv = zv[:, cols]
        v_ref[2 * j] = jnp.where(first, vv, ones_col).astype(BF16)
        v_ref[2 * j + 1] = jnp.where(first, pltpu.roll(vv, HEAD_DIM, axis=1), ones_col).astype(BF16)
    for za_ref, (_, dil) in zip((za0_ref, za1_ref, za2_ref), A_GROUPS):
        z = mm(o, o + A_GROUP_COLS)
        o += A_GROUP_COLS
        if dil == 1:
            za_ref[...] = z.astype(BF16)
        else:
            for c in range(A_GROUP_COLS // LANE):
                fold_ref[c] = z[:, c * LANE:(c + 1) * LANE]
            for r in range(dil):
                for c in range(A_GROUP_COLS // LANE):
                    col = r * A_GROUP_COLS + c * LANE
                    za_ref[:, col:col + LANE] = (
                        fold_ref[c, pl.ds(r, tm // dil, stride=dil), :].astype(BF16))
    gate_ref[...] = jax.nn.sigmoid(mm(o, o + gate_ref.shape[-1])).astype(BF16)


def _projection(x2, sh, sc, g, w_ext, tabs, bsz, s):
    t, d = x2.shape
    tm = 256
    tpb = s // tm
    n_ext = w_ext.shape[1]
    n_gate = 2 * d
    row = lambda i: (i, 0)
    per_b = lambda i: (i // tpb, 0, 0)
    tab = lambda i: (i % tpb, 0)
    hm = lambda i: (i // tpb, 0, i % tpb, 0)
    out_shape = (
        [jax.ShapeDtypeStruct((bsz, s // dil, dil * A_GROUP_COLS), BF16) for _, dil in A_GROUPS]
        + [jax.ShapeDtypeStruct((bsz, B_HEADS // 2, s, PAIR), BF16),
           jax.ShapeDtypeStruct((bsz, B_KV_HEADS, s, PAIR), BF16),
           jax.ShapeDtypeStruct((bsz, B_KV_HEADS, s, PAIR), BF16),
           jax.ShapeDtypeStruct((t, n_gate), BF16)])
    out_specs = (
        [pl.BlockSpec((None, tm // dil, dil * A_GROUP_COLS), lambda i: (i // tpb, i % tpb, 0))
         for _, dil in A_GROUPS]
        + [pl.BlockSpec((None, B_HEADS // 2, tm, PAIR), hm),
           pl.BlockSpec((None, B_KV_HEADS, tm, PAIR), hm),
           pl.BlockSpec((None, B_KV_HEADS, tm, PAIR), hm),
           pl.BlockSpec((tm, n_gate), row)])
    return pl.pallas_call(
        _proj_kernel,
        out_shape=out_shape,
        grid=(t // tm,),
        in_specs=[pl.BlockSpec((tm, d), row),
                  pl.BlockSpec((None, 1, d), per_b),
                  pl.BlockSpec((None, 1, d), per_b),
                  pl.BlockSpec((1, d), lambda i: (0, 0)),
                  pl.BlockSpec((d, n_ext), lambda i: (0, 0))]
                 + [pl.BlockSpec((tm, PAIR), tab)] * 4,
        out_specs=out_specs,
        scratch_shapes=[pltpu.VMEM((A_GROUP_COLS // LANE, tm, LANE), F32)],
        compiler_params=_cparams("arbitrary"),
        name="in_proj",
    )(x2, sh, sc, g, w_ext, *tabs)


def _dilated_kernel(q_ref, kp_ref, kc_ref, kn_ref, vp_ref, vc_ref, vn_ref, bias_ref,
                    o_ref, lse_ref, *, tq, seq_len):
    i = pl.program_id(2)
    k = jnp.concatenate([kp_ref[...], kc_ref[...], kn_ref[...]], axis=0)
    v = jnp.concatenate([vp_ref[...], vc_ref[...], vn_ref[...]], axis=0)
    nk = DIL_SUB + 2 * N_SIDE
    col = lax.broadcasted_iota(jnp.int32, (DIL_SUB, nk), 1)
    for sb in range(tq // DIL_SUB):
        rows = slice(sb * DIL_SUB, (sb + 1) * DIL_SUB)
        q = q_ref[rows, :]
        kb = k[sb * DIL_SUB:sb * DIL_SUB + nk]
        vb = v[sb * DIL_SUB:sb * DIL_SUB + nk]
        kpos = i * tq + sb * DIL_SUB - N_SIDE + col
        valid = (kpos >= 0) & (kpos < seq_len)
        for hh in range(A_HEADS):
            sl = slice(hh * HEAD_DIM, (hh + 1) * HEAD_DIM)
            sc = lax.dot_general(q[:, sl], kb[:, sl], (((1,), (1,)), ((), ())),
                                 preferred_element_type=F32)
            sc = sc * (HEAD_DIM ** -0.5) + bias_ref[hh]
            sc = jnp.where(valid, sc, NEG_INF)
            m = jnp.max(sc, axis=-1, keepdims=True)
            p = jnp.exp(sc - m)
            den = jnp.sum(p, axis=-1, keepdims=True)
            o = jnp.dot(p.astype(BF16), vb[:, sl], preferred_element_type=F32) / den
            o_ref[rows, sl] = o
            lse_ref[rows, sl] = jnp.broadcast_to(m + jnp.log(den), (DIL_SUB, HEAD_DIM))


def _t5_bucket(rel):
    nb = REL_BUCKETS // 2
    max_exact = nb // 2
    ret = jnp.where(rel > 0, nb, 0)
    n = jnp.abs(rel)
    nf = jnp.maximum(n, 1).astype(F32)
    large = max_exact + (jnp.log(nf / max_exact) / math.log(REL_MAX_DIST / max_exact)
                         * (nb - max_exact)).astype(jnp.int32)
    large = jnp.minimum(large, nb - 1)
    return ret + jnp.where(n < max_exact, n, large)


def _band_bias(rel_bias_g, dilation, tq):
    nk = tq + 2 * N_SIDE
    rel = jnp.arange(nk)[None, :] - N_SIDE - jnp.arange(tq)[:, None]
    bucket = _t5_bucket(rel * dilation)[None]
    bias = jnp.full((A_HEADS,) + rel.shape, NEG_INF, F32)
    for b in range(REL_BUCKETS):
        bias = jnp.where(bucket == b, rel_bias_g[b].astype(F32)[:, None, None], bias)
    return jnp.where((jnp.abs(rel) <= N_SIDE)[None], bias, NEG_INF)


def _dilated_group(za, rel_bias_g, dilation, bsz, s):
    ll = s // dilation
    tq = min(DIL_STEP_ROWS, ll)
    assert ll % tq == 0 and tq % DIL_SUB == 0
    zv = za
    nhalf = ll // N_SIDE
    per = tq // N_SIDE
    cur = lambda c: (lambda b, r, i: (b, i, 3 * r + c))
    prv = lambda c: (lambda b, r, i: (b, jnp.maximum(per * i - 1, 0), 3 * r + c))
    nxt = lambda c: (lambda b, r, i: (b, jnp.minimum(per * (i + 1), nhalf - 1), 3 * r + c))
    full = pl.BlockSpec((None, tq, A_OUT), cur(0))
    half = lambda f: pl.BlockSpec((None, N_SIDE, A_OUT), f)
    bias = _band_bias(rel_bias_g, dilation, DIL_SUB)
    out_sd = jax.ShapeDtypeStruct((bsz, ll, dilation * A_OUT), F32)
    out_spec = pl.BlockSpec((None, tq, A_OUT), lambda b, r, i: (b, i, r))
    o, lse = pl.pallas_call(
        functools.partial(_dilated_kernel, tq=tq, seq_len=ll),
        out_shape=[out_sd, out_sd],
        grid=(bsz, dilation, ll // tq),
        in_specs=[full,
                  half(prv(1)), pl.BlockSpec((None, tq, A_OUT), cur(1)), half(nxt(1)),
                  half(prv(2)), pl.BlockSpec((None, tq, A_OUT), cur(2)), half(nxt(2)),
                  pl.BlockSpec(bias.shape, lambda b, r, i: (0, 0, 0))],
        out_specs=[out_spec, out_spec],
        compiler_params=_cparams("arbitrary", "arbitrary", "arbitrary"),
        name=f"dilated_attn_d{dilation}",
    )(zv, zv, zv, zv, zv, zv, zv, bias)
    return o, lse


GQA_NO_SHIFT_MAX_LOG2 = 80.0
GQA_KEY_CHUNK = 512


def _stacked_heads(q_ref):
    first = lax.broadcasted_iota(jnp.int32, q_ref.shape[1:], 1) < HEAD_DIM
    blocks = []
    for j in range(q_ref.shape[0]):
        qp = q_ref[j]
        blocks += [jnp.where(first, qp, jnp.zeros_like(qp)), jnp.where(first, jnp.zeros_like(qp), qp)]
    return jnp.concatenate(blocks, axis=0)


def _gqa_kernel_noshift(q_ref, k_ref, v_ref, o_ref):
    tq = q_ref.shape[1]
    q = _stacked_heads(q_ref)
    nt = (((1,), (1,)), ((), ()))
    acc = jnp.zeros((q.shape[0], PAIR), F32)
    for c in range(k_ref.shape[0] // GQA_KEY_CHUNK):
        rows = slice(c * GQA_KEY_CHUNK, (c + 1) * GQA_KEY_CHUNK)
        p = jnp.exp2(lax.dot_general(q, k_ref[rows, :], nt, preferred_element_type=F32))
        acc = acc + jnp.dot(p.astype(BF16), v_ref[rows, :], preferred_element_type=F32)
    o = acc[:, :HEAD_DIM] / acc[:, HEAD_DIM:HEAD_DIM + 1]
    for hh in range(q.shape[0] // tq):
        o_ref[:, hh * HEAD_DIM:(hh + 1) * HEAD_DIM] = o[hh * tq:(hh + 1) * tq].astype(BF16)


def _gqa_kernel_rowmax(q_ref, k_ref, v_ref, o_ref):
    tq = q_ref.shape[1]
    q = _stacked_heads(q_ref)
    k = k_ref[...]
    v = v_ref[:, :HEAD_DIM]
    for hh in range(q.shape[0] // tq):
        sc = lax.dot_general(q[hh * tq:(hh + 1) * tq], k, (((1,), (1,)), ((), ())),
                             preferred_element_type=F32)
        m = jnp.max(sc, axis=-1, keepdims=True)
        p = jnp.exp2(sc - m)
        den = jnp.sum(p, axis=-1, keepdims=True)
        o = jnp.dot(p.astype(BF16), v, preferred_element_type=F32) / den
        o_ref[:, hh * HEAD_DIM:(hh + 1) * HEAD_DIM] = o.astype(BF16)


def _gqa_call(body, name, tq, q, k, v):
    bsz, _, s, _ = q.shape
    return pl.pallas_call(
        body,
        out_shape=jax.ShapeDtypeStruct((bsz, s, B_Q), BF16),
        grid=(bsz, B_KV_HEADS, s // tq),
        in_specs=[pl.BlockSpec((None, B_GRP // 2, tq, PAIR), lambda b, h, i: (b, h, i, 0)),
                  pl.BlockSpec((None, None, s, PAIR), lambda b, h, i: (b, h, 0, 0)),
                  pl.BlockSpec((None, None, s, PAIR), lambda b, h, i: (b, h, 0, 0))],
        out_specs=pl.BlockSpec((None, tq, B_GRP * HEAD_DIM), lambda b, h, i: (b, i, h)),
        compiler_params=_cparams("arbitrary", "arbitrary", "arbitrary"),
        name=name,
    )(q, k, v)


def _gqa_attention(q, k, v, q_gain, k_gain):
    bound = (HEAD_DIM ** 0.5) * LOG2E * 1.01 * jnp.max(jnp.abs(q_gain)) * jnp.max(jnp.abs(k_gain))
    return lax.cond(bound <= GQA_NO_SHIFT_MAX_LOG2,
                    functools.partial(_gqa_call, _gqa_kernel_noshift, "gqa_attn", 512),
                    functools.partial(_gqa_call, _gqa_kernel_rowmax, "gqa_attn_rowmax", 256),
                    q, k, v)


def _merge_kernel(o0_ref, o1_ref, o2_ref, l0_ref, l1_ref, l2_ref, ob_ref, gate_ref, x_ref, g1_ref,
                  wa_ref, wb_ref, wo_ref, xo_ref, *unfold_refs):
    tm, d = x_ref.shape
    scratch = list(unfold_refs)

    def token_order(ref, dil):
        if dil == 1:
            return ref[...]
        buf = scratch.pop()
        for r in range(dil):
            for c in range(A_OUT // LANE):
                col = r * A_OUT + c * LANE
                buf[c, pl.ds(r, tm // dil, stride=dil), :] = ref[:, col:col + LANE]
        return jnp.concatenate([buf[c] for c in range(A_OUT // LANE)], axis=1)

    dils = [dil for _, dil in A_GROUPS]
    o0, o1, o2 = [token_order(r, dl) for r, dl in zip((o0_ref, o1_ref, o2_ref), dils)]
    l0, l1, l2 = [token_order(r, dl) for r, dl in zip((l0_ref, l1_ref, l2_ref), dils)]
    m = jnp.maximum(jnp.maximum(l0, l1), l2)
    e0, e1, e2 = jnp.exp(l0 - m), jnp.exp(l1 - m), jnp.exp(l2 - m)
    tot = e0 + e1 + e2
    oa = (e0 / tot) * o0 + (e1 / tot) * o1 + (e2 / tot) * o2
    ya = jnp.dot(oa.astype(BF16), wa_ref[...], preferred_element_type=F32)
    yb = jnp.dot(ob_ref[...], wb_ref[...], preferred_element_type=F32)
    merged = gate_ref[:, :d].astype(F32) * ya + gate_ref[:, d:].astype(F32) * yb
    y = jnp.dot(merged.astype(BF16), wo_ref[...], preferred_element_type=F32)
    xo_ref[...] = x_ref[...] + g1_ref[...] * y


def _merge(os_, ls_, ob, gate, x2, g1, wa, wb, wo, s):
    t, d = x2.shape
    tm = 256
    tpb = s // tm
    row = lambda i: (i, 0)
    const = lambda i: (0, 0)
    a_specs = [pl.BlockSpec((None, tm // dil, dil * A_OUT), lambda i: (i // tpb, i % tpb, 0))
               for _, dil in A_GROUPS]
    n_unfold = 2 * sum(1 for _, dil in A_GROUPS if dil > 1)
    return pl.pallas_call(
        _merge_kernel,
        out_shape=jax.ShapeDtypeStruct((t, d), F32),
        grid=(t // tm,),
        scratch_shapes=[pltpu.VMEM((A_OUT // LANE, tm, LANE), F32)] * n_unfold,
        in_specs=a_specs * 2
                 + [pl.BlockSpec((tm, B_Q), row),
                    pl.BlockSpec((tm, 2 * d), row),
                    pl.BlockSpec((tm, d), row),
                    pl.BlockSpec((None, 1, d), lambda i: (i // tpb, 0, 0)),
                    pl.BlockSpec(wa.shape, const),
                    pl.BlockSpec(wb.shape, const),
                    pl.BlockSpec(wo.shape, const)],
        out_specs=pl.BlockSpec((tm, d), row),
        compiler_params=_cparams("arbitrary"),
        name="branch_merge",
    )(*os_, *ls_, ob, gate, x2, g1, wa, wb, wo)


def _router_kernel(x_ref, sh_ref, sc_ref, g_ref, whi_ref, wlo_ref, b_ref,
                   h_ref, e_ref, w_ref, pos_ref, cnt_ref, run_ref):
    @pl.when(pl.program_id(0) == 0)
    def _():
        run_ref[...] = jnp.zeros_like(run_ref)

    x = x_ref[...]
    ms = jnp.mean(x * x, axis=-1, keepdims=True)
    h = x * lax.rsqrt(ms + EPS) * g_ref[...]
    h = h * (1.0 + sc_ref[...]) + sh_ref[...]
    h_ref[...] = h
    hhi = h.astype(BF16)
    hlo = (h - hhi.astype(F32)).astype(BF16)
    nt = (((1,), (1,)), ((), ()))
    dotf = lambda a, b: lax.dot_general(a, b, nt, preferred_element_type=F32)
    logits = dotf(whi_ref[...], hhi) + (dotf(whi_ref[...], hlo) + dotf(wlo_ref[...], hhi)) + b_ref[...]
    ne, tm = logits.shape
    iota = lax.broadcasted_iota(jnp.int32, (ne, tm), 0).astype(F32)
    vals, idxs = [], []
    cur = logits
    for _ in range(TOP_K):
        m = jnp.max(cur, axis=0, keepdims=True)
        idx = jnp.min(jnp.where(cur == m, iota, float(ne)), axis=0, keepdims=True)
        vals.append(m)
        idxs.append(idx)
        cur = jnp.where(iota == idx, -jnp.inf, cur)
    tv = jnp.concatenate(vals, axis=0)
    ex = jnp.exp(tv - tv[0:1])
    w_ref[...] = ex / jnp.sum(ex, axis=0, keepdims=True)
    e_ref[...] = jnp.concatenate(idxs, axis=0).astype(jnp.int32)
    onehot = jnp.zeros((ne, tm), F32)
    for idx in idxs:
        onehot = onehot + jnp.where(iota == idx, 1.0, 0.0)
    earlier = (lax.broadcasted_iota(jnp.int32, (tm, tm), 0)
               < lax.broadcasted_iota(jnp.int32, (tm, tm), 1))
    rank = jnp.dot(onehot.astype(BF16), jnp.where(earlier, 1.0, 0.0).astype(BF16),
                   preferred_element_type=F32) + run_ref[...]
    pos_ref[...] = jnp.concatenate(
        [jnp.sum(jnp.where(iota == idx, rank, 0.0), axis=0, keepdims=True) for idx in idxs],
        axis=0).astype(jnp.int32)
    run_ref[...] = run_ref[...] + jnp.sum(onehot, axis=1, keepdims=True)
    cnt_ref[...] = run_ref[...]


def _router(x2, sh, sc, g, w_router, b_router, s):
    t, d = x2.shape
    tm = 256
    tpb = s // tm
    ne = w_router.shape[1]
    wt = w_router.T
    whi = wt.astype(BF16)
    wlo = (wt - whi.astype(F32)).astype(BF16)
    per_b = lambda i: (i // tpb, 0, 0)
    const = lambda i: (0, 0)
    return pl.pallas_call(
        _router_kernel,
        out_shape=[jax.ShapeDtypeStruct((t, d), F32),
                   jax.ShapeDtypeStruct((TOP_K, t), jnp.int32),
                   jax.ShapeDtypeStruct((TOP_K, t), F32),
                   jax.ShapeDtypeStruct((TOP_K, t), jnp.int32),
                   jax.ShapeDtypeStruct((ne, 1), F32)],
        grid=(t // tm,),
        in_specs=[pl.BlockSpec((tm, d), lambda i: (i, 0)),
                  pl.BlockSpec((None, 1, d), per_b),
                  pl.BlockSpec((None, 1, d), per_b),
                  pl.BlockSpec((1, d), const),
                  pl.BlockSpec((ne, d), const),
                  pl.BlockSpec((ne, d), const),
                  pl.BlockSpec((ne, 1), const)],
        out_specs=[pl.BlockSpec((tm, d), lambda i: (i, 0)),
                   pl.BlockSpec((TOP_K, tm), lambda i: (0, i)),
                   pl.BlockSpec((TOP_K, tm), lambda i: (0, i)),
                   pl.BlockSpec((TOP_K, tm), lambda i: (0, i)),
                   pl.BlockSpec((ne, 1), const)],
        scratch_shapes=[pltpu.VMEM((ne, 1), F32)],
        compiler_params=_cparams("arbitrary"),
        name="moe_router",
    )(x2, sh, sc, g, whi, wlo, b_router.reshape(ne, 1))


SC_CORES = 2
SC_SUBCORES = 16
SC_ROWS = 32


def _sc_scatter_rows(rows, dest_kt, n_slots):
    t, d = rows.shape
    n_workers = SC_CORES * SC_SUBCORES
    per_w = t // n_workers
    n_chunks = per_w // SC_ROWS
    assert per_w * n_workers == t and n_chunks * SC_ROWS == per_w and n_chunks % 2 == 0
    idx = dest_kt.reshape(TOP_K, n_workers, n_chunks, SC_ROWS).transpose(1, 2, 0, 3)
    idx = idx.reshape(n_workers * n_chunks * TOP_K, SC_ROWS)
    lists_per_w = n_chunks * TOP_K
    mesh = plsc.VectorSubcoreMesh(core_axis_name="c", subcore_axis_name="s")

    @functools.partial(
        pl.kernel, mesh=mesh,
        out_type=jax.ShapeDtypeStruct((n_slots, d), rows.dtype),
        scratch_types=[pltpu.VMEM((lists_per_w, SC_ROWS), jnp.int32),
                       pltpu.VMEM((SC_ROWS, d), rows.dtype),
                       pltpu.VMEM((SC_ROWS, d), rows.dtype)] + [pltpu.SemaphoreType.DMA] * 4,
        name="sc_scatter_rows",
    )
    def scatter_kernel(rows_hbm, idx_hbm, out_hbm, idx_v, buf_a, buf_b, ld_a, ld_b, st_a, st_b):
        wid = lax.axis_index("s") * SC_CORES + lax.axis_index("c")
        pltpu.sync_copy(idx_hbm.at[pl.ds(wid * lists_per_w, lists_per_w)], idx_v)

        def load(j, buf, sem):
            return pltpu.make_async_copy(rows_hbm.at[pl.ds(wid * per_w + j * SC_ROWS, SC_ROWS)], buf, sem)

        def scatter_all(j, buf, sem):
            copies = [pltpu.make_async_copy(buf, out_hbm.at[idx_v.at[j * TOP_K + kk]], sem)
                      for kk in range(TOP_K)]
            for cp in copies:
                cp.start()
            for cp in copies:
                cp.wait()

        load(0, buf_a, ld_a).start()

        @pl.loop(0, n_chunks, step=2)
        def _(j):
            load(j + 1, buf_b, ld_b).start()
            load(j, buf_a, ld_a).wait()
            scatter_all(j, buf_a, st_a)

            @pl.when(j + 2 < n_chunks)
            def _():
                load(j + 2, buf_a, ld_a).start()
            load(j + 1, buf_b, ld_b).wait()
            scatter_all(j + 1, buf_b, st_b)

    return scatter_kernel(rows, idx)


def _expert_kernel(blk_e_ref, n_used_ref, n_valid_ref, x_ref, wgu_ref, bgu_ref, wdn_ref, bdn_ref, o_ref,
                   wgu_bf, wdn_bf):
    j = pl.program_id(0)
    d = x_ref.shape[1]

    @pl.when(j < n_used_ref[0])
    def _():
        @pl.when((j == 0) | (blk_e_ref[j] != blk_e_ref[jnp.maximum(j - 1, 0)]))
        def _():
            wgu_bf[...] = wgu_ref[...].astype(BF16)
            wdn_bf[...] = wdn_ref[...].astype(BF16)

        row = lax.broadcasted_iota(jnp.int32, x_ref.shape, 0)
        xb = jnp.where(row < n_valid_ref[j], x_ref[...], 0.0).astype(BF16)
        gu = jnp.dot(xb, wgu_bf[...], preferred_element_type=F32) + bgu_ref[...]
        x_glu = jnp.minimum(gu[:, :d], SWIGLU_LIMIT)
        x_lin = jnp.clip(gu[:, d:], -SWIGLU_LIMIT, SWIGLU_LIMIT)
        act = x_glu * jax.nn.sigmoid(SWIGLU_ALPHA * x_glu) * (x_lin + 1.0)
        o_ref[...] = jnp.dot(act.astype(BF16), wdn_bf[...], preferred_element_type=F32) + bdn_ref[...]

    @pl.when(j >= n_used_ref[0])
    def _():
        o_ref[...] = jnp.zeros_like(o_ref)


def _experts(xs, blk_e, n_used, n_valid, wgu, bgu, wdn, bdn, layer):
    d = xs.shape[1]
    n_blocks = blk_e.shape[0]
    depth, ne = wgu.shape[:2]
    by_expert = lambda j, be, nu, nv: (layer, be[j], 0, 0)
    grid_spec = pltpu.PrefetchScalarGridSpec(
        num_scalar_prefetch=3,
        grid=(n_blocks,),
        in_specs=[pl.BlockSpec((MOE_BLOCK, d), lambda j, be, nu, nv: (jnp.minimum(j, nu[0] - 1), 0)),
                  pl.BlockSpec((None, None, d, 2 * d), by_expert),
                  pl.BlockSpec((None, None, 1, 2 * d), by_expert),
                  pl.BlockSpec((None, None, d, d), by_expert),
                  pl.BlockSpec((None, None, 1, d), by_expert)],
        out_specs=pl.BlockSpec((MOE_BLOCK, d), lambda j, be, nu, nv: (j, 0)),
        scratch_shapes=[pltpu.VMEM((d, 2 * d), BF16), pltpu.VMEM((d, d), BF16)],
    )
    return pl.pallas_call(
        _expert_kernel,
        out_shape=jax.ShapeDtypeStruct((n_blocks * MOE_BLOCK, d), F32),
        grid_spec=grid_spec,
        compiler_params=_cparams("arbitrary"),
        name="moe_experts",
    )(blk_e, n_used, n_valid, xs, wgu, bgu.reshape(depth, ne, 1, 2 * d), wdn, bdn.reshape(depth, ne, 1, d))


def _sc_gather_rows(table, idx):
    n_out = idx.shape[0]
    d = table.shape[1]
    n_workers = SC_CORES * SC_SUBCORES
    per_w = n_out // n_workers
    n_chunks = per_w // SC_ROWS
    assert per_w * n_workers == n_out and n_chunks * SC_ROWS == per_w and n_chunks % 2 == 0
    mesh = plsc.VectorSubcoreMesh(core_axis_name="c", subcore_axis_name="s")

    @functools.partial(
        pl.kernel, mesh=mesh,
        out_type=jax.ShapeDtypeStruct((n_out, d), table.dtype),
        scratch_types=[pltpu.VMEM((per_w,), jnp.int32),
                       pltpu.VMEM((SC_ROWS, d), table.dtype),
                       pltpu.VMEM((SC_ROWS, d), table.dtype),
                       pltpu.SemaphoreType.DMA, pltpu.SemaphoreType.DMA],
        name="sc_gather_rows",
    )
    def gather_kernel(table_hbm, idx_hbm, out_hbm, idx_v, rows_a, rows_b, sem_a, sem_b):
        base = (lax.axis_index("s") * SC_CORES + lax.axis_index("c")) * per_w
        pltpu.sync_copy(idx_hbm.at[pl.ds(base, per_w)], idx_v)

        def gather(j, buf, sem):
            rows = idx_v.at[pl.ds(j * SC_ROWS, SC_ROWS)]
            return pltpu.make_async_copy(table_hbm.at[rows], buf, sem)

        def write_back(j, buf):
            pltpu.sync_copy(buf, out_hbm.at[pl.ds(base + j * SC_ROWS, SC_ROWS)])

        gather(0, rows_a, sem_a).start()

        @pl.loop(0, n_chunks, step=2)
        def _(j):
            gather(j + 1, rows_b, sem_b).start()
            gather(j, rows_a, sem_a).wait()
            write_back(j, rows_a)

            @pl.when(j + 2 < n_chunks)
            def _():
                gather(j + 2, rows_a, sem_a).start()
            gather(j + 1, rows_b, sem_b).wait()
            write_back(j + 1, rows_b)

    return gather_kernel(table, idx)


def _combine_kernel(y0_ref, y1_ref, y2_ref, y3_ref, w_ref, x_ref, g2_ref, fg_ref, xo_ref, *, final):
    w = w_ref[...]
    y = w[:, 0:1] * y0_ref[...]
    for kk, y_ref in enumerate((y1_ref, y2_ref, y3_ref), start=1):
        y = y + w[:, kk:kk + 1] * y_ref[...]
    xn = x_ref[...] + g2_ref[...] * y
    if final:
        ms = jnp.mean(xn * xn, axis=-1, keepdims=True)
        xn = xn * lax.rsqrt(ms + EPS) * fg_ref[...]
    xo_ref[...] = xn


def _combine(dest_kt, out_sorted, gates_t, x2, g2, final_g, s, final):
    t, d = x2.shape
    tm = 256
    tpb = s // tm
    nt = t // tm
    yg = _sc_gather_rows(out_sorted, dest_kt)
    plane = lambda kk: pl.BlockSpec((tm, d), lambda i: (kk * nt + i, 0))
    return pl.pallas_call(
        functools.partial(_combine_kernel, final=final),
        out_shape=jax.ShapeDtypeStruct((t, d), F32),
        grid=(nt,),
        in_specs=[plane(0), plane(1), plane(2), plane(3),
                  pl.BlockSpec((tm, TOP_K), lambda i: (i, 0)),
                  pl.BlockSpec((tm, d), lambda i: (i, 0)),
                  pl.BlockSpec((None, 1, d), lambda i: (i // tpb, 0, 0)),
                  pl.BlockSpec((1, d), lambda i: (0, 0))],
        out_specs=pl.BlockSpec((tm, d), lambda i: (i, 0)),
        compiler_params=_cparams("arbitrary"),
        name="moe_combine",
    )(yg, yg, yg, yg, gates_t, x2, g2, final_g)


def _moe_plan(top_e, pos, counts):
    t = top_e.shape[1]
    sizes = counts[:, 0].astype(jnp.int32)
    padded = (sizes + MOE_BLOCK - 1) // MOE_BLOCK * MOE_BLOCK
    pad_end = jnp.cumsum(padded)
    pad_start = pad_end - padded
    start_of = jnp.zeros_like(top_e)
    for e in range(N_EXPERTS):
        start_of = jnp.where(top_e == e, pad_start[e], start_of)
    dest_kt = (start_of + pos).reshape(TOP_K * t)
    n_blocks = -(-t * TOP_K // MOE_BLOCK) + N_EXPERTS
    blk_start = jnp.arange(n_blocks, dtype=jnp.int32) * MOE_BLOCK
    blk_e = jnp.minimum(jnp.sum((pad_end[None, :] <= blk_start[:, None]).astype(jnp.int32), axis=1),
                        N_EXPERTS - 1).astype(jnp.int32)
    n_used = (pad_end[-1] // MOE_BLOCK).astype(jnp.int32).reshape(1)
    filled_end = pad_start + sizes
    n_valid = jnp.zeros((n_blocks,), jnp.int32)
    for e in range(N_EXPERTS):
        n_valid = jnp.where(blk_e == e, jnp.clip(filled_end[e] - blk_start, 0, MOE_BLOCK), n_valid)
    return dest_kt, blk_e, n_used, n_valid, n_blocks


def _rope_tables(s, gain):
    n_rows = s // GRID_W
    row = jnp.repeat(jnp.arange(n_rows), GRID_W).astype(F32)
    col = (jnp.arange(s) % GRID_W).astype(F32)
    half = HEAD_DIM // 2
    inv = ROPE_THETA ** (-jnp.arange(0, half, 2, dtype=F32) / half)
    ang_r = row[:, None] * inv
    ang_c = col[:, None] * inv
    cos = jnp.concatenate([jnp.cos(ang_r)] * 2 + [jnp.cos(ang_c)] * 2, axis=-1)
    sin = jnp.concatenate([-jnp.sin(ang_r), jnp.sin(ang_r), -jnp.sin(ang_c), jnp.sin(ang_c)], axis=-1)
    gain = gain.astype(F32)
    return cos * gain[None, :], sin * gain[_PARTNER][None, :]


_q = HEAD_DIM // 4
_PARTNER = np.concatenate([np.arange(_q, 2 * _q), np.arange(0, _q),
                           np.arange(3 * _q, 4 * _q), np.arange(2 * _q, 3 * _q)])


def _partner_cols(w, n_heads):
    idx = (np.arange(n_heads)[:, None] * HEAD_DIM + _PARTNER[None, :]).reshape(-1)
    return w[:, idx]


def kernel(x, c, w_ada, b_ada, norm1_g, w_in, q_norm_g, k_norm_g, rel_bias, w_br_a, w_br_b, w_out,
           norm2_g, w_router, b_router, w_gate_up, b_gate_up, w_down, b_down, final_norm_g):
    bsz, s, d = x.shape
    depth = w_ada.shape[0]
    mod = _ada_mod(c, w_ada, b_ada)
    n_streams = N_STREAMS if bsz % N_STREAMS == 0 else 1
    hb = bsz // n_streams
    ht = hb * s
    streams = range(n_streams)
    xs2 = [x[i * hb:(i + 1) * hb].reshape(ht, d) for i in streams]
    q_off = 3 * A_WIDTH
    k_off = q_off + B_Q
    for l in range(depth):
        mods = [[mod[l, i * hb:(i + 1) * hb, j * d:(j + 1) * d].reshape(hb, 1, d) for j in range(N_MOD)]
                for i in streams]
        w = w_in[l]
        wq, wk = w[:, q_off:q_off + B_Q], w[:, k_off:k_off + B_KV]
        w_ext = jnp.concatenate([wq, _partner_cols(wq, B_HEADS), wk, _partner_cols(wk, B_KV_HEADS),
                                 w[:, k_off + B_KV:k_off + 2 * B_KV], w[:, :q_off],
                                 w[:, k_off + 2 * B_KV:]], axis=1).astype(BF16)
        tabs = [jnp.tile(tb, (1, 2)) for tb in
                _rope_tables(s, q_norm_g[l]) + _rope_tables(s, k_norm_g[l])]
        wa, wb, wo = w_br_a[l].astype(BF16), w_br_b[l].astype(BF16), w_out[l].astype(BF16)
        proj = [_projection(xs2[i], mods[i][0], mods[i][1], norm1_g[l].reshape(1, d), w_ext, tabs, hb, s)
                for i in streams]
        attn_a = [[_dilated_group(proj[i][g], rel_bias[:, g * A_HEADS:(g + 1) * A_HEADS], dil, hb, s)
                   for g, (_, dil) in enumerate(A_GROUPS)] for i in streams]
        attn_b = [_gqa_attention(proj[i][3], proj[i][4], proj[i][5], q_norm_g[l], k_norm_g[l])
                  .reshape(ht, B_Q) for i in streams]
        xs2 = [_merge([o for o, _ in attn_a[i]], [ls for _, ls in attn_a[i]], attn_b[i], proj[i][6],
                      xs2[i], mods[i][2], wa, wb, wo, s) for i in streams]
        routed = [_router(xs2[i], mods[i][3], mods[i][4], norm2_g[l].reshape(1, d), w_router[l],
                          b_router[l], s) for i in streams]
        plans = [_moe_plan(routed[i][1], routed[i][3], routed[i][4]) for i in streams]
        slots = [_sc_scatter_rows(routed[i][0], plans[i][0], plans[i][4] * MOE_BLOCK) for i in streams]
        outs = [_experts(slots[i], plans[i][1], plans[i][2], plans[i][3],
                         w_gate_up, b_gate_up, w_down, b_down, l) for i in streams]
        xs2 = [_combine(plans[i][0], outs[i], routed[i][2].T, xs2[i], mods[i][5],
                        final_norm_g.reshape(1, d), s, final=(l == depth - 1)) for i in streams]
    return jnp.concatenate([xh.reshape(hb, s, d) for xh in xs2], axis=0)
```

```python
import functools
import math

import numpy as np
import jax
import jax.numpy as jnp
from jax import lax
from jax.experimental import pallas as pl
from jax.experimental.pallas import tpu as pltpu
from jax.experimental.pallas import tpu_sc as plsc

F32 = jnp.float32
BF16 = jnp.bfloat16

HEAD_DIM = 64
LANE = 128
PAIR = 2 * HEAD_DIM
A_GROUPS = ((128, 1), (512, 4), (2048, 16))
A_HEADS = 4
A_WIDTH = len(A_GROUPS) * A_HEADS * HEAD_DIM
A_OUT = A_HEADS * HEAD_DIM
A_GROUP_COLS = 3 * A_OUT
B_HEADS = 16
B_KV_HEADS = 4
B_GRP = B_HEADS // B_KV_HEADS
B_Q = B_HEADS * HEAD_DIM
B_KV = B_KV_HEADS * HEAD_DIM
GRID_W = 64
ROPE_THETA = 10000.0
REL_BUCKETS = 32
REL_MAX_DIST = 1024
N_EXPERTS = 32
TOP_K = 4
SWIGLU_LIMIT = 7.0
SWIGLU_ALPHA = 1.702
MOE_BLOCK = 512
N_MOD = 6
N_STREAMS = 1
EPS = 1e-6
NEG_INF = -1e30
LOG2E = math.log2(math.e)
N_SIDE = 64
DIL_SUB = 2 * N_SIDE
DIL_STEP_ROWS = 512

VMEM_LIMIT = 56 * 1024 * 1024


def _cparams(*sem):
    return pltpu.CompilerParams(dimension_semantics=sem, vmem_limit_bytes=VMEM_LIMIT)


def _ada_kernel(c_ref, w_ref, b_ref, o_ref):
    c = c_ref[...]
    ca = (c * jax.nn.sigmoid(c)).astype(BF16)
    o_ref[...] = jnp.dot(ca, w_ref[...].astype(BF16), preferred_element_type=F32) + b_ref[...]


def _ada_mod(c, w_ada, b_ada):
    depth, d, n = w_ada.shape
    bsz = c.shape[0]
    tn = 1536
    return pl.pallas_call(
        _ada_kernel,
        out_shape=jax.ShapeDtypeStruct((depth, bsz, n), F32),
        grid=(depth, n // tn),
        in_specs=[pl.BlockSpec((bsz, d), lambda l, j: (0, 0)),
                  pl.BlockSpec((None, d, tn), lambda l, j: (l, 0, j)),
                  pl.BlockSpec((None, 1, tn), lambda l, j: (l, 0, j))],
        out_specs=pl.BlockSpec((None, bsz, tn), lambda l, j: (l, 0, j)),
        compiler_params=_cparams("arbitrary", "arbitrary"),
        name="ada_mod",
    )(c, w_ada, b_ada.reshape(depth, 1, n))


def _proj_kernel(x_ref, sh_ref, sc_ref, g_ref, w_ref, aq_ref, bq_ref, ak_ref, bk_ref,
                 za0_ref, za1_ref, za2_ref, q_ref, k_ref, v_ref, gate_ref, fold_ref):
    x = x_ref[...]
    ms = jnp.mean(x * x, axis=-1, keepdims=True)
    h = x * lax.rsqrt(ms + EPS) * g_ref[...]
    h = h * (1.0 + sc_ref[...]) + sh_ref[...]
    hb = h.astype(BF16)

    def mm(lo, hi):
        return jnp.dot(hb, w_ref[:, lo:hi], preferred_element_type=F32)

    tm = x.shape[0]
    o = 0
    zq = mm(o, o + B_Q); o += B_Q
    zqp = mm(o, o + B_Q); o += B_Q
    zk = mm(o, o + B_KV); o += B_KV
    zkp = mm(o, o + B_KV); o += B_KV
    zv = mm(o, o + B_KV); o += B_KV
    lane = lax.broadcasted_iota(jnp.int32, (tm, PAIR), 1)
    first = lane < HEAD_DIM

    def pair_norm_rope(z, zp, a, b):
        zz = z * z
        ss = jnp.where(first,
                       jnp.sum(jnp.where(first, zz, 0.0), axis=-1, keepdims=True),
                       jnp.sum(jnp.where(first, 0.0, zz), axis=-1, keepdims=True))
        return lax.rsqrt(ss * (1.0 / HEAD_DIM) + EPS) * (z * a + zp * b)

    aq, bq, ak, bk = aq_ref[...], bq_ref[...], ak_ref[...], bk_ref[...]
    for j in range(B_HEADS // 2):
        cols = slice(j * PAIR, (j + 1) * PAIR)
        q_ref[j] = (pair_norm_rope(zq[:, cols], zqp[:, cols], aq, bq)
                    * (LOG2E * HEAD_DIM ** -0.5)).astype(BF16)
    ones_col = jnp.where(lane == HEAD_DIM, 1.0, 0.0)
    for j in range(B_KV_HEADS // 2):
        cols = slice(j * PAIR, (j + 1) * PAIR)
        kk = pair_norm_rope(zk[:, cols], zkp[:, cols], ak, bk)
        kk_sw = pltpu.roll(kk, HEAD_DIM, axis=1)
        k_ref[2 * j] = jnp.where(first, kk, kk_sw).astype(BF16)
        k_ref[2 * j + 1] = jnp.where(first, kk_sw, kk).astype(BF16)
        vv = zv[:, cols]
        v_ref[2 * j] = jnp.where(first, vv, ones_col).astype(BF16)
        v_ref[2 * j + 1] = jnp.where(first, pltpu.roll(vv, HEAD_DIM, axis=1), ones_col).astype(BF16)
    for za_ref, (_, dil) in zip((za0_ref, za1_ref, za2_ref), A_GROUPS):
        z = mm(o, o + A_GROUP_COLS)
        o += A_GROUP_COLS
        if dil == 1:
            za_ref[...] = z.astype(BF16)
        else:
            for c in range(A_GROUP_COLS // LANE):
                fold_ref[c] = z[:, c * LANE:(c + 1) * LANE]
            for r in range(dil):
                for c in range(A_GROUP_COLS // LANE):
                    col = r * A_GROUP_COLS + c * LANE
                    za_ref[:, col:col + LANE] = (
                        fold_ref[c, pl.ds(r, tm // dil, stride=dil), :].astype(BF16))
    gate_ref[...] = jax.nn.sigmoid(mm(o, o + gate_ref.shape[-1])).astype(BF16)


def _projection(x2, sh, sc, g, w_ext, tabs, bsz, s):
    t, d = x2.shape
    tm = 256
    tpb = s // tm
    n_ext = w_ext.shape[1]
    n_gate = 2 * d
    row = lambda i: (i, 0)
    per_b = lambda i: (i // tpb, 0, 0)
    tab = lambda i: (i % tpb, 0)
    hm = lambda i: (i // tpb, 0, i % tpb, 0)
    out_shape = (
        [jax.ShapeDtypeStruct((bsz, s // dil, dil * A_GROUP_COLS), BF16) for _, dil in A_GROUPS]
        + [jax.ShapeDtypeStruct((bsz, B_HEADS // 2, s, PAIR), BF16),
           jax.ShapeDtypeStruct((bsz, B_KV_HEADS, s, PAIR), BF16),
           jax.ShapeDtypeStruct((bsz, B_KV_HEADS, s, PAIR), BF16),
           jax.ShapeDtypeStruct((t, n_gate), BF16)])
    out_specs = (
        [pl.BlockSpec((None, tm // dil, dil * A_GROUP_COLS), lambda i: (i // tpb, i % tpb, 0))
         for _, dil in A_GROUPS]
        + [pl.BlockSpec((None, B_HEADS // 2, tm, PAIR), hm),
           pl.BlockSpec((None, B_KV_HEADS, tm, PAIR), hm),
           pl.BlockSpec((None, B_KV_HEADS, tm, PAIR), hm),
           pl.BlockSpec((tm, n_gate), row)])
    return pl.pallas_call(
        _proj_kernel,
        out_shape=out_shape,
        grid=(t // tm,),
        in_specs=[pl.BlockSpec((tm, d), row),
                  pl.BlockSpec((None, 1, d), per_b),
                  pl.BlockSpec((None, 1, d), per_b),
                  pl.BlockSpec((1, d), lambda i: (0, 0)),
                  pl.BlockSpec((d, n_ext), lambda i: (0, 0))]
                 + [pl.BlockSpec((tm, PAIR), tab)] * 4,
        out_specs=out_specs,
        scratch_shapes=[pltpu.VMEM((A_GROUP_COLS // LANE, tm, LANE), F32)],
        compiler_params=_cparams("arbitrary"),
        name="in_proj",
    )(x2, sh, sc, g, w_ext, *tabs)


def _dilated_kernel(q_ref, kp_ref, kc_ref, kn_ref, vp_ref, vc_ref, vn_ref, bias_ref,
                    o_ref, lse_ref, *, tq, seq_len):
    i = pl.program_id(2)
    k = jnp.concatenate([kp_ref[...], kc_ref[...], kn_ref[...]], axis=0)
    v = jnp.concatenate([vp_ref[...], vc_ref[...], vn_ref[...]], axis=0)
    nk = DIL_SUB + 2 * N_SIDE
    col = lax.broadcasted_iota(jnp.int32, (DIL_SUB, nk), 1)
    for sb in range(tq // DIL_SUB):
        rows = slice(sb * DIL_SUB, (sb + 1) * DIL_SUB)
        q = q_ref[rows, :]
        kb = k[sb * DIL_SUB:sb * DIL_SUB + nk]
        vb = v[sb * DIL_SUB:sb * DIL_SUB + nk]
        kpos = i * tq + sb * DIL_SUB - N_SIDE + col
        valid = (kpos >= 0) & (kpos < seq_len)
        for hh in range(A_HEADS):
            sl = slice(hh * HEAD_DIM, (hh + 1) * HEAD_DIM)
            sc = lax.dot_general(q[:, sl], kb[:, sl], (((1,), (1,)), ((), ())),
                                 preferred_element_type=F32)
            sc = sc * (HEAD_DIM ** -0.5) + bias_ref[hh]
            sc = jnp.where(valid, sc, NEG_INF)
            m = jnp.max(sc, axis=-1, keepdims=True)
            p = jnp.exp(sc - m)
            den = jnp.sum(p, axis=-1, keepdims=True)
            o = jnp.dot(p.astype(BF16), vb[:, sl], preferred_element_type=F32) / den
            o_ref[rows, sl] = o
            lse_ref[rows, sl] = jnp.broadcast_to(m + jnp.log(den), (DIL_SUB, HEAD_DIM))


def _t5_bucket(rel):
    nb = REL_BUCKETS // 2
    max_exact = nb // 2
    ret = jnp.where(rel > 0, nb, 0)
    n = jnp.abs(rel)
    nf = jnp.maximum(n, 1).astype(F32)
    large = max_exact + (jnp.log(nf / max_exact) / math.log(REL_MAX_DIST / max_exact)
                         * (nb - max_exact)).astype(jnp.int32)
    large = jnp.minimum(large, nb - 1)
    return ret + jnp.where(n < max_exact, n, large)


def _band_bias(rel_bias_g, dilation, tq):
    nk = tq + 2 * N_SIDE
    rel = jnp.arange(nk)[None, :] - N_SIDE - jnp.arange(tq)[:, None]
    bucket = _t5_bucket(rel * dilation)[None]
    bias = jnp.full((A_HEADS,) + rel.shape, NEG_INF, F32)
    for b in range(REL_BUCKETS):
        bias = jnp.where(bucket == b, rel_bias_g[b].astype(F32)[:, None, None], bias)
    return jnp.where((jnp.abs(rel) <= N_SIDE)[None], bias, NEG_INF)


def _dilated_group(za, rel_bias_g, dilation, bsz, s):
    ll = s // dilation
    tq = min(DIL_STEP_ROWS, ll)
    assert ll % tq == 0 and tq % DIL_SUB == 0
    zv = za
    nhalf = ll // N_SIDE
    per = tq // N_SIDE
    cur = lambda c: (lambda b, r, i: (b, i, 3 * r + c))
    prv = lambda c: (lambda b, r, i: (b, jnp.maximum(per * i - 1, 0), 3 * r + c))
    nxt = lambda c: (lambda b, r, i: (b, jnp.minimum(per * (i + 1), nhalf - 1), 3 * r + c))
    full = pl.BlockSpec((None, tq, A_OUT), cur(0))
    half = lambda f: pl.BlockSpec((None, N_SIDE, A_OUT), f)
    bias = _band_bias(rel_bias_g, dilation, DIL_SUB)
    out_sd = jax.ShapeDtypeStruct((bsz, ll, dilation * A_OUT), F32)
    out_spec = pl.BlockSpec((None, tq, A_OUT), lambda b, r, i: (b, i, r))
    o, lse = pl.pallas_call(
        functools.partial(_dilated_kernel, tq=tq, seq_len=ll),
        out_shape=[out_sd, out_sd],
        grid=(bsz, dilation, ll // tq),
        in_specs=[full,
                  half(prv(1)), pl.BlockSpec((None, tq, A_OUT), cur(1)), half(nxt(1)),
                  half(prv(2)), pl.BlockSpec((None, tq, A_OUT), cur(2)), half(nxt(2)),
                  pl.BlockSpec(bias.shape, lambda b, r, i: (0, 0, 0))],
        out_specs=[out_spec, out_spec],
        compiler_params=_cparams("arbitrary", "arbitrary", "arbitrary"),
        name=f"dilated_attn_d{dilation}",
    )(zv, zv, zv, zv, zv, zv, zv, bias)
    return o, lse


GQA_NO_SHIFT_MAX_LOG2 = 80.0
GQA_KEY_CHUNK = 128


def _stacked_heads(q_ref):
    first = lax.broadcasted_iota(jnp.int32, q_ref.shape[1:], 1) < HEAD_DIM
    blocks = []
    for j in range(q_ref.shape[0]):
        qp = q_ref[j]
        blocks += [jnp.where(first, qp, jnp.zeros_like(qp)), jnp.where(first, jnp.zeros_like(qp), qp)]
    return jnp.concatenate(blocks, axis=0)


def _gqa_kernel_noshift(q_ref, k_ref, v_ref, o_ref):
    tq = q_ref.shape[1]
    q = _stacked_heads(q_ref)
    nt = (((1,), (1,)), ((), ()))
    acc = jnp.zeros((q.shape[0], PAIR), F32)
    for c in range(k_ref.shape[0] // GQA_KEY_CHUNK):
        rows = slice(c * GQA_KEY_CHUNK, (c + 1) * GQA_KEY_CHUNK)
        p = jnp.exp2(lax.dot_general(q, k_ref[rows, :], nt, preferred_element_type=F32))
        acc = acc + jnp.dot(p.astype(BF16), v_ref[rows, :], preferred_element_type=F32)
    o = acc[:, :HEAD_DIM] / acc[:, HEAD_DIM:HEAD_DIM + 1]
    for hh in range(q.shape[0] // tq):
        o_ref[:, hh * HEAD_DIM:(hh + 1) * HEAD_DIM] = o[hh * tq:(hh + 1) * tq].astype(BF16)


def _gqa_kernel_rowmax(q_ref, k_ref, v_ref, o_ref):
    tq = q_ref.shape[1]
    q = _stacked_heads(q_ref)
    k = k_ref[...]
    v = v_ref[:, :HEAD_DIM]
    for hh in range(q.shape[0] // tq):
        sc = lax.dot_general(q[hh * tq:(hh + 1) * tq], k, (((1,), (1,)), ((), ())),
                             preferred_element_type=F32)
        m = jnp.max(sc, axis=-1, keepdims=True)
        p = jnp.exp2(sc - m)
        den = jnp.sum(p, axis=-1, keepdims=True)
        o = jnp.dot(p.astype(BF16), v, preferred_element_type=F32) / den
        o_ref[:, hh * HEAD_DIM:(hh + 1) * HEAD_DIM] = o.astype(BF16)


def _gqa_call(body, name, tq, q, k, v):
    bsz, _, s, _ = q.shape
    return pl.pallas_call(
        body,
        out_shape=jax.ShapeDtypeStruct((bsz, s, B_Q), BF16),
        grid=(bsz, B_KV_HEADS, s // tq),
        in_specs=[pl.BlockSpec((None, B_GRP // 2, tq, PAIR), lambda b, h, i: (b, h, i, 0)),
                  pl.BlockSpec((None, None, s, PAIR), lambda b, h, i: (b, h, 0, 0)),
                  pl.BlockSpec((None, None, s, PAIR), lambda b, h, i: (b, h, 0, 0))],
        out_specs=pl.BlockSpec((None, tq, B_GRP * HEAD_DIM), lambda b, h, i: (b, i, h)),
        compiler_params=_cparams("arbitrary", "arbitrary", "arbitrary"),
        name=name,
    )(q, k, v)


def _gqa_attention(q, k, v, q_gain, k_gain):
    bound = (HEAD_DIM ** 0.5) * LOG2E * 1.01 * jnp.max(jnp.abs(q_gain)) * jnp.max(jnp.abs(k_gain))
    return lax.cond(bound <= GQA_NO_SHIFT_MAX_LOG2,
                    functools.partial(_gqa_call, _gqa_kernel_noshift, "gqa_attn", 1024),
                    functools.partial(_gqa_call, _gqa_kernel_rowmax, "gqa_attn_rowmax", 256),
                    q, k, v)


def _merge_kernel(o0_ref, o1_ref, o2_ref, l0_ref, l1_ref, l2_ref, ob_ref, gate_ref, x_ref, g1_ref,
                  wa_ref, wb_ref, wo_ref, xo_ref, *unfold_refs):
    tm, d = x_ref.shape
    scratch = list(unfold_refs)

    def token_order(ref, dil):
        if dil == 1:
            return ref[...]
        buf = scratch.pop()
        for r in range(dil):
            for c in range(A_OUT // LANE):
                col = r * A_OUT + c * LANE
                buf[c, pl.ds(r, tm // dil, stride=dil), :] = ref[:, col:col + LANE]
        return jnp.concatenate([buf[c] for c in range(A_OUT // LANE)], axis=1)

    dils = [dil for _, dil in A_GROUPS]
    o0, o1, o2 = [token_order(r, dl) for r, dl in zip((o0_ref, o1_ref, o2_ref), dils)]
    l0, l1, l2 = [token_order(r, dl) for r, dl in zip((l0_ref, l1_ref, l2_ref), dils)]
    m = jnp.maximum(jnp.maximum(l0, l1), l2)
    e0, e1, e2 = jnp.exp(l0 - m), jnp.exp(l1 - m), jnp.exp(l2 - m)
    tot = e0 + e1 + e2
    oa = (e0 / tot) * o0 + (e1 / tot) * o1 + (e2 / tot) * o2
    ya = jnp.dot(oa.astype(BF16), wa_ref[...], preferred_element_type=F32)
    yb = jnp.dot(ob_ref[...], wb_ref[...], preferred_element_type=F32)
    merged = gate_ref[:, :d].astype(F32) * ya + gate_ref[:, d:].astype(F32) * yb
    y = jnp.dot(merged.astype(BF16), wo_ref[...], preferred_element_type=F32)
    xo_ref[...] = x_ref[...] + g1_ref[...] * y


def _merge(os_, ls_, ob, gate, x2, g1, wa, wb, wo, s):
    t, d = x2.shape
    tm = 256
    tpb = s // tm
    row = lambda i: (i, 0)
    const = lambda i: (0, 0)
    a_specs = [pl.BlockSpec((None, tm // dil, dil * A_OUT), lambda i: (i // tpb, i % tpb, 0))
               for _, dil in A_GROUPS]
    n_unfold = 2 * sum(1 for _, dil in A_GROUPS if dil > 1)
    return pl.pallas_call(
        _merge_kernel,
        out_shape=jax.ShapeDtypeStruct((t, d), F32),
        grid=(t // tm,),
        scratch_shapes=[pltpu.VMEM((A_OUT // LANE, tm, LANE), F32)] * n_unfold,
        in_specs=a_specs * 2
                 + [pl.BlockSpec((tm, B_Q), row),
                    pl.BlockSpec((tm, 2 * d), row),
                    pl.BlockSpec((tm, d), row),
                    pl.BlockSpec((None, 1, d), lambda i: (i // tpb, 0, 0)),
                    pl.BlockSpec(wa.shape, const),
                    pl.BlockSpec(wb.shape, const),
                    pl.BlockSpec(wo.shape, const)],
        out_specs=pl.BlockSpec((tm, d), row),
        compiler_params=_cparams("arbitrary"),
        name="branch_merge",
    )(*os_, *ls_, ob, gate, x2, g1, wa, wb, wo)


def _router_kernel(x_ref, sh_ref, sc_ref, g_ref, whi_ref, wlo_ref, b_ref,
                   h_ref, e_ref, w_ref, pos_ref, cnt_ref, run_ref):
    @pl.when(pl.program_id(0) == 0)
    def _():
        run_ref[...] = jnp.zeros_like(run_ref)

    x = x_ref[...]
    ms = jnp.mean(x * x, axis=-1, keepdims=True)
    h = x * lax.rsqrt(ms + EPS) * g_ref[...]
    h = h * (1.0 + sc_ref[...]) + sh_ref[...]
    h_ref[...] = h
    hhi = h.astype(BF16)
    hlo = (h - hhi.astype(F32)).astype(BF16)
    nt = (((1,), (1,)), ((), ()))
    dotf = lambda a, b: lax.dot_general(a, b, nt, preferred_element_type=F32)
    logits = dotf(whi_ref[...], hhi) + (dotf(whi_ref[...], hlo) + dotf(wlo_ref[...], hhi)) + b_ref[...]
    ne, tm = logits.shape
    iota = lax.broadcasted_iota(jnp.int32, (ne, tm), 0).astype(F32)
    vals, idxs = [], []
    cur = logits
    for _ in range(TOP_K):
        m = jnp.max(cur, axis=0, keepdims=True)
        idx = jnp.min(jnp.where(cur == m, iota, float(ne)), axis=0, keepdims=True)
        vals.append(m)
        idxs.append(idx)
        cur = jnp.where(iota == idx, -jnp.inf, cur)
    tv = jnp.concatenate(vals, axis=0)
    ex = jnp.exp(tv - tv[0:1])
    w_ref[...] = ex / jnp.sum(ex, axis=0, keepdims=True)
    e_ref[...] = jnp.concatenate(idxs, axis=0).astype(jnp.int32)
    onehot = jnp.zeros((ne, tm), F32)
    for idx in idxs:
        onehot = onehot + jnp.where(iota == idx, 1.0, 0.0)
    earlier = (lax.broadcasted_iota(jnp.int32, (tm, tm), 0)
               < lax.broadcasted_iota(jnp.int32, (tm, tm), 1))
    rank = jnp.dot(onehot.astype(BF16), jnp.where(earlier, 1.0, 0.0).astype(BF16),
                   preferred_element_type=F32) + run_ref[...]
    pos_ref[...] = jnp.concatenate(
        [jnp.sum(jnp.where(iota == idx, rank, 0.0), axis=0, keepdims=True) for idx in idxs],
        axis=0).astype(jnp.int32)
    run_ref[...] = run_ref[...] + jnp.sum(onehot, axis=1, keepdims=True)
    cnt_ref[...] = run_ref[...]


def _router(x2, sh, sc, g, w_router, b_router, s):
    t, d = x2.shape
    tm = 256
    tpb = s // tm
    ne = w_router.shape[1]
    wt = w_router.T
    whi = wt.astype(BF16)
    wlo = (wt - whi.astype(F32)).astype(BF16)
    per_b = lambda i: (i // tpb, 0, 0)
    const = lambda i: (0, 0)
    return pl.pallas_call(
        _router_kernel,
        out_shape=[jax.ShapeDtypeStruct((t, d), F32),
                   jax.ShapeDtypeStruct((TOP_K, t), jnp.int32),
                   jax.ShapeDtypeStruct((TOP_K, t), F32),
                   jax.ShapeDtypeStruct((TOP_K, t), jnp.int32),
                   jax.ShapeDtypeStruct((ne, 1), F32)],
        grid=(t // tm,),
        in_specs=[pl.BlockSpec((tm, d), lambda i: (i, 0)),
                  pl.BlockSpec((None, 1, d), per_b),
                  pl.BlockSpec((None, 1, d), per_b),
                  pl.BlockSpec((1, d), const),
                  pl.BlockSpec((ne, d), const),
                  pl.BlockSpec((ne, d), const),
                  pl.BlockSpec((ne, 1), const)],
        out_specs=[pl.BlockSpec((tm, d), lambda i: (i, 0)),
                   pl.BlockSpec((TOP_K, tm), lambda i: (0, i)),
                   pl.BlockSpec((TOP_K, tm), lambda i: (0, i)),
                   pl.BlockSpec((TOP_K, tm), lambda i: (0, i)),
                   pl.BlockSpec((ne, 1), const)],
        scratch_shapes=[pltpu.VMEM((ne, 1), F32)],
        compiler_params=_cparams("arbitrary"),
        name="moe_router",
    )(x2, sh, sc, g, whi, wlo, b_router.reshape(ne, 1))


SC_CORES = 2
SC_SUBCORES = 16
SC_ROWS = 32


def _sc_scatter_rows(rows, dest_kt, n_slots):
    t, d = rows.shape
    n_workers = SC_CORES * SC_SUBCORES
    per_w = t // n_workers
    n_chunks = per_w // SC_ROWS
    assert per_w * n_workers == t and n_chunks * SC_ROWS == per_w and n_chunks % 2 == 0
    idx = dest_kt.reshape(TOP_K, n_workers, n_chunks, SC_ROWS).transpose(1, 2, 0, 3)
    idx = idx.reshape(n_workers * n_chunks * TOP_K, SC_ROWS)
    lists_per_w = n_chunks * TOP_K
    mesh = plsc.VectorSubcoreMesh(core_axis_name="c", subcore_axis_name="s")

    @functools.partial(
        pl.kernel, mesh=mesh,
        out_type=jax.ShapeDtypeStruct((n_slots, d), rows.dtype),
        scratch_types=[pltpu.VMEM((lists_per_w, SC_ROWS), jnp.int32),
                       pltpu.VMEM((SC_ROWS, d), rows.dtype),
                       pltpu.VMEM((SC_ROWS, d), rows.dtype)] + [pltpu.SemaphoreType.DMA] * 4,
        name="sc_scatter_rows",
    )
    def scatter_kernel(rows_hbm, idx_hbm, out_hbm, idx_v, buf_a, buf_b, ld_a, ld_b, st_a, st_b):
        wid = lax.axis_index("s") * SC_CORES + lax.axis_index("c")
        pltpu.sync_copy(idx_hbm.at[pl.ds(wid * lists_per_w, lists_per_w)], idx_v)

        def load(j, buf, sem):
            return pltpu.make_async_copy(rows_hbm.at[pl.ds(wid * per_w + j * SC_ROWS, SC_ROWS)], buf, sem)

        def scatter_all(j, buf, sem):
            copies = [pltpu.make_async_copy(buf, out_hbm.at[idx_v.at[j * TOP_K + kk]], sem)
                      for kk in range(TOP_K)]
            for cp in copies:
                cp.start()
            for cp in copies:
                cp.wait()

        load(0, buf_a, ld_a).start()

        @pl.loop(0, n_chunks, step=2)
        def _(j):
            load(j + 1, buf_b, ld_b).start()
            load(j, buf_a, ld_a).wait()
            scatter_all(j, buf_a, st_a)

            @pl.when(j + 2 < n_chunks)
            def _():
                load(j + 2, buf_a, ld_a).start()
            load(j + 1, buf_b, ld_b).wait()
            scatter_all(j + 1, buf_b, st_b)

    return scatter_kernel(rows, idx)


def _expert_kernel(blk_e_ref, n_used_ref, n_valid_ref, x_ref, wgu_ref, bgu_ref, wdn_ref, bdn_ref, o_ref,
                   wgu_bf, wdn_bf):
    j = pl.program_id(0)
    d = x_ref.shape[1]

    @pl.when(j < n_used_ref[0])
    def _():
        @pl.when((j == 0) | (blk_e_ref[j] != blk_e_ref[jnp.maximum(j - 1, 0)]))
        def _():
            wgu_bf[...] = wgu_ref[...].astype(BF16)
            wdn_bf[...] = wdn_ref[...].astype(BF16)

        row = lax.broadcasted_iota(jnp.int32, x_ref.shape, 0)
        xb = jnp.where(row < n_valid_ref[j], x_ref[...], 0.0).astype(BF16)
        gu = jnp.dot(xb, wgu_bf[...], preferred_element_type=F32) + bgu_ref[...]
        x_glu = jnp.minimum(gu[:, :d], SWIGLU_LIMIT)
        x_lin = jnp.clip(gu[:, d:], -SWIGLU_LIMIT, SWIGLU_LIMIT)
        act = x_glu * jax.nn.sigmoid(SWIGLU_ALPHA * x_glu) * (x_lin + 1.0)
        o_ref[...] = jnp.dot(act.astype(BF16), wdn_bf[...], preferred_element_type=F32) + bdn_ref[...]

    @pl.when(j >= n_used_ref[0])
    def _():
        o_ref[...] = jnp.zeros_like(o_ref)


def _experts(xs, blk_e, n_used, n_valid, wgu, bgu, wdn, bdn, layer):
    d = xs.shape[1]
    n_blocks = blk_e.shape[0]
    depth, ne = wgu.shape[:2]
    by_expert = lambda j, be, nu, nv: (layer, be[j], 0, 0)
    grid_spec = pltpu.PrefetchScalarGridSpec(
        num_scalar_prefetch=3,
        grid=(n_blocks,),
        in_specs=[pl.BlockSpec((MOE_BLOCK, d), lambda j, be, nu, nv: (jnp.minimum(j, nu[0] - 1), 0)),
                  pl.BlockSpec((None, None, d, 2 * d), by_expert),
                  pl.BlockSpec((None, None, 1, 2 * d), by_expert),
                  pl.BlockSpec((None, None, d, d), by_expert),
                  pl.BlockSpec((None, None, 1, d), by_expert)],
        out_specs=pl.BlockSpec((MOE_BLOCK, d), lambda j, be, nu, nv: (j, 0)),
        scratch_shapes=[pltpu.VMEM((d, 2 * d), BF16), pltpu.VMEM((d, d), BF16)],
    )
    return pl.pallas_call(
        _expert_kernel,
        out_shape=jax.ShapeDtypeStruct((n_blocks * MOE_BLOCK, d), F32),
        grid_spec=grid_spec,
        compiler_params=_cparams("arbitrary"),
        name="moe_experts",
    )(blk_e, n_used, n_valid, xs, wgu, bgu.reshape(depth, ne, 1, 2 * d), wdn, bdn.reshape(depth, ne, 1, d))


def _sc_gather_rows(table, idx):
    n_out = idx.shape[0]
    d = table.shape[1]
    n_workers = SC_CORES * SC_SUBCORES
    per_w = n_out // n_workers
    n_chunks = per_w // SC_ROWS
    assert per_w * n_workers == n_out and n_chunks * SC_ROWS == per_w and n_chunks % 2 == 0
    mesh = plsc.VectorSubcoreMesh(core_axis_name="c", subcore_axis_name="s")

    @functools.partial(
        pl.kernel, mesh=mesh,
        out_type=jax.ShapeDtypeStruct((n_out, d), table.dtype),
        scratch_types=[pltpu.VMEM((per_w,), jnp.int32),
                       pltpu.VMEM((SC_ROWS, d), table.dtype),
                       pltpu.VMEM((SC_ROWS, d), table.dtype),
                       pltpu.SemaphoreType.DMA, pltpu.SemaphoreType.DMA],
        name="sc_gather_rows",
    )
    def gather_kernel(table_hbm, idx_hbm, out_hbm, idx_v, rows_a, rows_b, sem_a, sem_b):
        base = (lax.axis_index("s") * SC_CORES + lax.axis_index("c")) * per_w
        pltpu.sync_copy(idx_hbm.at[pl.ds(base, per_w)], idx_v)

        def gather(j, buf, sem):
            rows = idx_v.at[pl.ds(j * SC_ROWS, SC_ROWS)]
            return pltpu.make_async_copy(table_hbm.at[rows], buf, sem)

        def write_back(j, buf):
            pltpu.sync_copy(buf, out_hbm.at[pl.ds(base + j * SC_ROWS, SC_ROWS)])

        gather(0, rows_a, sem_a).start()

        @pl.loop(0, n_chunks, step=2)
        def _(j):
            gather(j + 1, rows_b, sem_b).start()
            gather(j, rows_a, sem_a).wait()
            write_back(j, rows_a)

            @pl.when(j + 2 < n_chunks)
            def _():
                gather(j + 2, rows_a, sem_a).start()
            gather(j + 1, rows_b, sem_b).wait()
            write_back(j + 1, rows_b)

    return gather_kernel(table, idx)


def _combine_kernel(y0_ref, y1_ref, y2_ref, y3_ref, w_ref, x_ref, g2_ref, fg_ref, xo_ref, *, final):
    w = w_ref[...]
    y = w[:, 0:1] * y0_ref[...]
    for kk, y_ref in enumerate((y1_ref, y2_ref, y3_ref), start=1):
        y = y + w[:, kk:kk + 1] * y_ref[...]
    xn = x_ref[...] + g2_ref[...] * y
    if final:
        ms = jnp.mean(xn * xn, axis=-1, keepdims=True)
        xn = xn * lax.rsqrt(ms + EPS) * fg_ref[...]
    xo_ref[...] = xn


def _combine(dest_kt, out_sorted, gates_t, x2, g2, final_g, s, final):
    t, d = x2.shape
    tm = 256
    tpb = s // tm
    nt = t // tm
    yg = _sc_gather_rows(out_sorted, dest_kt)
    plane = lambda kk: pl.BlockSpec((tm, d), lambda i: (kk * nt + i, 0))
    return pl.pallas_call(
        functools.partial(_combine_kernel, final=final),
        out_shape=jax.ShapeDtypeStruct((t, d), F32),
        grid=(nt,),
        in_specs=[plane(0), plane(1), plane(2), plane(3),
                  pl.BlockSpec((tm, TOP_K), lambda i: (i, 0)),
                  pl.BlockSpec((tm, d), lambda i: (i, 0)),
                  pl.BlockSpec((None, 1, d), lambda i: (i // tpb, 0, 0)),
                  pl.BlockSpec((1, d), lambda i: (0, 0))],
        out_specs=pl.BlockSpec((tm, d), lambda i: (i, 0)),
        compiler_params=_cparams("arbitrary"),
        name="moe_combine",
    )(yg, yg, yg, yg, gates_t, x2, g2, final_g)


def _moe_plan(top_e, pos, counts):
    t = top_e.shape[1]
    sizes = counts[:, 0].astype(jnp.int32)
    padded = (sizes + MOE_BLOCK - 1) // MOE_BLOCK * MOE_BLOCK
    pad_end = jnp.cumsum(padded)
    pad_start = pad_end - padded
    start_of = jnp.zeros_like(top_e)
    for e in range(N_EXPERTS):
        start_of = jnp.where(top_e == e, pad_start[e], start_of)
    dest_kt = (start_of + pos).reshape(TOP_K * t)
    n_blocks = -(-t * TOP_K // MOE_BLOCK) + N_EXPERTS
    blk_start = jnp.arange(n_blocks, dtype=jnp.int32) * MOE_BLOCK
    blk_e = jnp.minimum(jnp.sum((pad_end[None, :] <= blk_start[:, None]).astype(jnp.int32), axis=1),
                        N_EXPERTS - 1).astype(jnp.int32)
    n_used = (pad_end[-1] // MOE_BLOCK).astype(jnp.int32).reshape(1)
    filled_end = pad_start + sizes
    n_valid = jnp.zeros((n_blocks,), jnp.int32)
    for e in range(N_EXPERTS):
        n_valid = jnp.where(blk_e == e, jnp.clip(filled_end[e] - blk_start, 0, MOE_BLOCK), n_valid)
    return dest_kt, blk_e, n_used, n_valid, n_blocks


def _rope_tables(s, gain):
    n_rows = s // GRID_W
    row = jnp.repeat(jnp.arange(n_rows), GRID_W).astype(F32)
    col = (jnp.arange(s) % GRID_W).astype(F32)
    half = HEAD_DIM // 2
    inv = ROPE_THETA ** (-jnp.arange(0, half, 2, dtype=F32) / half)
    ang_r = row[:, None] * inv
    ang_c = col[:, None] * inv
    cos = jnp.concatenate([jnp.cos(ang_r)] * 2 + [jnp.cos(ang_c)] * 2, axis=-1)
    sin = jnp.concatenate([-jnp.sin(ang_r), jnp.sin(ang_r), -jnp.sin(ang_c), jnp.sin(ang_c)], axis=-1)
    gain = gain.astype(F32)
    return cos * gain[None, :], sin * gain[_PARTNER][None, :]


_q = HEAD_DIM // 4
_PARTNER = np.concatenate([np.arange(_q, 2 * _q), np.arange(0, _q),
                           np.arange(3 * _q, 4 * _q), np.arange(2 * _q, 3 * _q)])


def _partner_cols(w, n_heads):
    idx = (np.arange(n_heads)[:, None] * HEAD_DIM + _PARTNER[None, :]).reshape(-1)
    return w[:, idx]


def kernel(x, c, w_ada, b_ada, norm1_g, w_in, q_norm_g, k_norm_g, rel_bias, w_br_a, w_br_b, w_out,
           norm2_g, w_router, b_router, w_gate_up, b_gate_up, w_down, b_down, final_norm_g):
    bsz, s, d = x.shape
    depth = w_ada.shape[0]
    mod = _ada_mod(c, w_ada, b_ada)
    n_streams = N_STREAMS if bsz % N_STREAMS == 0 else 1
    hb = bsz // n_streams
    ht = hb * s
    streams = range(n_streams)
    xs2 = [x[i * hb:(i + 1) * hb].reshape(ht, d) for i in streams]
    q_off = 3 * A_WIDTH
    k_off = q_off + B_Q
    for l in range(depth):
        mods = [[mod[l, i * hb:(i + 1) * hb, j * d:(j + 1) * d].reshape(hb, 1, d) for j in range(N_MOD)]
                for i in streams]
        w = w_in[l]
        wq, wk = w[:, q_off:q_off + B_Q], w[:, k_off:k_off + B_KV]
        w_ext = jnp.concatenate([wq, _partner_cols(wq, B_HEADS), wk, _partner_cols(wk, B_KV_HEADS),
                                 w[:, k_off + B_KV:k_off + 2 * B_KV], w[:, :q_off],
                                 w[:, k_off + 2 * B_KV:]], axis=1).astype(BF16)
        tabs = [jnp.tile(tb, (1, 2)) for tb in
                _rope_tables(s, q_norm_g[l]) + _rope_tables(s, k_norm_g[l])]
        wa, wb, wo = w_br_a[l].astype(BF16), w_br_b[l].astype(BF16), w_out[l].astype(BF16)
        proj = [_projection(xs2[i], mods[i][0], mods[i][1], norm1_g[l].reshape(1, d), w_ext, tabs, hb, s)
                for i in streams]
        attn_a = [[_dilated_group(proj[i][g], rel_bias[:, g * A_HEADS:(g + 1) * A_HEADS], dil, hb, s)
                   for g, (_, dil) in enumerate(A_GROUPS)] for i in streams]
        attn_b = [_gqa_attention(proj[i][3], proj[i][4], proj[i][5], q_norm_g[l], k_norm_g[l])
                  .reshape(ht, B_Q) for i in streams]
        xs2 = [_merge([o for o, _ in attn_a[i]], [ls for _, ls in attn_a[i]], attn_b[i], proj[i][6],
                      xs2[i], mods[i][2], wa, wb, wo, s) for i in streams]
        routed = [_router(xs2[i], mods[i][3], mods[i][4], norm2_g[l].reshape(1, d), w_router[l],
                          b_router[l], s) for i in streams]
        plans = [_moe_plan(routed[i][1], routed[i][3], routed[i][4]) for i in streams]
        slots = [_sc_scatter_rows(routed[i][0], plans[i][0], plans[i][4] * MOE_BLOCK) for i in streams]
        outs = [_experts(slots[i], plans[i][1], plans[i][2], plans[i][3],
                         w_gate_up, b_gate_up, w_down, b_down, l) for i in streams]
        xs2 = [_combine(plans[i][0], outs[i], routed[i][2].T, xs2[i], mods[i][5],
                        final_norm_g.reshape(1, d), s, final=(l == depth - 1)) for i in streams]
    return jnp.concatenate([xh.reshape(hb, s, d) for xh in xs2], axis=0)
```

```python
import functools
import math

import numpy as np
import jax
import jax.numpy as jnp
from jax import lax
from jax.experimental import pallas as pl
from jax.experimental.pallas import tpu as pltpu
from jax.experimental.pallas import tpu_sc as plsc

F32 = jnp.float32
BF16 = jnp.bfloat16

HEAD_DIM = 64
LANE = 128
PAIR = 2 * HEAD_DIM
A_GROUPS = ((128, 1), (512, 4), (2048, 16))
A_HEADS = 4
A_WIDTH = len(A_GROUPS) * A_HEADS * HEAD_DIM
A_OUT = A_HEADS * HEAD_DIM
A_GROUP_COLS = 3 * A_OUT
B_HEADS = 16
B_KV_HEADS = 4
B_GRP = B_HEADS // B_KV_HEADS
B_Q = B_HEADS * HEAD_DIM
B_KV = B_KV_HEADS * HEAD_DIM
GRID_W = 64
ROPE_THETA = 10000.0
REL_BUCKETS = 32
REL_MAX_DIST = 1024
N_EXPERTS = 32
TOP_K = 4
SWIGLU_LIMIT = 7.0
SWIGLU_ALPHA = 1.702
MOE_BLOCK = 512
COMBINE_CHUNKS = 4
N_MOD = 6
N_STREAMS = 1
EPS = 1e-6
NEG_INF = -1e30
LOG2E = math.log2(math.e)
N_SIDE = 64
DIL_SUB = 2 * N_SIDE
DIL_STEP_ROWS = 512

VMEM_LIMIT = 56 * 1024 * 1024


def _cparams(*sem):
    return pltpu.CompilerParams(dimension_semantics=sem, vmem_limit_bytes=VMEM_LIMIT)


def _ada_kernel(c_ref, w_ref, b_ref, o_ref):
    c = c_ref[...]
    ca = (c * jax.nn.sigmoid(c)).astype(BF16)
    o_ref[...] = jnp.dot(ca, w_ref[...].astype(BF16), preferred_element_type=F32) + b_ref[...]


def _ada_mod(c, w_ada, b_ada):
    depth, d, n = w_ada.shape
    bsz = c.shape[0]
    tn = 1536
    return pl.pallas_call(
        _ada_kernel,
        out_shape=jax.ShapeDtypeStruct((depth, bsz, n), F32),
        grid=(depth, n // tn),
        in_specs=[pl.BlockSpec((bsz, d), lambda l, j: (0, 0)),
                  pl.BlockSpec((None, d, tn), lambda l, j: (l, 0, j)),
                  pl.BlockSpec((None, 1, tn), lambda l, j: (l, 0, j))],
        out_specs=pl.BlockSpec((None, bsz, tn), lambda l, j: (l, 0, j)),
        compiler_params=_cparams("arbitrary", "arbitrary"),
        name="ada_mod",
    )(c, w_ada, b_ada.reshape(depth, 1, n))


def _proj_kernel(x_ref, sh_ref, sc_ref, g_ref, w_ref, aq_ref, bq_ref, ak_ref, bk_ref,
                 za0_ref, za1_ref, za2_ref, q_ref, k_ref, v_ref, gate_ref, fold_ref):
    x = x_ref[...]
    ms = jnp.mean(x * x, axis=-1, keepdims=True)
    h = x * lax.rsqrt(ms + EPS) * g_ref[...]
    h = h * (1.0 + sc_ref[...]) + sh_ref[...]
    hb = h.astype(BF16)

    def mm(lo, hi):
        return jnp.dot(hb, w_ref[:, lo:hi], preferred_element_type=F32)

    tm = x.shape[0]
    o = 0
    zq = mm(o, o + B_Q); o += B_Q
    zqp = mm(o, o + B_Q); o += B_Q
    zk = mm(o, o + B_KV); o += B_KV
    zkp = mm(o, o + B_KV); o += B_KV
    zv = mm(o, o + B_KV); o += B_KV
    lane = lax.broadcasted_iota(jnp.int32, (tm, PAIR), 1)
    first = lane < HEAD_DIM

    def pair_norm_rope(z, zp, a, b):
        zz = z * z
        ss = jnp.where(first,
                       jnp.sum(jnp.where(first, zz, 0.0), axis=-1, keepdims=True),
                       jnp.sum(jnp.where(first, 0.0, zz), axis=-1, keepdims=True))
        return lax.rsqrt(ss * (1.0 / HEAD_DIM) + EPS) * (z * a + zp * b)

    aq, bq, ak, bk = aq_ref[...], bq_ref[...], ak_ref[...], bk_ref[...]
    for j in range(B_HEADS // 2):
        cols = slice(j * PAIR, (j + 1) * PAIR)
        q_ref[j] = (pair_norm_rope(zq[:, cols], zqp[:, cols], aq, bq)
                    * (LOG2E * HEAD_DIM ** -0.5)).astype(BF16)
    ones_col = jnp.where(lane == HEAD_DIM, 1.0, 0.0)
    for j in range(B_KV_HEADS // 2):
        cols = slice(j * PAIR, (j + 1) * PAIR)
        kk = pair_norm_rope(zk[:, cols], zkp[:, cols], ak, bk)
        kk_sw = pltpu.roll(kk, HEAD_DIM, axis=1)
        k_ref[2 * j] = jnp.where(first, kk, kk_sw).astype(BF16)
        k_ref[2 * j + 1] = jnp.where(first, kk_sw, kk).astype(BF16)
        vv = zv[:, cols]
        v_ref[2 * j] = jnp.where(first, vv, ones_col).astype(BF16)
        v_ref[2 * j + 1] = jnp.where(first, pltpu.roll(vv, HEAD_DIM, axis=1), ones_col).astype(BF16)
    for za_ref, (_, dil) in zip((za0_ref, za1_ref, za2_ref), A_GROUPS):
        z = mm(o, o + A_GROUP_COLS)
        o += A_GROUP_COLS
        if dil == 1:
            za_ref[...] = z.astype(BF16)
        else:
            for c in range(A_GROUP_COLS // LANE):
                fold_ref[c] = z[:, c * LANE:(c + 1) * LANE]
            for r in range(dil):
                for c in range(A_GROUP_COLS // LANE):
                    col = r * A_GROUP_COLS + c * LANE
                    za_ref[:, col:col + LANE] = (
                        fold_ref[c, pl.ds(r, tm // dil, stride=dil), :].astype(BF16))
    gate_ref[...] = jax.nn.sigmoid(mm(o, o + gate_ref.shape[-1])).astype(BF16)


def _projection(x2, sh, sc, g, w_ext, tabs, bsz, s):
    t, d = x2.shape
    tm = 256
    tpb = s // tm
    n_ext = w_ext.shape[1]
    n_gate = 2 * d
    row = lambda i: (i, 0)
    per_b = lambda i: (i // tpb, 0, 0)
    tab = lambda i: (i % tpb, 0)
    hm = lambda i: (i // tpb, 0, i % tpb, 0)
    out_shape = (
        [jax.ShapeDtypeStruct((bsz, s // dil, dil * A_GROUP_COLS), BF16) for _, dil in A_GROUPS]
        + [jax.ShapeDtypeStruct((bsz, B_HEADS // 2, s, PAIR), BF16),
           jax.ShapeDtypeStruct((bsz, B_KV_HEADS, s, PAIR), BF16),
           jax.ShapeDtypeStruct((bsz, B_KV_HEADS, s, PAIR), BF16),
           jax.ShapeDtypeStruct((t, n_gate), BF16)])
    out_specs = (
        [pl.BlockSpec((None, tm // dil, dil * A_GROUP_COLS), lambda i: (i // tpb, i % tpb, 0))
         for _, dil in A_GROUPS]
        + [pl.BlockSpec((None, B_HEADS // 2, tm, PAIR), hm),
           pl.BlockSpec((None, B_KV_HEADS, tm, PAIR), hm),
           pl.BlockSpec((None, B_KV_HEADS, tm, PAIR), hm),
           pl.BlockSpec((tm, n_gate), row)])
    return pl.pallas_call(
        _proj_kernel,
        out_shape=out_shape,
        grid=(t // tm,),
        in_specs=[pl.BlockSpec((tm, d), row),
                  pl.BlockSpec((None, 1, d), per_b),
                  pl.BlockSpec((None, 1, d), per_b),
                  pl.BlockSpec((1, d), lambda i: (0, 0)),
                  pl.BlockSpec((d, n_ext), lambda i: (0, 0))]
                 + [pl.BlockSpec((tm, PAIR), tab)] * 4,
        out_specs=out_specs,
        scratch_shapes=[pltpu.VMEM((A_GROUP_COLS // LANE, tm, LANE), F32)],
        compiler_params=_cparams("arbitrary"),
        name="in_proj",
    )(x2, sh, sc, g, w_ext, *tabs)


def _dilated_kernel(q_ref, kp_ref, kc_ref, kn_ref, vp_ref, vc_ref, vn_ref, bias_ref,
                    o_ref, lse_ref, *, tq, seq_len):
    i = pl.program_id(2)
    k = jnp.concatenate([kp_ref[...], kc_ref[...], kn_ref[...]], axis=0)
    v = jnp.concatenate([vp_ref[...], vc_ref[...], vn_ref[...]], axis=0)
    nk = DIL_SUB + 2 * N_SIDE
    col = lax.broadcasted_iota(jnp.int32, (DIL_SUB, nk), 1)
    for sb in range(tq // DIL_SUB):
        rows = slice(sb * DIL_SUB, (sb + 1) * DIL_SUB)
        q = q_ref[rows, :]
        kb = k[sb * DIL_SUB:sb * DIL_SUB + nk]
        vb = v[sb * DIL_SUB:sb * DIL_SUB + nk]
        kpos = i * tq + sb * DIL_SUB - N_SIDE + col
        valid = (kpos >= 0) & (kpos < seq_len)
        for hh in range(A_HEADS):
            sl = slice(hh * HEAD_DIM, (hh + 1) * HEAD_DIM)
            sc = lax.dot_general(q[:, sl], kb[:, sl], (((1,), (1,)), ((), ())),
                                 preferred_element_type=F32)
            sc = sc * (HEAD_DIM ** -0.5) + bias_ref[hh]
            sc = jnp.where(valid, sc, NEG_INF)
            m = jnp.max(sc, axis=-1, keepdims=True)
            p = jnp.exp(sc - m)
            den = jnp.sum(p, axis=-1, keepdims=True)
            o = jnp.dot(p.astype(BF16), vb[:, sl], preferred_element_type=F32) / den
            o_ref[rows, sl] = o
            lse_ref[rows, sl] = jnp.broadcast_to(m + jnp.log(den), (DIL_SUB, HEAD_DIM))


def _t5_bucket(rel):
    nb = REL_BUCKETS // 2
    max_exact = nb // 2
    ret = jnp.where(rel > 0, nb, 0)
    n = jnp.abs(rel)
    nf = jnp.maximum(n, 1).astype(F32)
    large = max_exact + (jnp.log(nf / max_exact) / math.log(REL_MAX_DIST / max_exact)
                         * (nb - max_exact)).astype(jnp.int32)
    large = jnp.minimum(large, nb - 1)
    return ret + jnp.where(n < max_exact, n, large)


def _band_bias(rel_bias_g, dilation, tq):
    nk = tq + 2 * N_SIDE
    rel = jnp.arange(nk)[None, :] - N_SIDE - jnp.arange(tq)[:, None]
    bucket = _t5_bucket(rel * dilation)[None]
    bias = jnp.full((A_HEADS,) + rel.shape, NEG_INF, F32)
    for b in range(REL_BUCKETS):
        bias = jnp.where(bucket == b, rel_bias_g[b].astype(F32)[:, None, None], bias)
    return jnp.where((jnp.abs(rel) <= N_SIDE)[None], bias, NEG_INF)


def _dilated_group(za, rel_bias_g, dilation, bsz, s):
    ll = s // dilation
    tq = min(DIL_STEP_ROWS, ll)
    assert ll % tq == 0 and tq % DIL_SUB == 0
    zv = za
    nhalf = ll // N_SIDE
    per = tq // N_SIDE
    cur = lambda c: (lambda b, r, i: (b, i, 3 * r + c))
    prv = lambda c: (lambda b, r, i: (b, jnp.maximum(per * i - 1, 0), 3 * r + c))
    nxt = lambda c: (lambda b, r, i: (b, jnp.minimum(per * (i + 1), nhalf - 1), 3 * r + c))
    full = pl.BlockSpec((None, tq, A_OUT), cur(0))
    half = lambda f: pl.BlockSpec((None, N_SIDE, A_OUT), f)
    bias = _band_bias(rel_bias_g, dilation, DIL_SUB)
    out_sd = jax.ShapeDtypeStruct((bsz, ll, dilation * A_OUT), F32)
    out_spec = pl.BlockSpec((None, tq, A_OUT), lambda b, r, i: (b, i, r))
    o, lse = pl.pallas_call(
        functools.partial(_dilated_kernel, tq=tq, seq_len=ll),
        out_shape=[out_sd, out_sd],
        grid=(bsz, dilation, ll // tq),
        in_specs=[full,
                  half(prv(1)), pl.BlockSpec((None, tq, A_OUT), cur(1)), half(nxt(1)),
                  half(prv(2)), pl.BlockSpec((None, tq, A_OUT), cur(2)), half(nxt(2)),
                  pl.BlockSpec(bias.shape, lambda b, r, i: (0, 0, 0))],
        out_specs=[out_spec, out_spec],
        compiler_params=_cparams("arbitrary", "arbitrary", "arbitrary"),
        name=f"dilated_attn_d{dilation}",
    )(zv, zv, zv, zv, zv, zv, zv, bias)
    return o, lse


GQA_NO_SHIFT_MAX_LOG2 = 80.0
GQA_KEY_CHUNK = 128


def _stacked_heads(q_ref):
    first = lax.broadcasted_iota(jnp.int32, q_ref.shape[1:], 1) < HEAD_DIM
    blocks = []
    for j in range(q_ref.shape[0]):
        qp = q_ref[j]
        blocks += [jnp.where(first, qp, jnp.zeros_like(qp)), jnp.where(first, jnp.zeros_like(qp), qp)]
    return jnp.concatenate(blocks, axis=0)


def _gqa_kernel_noshift(q_ref, k_ref, v_ref, o_ref):
    tq = q_ref.shape[1]
    q = _stacked_heads(q_ref)
    nt = (((1,), (1,)), ((), ()))
    acc = jnp.zeros((q.shape[0], PAIR), F32)
    for c in range(k_ref.shape[0] // GQA_KEY_CHUNK):
        rows = slice(c * GQA_KEY_CHUNK, (c + 1) * GQA_KEY_CHUNK)
        p = jnp.exp2(lax.dot_general(q, k_ref[rows, :], nt, preferred_element_type=F32))
        acc = acc + jnp.dot(p.astype(BF16), v_ref[rows, :], preferred_element_type=F32)
    o = acc[:, :HEAD_DIM] / acc[:, HEAD_DIM:HEAD_DIM + 1]
    for hh in range(q.shape[0] // tq):
        o_ref[:, hh * HEAD_DIM:(hh + 1) * HEAD_DIM] = o[hh * tq:(hh + 1) * tq].astype(BF16)


def _gqa_kernel_rowmax(q_ref, k_ref, v_ref, o_ref):
    tq = q_ref.shape[1]
    q = _stacked_heads(q_ref)
    k = k_ref[...]
    v = v_ref[:, :HEAD_DIM]
    for hh in range(q.shape[0] // tq):
        sc = lax.dot_general(q[hh * tq:(hh + 1) * tq], k, (((1,), (1,)), ((), ())),
                             preferred_element_type=F32)
        m = jnp.max(sc, axis=-1, keepdims=True)
        p = jnp.exp2(sc - m)
        den = jnp.sum(p, axis=-1, keepdims=True)
        o = jnp.dot(p.astype(BF16), v, preferred_element_type=F32) / den
        o_ref[:, hh * HEAD_DIM:(hh + 1) * HEAD_DIM] = o.astype(BF16)


def _gqa_call(body, name, tq, q, k, v):
    bsz, _, s, _ = q.shape
    return pl.pallas_call(
        body,
        out_shape=jax.ShapeDtypeStruct((bsz, s, B_Q), BF16),
        grid=(bsz, B_KV_HEADS, s // tq),
        in_specs=[pl.BlockSpec((None, B_GRP // 2, tq, PAIR), lambda b, h, i: (b, h, i, 0)),
                  pl.BlockSpec((None, None, s, PAIR), lambda b, h, i: (b, h, 0, 0)),
                  pl.BlockSpec((None, None, s, PAIR), lambda b, h, i: (b, h, 0, 0))],
        out_specs=pl.BlockSpec((None, tq, B_GRP * HEAD_DIM), lambda b, h, i: (b, i, h)),
        compiler_params=_cparams("arbitrary", "arbitrary", "arbitrary"),
        name=name,
    )(q, k, v)


def _gqa_attention(q, k, v, q_gain, k_gain):
    bound = (HEAD_DIM ** 0.5) * LOG2E * 1.01 * jnp.max(jnp.abs(q_gain)) * jnp.max(jnp.abs(k_gain))
    return lax.cond(bound <= GQA_NO_SHIFT_MAX_LOG2,
                    functools.partial(_gqa_call, _gqa_kernel_noshift, "gqa_attn", 1024),
                    functools.partial(_gqa_call, _gqa_kernel_rowmax, "gqa_attn_rowmax", 256),
                    q, k, v)


def _merge_kernel(o0_ref, o1_ref, o2_ref, l0_ref, l1_ref, l2_ref, ob_ref, gate_ref, x_ref, g1_ref,
                  wa_ref, wb_ref, wo_ref, xo_ref, *unfold_refs):
    tm, d = x_ref.shape
    scratch = list(unfold_refs)

    def token_order(ref, dil):
        if dil == 1:
            return ref[...]
        buf = scratch.pop()
        for r in range(dil):
            for c in range(A_OUT // LANE):
                col = r * A_OUT + c * LANE
                buf[c, pl.ds(r, tm // dil, stride=dil), :] = ref[:, col:col + LANE]
        return jnp.concatenate([buf[c] for c in range(A_OUT // LANE)], axis=1)

    dils = [dil for _, dil in A_GROUPS]
    o0, o1, o2 = [token_order(r, dl) for r, dl in zip((o0_ref, o1_ref, o2_ref), dils)]
    l0, l1, l2 = [token_order(r, dl) for r, dl in zip((l0_ref, l1_ref, l2_ref), dils)]
    m = jnp.maximum(jnp.maximum(l0, l1), l2)
    e0, e1, e2 = jnp.exp(l0 - m), jnp.exp(l1 - m), jnp.exp(l2 - m)
    tot = e0 + e1 + e2
    oa = (e0 / tot) * o0 + (e1 / tot) * o1 + (e2 / tot) * o2
    ya = jnp.dot(oa.astype(BF16), wa_ref[...], preferred_element_type=F32)
    yb = jnp.dot(ob_ref[...], wb_ref[...], preferred_element_type=F32)
    merged = gate_ref[:, :d].astype(F32) * ya + gate_ref[:, d:].astype(F32) * yb
    y = jnp.dot(merged.astype(BF16), wo_ref[...], preferred_element_type=F32)
    xo_ref[...] = x_ref[...] + g1_ref[...] * y


def _merge(os_, ls_, ob, gate, x2, g1, wa, wb, wo, s):
    t, d = x2.shape
    tm = 256
    tpb = s // tm
    row = lambda i: (i, 0)
    const = lambda i: (0, 0)
    a_specs = [pl.BlockSpec((None, tm // dil, dil * A_OUT), lambda i: (i // tpb, i % tpb, 0))
               for _, dil in A_GROUPS]
    n_unfold = 2 * sum(1 for _, dil in A_GROUPS if dil > 1)
    return pl.pallas_call(
        _merge_kernel,
        out_shape=jax.ShapeDtypeStruct((t, d), F32),
        grid=(t // tm,),
        scratch_shapes=[pltpu.VMEM((A_OUT // LANE, tm, LANE), F32)] * n_unfold,
        in_specs=a_specs * 2
                 + [pl.BlockSpec((tm, B_Q), row),
                    pl.BlockSpec((tm, 2 * d), row),
                    pl.BlockSpec((tm, d), row),
                    pl.BlockSpec((None, 1, d), lambda i: (i // tpb, 0, 0)),
                    pl.BlockSpec(wa.shape, const),
                    pl.BlockSpec(wb.shape, const),
                    pl.BlockSpec(wo.shape, const)],
        out_specs=pl.BlockSpec((tm, d), row),
        compiler_params=_cparams("arbitrary"),
        name="branch_merge",
    )(*os_, *ls_, ob, gate, x2, g1, wa, wb, wo)


def _router_kernel(x_ref, sh_ref, sc_ref, g_ref, whi_ref, wlo_ref, b_ref,
                   h_ref, e_ref, w_ref, pos_ref, cnt_ref, run_ref):
    @pl.when(pl.program_id(0) == 0)
    def _():
        run_ref[...] = jnp.zeros_like(run_ref)

    x = x_ref[...]
    ms = jnp.mean(x * x, axis=-1, keepdims=True)
    h = x * lax.rsqrt(ms + EPS) * g_ref[...]
    h = h * (1.0 + sc_ref[...]) + sh_ref[...]
    bits = pltpu.bitcast(h.astype(BF16).astype(F32), jnp.uint32)
    half = h.shape[1] // 2
    h_ref[...] = (bits[:, :half] >> 16) | bits[:, half:]
    hhi = h.astype(BF16)
    hlo = (h - hhi.astype(F32)).astype(BF16)
    nt = (((1,), (1,)), ((), ()))
    dotf = lambda a, b: lax.dot_general(a, b, nt, preferred_element_type=F32)
    logits = dotf(whi_ref[...], hhi) + (dotf(whi_ref[...], hlo) + dotf(wlo_ref[...], hhi)) + b_ref[...]
    ne, tm = logits.shape
    iota = lax.broadcasted_iota(jnp.int32, (ne, tm), 0).astype(F32)
    vals, idxs = [], []
    cur = logits
    for _ in range(TOP_K):
        m = jnp.max(cur, axis=0, keepdims=True)
        idx = jnp.min(jnp.where(cur == m, iota, float(ne)), axis=0, keepdims=True)
        vals.append(m)
        idxs.append(idx)
        cur = jnp.where(iota == idx, -jnp.inf, cur)
    tv = jnp.concatenate(vals, axis=0)
    ex = jnp.exp(tv - tv[0:1])
    w_ref[...] = ex / jnp.sum(ex, axis=0, keepdims=True)
    e_ref[...] = jnp.concatenate(idxs, axis=0).astype(jnp.int32)
    onehot = jnp.zeros((ne, tm), F32)
    for idx in idxs:
        onehot = onehot + jnp.where(iota == idx, 1.0, 0.0)
    earlier = (lax.broadcasted_iota(jnp.int32, (tm, tm), 0)
               < lax.broadcasted_iota(jnp.int32, (tm, tm), 1))
    rank = jnp.dot(onehot.astype(BF16), jnp.where(earlier, 1.0, 0.0).astype(BF16),
                   preferred_element_type=F32) + run_ref[...]
    pos_ref[...] = jnp.concatenate(
        [jnp.sum(jnp.where(iota == idx, rank, 0.0), axis=0, keepdims=True) for idx in idxs],
        axis=0).astype(jnp.int32)
    run_ref[...] = run_ref[...] + jnp.sum(onehot, axis=1, keepdims=True)
    cnt_ref[...] = run_ref[...]


def _router(x2, sh, sc, g, w_router, b_router, s):
    t, d = x2.shape
    tm = 256
    tpb = s // tm
    ne = w_router.shape[1]
    wt = w_router.T
    whi = wt.astype(BF16)
    wlo = (wt - whi.astype(F32)).astype(BF16)
    per_b = lambda i: (i // tpb, 0, 0)
    const = lambda i: (0, 0)
    return pl.pallas_call(
        _router_kernel,
        out_shape=[jax.ShapeDtypeStruct((t, d // 2), jnp.uint32),
                   jax.ShapeDtypeStruct((TOP_K, t), jnp.int32),
                   jax.ShapeDtypeStruct((TOP_K, t), F32),
                   jax.ShapeDtypeStruct((TOP_K, t), jnp.int32),
                   jax.ShapeDtypeStruct((ne, 1), F32)],
        grid=(t // tm,),
        in_specs=[pl.BlockSpec((tm, d), lambda i: (i, 0)),
                  pl.BlockSpec((None, 1, d), per_b),
                  pl.BlockSpec((None, 1, d), per_b),
                  pl.BlockSpec((1, d), const),
                  pl.BlockSpec((ne, d), const),
                  pl.BlockSpec((ne, d), const),
                  pl.BlockSpec((ne, 1), const)],
        out_specs=[pl.BlockSpec((tm, d // 2), lambda i: (i, 0)),
                   pl.BlockSpec((TOP_K, tm), lambda i: (0, i)),
                   pl.BlockSpec((TOP_K, tm), lambda i: (0, i)),
                   pl.BlockSpec((TOP_K, tm), lambda i: (0, i)),
                   pl.BlockSpec((ne, 1), const)],
        scratch_shapes=[pltpu.VMEM((ne, 1), F32)],
        compiler_params=_cparams("arbitrary"),
        name="moe_router",
    )(x2, sh, sc, g, whi, wlo, b_router.reshape(ne, 1))


SC_CORES = 2
SC_SUBCORES = 16
SC_ROWS = 32


def _sc_scatter_rows(rows, dest_kt, n_slots):
    t, d = rows.shape
    n_workers = SC_CORES * SC_SUBCORES
    per_w = t // n_workers
    n_chunks = per_w // SC_ROWS
    assert per_w * n_workers == t and n_chunks * SC_ROWS == per_w and n_chunks % 2 == 0
    idx = dest_kt.reshape(TOP_K, n_workers, n_chunks, SC_ROWS).transpose(1, 2, 0, 3)
    idx = idx.reshape(n_workers * n_chunks * TOP_K, SC_ROWS)
    lists_per_w = n_chunks * TOP_K
    mesh = plsc.VectorSubcoreMesh(core_axis_name="c", subcore_axis_name="s")

    @functools.partial(
        pl.kernel, mesh=mesh,
        out_type=jax.ShapeDtypeStruct((n_slots, d), rows.dtype),
        scratch_types=[pltpu.VMEM((lists_per_w, SC_ROWS), jnp.int32),
                       pltpu.VMEM((SC_ROWS, d), rows.dtype),
                       pltpu.VMEM((SC_ROWS, d), rows.dtype)] + [pltpu.SemaphoreType.DMA] * 4,
        name="sc_scatter_rows",
    )
    def scatter_kernel(rows_hbm, idx_hbm, out_hbm, idx_v, buf_a, buf_b, ld_a, ld_b, st_a, st_b):
        wid = lax.axis_index("s") * SC_CORES + lax.axis_index("c")
        pltpu.sync_copy(idx_hbm.at[pl.ds(wid * lists_per_w, lists_per_w)], idx_v)

        def load(j, buf, sem):
            return pltpu.make_async_copy(rows_hbm.at[pl.ds(wid * per_w + j * SC_ROWS, SC_ROWS)], buf, sem)

        def scatter_all(j, buf, sem):
            copies = [pltpu.make_async_copy(buf, out_hbm.at[idx_v.at[j * TOP_K + kk]], sem)
                      for kk in range(TOP_K)]
            for cp in copies:
                cp.start()
            for cp in copies:
                cp.wait()

        load(0, buf_a, ld_a).start()

        @pl.loop(0, n_chunks, step=2)
        def _(j):
            load(j + 1, buf_b, ld_b).start()
            load(j, buf_a, ld_a).wait()
            scatter_all(j, buf_a, st_a)

            @pl.when(j + 2 < n_chunks)
            def _():
                load(j + 2, buf_a, ld_a).start()
            load(j + 1, buf_b, ld_b).wait()
            scatter_all(j + 1, buf_b, st_b)

    return scatter_kernel(rows, idx)


def _expert_kernel(blk_e_ref, n_used_ref, n_valid_ref, x_ref, wgu_ref, bgu_ref, wdn_ref, bdn_ref, o_ref,
                   wgu_bf, wdn_bf):
    j = pl.program_id(0)
    d = o_ref.shape[1]

    @pl.when(j < n_used_ref[0])
    def _():
        @pl.when((j == 0) | (blk_e_ref[j] != blk_e_ref[jnp.maximum(j - 1, 0)]))
        def _():
            wgu_bf[...] = wgu_ref[...].astype(BF16)
            wdn_bf[...] = wdn_ref[...].astype(BF16)

        row = lax.broadcasted_iota(jnp.int32, x_ref.shape, 0)
        packed = jnp.where(row < n_valid_ref[j], x_ref[...], jnp.uint32(0))
        xb = jnp.concatenate([pltpu.bitcast(packed << 16, F32),
                              pltpu.bitcast(packed & jnp.uint32(0xFFFF0000), F32)], axis=1).astype(BF16)
        gu = jnp.dot(xb, wgu_bf[...], preferred_element_type=F32) + bgu_ref[...]
        x_glu = jnp.minimum(gu[:, :d], SWIGLU_LIMIT)
        x_lin = jnp.clip(gu[:, d:], -SWIGLU_LIMIT, SWIGLU_LIMIT)
        act = x_glu * jax.nn.sigmoid(SWIGLU_ALPHA * x_glu) * (x_lin + 1.0)
        o_ref[...] = jnp.dot(act.astype(BF16), wdn_bf[...], preferred_element_type=F32) + bdn_ref[...]

    @pl.when(j >= n_used_ref[0])
    def _():
        o_ref[...] = jnp.zeros_like(o_ref)


def _experts(xs, blk_e, n_used, n_valid, wgu, bgu, wdn, bdn, layer):
    d = wdn.shape[-1]
    n_blocks = blk_e.shape[0]
    depth, ne = wgu.shape[:2]
    by_expert = lambda j, be, nu, nv: (layer, be[j], 0, 0)
    grid_spec = pltpu.PrefetchScalarGridSpec(
        num_scalar_prefetch=3,
        grid=(n_blocks,),
        in_specs=[pl.BlockSpec((MOE_BLOCK, d // 2), lambda j, be, nu, nv: (jnp.minimum(j, nu[0] - 1), 0)),
                  pl.BlockSpec((None, None, d, 2 * d), by_expert),
                  pl.BlockSpec((None, None, 1, 2 * d), by_expert),
                  pl.BlockSpec((None, None, d, d), by_expert),
                  pl.BlockSpec((None, None, 1, d), by_expert)],
        out_specs=pl.BlockSpec((MOE_BLOCK, d), lambda j, be, nu, nv: (j, 0)),
        scratch_shapes=[pltpu.VMEM((d, 2 * d), BF16), pltpu.VMEM((d, d), BF16)],
    )
    return pl.pallas_call(
        _expert_kernel,
        out_shape=jax.ShapeDtypeStruct((n_blocks * MOE_BLOCK, d), F32),
        grid_spec=grid_spec,
        compiler_params=_cparams("arbitrary"),
        name="moe_experts",
    )(blk_e, n_used, n_valid, xs, wgu, bgu.reshape(depth, ne, 1, 2 * d), wdn, bdn.reshape(depth, ne, 1, d))


def _sc_gather_rows(table, idx):
    n_out = idx.shape[0]
    d = table.shape[1]
    n_workers = SC_CORES * SC_SUBCORES
    per_w = n_out // n_workers
    n_chunks = per_w // SC_ROWS
    assert per_w * n_workers == n_out and n_chunks * SC_ROWS == per_w and n_chunks % 2 == 0
    mesh = plsc.VectorSubcoreMesh(core_axis_name="c", subcore_axis_name="s")

    @functools.partial(
        pl.kernel, mesh=mesh,
        out_type=jax.ShapeDtypeStruct((n_out, d), table.dtype),
        scratch_types=[pltpu.VMEM((per_w,), jnp.int32),
                       pltpu.VMEM((SC_ROWS, d), table.dtype),
                       pltpu.VMEM((SC_ROWS, d), table.dtype),
                       pltpu.SemaphoreType.DMA, pltpu.SemaphoreType.DMA],
        name="sc_gather_rows",
    )
    def gather_kernel(table_hbm, idx_hbm, out_hbm, idx_v, rows_a, rows_b, sem_a, sem_b):
        base = (lax.axis_index("s") * SC_CORES + lax.axis_index("c")) * per_w
        pltpu.sync_copy(idx_hbm.at[pl.ds(base, per_w)], idx_v)

        def gather(j, buf, sem):
            rows = idx_v.at[pl.ds(j * SC_ROWS, SC_ROWS)]
            return pltpu.make_async_copy(table_hbm.at[rows], buf, sem)

        def write_back(j, buf):
            pltpu.sync_copy(buf, out_hbm.at[pl.ds(base + j * SC_ROWS, SC_ROWS)])

        gather(0, rows_a, sem_a).start()

        @pl.loop(0, n_chunks, step=2)
        def _(j):
            gather(j + 1, rows_b, sem_b).start()
            gather(j, rows_a, sem_a).wait()
            write_back(j, rows_a)

            @pl.when(j + 2 < n_chunks)
            def _():
                gather(j + 2, rows_a, sem_a).start()
            gather(j + 1, rows_b, sem_b).wait()
            write_back(j + 1, rows_b)

    return gather_kernel(table, idx)


def _combine_kernel(y0_ref, y1_ref, y2_ref, y3_ref, w_ref, x_ref, g2_ref, fg_ref, xo_ref, *, final):
    w = w_ref[...]
    y = w[:, 0:1] * y0_ref[...]
    for kk, y_ref in enumerate((y1_ref, y2_ref, y3_ref), start=1):
        y = y + w[:, kk:kk + 1] * y_ref[...]
    xn = x_ref[...] + g2_ref[...] * y
    if final:
        ms = jnp.mean(xn * xn, axis=-1, keepdims=True)
        xn = xn * lax.rsqrt(ms + EPS) * fg_ref[...]
    xo_ref[...] = xn


def _combine(dest, out_sorted, gates_t, x2, g2, final_g, s, final):
    t, d = x2.shape
    tm = 256
    tpb = s // tm
    sc_unit = 2 * SC_ROWS * SC_CORES * SC_SUBCORES // TOP_K
    chunkable = t % (COMBINE_CHUNKS * sc_unit) == 0 and t % (COMBINE_CHUNKS * tm) == 0
    n_chunks = COMBINE_CHUNKS if chunkable else 1
    tc = t // n_chunks
    nt = tc // tm
    for c in range(n_chunks):
        yg = _sc_gather_rows(out_sorted, dest[:, c * tc:(c + 1) * tc].reshape(TOP_K * tc))
        plane = lambda kk: pl.BlockSpec((tm, d), lambda i: (kk * nt + i, 0))
        tile = lambda i, c=c: (c * nt + i, 0)
        x2 = pl.pallas_call(
            functools.partial(_combine_kernel, final=final),
            out_shape=jax.ShapeDtypeStruct((t, d), F32),
            grid=(nt,),
            in_specs=[plane(0), plane(1), plane(2), plane(3),
                      pl.BlockSpec((tm, TOP_K), tile),
                      pl.BlockSpec((tm, d), tile),
                      pl.BlockSpec((None, 1, d), lambda i, c=c: ((c * nt + i) // tpb, 0, 0)),
                      pl.BlockSpec((1, d), lambda i: (0, 0))],
            out_specs=pl.BlockSpec((tm, d), tile),
            input_output_aliases={5: 0},
            compiler_params=_cparams("arbitrary"),
            name="moe_combine",
        )(yg, yg, yg, yg, gates_t, x2, g2, final_g)
    return x2


def _moe_plan(top_e, pos, counts):
    t = top_e.shape[1]
    sizes = counts[:, 0].astype(jnp.int32)
    padded = (sizes + MOE_BLOCK - 1) // MOE_BLOCK * MOE_BLOCK
    pad_end = jnp.cumsum(padded)
    pad_start = pad_end - padded
    start_of = jnp.zeros_like(top_e)
    for e in range(N_EXPERTS):
        start_of = jnp.where(top_e == e, pad_start[e], start_of)
    dest_kt = start_of + pos
    n_blocks = -(-t * TOP_K // MOE_BLOCK) + N_EXPERTS
    blk_start = jnp.arange(n_blocks, dtype=jnp.int32) * MOE_BLOCK
    blk_e = jnp.minimum(jnp.sum((pad_end[None, :] <= blk_start[:, None]).astype(jnp.int32), axis=1),
                        N_EXPERTS - 1).astype(jnp.int32)
    n_used = (pad_end[-1] // MOE_BLOCK).astype(jnp.int32).reshape(1)
    filled_end = pad_start + sizes
    n_valid = jnp.zeros((n_blocks,), jnp.int32)
    for e in range(N_EXPERTS):
        n_valid = jnp.where(blk_e == e, jnp.clip(filled_end[e] - blk_start, 0, MOE_BLOCK), n_valid)
    return dest_kt, blk_e, n_used, n_valid, n_blocks


def _rope_tables(s, gain):
    n_rows = s // GRID_W
    row = jnp.repeat(jnp.arange(n_rows), GRID_W).astype(F32)
    col = (jnp.arange(s) % GRID_W).astype(F32)
    half = HEAD_DIM // 2
    inv = ROPE_THETA ** (-jnp.arange(0, half, 2, dtype=F32) / half)
    ang_r = row[:, None] * inv
    ang_c = col[:, None] * inv
    cos = jnp.concatenate([jnp.cos(ang_r)] * 2 + [jnp.cos(ang_c)] * 2, axis=-1)
    sin = jnp.concatenate([-jnp.sin(ang_r), jnp.sin(ang_r), -jnp.sin(ang_c), jnp.sin(ang_c)], axis=-1)
    gain = gain.astype(F32)
    return cos * gain[None, :], sin * gain[_PARTNER][None, :]


_q = HEAD_DIM // 4
_PARTNER = np.concatenate([np.arange(_q, 2 * _q), np.arange(0, _q),
                           np.arange(3 * _q, 4 * _q), np.arange(2 * _q, 3 * _q)])


def _partner_cols(w, n_heads):
    idx = (np.arange(n_heads)[:, None] * HEAD_DIM + _PARTNER[None, :]).reshape(-1)
    return w[:, idx]


def kernel(x, c, w_ada, b_ada, norm1_g, w_in, q_norm_g, k_norm_g, rel_bias, w_br_a, w_br_b, w_out,
           norm2_g, w_router, b_router, w_gate_up, b_gate_up, w_down, b_down, final_norm_g):
    bsz, s, d = x.shape
    depth = w_ada.shape[0]
    mod = _ada_mod(c, w_ada, b_ada)
    n_streams = N_STREAMS if bsz % N_STREAMS == 0 else 1
    hb = bsz // n_streams
    ht = hb * s
    streams = range(n_streams)
    xs2 = [x[i * hb:(i + 1) * hb].reshape(ht, d) for i in streams]
    q_off = 3 * A_WIDTH
    k_off = q_off + B_Q
    for l in range(depth):
        mods = [[mod[l, i * hb:(i + 1) * hb, j * d:(j + 1) * d].reshape(hb, 1, d) for j in range(N_MOD)]
                for i in streams]
        w = w_in[l]
        wq, wk = w[:, q_off:q_off + B_Q], w[:, k_off:k_off + B_KV]
        w_ext = jnp.concatenate([wq, _partner_cols(wq, B_HEADS), wk, _partner_cols(wk, B_KV_HEADS),
                                 w[:, k_off + B_KV:k_off + 2 * B_KV], w[:, :q_off],
                                 w[:, k_off + 2 * B_KV:]], axis=1).astype(BF16)
        tabs = [jnp.tile(tb, (1, 2)) for tb in
                _rope_tables(s, q_norm_g[l]) + _rope_tables(s, k_norm_g[l])]
        wa, wb, wo = w_br_a[l].astype(BF16), w_br_b[l].astype(BF16), w_out[l].astype(BF16)
        proj = [_projection(xs2[i], mods[i][0], mods[i][1], norm1_g[l].reshape(1, d), w_ext, tabs, hb, s)
                for i in streams]
        attn_a = [[_dilated_group(proj[i][g], rel_bias[:, g * A_HEADS:(g + 1) * A_HEADS], dil, hb, s)
                   for g, (_, dil) in enumerate(A_GROUPS)] for i in streams]
        attn_b = [_gqa_attention(proj[i][3], proj[i][4], proj[i][5], q_norm_g[l], k_norm_g[l])
                  .reshape(ht, B_Q) for i in streams]
        xs2 = [_merge([o for o, _ in attn_a[i]], [ls for _, ls in attn_a[i]], attn_b[i], proj[i][6],
                      xs2[i], mods[i][2], wa, wb, wo, s) for i in streams]
        routed = [_router(xs2[i], mods[i][3], mods[i][4], norm2_g[l].reshape(1, d), w_router[l],
                          b_router[l], s) for i in streams]
        plans = [_moe_plan(routed[i][1], routed[i][3], routed[i][4]) for i in streams]
        slots = [_sc_scatter_rows(routed[i][0], plans[i][0].reshape(TOP_K * ht), plans[i][4] * MOE_BLOCK)
                 for i in streams]
        outs = [_experts(slots[i], plans[i][1], plans[i][2], plans[i][3],
                         w_gate_up, b_gate_up, w_down, b_down, l) for i in streams]
        xs2 = [_combine(plans[i][0], outs[i], routed[i][2].T, xs2[i], mods[i][5],
                        final_norm_g.reshape(1, d), s, final=(l == depth - 1)) for i in streams]
    return jnp.concatenate([xh.reshape(hb, s, d) for xh in xs2], axis=0)
```

```python
import functools
import math

import numpy as np
import jax
import jax.numpy as jnp
from jax import lax
from jax.experimental import pallas as pl
from jax.experimental.pallas import tpu as pltpu
from jax.experimental.pallas import tpu_sc as plsc

F32 = jnp.float32
BF16 = jnp.bfloat16

HEAD_DIM = 64
LANE = 128
PAIR = 2 * HEAD_DIM
A_GROUPS = ((128, 1), (512, 4), (2048, 16))
A_HEADS = 4
A_WIDTH = len(A_GROUPS) * A_HEADS * HEAD_DIM
A_OUT = A_HEADS * HEAD_DIM
A_GROUP_COLS = 3 * A_OUT
B_HEADS = 16
B_KV_HEADS = 4
B_GRP = B_HEADS // B_KV_HEADS
B_Q = B_HEADS * HEAD_DIM
B_KV = B_KV_HEADS * HEAD_DIM
GRID_W = 64
ROPE_THETA = 10000.0
REL_BUCKETS = 32
REL_MAX_DIST = 1024
N_EXPERTS = 32
TOP_K = 4
SWIGLU_LIMIT = 7.0
SWIGLU_ALPHA = 1.702
MOE_BLOCK = 512
COMBINE_CHUNKS = 4
N_MOD = 6
N_STREAMS = 1
EPS = 1e-6
NEG_INF = -1e30
LOG2E = math.log2(math.e)
N_SIDE = 64
DIL_SUB = 2 * N_SIDE
DIL_STEP_ROWS = 512

VMEM_LIMIT = 56 * 1024 * 1024


def _pack_bf16_pairs(x):
    bits = pltpu.bitcast(x.astype(BF16).astype(F32), jnp.uint32)
    half = x.shape[1] // 2
    return (bits[:, :half] >> 16) | bits[:, half:]


def _unpack_bf16_pairs(u):
    return jnp.concatenate([pltpu.bitcast(u << 16, F32),
                            pltpu.bitcast(u & jnp.uint32(0xFFFF0000), F32)], axis=1)


def _cparams(*sem):
    return pltpu.CompilerParams(dimension_semantics=sem, vmem_limit_bytes=VMEM_LIMIT)


def _ada_kernel(c_ref, w_ref, b_ref, o_ref):
    c = c_ref[...]
    ca = (c * jax.nn.sigmoid(c)).astype(BF16)
    o_ref[...] = jnp.dot(ca, w_ref[...].astype(BF16), preferred_element_type=F32) + b_ref[...]


def _ada_mod(c, w_ada, b_ada):
    depth, d, n = w_ada.shape
    bsz = c.shape[0]
    tn = 1536
    return pl.pallas_call(
        _ada_kernel,
        out_shape=jax.ShapeDtypeStruct((depth, bsz, n), F32),
        grid=(depth, n // tn),
        in_specs=[pl.BlockSpec((bsz, d), lambda l, j: (0, 0)),
                  pl.BlockSpec((None, d, tn), lambda l, j: (l, 0, j)),
                  pl.BlockSpec((None, 1, tn), lambda l, j: (l, 0, j))],
        out_specs=pl.BlockSpec((None, bsz, tn), lambda l, j: (l, 0, j)),
        compiler_params=_cparams("arbitrary", "arbitrary"),
        name="ada_mod",
    )(c, w_ada, b_ada.reshape(depth, 1, n))


def _proj_kernel(x_ref, sh_ref, sc_ref, g_ref, w_ref, aq_ref, bq_ref, ak_ref, bk_ref,
                 za0_ref, za1_ref, za2_ref, q_ref, k_ref, v_ref, gate_ref, fold_ref):
    x = x_ref[...]
    ms = jnp.mean(x * x, axis=-1, keepdims=True)
    h = x * lax.rsqrt(ms + EPS) * g_ref[...]
    h = h * (1.0 + sc_ref[...]) + sh_ref[...]
    hb = h.astype(BF16)

    def mm(lo, hi):
        return jnp.dot(hb, w_ref[:, lo:hi], preferred_element_type=F32)

    tm = x.shape[0]
    o = 0
    zq = mm(o, o + B_Q); o += B_Q
    zqp = mm(o, o + B_Q); o += B_Q
    zk = mm(o, o + B_KV); o += B_KV
    zkp = mm(o, o + B_KV); o += B_KV
    zv = mm(o, o + B_KV); o += B_KV
    lane = lax.broadcasted_iota(jnp.int32, (tm, PAIR), 1)
    first = lane < HEAD_DIM

    def pair_norm_rope(z, zp, a, b):
        zz = z * z
        ss = jnp.where(first,
                       jnp.sum(jnp.where(first, zz, 0.0), axis=-1, keepdims=True),
                       jnp.sum(jnp.where(first, 0.0, zz), axis=-1, keepdims=True))
        return lax.rsqrt(ss * (1.0 / HEAD_DIM) + EPS) * (z * a + zp * b)

    aq, bq, ak, bk = aq_ref[...], bq_ref[...], ak_ref[...], bk_ref[...]
    for j in range(B_HEADS // 2):
        cols = slice(j * PAIR, (j + 1) * PAIR)
        q_ref[j] = (pair_norm_rope(zq[:, cols], zqp[:, cols], aq, bq)
                    * (LOG2E * HEAD_DIM ** -0.5)).astype(BF16)
    ones_col = jnp.where(lane == HEAD_DIM, 1.0, 0.0)
    for j in range(B_KV_HEADS // 2):
        cols = slice(j * PAIR, (j + 1) * PAIR)
        kk = pair_norm_rope(zk[:, cols], zkp[:, cols], ak, bk)
        kk_sw = pltpu.roll(kk, HEAD_DIM, axis=1)
        k_ref[2 * j] = jnp.where(first, kk, kk_sw).astype(BF16)
        k_ref[2 * j + 1] = jnp.where(first, kk_sw, kk).astype(BF16)
        vv = zv[:, cols]
        v_ref[2 * j] = jnp.where(first, vv, ones_col).astype(BF16)
        v_ref[2 * j + 1] = jnp.where(first, pltpu.roll(vv, HEAD_DIM, axis=1), ones_col).astype(BF16)
    for za_ref, (_, dil) in zip((za0_ref, za1_ref, za2_ref), A_GROUPS):
        z = mm(o, o + A_GROUP_COLS)
        o += A_GROUP_COLS
        if dil == 1:
            za_ref[...] = z.astype(BF16)
        else:
            for c in range(A_GROUP_COLS // LANE):
                fold_ref[c] = z[:, c * LANE:(c + 1) * LANE]
            for r in range(dil):
                for c in range(A_GROUP_COLS // LANE):
                    col = r * A_GROUP_COLS + c * LANE
                    za_ref[:, col:col + LANE] = (
                        fold_ref[c, pl.ds(r, tm // dil, stride=dil), :].astype(BF16))
    gate_ref[...] = jax.nn.sigmoid(mm(o, o + gate_ref.shape[-1])).astype(BF16)


def _projection(x2, sh, sc, g, w_ext, tabs, bsz, s):
    t, d = x2.shape
    tm = 256
    tpb = s // tm
    n_ext = w_ext.shape[1]
    n_gate = 2 * d
    row = lambda i: (i, 0)
    per_b = lambda i: (i // tpb, 0, 0)
    tab = lambda i: (i % tpb, 0)
    hm = lambda i: (i // tpb, 0, i % tpb, 0)
    out_shape = (
        [jax.ShapeDtypeStruct((bsz, s // dil, dil * A_GROUP_COLS), BF16) for _, dil in A_GROUPS]
        + [jax.ShapeDtypeStruct((bsz, B_HEADS // 2, s, PAIR), BF16),
           jax.ShapeDtypeStruct((bsz, B_KV_HEADS, s, PAIR), BF16),
           jax.ShapeDtypeStruct((bsz, B_KV_HEADS, s, PAIR), BF16),
           jax.ShapeDtypeStruct((t, n_gate), BF16)])
    out_specs = (
        [pl.BlockSpec((None, tm // dil, dil * A_GROUP_COLS), lambda i: (i // tpb, i % tpb, 0))
         for _, dil in A_GROUPS]
        + [pl.BlockSpec((None, B_HEADS // 2, tm, PAIR), hm),
           pl.BlockSpec((None, B_KV_HEADS, tm, PAIR), hm),
           pl.BlockSpec((None, B_KV_HEADS, tm, PAIR), hm),
           pl.BlockSpec((tm, n_gate), row)])
    return pl.pallas_call(
        _proj_kernel,
        out_shape=out_shape,
        grid=(t // tm,),
        in_specs=[pl.BlockSpec((tm, d), row),
                  pl.BlockSpec((None, 1, d), per_b),
                  pl.BlockSpec((None, 1, d), per_b),
                  pl.BlockSpec((1, d), lambda i: (0, 0)),
                  pl.BlockSpec((d, n_ext), lambda i: (0, 0))]
                 + [pl.BlockSpec((tm, PAIR), tab)] * 4,
        out_specs=out_specs,
        scratch_shapes=[pltpu.VMEM((A_GROUP_COLS // LANE, tm, LANE), F32)],
        compiler_params=_cparams("arbitrary"),
        name="in_proj",
    )(x2, sh, sc, g, w_ext, *tabs)


def _dilated_kernel(q_ref, kp_ref, kc_ref, kn_ref, vp_ref, vc_ref, vn_ref, bias_ref,
                    o_ref, lse_ref, *, tq, seq_len):
    i = pl.program_id(2)
    k = jnp.concatenate([kp_ref[...], kc_ref[...], kn_ref[...]], axis=0)
    v = jnp.concatenate([vp_ref[...], vc_ref[...], vn_ref[...]], axis=0)
    nk = DIL_SUB + 2 * N_SIDE
    col = lax.broadcasted_iota(jnp.int32, (DIL_SUB, nk), 1)
    for sb in range(tq // DIL_SUB):
        rows = slice(sb * DIL_SUB, (sb + 1) * DIL_SUB)
        q = q_ref[rows, :]
        kb = k[sb * DIL_SUB:sb * DIL_SUB + nk]
        vb = v[sb * DIL_SUB:sb * DIL_SUB + nk]
        kpos = i * tq + sb * DIL_SUB - N_SIDE + col
        valid = (kpos >= 0) & (kpos < seq_len)
        for hh in range(A_HEADS):
            sl = slice(hh * HEAD_DIM, (hh + 1) * HEAD_DIM)
            sc = lax.dot_general(q[:, sl], kb[:, sl], (((1,), (1,)), ((), ())),
                                 preferred_element_type=F32)
            sc = sc * (HEAD_DIM ** -0.5) + bias_ref[hh]
            sc = jnp.where(valid, sc, NEG_INF)
            m = jnp.max(sc, axis=-1, keepdims=True)
            p = jnp.exp(sc - m)
            den = jnp.sum(p, axis=-1, keepdims=True)
            o = jnp.dot(p.astype(BF16), vb[:, sl], preferred_element_type=F32) / den
            o_ref[rows, sl] = o
            lse_ref[rows, sl] = jnp.broadcast_to(m + jnp.log(den), (DIL_SUB, HEAD_DIM))


def _t5_bucket(rel):
    nb = REL_BUCKETS // 2
    max_exact = nb // 2
    ret = jnp.where(rel > 0, nb, 0)
    n = jnp.abs(rel)
    nf = jnp.maximum(n, 1).astype(F32)
    large = max_exact + (jnp.log(nf / max_exact) / math.log(REL_MAX_DIST / max_exact)
                         * (nb - max_exact)).astype(jnp.int32)
    large = jnp.minimum(large, nb - 1)
    return ret + jnp.where(n < max_exact, n, large)


def _band_bias(rel_bias_g, dilation, tq):
    nk = tq + 2 * N_SIDE
    rel = jnp.arange(nk)[None, :] - N_SIDE - jnp.arange(tq)[:, None]
    bucket = _t5_bucket(rel * dilation)[None]
    bias = jnp.full((A_HEADS,) + rel.shape, NEG_INF, F32)
    for b in range(REL_BUCKETS):
        bias = jnp.where(bucket == b, rel_bias_g[b].astype(F32)[:, None, None], bias)
    return jnp.where((jnp.abs(rel) <= N_SIDE)[None], bias, NEG_INF)


def _dilated_group(za, rel_bias_g, dilation, bsz, s):
    ll = s // dilation
    tq = min(DIL_STEP_ROWS, ll)
    assert ll % tq == 0 and tq % DIL_SUB == 0
    zv = za
    nhalf = ll // N_SIDE
    per = tq // N_SIDE
    cur = lambda c: (lambda b, r, i: (b, i, 3 * r + c))
    prv = lambda c: (lambda b, r, i: (b, jnp.maximum(per * i - 1, 0), 3 * r + c))
    nxt = lambda c: (lambda b, r, i: (b, jnp.minimum(per * (i + 1), nhalf - 1), 3 * r + c))
    full = pl.BlockSpec((None, tq, A_OUT), cur(0))
    half = lambda f: pl.BlockSpec((None, N_SIDE, A_OUT), f)
    bias = _band_bias(rel_bias_g, dilation, DIL_SUB)
    out_sd = jax.ShapeDtypeStruct((bsz, ll, dilation * A_OUT), F32)
    out_spec = pl.BlockSpec((None, tq, A_OUT), lambda b, r, i: (b, i, r))
    o, lse = pl.pallas_call(
        functools.partial(_dilated_kernel, tq=tq, seq_len=ll),
        out_shape=[out_sd, out_sd],
        grid=(bsz, dilation, ll // tq),
        in_specs=[full,
                  half(prv(1)), pl.BlockSpec((None, tq, A_OUT), cur(1)), half(nxt(1)),
                  half(prv(2)), pl.BlockSpec((None, tq, A_OUT), cur(2)), half(nxt(2)),
                  pl.BlockSpec(bias.shape, lambda b, r, i: (0, 0, 0))],
        out_specs=[out_spec, out_spec],
        compiler_params=_cparams("arbitrary", "arbitrary", "arbitrary"),
        name=f"dilated_attn_d{dilation}",
    )(zv, zv, zv, zv, zv, zv, zv, bias)
    return o, lse


GQA_NO_SHIFT_MAX_LOG2 = 80.0
GQA_KEY_CHUNK = 128


def _stacked_heads(q_ref):
    first = lax.broadcasted_iota(jnp.int32, q_ref.shape[1:], 1) < HEAD_DIM
    blocks = []
    for j in range(q_ref.shape[0]):
        qp = q_ref[j]
        blocks += [jnp.where(first, qp, jnp.zeros_like(qp)), jnp.where(first, jnp.zeros_like(qp), qp)]
    return jnp.concatenate(blocks, axis=0)


def _gqa_kernel_noshift(q_ref, k_ref, v_ref, o_ref):
    tq = q_ref.shape[1]
    q = _stacked_heads(q_ref)
    nt = (((1,), (1,)), ((), ()))
    acc = jnp.zeros((q.shape[0], PAIR), F32)
    for c in range(k_ref.shape[0] // GQA_KEY_CHUNK):
        rows = slice(c * GQA_KEY_CHUNK, (c + 1) * GQA_KEY_CHUNK)
        p = jnp.exp2(lax.dot_general(q, k_ref[rows, :], nt, preferred_element_type=F32))
        acc = acc + jnp.dot(p.astype(BF16), v_ref[rows, :], preferred_element_type=F32)
    o = acc[:, :HEAD_DIM] / acc[:, HEAD_DIM:HEAD_DIM + 1]
    for hh in range(q.shape[0] // tq):
        o_ref[:, hh * HEAD_DIM:(hh + 1) * HEAD_DIM] = o[hh * tq:(hh + 1) * tq].astype(BF16)


def _gqa_kernel_rowmax(q_ref, k_ref, v_ref, o_ref):
    tq = q_ref.shape[1]
    q = _stacked_heads(q_ref)
    k = k_ref[...]
    v = v_ref[:, :HEAD_DIM]
    for hh in range(q.shape[0] // tq):
        sc = lax.dot_general(q[hh * tq:(hh + 1) * tq], k, (((1,), (1,)), ((), ())),
                             preferred_element_type=F32)
        m = jnp.max(sc, axis=-1, keepdims=True)
        p = jnp.exp2(sc - m)
        den = jnp.sum(p, axis=-1, keepdims=True)
        o = jnp.dot(p.astype(BF16), v, preferred_element_type=F32) / den
        o_ref[:, hh * HEAD_DIM:(hh + 1) * HEAD_DIM] = o.astype(BF16)


def _gqa_call(body, name, tq, q, k, v):
    bsz, _, s, _ = q.shape
    return pl.pallas_call(
        body,
        out_shape=jax.ShapeDtypeStruct((bsz, s, B_Q), BF16),
        grid=(bsz, B_KV_HEADS, s // tq),
        in_specs=[pl.BlockSpec((None, B_GRP // 2, tq, PAIR), lambda b, h, i: (b, h, i, 0)),
                  pl.BlockSpec((None, None, s, PAIR), lambda b, h, i: (b, h, 0, 0)),
                  pl.BlockSpec((None, None, s, PAIR), lambda b, h, i: (b, h, 0, 0))],
        out_specs=pl.BlockSpec((None, tq, B_GRP * HEAD_DIM), lambda b, h, i: (b, i, h)),
        compiler_params=_cparams("arbitrary", "arbitrary", "arbitrary"),
        name=name,
    )(q, k, v)


def _gqa_attention(q, k, v, q_gain, k_gain):
    bound = (HEAD_DIM ** 0.5) * LOG2E * 1.01 * jnp.max(jnp.abs(q_gain)) * jnp.max(jnp.abs(k_gain))
    return lax.cond(bound <= GQA_NO_SHIFT_MAX_LOG2,
                    functools.partial(_gqa_call, _gqa_kernel_noshift, "gqa_attn", 1024),
                    functools.partial(_gqa_call, _gqa_kernel_rowmax, "gqa_attn_rowmax", 256),
                    q, k, v)


def _merge_kernel(o0_ref, o1_ref, o2_ref, l0_ref, l1_ref, l2_ref, ob_ref, gate_ref, x_ref, g1_ref,
                  wa_ref, wb_ref, wo_ref, xo_ref, *unfold_refs):
    tm, d = x_ref.shape
    scratch = list(unfold_refs)

    def token_order(ref, dil):
        if dil == 1:
            return ref[...]
        buf = scratch.pop()
        for r in range(dil):
            for c in range(A_OUT // LANE):
                col = r * A_OUT + c * LANE
                buf[c, pl.ds(r, tm // dil, stride=dil), :] = ref[:, col:col + LANE]
        return jnp.concatenate([buf[c] for c in range(A_OUT // LANE)], axis=1)

    dils = [dil for _, dil in A_GROUPS]
    o0, o1, o2 = [token_order(r, dl) for r, dl in zip((o0_ref, o1_ref, o2_ref), dils)]
    l0, l1, l2 = [token_order(r, dl) for r, dl in zip((l0_ref, l1_ref, l2_ref), dils)]
    m = jnp.maximum(jnp.maximum(l0, l1), l2)
    e0, e1, e2 = jnp.exp(l0 - m), jnp.exp(l1 - m), jnp.exp(l2 - m)
    tot = e0 + e1 + e2
    oa = (e0 / tot) * o0 + (e1 / tot) * o1 + (e2 / tot) * o2
    ya = jnp.dot(oa.astype(BF16), wa_ref[...], preferred_element_type=F32)
    yb = jnp.dot(ob_ref[...], wb_ref[...], preferred_element_type=F32)
    merged = gate_ref[:, :d].astype(F32) * ya + gate_ref[:, d:].astype(F32) * yb
    y = jnp.dot(merged.astype(BF16), wo_ref[...], preferred_element_type=F32)
    xo_ref[...] = x_ref[...] + g1_ref[...] * y


def _merge(os_, ls_, ob, gate, x2, g1, wa, wb, wo, s):
    t, d = x2.shape
    tm = 256
    tpb = s // tm
    row = lambda i: (i, 0)
    const = lambda i: (0, 0)
    a_specs = [pl.BlockSpec((None, tm // dil, dil * A_OUT), lambda i: (i // tpb, i % tpb, 0))
               for _, dil in A_GROUPS]
    n_unfold = 2 * sum(1 for _, dil in A_GROUPS if dil > 1)
    return pl.pallas_call(
        _merge_kernel,
        out_shape=jax.ShapeDtypeStruct((t, d), F32),
        grid=(t // tm,),
        scratch_shapes=[pltpu.VMEM((A_OUT // LANE, tm, LANE), F32)] * n_unfold,
        in_specs=a_specs * 2
                 + [pl.BlockSpec((tm, B_Q), row),
                    pl.BlockSpec((tm, 2 * d), row),
                    pl.BlockSpec((tm, d), row),
                    pl.BlockSpec((None, 1, d), lambda i: (i // tpb, 0, 0)),
                    pl.BlockSpec(wa.shape, const),
                    pl.BlockSpec(wb.shape, const),
                    pl.BlockSpec(wo.shape, const)],
        out_specs=pl.BlockSpec((tm, d), row),
        compiler_params=_cparams("arbitrary"),
        name="branch_merge",
    )(*os_, *ls_, ob, gate, x2, g1, wa, wb, wo)


def _router_kernel(x_ref, sh_ref, sc_ref, g_ref, whi_ref, wlo_ref, b_ref,
                   h_ref, e_ref, w_ref, pos_ref, cnt_ref, run_ref):
    @pl.when(pl.program_id(0) == 0)
    def _():
        run_ref[...] = jnp.zeros_like(run_ref)

    x = x_ref[...]
    ms = jnp.mean(x * x, axis=-1, keepdims=True)
    h = x * lax.rsqrt(ms + EPS) * g_ref[...]
    h = h * (1.0 + sc_ref[...]) + sh_ref[...]
    h_ref[...] = _pack_bf16_pairs(h)
    hhi = h.astype(BF16)
    hlo = (h - hhi.astype(F32)).astype(BF16)
    nt = (((1,), (1,)), ((), ()))
    dotf = lambda a, b: lax.dot_general(a, b, nt, preferred_element_type=F32)
    logits = dotf(whi_ref[...], hhi) + (dotf(whi_ref[...], hlo) + dotf(wlo_ref[...], hhi)) + b_ref[...]
    ne, tm = logits.shape
    iota = lax.broadcasted_iota(jnp.int32, (ne, tm), 0).astype(F32)
    vals, idxs = [], []
    cur = logits
    for _ in range(TOP_K):
        m = jnp.max(cur, axis=0, keepdims=True)
        idx = jnp.min(jnp.where(cur == m, iota, float(ne)), axis=0, keepdims=True)
        vals.append(m)
        idxs.append(idx)
        cur = jnp.where(iota == idx, -jnp.inf, cur)
    tv = jnp.concatenate(vals, axis=0)
    ex = jnp.exp(tv - tv[0:1])
    w_ref[...] = ex / jnp.sum(ex, axis=0, keepdims=True)
    e_ref[...] = jnp.concatenate(idxs, axis=0).astype(jnp.int32)
    onehot = jnp.zeros((ne, tm), F32)
    for idx in idxs:
        onehot = onehot + jnp.where(iota == idx, 1.0, 0.0)
    earlier = (lax.broadcasted_iota(jnp.int32, (tm, tm), 0)
               < lax.broadcasted_iota(jnp.int32, (tm, tm), 1))
    rank = jnp.dot(onehot.astype(BF16), jnp.where(earlier, 1.0, 0.0).astype(BF16),
                   preferred_element_type=F32) + run_ref[...]
    pos_ref[...] = jnp.concatenate(
        [jnp.sum(jnp.where(iota == idx, rank, 0.0), axis=0, keepdims=True) for idx in idxs],
        axis=0).astype(jnp.int32)
    run_ref[...] = run_ref[...] + jnp.sum(onehot, axis=1, keepdims=True)
    cnt_ref[...] = run_ref[...]


def _router(x2, sh, sc, g, w_router, b_router, s):
    t, d = x2.shape
    tm = 256
    tpb = s // tm
    ne = w_router.shape[1]
    wt = w_router.T
    whi = wt.astype(BF16)
    wlo = (wt - whi.astype(F32)).astype(BF16)
    per_b = lambda i: (i // tpb, 0, 0)
    const = lambda i: (0, 0)
    return pl.pallas_call(
        _router_kernel,
        out_shape=[jax.ShapeDtypeStruct((t, d // 2), jnp.uint32),
                   jax.ShapeDtypeStruct((TOP_K, t), jnp.int32),
                   jax.ShapeDtypeStruct((TOP_K, t), F32),
                   jax.ShapeDtypeStruct((TOP_K, t), jnp.int32),
                   jax.ShapeDtypeStruct((ne, 1), F32)],
        grid=(t // tm,),
        in_specs=[pl.BlockSpec((tm, d), lambda i: (i, 0)),
                  pl.BlockSpec((None, 1, d), per_b),
                  pl.BlockSpec((None, 1, d), per_b),
                  pl.BlockSpec((1, d), const),
                  pl.BlockSpec((ne, d), const),
                  pl.BlockSpec((ne, d), const),
                  pl.BlockSpec((ne, 1), const)],
        out_specs=[pl.BlockSpec((tm, d // 2), lambda i: (i, 0)),
                   pl.BlockSpec((TOP_K, tm), lambda i: (0, i)),
                   pl.BlockSpec((TOP_K, tm), lambda i: (0, i)),
                   pl.BlockSpec((TOP_K, tm), lambda i: (0, i)),
                   pl.BlockSpec((ne, 1), const)],
        scratch_shapes=[pltpu.VMEM((ne, 1), F32)],
        compiler_params=_cparams("arbitrary"),
        name="moe_router",
    )(x2, sh, sc, g, whi, wlo, b_router.reshape(ne, 1))


SC_CORES = 2
SC_SUBCORES = 16
SC_ROWS = 32


def _sc_scatter_rows(rows, dest_kt, n_slots):
    t, d = rows.shape
    n_workers = SC_CORES * SC_SUBCORES
    per_w = t // n_workers
    n_chunks = per_w // SC_ROWS
    assert per_w * n_workers == t and n_chunks * SC_ROWS == per_w and n_chunks % 2 == 0
    idx = dest_kt.reshape(TOP_K, n_workers, n_chunks, SC_ROWS).transpose(1, 2, 0, 3)
    idx = idx.reshape(n_workers * n_chunks * TOP_K, SC_ROWS)
    lists_per_w = n_chunks * TOP_K
    mesh = plsc.VectorSubcoreMesh(core_axis_name="c", subcore_axis_name="s")

    @functools.partial(
        pl.kernel, mesh=mesh,
        out_type=jax.ShapeDtypeStruct((n_slots, d), rows.dtype),
        scratch_types=[pltpu.VMEM((lists_per_w, SC_ROWS), jnp.int32),
                       pltpu.VMEM((SC_ROWS, d), rows.dtype),
                       pltpu.VMEM((SC_ROWS, d), rows.dtype)] + [pltpu.SemaphoreType.DMA] * 4,
        name="sc_scatter_rows",
    )
    def scatter_kernel(rows_hbm, idx_hbm, out_hbm, idx_v, buf_a, buf_b, ld_a, ld_b, st_a, st_b):
        wid = lax.axis_index("s") * SC_CORES + lax.axis_index("c")
        pltpu.sync_copy(idx_hbm.at[pl.ds(wid * lists_per_w, lists_per_w)], idx_v)

        def load(j, buf, sem):
            return pltpu.make_async_copy(rows_hbm.at[pl.ds(wid * per_w + j * SC_ROWS, SC_ROWS)], buf, sem)

        def scatter_all(j, buf, sem):
            copies = [pltpu.make_async_copy(buf, out_hbm.at[idx_v.at[j * TOP_K + kk]], sem)
                      for kk in range(TOP_K)]
            for cp in copies:
                cp.start()
            for cp in copies:
                cp.wait()

        load(0, buf_a, ld_a).start()

        @pl.loop(0, n_chunks, step=2)
        def _(j):
            load(j + 1, buf_b, ld_b).start()
            load(j, buf_a, ld_a).wait()
            scatter_all(j, buf_a, st_a)

            @pl.when(j + 2 < n_chunks)
            def _():
                load(j + 2, buf_a, ld_a).start()
            load(j + 1, buf_b, ld_b).wait()
            scatter_all(j + 1, buf_b, st_b)

    return scatter_kernel(rows, idx)


def _expert_kernel(blk_e_ref, n_used_ref, n_valid_ref, x_ref, wgu_ref, bgu_ref, wdn_ref, bdn_ref, o_ref,
                   wgu_bf, wdn_bf):
    j = pl.program_id(0)
    d = wdn_bf.shape[1]

    @pl.when(j < n_used_ref[0])
    def _():
        @pl.when((j == 0) | (blk_e_ref[j] != blk_e_ref[jnp.maximum(j - 1, 0)]))
        def _():
            wgu_bf[...] = wgu_ref[...].astype(BF16)
            wdn_bf[...] = wdn_ref[...].astype(BF16)

        row = lax.broadcasted_iota(jnp.int32, x_ref.shape, 0)
        packed = jnp.where(row < n_valid_ref[j], x_ref[...], jnp.uint32(0))
        xb = _unpack_bf16_pairs(packed).astype(BF16)
        gu = jnp.dot(xb, wgu_bf[...], preferred_element_type=F32) + bgu_ref[...]
        x_glu = jnp.minimum(gu[:, :d], SWIGLU_LIMIT)
        x_lin = jnp.clip(gu[:, d:], -SWIGLU_LIMIT, SWIGLU_LIMIT)
        act = x_glu * jax.nn.sigmoid(SWIGLU_ALPHA * x_glu) * (x_lin + 1.0)
        o_ref[...] = _pack_bf16_pairs(
            jnp.dot(act.astype(BF16), wdn_bf[...], preferred_element_type=F32) + bdn_ref[...])

    @pl.when(j >= n_used_ref[0])
    def _():
        o_ref[...] = jnp.zeros_like(o_ref)


def _experts(xs, blk_e, n_used, n_valid, wgu, bgu, wdn, bdn, layer):
    d = wdn.shape[-1]
    n_blocks = blk_e.shape[0]
    depth, ne = wgu.shape[:2]
    by_expert = lambda j, be, nu, nv: (layer, be[j], 0, 0)
    grid_spec = pltpu.PrefetchScalarGridSpec(
        num_scalar_prefetch=3,
        grid=(n_blocks,),
        in_specs=[pl.BlockSpec((MOE_BLOCK, d // 2), lambda j, be, nu, nv: (jnp.minimum(j, nu[0] - 1), 0)),
                  pl.BlockSpec((None, None, d, 2 * d), by_expert),
                  pl.BlockSpec((None, None, 1, 2 * d), by_expert),
                  pl.BlockSpec((None, None, d, d), by_expert),
                  pl.BlockSpec((None, None, 1, d), by_expert)],
        out_specs=pl.BlockSpec((MOE_BLOCK, d // 2), lambda j, be, nu, nv: (j, 0)),
        scratch_shapes=[pltpu.VMEM((d, 2 * d), BF16), pltpu.VMEM((d, d), BF16)],
    )
    return pl.pallas_call(
        _expert_kernel,
        out_shape=jax.ShapeDtypeStruct((n_blocks * MOE_BLOCK, d // 2), jnp.uint32),
        grid_spec=grid_spec,
        compiler_params=_cparams("arbitrary"),
        name="moe_experts",
    )(blk_e, n_used, n_valid, xs, wgu, bgu.reshape(depth, ne, 1, 2 * d), wdn, bdn.reshape(depth, ne, 1, d))


def _sc_gather_rows(table, idx):
    n_out = idx.shape[0]
    d = table.shape[1]
    n_workers = SC_CORES * SC_SUBCORES
    per_w = n_out // n_workers
    n_chunks = per_w // SC_ROWS
    assert per_w * n_workers == n_out and n_chunks * SC_ROWS == per_w and n_chunks % 2 == 0
    mesh = plsc.VectorSubcoreMesh(core_axis_name="c", subcore_axis_name="s")

    @functools.partial(
        pl.kernel, mesh=mesh,
        out_type=jax.ShapeDtypeStruct((n_out, d), table.dtype),
        scratch_types=[pltpu.VMEM((per_w,), jnp.int32),
                       pltpu.VMEM((SC_ROWS, d), table.dtype),
                       pltpu.VMEM((SC_ROWS, d), table.dtype),
                       pltpu.SemaphoreType.DMA, pltpu.SemaphoreType.DMA],
        name="sc_gather_rows",
    )
    def gather_kernel(table_hbm, idx_hbm, out_hbm, idx_v, rows_a, rows_b, sem_a, sem_b):
        base = (lax.axis_index("s") * SC_CORES + lax.axis_index("c")) * per_w
        pltpu.sync_copy(idx_hbm.at[pl.ds(base, per_w)], idx_v)

        def gather(j, buf, sem):
            rows = idx_v.at[pl.ds(j * SC_ROWS, SC_ROWS)]
            return pltpu.make_async_copy(table_hbm.at[rows], buf, sem)

        def write_back(j, buf):
            pltpu.sync_copy(buf, out_hbm.at[pl.ds(base + j * SC_ROWS, SC_ROWS)])

        gather(0, rows_a, sem_a).start()

        @pl.loop(0, n_chunks, step=2)
        def _(j):
            gather(j + 1, rows_b, sem_b).start()
            gather(j, rows_a, sem_a).wait()
            write_back(j, rows_a)

            @pl.when(j + 2 < n_chunks)
            def _():
                gather(j + 2, rows_a, sem_a).start()
            gather(j + 1, rows_b, sem_b).wait()
            write_back(j + 1, rows_b)

    return gather_kernel(table, idx)


def _combine_kernel(y0_ref, y1_ref, y2_ref, y3_ref, w_ref, x_ref, g2_ref, fg_ref, xo_ref, *, final):
    w = w_ref[...]
    y = w[:, 0:1] * _unpack_bf16_pairs(y0_ref[...])
    for kk, y_ref in enumerate((y1_ref, y2_ref, y3_ref), start=1):
        y = y + w[:, kk:kk + 1] * _unpack_bf16_pairs(y_ref[...])
    xn = x_ref[...] + g2_ref[...] * y
    if final:
        ms = jnp.mean(xn * xn, axis=-1, keepdims=True)
        xn = xn * lax.rsqrt(ms + EPS) * fg_ref[...]
    xo_ref[...] = xn


def _combine(dest, out_sorted, gates_t, x2, g2, final_g, s, final):
    t, d = x2.shape
    tm = 256
    tpb = s // tm
    sc_unit = 2 * SC_ROWS * SC_CORES * SC_SUBCORES // TOP_K
    chunkable = t % (COMBINE_CHUNKS * sc_unit) == 0 and t % (COMBINE_CHUNKS * tm) == 0
    n_chunks = COMBINE_CHUNKS if chunkable else 1
    tc = t // n_chunks
    nt = tc // tm
    for c in range(n_chunks):
        yg = _sc_gather_rows(out_sorted, dest[:, c * tc:(c + 1) * tc].reshape(TOP_K * tc))
        plane = lambda kk: pl.BlockSpec((tm, d // 2), lambda i: (kk * nt + i, 0))
        tile = lambda i, c=c: (c * nt + i, 0)
        x2 = pl.pallas_call(
            functools.partial(_combine_kernel, final=final),
            out_shape=jax.ShapeDtypeStruct((t, d), F32),
            grid=(nt,),
            in_specs=[plane(0), plane(1), plane(2), plane(3),
                      pl.BlockSpec((tm, TOP_K), tile),
                      pl.BlockSpec((tm, d), tile),
                      pl.BlockSpec((None, 1, d), lambda i, c=c: ((c * nt + i) // tpb, 0, 0)),
                      pl.BlockSpec((1, d), lambda i: (0, 0))],
            out_specs=pl.BlockSpec((tm, d), tile),
            input_output_aliases={5: 0},
            compiler_params=_cparams("arbitrary"),
            name="moe_combine",
        )(yg, yg, yg, yg, gates_t, x2, g2, final_g)
    return x2


def _moe_plan(top_e, pos, counts):
    t = top_e.shape[1]
    sizes = counts[:, 0].astype(jnp.int32)
    padded = (sizes + MOE_BLOCK - 1) // MOE_BLOCK * MOE_BLOCK
    pad_end = jnp.cumsum(padded)
    pad_start = pad_end - padded
    start_of = jnp.zeros_like(top_e)
    for e in range(N_EXPERTS):
        start_of = jnp.where(top_e == e, pad_start[e], start_of)
    dest_kt = start_of + pos
    n_blocks = -(-t * TOP_K // MOE_BLOCK) + N_EXPERTS
    blk_start = jnp.arange(n_blocks, dtype=jnp.int32) * MOE_BLOCK
    blk_e = jnp.minimum(jnp.sum((pad_end[None, :] <= blk_start[:, None]).astype(jnp.int32), axis=1),
                        N_EXPERTS - 1).astype(jnp.int32)
    n_used = (pad_end[-1] // MOE_BLOCK).astype(jnp.int32).reshape(1)
    filled_end = pad_start + sizes
    n_valid = jnp.zeros((n_blocks,), jnp.int32)
    for e in range(N_EXPERTS):
        n_valid = jnp.where(blk_e == e, jnp.clip(filled_end[e] - blk_start, 0, MOE_BLOCK), n_valid)
    return dest_kt, blk_e, n_used, n_valid, n_blocks


def _rope_tables(s, gain):
    n_rows = s // GRID_W
    row = jnp.repeat(jnp.arange(n_rows), GRID_W).astype(F32)
    col = (jnp.arange(s) % GRID_W).astype(F32)
    half = HEAD_DIM // 2
    inv = ROPE_THETA ** (-jnp.arange(0, half, 2, dtype=F32) / half)
    ang_r = row[:, None] * inv
    ang_c = col[:, None] * inv
    cos = jnp.concatenate([jnp.cos(ang_r)] * 2 + [jnp.cos(ang_c)] * 2, axis=-1)
    sin = jnp.concatenate([-jnp.sin(ang_r), jnp.sin(ang_r), -jnp.sin(ang_c), jnp.sin(ang_c)], axis=-1)
    gain = gain.astype(F32)
    return cos * gain[None, :], sin * gain[_PARTNER][None, :]


_q = HEAD_DIM // 4
_PARTNER = np.concatenate([np.arange(_q, 2 * _q), np.arange(0, _q),
                           np.arange(3 * _q, 4 * _q), np.arange(2 * _q, 3 * _q)])


def _partner_cols(w, n_heads):
    idx = (np.arange(n_heads)[:, None] * HEAD_DIM + _PARTNER[None, :]).reshape(-1)
    return w[:, idx]


def kernel(x, c, w_ada, b_ada, norm1_g, w_in, q_norm_g, k_norm_g, rel_bias, w_br_a, w_br_b, w_out,
           norm2_g, w_router, b_router, w_gate_up, b_gate_up, w_down, b_down, final_norm_g):
    bsz, s, d = x.shape
    depth = w_ada.shape[0]
    mod = _ada_mod(c, w_ada, b_ada)
    n_streams = N_STREAMS if bsz % N_STREAMS == 0 else 1
    hb = bsz // n_streams
    ht = hb * s
    streams = range(n_streams)
    xs2 = [x[i * hb:(i + 1) * hb].reshape(ht, d) for i in streams]
    q_off = 3 * A_WIDTH
    k_off = q_off + B_Q
    for l in range(depth):
        mods = [[mod[l, i * hb:(i + 1) * hb, j * d:(j + 1) * d].reshape(hb, 1, d) for j in range(N_MOD)]
                for i in streams]
        w = w_in[l]
        wq, wk = w[:, q_off:q_off + B_Q], w[:, k_off:k_off + B_KV]
        w_ext = jnp.concatenate([wq, _partner_cols(wq, B_HEADS), wk, _partner_cols(wk, B_KV_HEADS),
                                 w[:, k_off + B_KV:k_off + 2 * B_KV], w[:, :q_off],
                                 w[:, k_off + 2 * B_KV:]], axis=1).astype(BF16)
        tabs = [jnp.tile(tb, (1, 2)) for tb in
                _rope_tables(s, q_norm_g[l]) + _rope_tables(s, k_norm_g[l])]
        wa, wb, wo = w_br_a[l].astype(BF16), w_br_b[l].astype(BF16), w_out[l].astype(BF16)
        proj = [_projection(xs2[i], mods[i][0], mods[i][1], norm1_g[l].reshape(1, d), w_ext, tabs, hb, s)
                for i in streams]
        attn_a = [[_dilated_group(proj[i][g], rel_bias[:, g * A_HEADS:(g + 1) * A_HEADS], dil, hb, s)
                   for g, (_, dil) in enumerate(A_GROUPS)] for i in streams]
        attn_b = [_gqa_attention(proj[i][3], proj[i][4], proj[i][5], q_norm_g[l], k_norm_g[l])
                  .reshape(ht, B_Q) for i in streams]
        xs2 = [_merge([o for o, _ in attn_a[i]], [ls for _, ls in attn_a[i]], attn_b[i], proj[i][6],
                      xs2[i], mods[i][2], wa, wb, wo, s) for i in streams]
        routed = [_router(xs2[i], mods[i][3], mods[i][4], norm2_g[l].reshape(1, d), w_router[l],
                          b_router[l], s) for i in streams]
        plans = [_moe_plan(routed[i][1], routed[i][3], routed[i][4]) for i in streams]
        slots = [_sc_scatter_rows(routed[i][0], plans[i][0].reshape(TOP_K * ht), plans[i][4] * MOE_BLOCK)
                 for i in streams]
        outs = [_experts(slots[i], plans[i][1], plans[i][2], plans[i][3],
                         w_gate_up, b_gate_up, w_down, b_down, l) for i in streams]
        xs2 = [_combine(plans[i][0], outs[i], routed[i][2].T, xs2[i], mods[i][5],
                        final_norm_g.reshape(1, d), s, final=(l == depth - 1)) for i in streams]
    return jnp.concatenate([xh.reshape(hb, s, d) for xh in xs2], axis=0)
```

```python
import functools
import math

import numpy as np
import jax
import jax.numpy as jnp
from jax import lax
from jax.experimental import pallas as pl
from jax.experimental.pallas import tpu as pltpu
from jax.experimental.pallas import tpu_sc as plsc

F32 = jnp.float32
BF16 = jnp.bfloat16

HEAD_DIM = 64
LANE = 128
PAIR = 2 * HEAD_DIM
A_GROUPS = ((128, 1), (512, 4), (2048, 16))
A_HEADS = 4
A_WIDTH = len(A_GROUPS) * A_HEADS * HEAD_DIM
A_OUT = A_HEADS * HEAD_DIM
A_GROUP_COLS = 3 * A_OUT
B_HEADS = 16
B_KV_HEADS = 4
B_GRP = B_HEADS // B_KV_HEADS
B_Q = B_HEADS * HEAD_DIM
B_KV = B_KV_HEADS * HEAD_DIM
GRID_W = 64
ROPE_THETA = 10000.0
REL_BUCKETS = 32
REL_MAX_DIST = 1024
N_EXPERTS = 32
TOP_K = 4
SWIGLU_LIMIT = 7.0
SWIGLU_ALPHA = 1.702
MOE_BLOCK = 512
COMBINE_CHUNKS = 4
MERGE_ROWS = 128
N_MOD = 6
N_STREAMS = 1
EPS = 1e-6
NEG_INF = -1e30
LOG2E = math.log2(math.e)
N_SIDE = 64
DIL_SUB = 2 * N_SIDE
DIL_STEP_ROWS = 512

VMEM_LIMIT = 56 * 1024 * 1024


def _pack_bf16_pairs(x):
    bits = pltpu.bitcast(x.astype(BF16).astype(F32), jnp.uint32)
    half = x.shape[1] // 2
    return (bits[:, :half] >> 16) | bits[:, half:]


def _unpack_bf16_pairs(u):
    return jnp.concatenate([pltpu.bitcast(u << 16, F32),
                            pltpu.bitcast(u & jnp.uint32(0xFFFF0000), F32)], axis=1)


def _cparams(*sem):
    return pltpu.CompilerParams(dimension_semantics=sem, vmem_limit_bytes=VMEM_LIMIT)


def _ada_kernel(c_ref, w_ref, b_ref, o_ref):
    c = c_ref[...]
    ca = (c * jax.nn.sigmoid(c)).astype(BF16)
    o_ref[...] = jnp.dot(ca, w_ref[...].astype(BF16), preferred_element_type=F32) + b_ref[...]


def _ada_mod(c, w_ada, b_ada):
    depth, d, n = w_ada.shape
    bsz = c.shape[0]
    tn = 1536
    return pl.pallas_call(
        _ada_kernel,
        out_shape=jax.ShapeDtypeStruct((depth, bsz, n), F32),
        grid=(depth, n // tn),
        in_specs=[pl.BlockSpec((bsz, d), lambda l, j: (0, 0)),
                  pl.BlockSpec((None, d, tn), lambda l, j: (l, 0, j)),
                  pl.BlockSpec((None, 1, tn), lambda l, j: (l, 0, j))],
        out_specs=pl.BlockSpec((None, bsz, tn), lambda l, j: (l, 0, j)),
        compiler_params=_cparams("arbitrary", "arbitrary"),
        name="ada_mod",
    )(c, w_ada, b_ada.reshape(depth, 1, n))


def _proj_kernel(x_ref, sh_ref, sc_ref, g_ref, w_ref, aq_ref, bq_ref, ak_ref, bk_ref,
                 za0_ref, za1_ref, za2_ref, q_ref, k_ref, v_ref, gate_ref, fold_ref):
    x = x_ref[...]
    ms = jnp.mean(x * x, axis=-1, keepdims=True)
    h = x * lax.rsqrt(ms + EPS) * g_ref[...]
    h = h * (1.0 + sc_ref[...]) + sh_ref[...]
    hb = h.astype(BF16)

    def mm(lo, hi):
        return jnp.dot(hb, w_ref[:, lo:hi], preferred_element_type=F32)

    tm = x.shape[0]
    o = 0
    zq = mm(o, o + B_Q); o += B_Q
    zqp = mm(o, o + B_Q); o += B_Q
    zk = mm(o, o + B_KV); o += B_KV
    zkp = mm(o, o + B_KV); o += B_KV
    zv = mm(o, o + B_KV); o += B_KV
    lane = lax.broadcasted_iota(jnp.int32, (tm, PAIR), 1)
    first = lane < HEAD_DIM

    def pair_norm_rope(z, zp, a, b):
        zz = z * z
        ss = jnp.where(first,
                       jnp.sum(jnp.where(first, zz, 0.0), axis=-1, keepdims=True),
                       jnp.sum(jnp.where(first, 0.0, zz), axis=-1, keepdims=True))
        return lax.rsqrt(ss * (1.0 / HEAD_DIM) + EPS) * (z * a + zp * b)

    aq, bq, ak, bk = aq_ref[...], bq_ref[...], ak_ref[...], bk_ref[...]
    for j in range(B_HEADS // 2):
        cols = slice(j * PAIR, (j + 1) * PAIR)
        q_ref[j] = (pair_norm_rope(zq[:, cols], zqp[:, cols], aq, bq)
                    * (LOG2E * HEAD_DIM ** -0.5)).astype(BF16)
    ones_col = jnp.where(lane == HEAD_DIM, 1.0, 0.0)
    for j in range(B_KV_HEADS // 2):
        cols = slice(j * PAIR, (j + 1) * PAIR)
        kk = pair_norm_rope(zk[:, cols], zkp[:, cols], ak, bk)
        kk_sw = pltpu.roll(kk, HEAD_DIM, axis=1)
        k_ref[2 * j] = jnp.where(first, kk, kk_sw).astype(BF16)
        k_ref[2 * j + 1] = jnp.where(first, kk_sw, kk).astype(BF16)
        vv = zv[:, cols]
        v_ref[2 * j] = jnp.where(first, vv, ones_col).astype(BF16)
        v_ref[2 * j + 1] = jnp.where(first, pltpu.roll(vv, HEAD_DIM, axis=1), ones_col).astype(BF16)
    for za_ref, (_, dil) in zip((za0_ref, za1_ref, za2_ref), A_GROUPS):
        z = mm(o, o + A_GROUP_COLS)
        o += A_GROUP_COLS
        if dil == 1:
            za_ref[...] = z.astype(BF16)
        else:
            for c in range(A_GROUP_COLS // LANE):
                fold_ref[c] = z[:, c * LANE:(c + 1) * LANE]
            for r in range(dil):
                for c in range(A_GROUP_COLS // LANE):
                    col = r * A_GROUP_COLS + c * LANE
                    za_ref[:, col:col + LANE] = (
                        fold_ref[c, pl.ds(r, tm // dil, stride=dil), :].astype(BF16))
    gate_ref[...] = jax.nn.sigmoid(mm(o, o + gate_ref.shape[-1])).astype(BF16)


def _projection(x2, sh, sc, g, w_ext, tabs, bsz, s):
    t, d = x2.shape
    tm = 256
    tpb = s // tm
    n_ext = w_ext.shape[1]
    n_gate = 2 * d
    row = lambda i: (i, 0)
    per_b = lambda i: (i // tpb, 0, 0)
    tab = lambda i: (i % tpb, 0)
    hm = lambda i: (i // tpb, 0, i % tpb, 0)
    out_shape = (
        [jax.ShapeDtypeStruct((bsz, s // dil, dil * A_GROUP_COLS), BF16) for _, dil in A_GROUPS]
        + [jax.ShapeDtypeStruct((bsz, B_HEADS // 2, s, PAIR), BF16),
           jax.ShapeDtypeStruct((bsz, B_KV_HEADS, s, PAIR), BF16),
           jax.ShapeDtypeStruct((bsz, B_KV_HEADS, s, PAIR), BF16),
           jax.ShapeDtypeStruct((t, n_gate), BF16)])
    out_specs = (
        [pl.BlockSpec((None, tm // dil, dil * A_GROUP_COLS), lambda i: (i // tpb, i % tpb, 0))
         for _, dil in A_GROUPS]
        + [pl.BlockSpec((None, B_HEADS // 2, tm, PAIR), hm),
           pl.BlockSpec((None, B_KV_HEADS, tm, PAIR), hm),
           pl.BlockSpec((None, B_KV_HEADS, tm, PAIR), hm),
           pl.BlockSpec((tm, n_gate), row)])
    return pl.pallas_call(
        _proj_kernel,
        out_shape=out_shape,
        grid=(t // tm,),
        in_specs=[pl.BlockSpec((tm, d), row),
                  pl.BlockSpec((None, 1, d), per_b),
                  pl.BlockSpec((None, 1, d), per_b),
                  pl.BlockSpec((1, d), lambda i: (0, 0)),
                  pl.BlockSpec((d, n_ext), lambda i: (0, 0))]
                 + [pl.BlockSpec((tm, PAIR), tab)] * 4,
        out_specs=out_specs,
        scratch_shapes=[pltpu.VMEM((A_GROUP_COLS // LANE, tm, LANE), F32)],
        compiler_params=_cparams("arbitrary"),
        name="in_proj",
    )(x2, sh, sc, g, w_ext, *tabs)


def _dilated_kernel(q_ref, kp_ref, kc_ref, kn_ref, vp_ref, vc_ref, vn_ref, bias_ref,
                    o_ref, lse_ref, *, tq, seq_len):
    i = pl.program_id(2)
    k = jnp.concatenate([kp_ref[...], kc_ref[...], kn_ref[...]], axis=0)
    v = jnp.concatenate([vp_ref[...], vc_ref[...], vn_ref[...]], axis=0)
    nk = DIL_SUB + 2 * N_SIDE
    col = lax.broadcasted_iota(jnp.int32, (DIL_SUB, nk), 1)
    for sb in range(tq // DIL_SUB):
        rows = slice(sb * DIL_SUB, (sb + 1) * DIL_SUB)
        q = q_ref[rows, :]
        kb = k[sb * DIL_SUB:sb * DIL_SUB + nk]
        vb = v[sb * DIL_SUB:sb * DIL_SUB + nk]
        kpos = i * tq + sb * DIL_SUB - N_SIDE + col
        valid = (kpos >= 0) & (kpos < seq_len)
        for hh in range(A_HEADS):
            sl = slice(hh * HEAD_DIM, (hh + 1) * HEAD_DIM)
            sc = lax.dot_general(q[:, sl], kb[:, sl], (((1,), (1,)), ((), ())),
                                 preferred_element_type=F32)
            sc = sc * (HEAD_DIM ** -0.5) + bias_ref[hh]
            sc = jnp.where(valid, sc, NEG_INF)
            m = jnp.max(sc, axis=-1, keepdims=True)
            p = jnp.exp(sc - m)
            den = jnp.sum(p, axis=-1, keepdims=True)
            o = jnp.dot(p.astype(BF16), vb[:, sl], preferred_element_type=F32) / den
            o_ref[rows, sl] = o
            lse_ref[rows, sl] = jnp.broadcast_to(m + jnp.log(den), (DIL_SUB, HEAD_DIM))


def _t5_bucket(rel):
    nb = REL_BUCKETS // 2
    max_exact = nb // 2
    ret = jnp.where(rel > 0, nb, 0)
    n = jnp.abs(rel)
    nf = jnp.maximum(n, 1).astype(F32)
    large = max_exact + (jnp.log(nf / max_exact) / math.log(REL_MAX_DIST / max_exact)
                         * (nb - max_exact)).astype(jnp.int32)
    large = jnp.minimum(large, nb - 1)
    return ret + jnp.where(n < max_exact, n, large)


def _band_bias(rel_bias_g, dilation, tq):
    nk = tq + 2 * N_SIDE
    rel = jnp.arange(nk)[None, :] - N_SIDE - jnp.arange(tq)[:, None]
    bucket = _t5_bucket(rel * dilation)[None]
    bias = jnp.full((A_HEADS,) + rel.shape, NEG_INF, F32)
    for b in range(REL_BUCKETS):
        bias = jnp.where(bucket == b, rel_bias_g[b].astype(F32)[:, None, None], bias)
    return jnp.where((jnp.abs(rel) <= N_SIDE)[None], bias, NEG_INF)


def _dilated_group(za, rel_bias_g, dilation, bsz, s):
    ll = s // dilation
    tq = min(DIL_STEP_ROWS, ll)
    assert ll % tq == 0 and tq % DIL_SUB == 0
    zv = za
    nhalf = ll // N_SIDE
    per = tq // N_SIDE
    cur = lambda c: (lambda b, r, i: (b, i, 3 * r + c))
    prv = lambda c: (lambda b, r, i: (b, jnp.maximum(per * i - 1, 0), 3 * r + c))
    nxt = lambda c: (lambda b, r, i: (b, jnp.minimum(per * (i + 1), nhalf - 1), 3 * r + c))
    full = pl.BlockSpec((None, tq, A_OUT), cur(0))
    half = lambda f: pl.BlockSpec((None, N_SIDE, A_OUT), f)
    bias = _band_bias(rel_bias_g, dilation, DIL_SUB)
    out_sd = jax.ShapeDtypeStruct((bsz, ll, dilation * A_OUT), F32)
    out_spec = pl.BlockSpec((None, tq, A_OUT), lambda b, r, i: (b, i, r))
    o, lse = pl.pallas_call(
        functools.partial(_dilated_kernel, tq=tq, seq_len=ll),
        out_shape=[out_sd, out_sd],
        grid=(bsz, dilation, ll // tq),
        in_specs=[full,
                  half(prv(1)), pl.BlockSpec((None, tq, A_OUT), cur(1)), half(nxt(1)),
                  half(prv(2)), pl.BlockSpec((None, tq, A_OUT), cur(2)), half(nxt(2)),
                  pl.BlockSpec(bias.shape, lambda b, r, i: (0, 0, 0))],
        out_specs=[out_spec, out_spec],
        compiler_params=_cparams("arbitrary", "arbitrary", "arbitrary"),
        name=f"dilated_attn_d{dilation}",
    )(zv, zv, zv, zv, zv, zv, zv, bias)
    return o, lse


GQA_NO_SHIFT_MAX_LOG2 = 80.0
GQA_KEY_CHUNK = 128


def _stacked_heads(q_ref):
    first = lax.broadcasted_iota(jnp.int32, q_ref.shape[1:], 1) < HEAD_DIM
    blocks = []
    for j in range(q_ref.shape[0]):
        qp = q_ref[j]
        blocks += [jnp.where(first, qp, jnp.zeros_like(qp)), jnp.where(first, jnp.zeros_like(qp), qp)]
    return jnp.concatenate(blocks, axis=0)


def _gqa_kernel_noshift(q_ref, k_ref, v_ref, o_ref):
    tq = q_ref.shape[1]
    q = _stacked_heads(q_ref)
    nt = (((1,), (1,)), ((), ()))
    acc = jnp.zeros((q.shape[0], PAIR), F32)
    for c in range(k_ref.shape[0] // GQA_KEY_CHUNK):
        rows = slice(c * GQA_KEY_CHUNK, (c + 1) * GQA_KEY_CHUNK)
        p = jnp.exp2(lax.dot_general(q, k_ref[rows, :], nt, preferred_element_type=F32))
        acc = acc + jnp.dot(p.astype(BF16), v_ref[rows, :], preferred_element_type=F32)
    o = acc[:, :HEAD_DIM] / acc[:, HEAD_DIM:HEAD_DIM + 1]
    for hh in range(q.shape[0] // tq):
        o_ref[:, hh * HEAD_DIM:(hh + 1) * HEAD_DIM] = o[hh * tq:(hh + 1) * tq].astype(BF16)


def _gqa_kernel_rowmax(q_ref, k_ref, v_ref, o_ref):
    tq = q_ref.shape[1]
    q = _stacked_heads(q_ref)
    k = k_ref[...]
    v = v_ref[:, :HEAD_DIM]
    for hh in range(q.shape[0] // tq):
        sc = lax.dot_general(q[hh * tq:(hh + 1) * tq], k, (((1,), (1,)), ((), ())),
                             preferred_element_type=F32)
        m = jnp.max(sc, axis=-1, keepdims=True)
        p = jnp.exp2(sc - m)
        den = jnp.sum(p, axis=-1, keepdims=True)
        o = jnp.dot(p.astype(BF16), v, preferred_element_type=F32) / den
        o_ref[:, hh * HEAD_DIM:(hh + 1) * HEAD_DIM] = o.astype(BF16)


def _gqa_call(body, name, tq, q, k, v):
    bsz, _, s, _ = q.shape
    return pl.pallas_call(
        body,
        out_shape=jax.ShapeDtypeStruct((bsz, s, B_Q), BF16),
        grid=(bsz, B_KV_HEADS, s // tq),
        in_specs=[pl.BlockSpec((None, B_GRP // 2, tq, PAIR), lambda b, h, i: (b, h, i, 0)),
                  pl.BlockSpec((None, None, s, PAIR), lambda b, h, i: (b, h, 0, 0)),
                  pl.BlockSpec((None, None, s, PAIR), lambda b, h, i: (b, h, 0, 0))],
        out_specs=pl.BlockSpec((None, tq, B_GRP * HEAD_DIM), lambda b, h, i: (b, i, h)),
        compiler_params=_cparams("arbitrary", "arbitrary", "arbitrary"),
        name=name,
    )(q, k, v)


def _gqa_attention(q, k, v, q_gain, k_gain):
    bound = (HEAD_DIM ** 0.5) * LOG2E * 1.01 * jnp.max(jnp.abs(q_gain)) * jnp.max(jnp.abs(k_gain))
    return lax.cond(bound <= GQA_NO_SHIFT_MAX_LOG2,
                    functools.partial(_gqa_call, _gqa_kernel_noshift, "gqa_attn", 1024),
                    functools.partial(_gqa_call, _gqa_kernel_rowmax, "gqa_attn_rowmax", 256),
                    q, k, v)


def _merge_kernel(o0_ref, o1_ref, o2_ref, l0_ref, l1_ref, l2_ref, ob_ref, gate_ref, x_ref, g1_ref,
                  wa_ref, wb_ref, wo_ref, sh_ref, sc_ref, g_ref, whi_ref, wlo_ref, b_ref,
                  xo_ref, h_ref, e_ref, w_ref, pos_ref, cnt_ref, run_ref, *unfold_refs):
    tm, d = x_ref.shape
    scratch = list(unfold_refs)

    def token_order(ref, dil):
        if dil == 1:
            return ref[...]
        buf = scratch.pop()
        for r in range(dil):
            for c in range(A_OUT // LANE):
                col = r * A_OUT + c * LANE
                buf[c, pl.ds(r, tm // dil, stride=dil), :] = ref[:, col:col + LANE]
        return jnp.concatenate([buf[c] for c in range(A_OUT // LANE)], axis=1)

    dils = [dil for _, dil in A_GROUPS]
    o0, o1, o2 = [token_order(r, dl) for r, dl in zip((o0_ref, o1_ref, o2_ref), dils)]
    l0, l1, l2 = [token_order(r, dl) for r, dl in zip((l0_ref, l1_ref, l2_ref), dils)]
    m = jnp.maximum(jnp.maximum(l0, l1), l2)
    e0, e1, e2 = jnp.exp(l0 - m), jnp.exp(l1 - m), jnp.exp(l2 - m)
    tot = e0 + e1 + e2
    oa = ((e0 / tot) * o0 + (e1 / tot) * o1 + (e2 / tot) * o2).astype(BF16)
    xn_rows = []
    for lo in range(0, tm, MERGE_ROWS):
        rows = slice(lo, lo + MERGE_ROWS)
        ya = jnp.dot(oa[rows], wa_ref[...], preferred_element_type=F32)
        yb = jnp.dot(ob_ref[rows, :], wb_ref[...], preferred_element_type=F32)
        merged = gate_ref[rows, :d].astype(F32) * ya + gate_ref[rows, d:].astype(F32) * yb
        y = jnp.dot(merged.astype(BF16), wo_ref[...], preferred_element_type=F32)
        xn_rows.append(x_ref[rows, :] + g1_ref[...] * y)
    xn = jnp.concatenate(xn_rows, axis=0)
    xo_ref[...] = xn
    _route_tile(xn, sh_ref, sc_ref, g_ref, whi_ref, wlo_ref, b_ref, h_ref, e_ref, w_ref, pos_ref, cnt_ref,
                run_ref)


def _merge_and_route(os_, ls_, ob, gate, x2, g1, wa, wb, wo, sh2, sc2, g2n, w_router, b_router, s):
    t, d = x2.shape
    tm = 256
    tpb = s // tm
    ne = w_router.shape[1]
    wt = w_router.T
    whi = wt.astype(BF16)
    wlo = (wt - whi.astype(F32)).astype(BF16)
    row = lambda i: (i, 0)
    const = lambda i: (0, 0)
    per_b = lambda i: (i // tpb, 0, 0)
    by_token = pl.BlockSpec((TOP_K, tm), lambda i: (0, i))
    a_specs = [pl.BlockSpec((None, tm // dil, dil * A_OUT), lambda i: (i // tpb, i % tpb, 0))
               for _, dil in A_GROUPS]
    n_unfold = 2 * sum(1 for _, dil in A_GROUPS if dil > 1)
    return pl.pallas_call(
        _merge_kernel,
        out_shape=[jax.ShapeDtypeStruct((t, d), F32),
                   jax.ShapeDtypeStruct((t, d // 2), jnp.uint32),
                   jax.ShapeDtypeStruct((TOP_K, t), jnp.int32),
                   jax.ShapeDtypeStruct((TOP_K, t), F32),
                   jax.ShapeDtypeStruct((TOP_K, t), jnp.int32),
                   jax.ShapeDtypeStruct((ne, 1), F32)],
        grid=(t // tm,),
        scratch_shapes=[pltpu.VMEM((ne, 1), F32)]
                       + [pltpu.VMEM((A_OUT // LANE, tm, LANE), F32)] * n_unfold,
        in_specs=a_specs * 2
                 + [pl.BlockSpec((tm, B_Q), row),
                    pl.BlockSpec((tm, 2 * d), row),
                    pl.BlockSpec((tm, d), row),
                    pl.BlockSpec((None, 1, d), per_b),
                    pl.BlockSpec(wa.shape, const),
                    pl.BlockSpec(wb.shape, const),
                    pl.BlockSpec(wo.shape, const),
                    pl.BlockSpec((None, 1, d), per_b),
                    pl.BlockSpec((None, 1, d), per_b),
                    pl.BlockSpec((1, d), const),
                    pl.BlockSpec((ne, d), const),
                    pl.BlockSpec((ne, d), const),
                    pl.BlockSpec((ne, 1), const)],
        out_specs=[pl.BlockSpec((tm, d), row),
                   pl.BlockSpec((tm, d // 2), row),
                   by_token, by_token, by_token,
                   pl.BlockSpec((ne, 1), const)],
        compiler_params=_cparams("arbitrary"),
        name="merge_route",
    )(*os_, *ls_, ob, gate, x2, g1, wa, wb, wo, sh2, sc2, g2n, whi, wlo, b_router.reshape(ne, 1))


def _route_tile(x, sh_ref, sc_ref, g_ref, whi_ref, wlo_ref, b_ref,
                h_ref, e_ref, w_ref, pos_ref, cnt_ref, run_ref):
    @pl.when(pl.program_id(0) == 0)
    def _():
        run_ref[...] = jnp.zeros_like(run_ref)

    ms = jnp.mean(x * x, axis=-1, keepdims=True)
    h = x * lax.rsqrt(ms + EPS) * g_ref[...]
    h = h * (1.0 + sc_ref[...]) + sh_ref[...]
    h_ref[...] = _pack_bf16_pairs(h)
    hhi = h.astype(BF16)
    hlo = (h - hhi.astype(F32)).astype(BF16)
    nt = (((1,), (1,)), ((), ()))
    dotf = lambda a, b: lax.dot_general(a, b, nt, preferred_element_type=F32)
    logits = dotf(whi_ref[...], hhi) + (dotf(whi_ref[...], hlo) + dotf(wlo_ref[...], hhi)) + b_ref[...]
    ne, tm = logits.shape
    iota = lax.broadcasted_iota(jnp.int32, (ne, tm), 0).astype(F32)
    vals, idxs = [], []
    cur = logits
    for _ in range(TOP_K):
        m = jnp.max(cur, axis=0, keepdims=True)
        idx = jnp.min(jnp.where(cur == m, iota, float(ne)), axis=0, keepdims=True)
        vals.append(m)
        idxs.append(idx)
        cur = jnp.where(iota == idx, -jnp.inf, cur)
    tv = jnp.concatenate(vals, axis=0)
    ex = jnp.exp(tv - tv[0:1])
    w_ref[...] = ex / jnp.sum(ex, axis=0, keepdims=True)
    e_ref[...] = jnp.concatenate(idxs, axis=0).astype(jnp.int32)
    onehot = jnp.zeros((ne, tm), F32)
    for idx in idxs:
        onehot = onehot + jnp.where(iota == idx, 1.0, 0.0)
    earlier = (lax.broadcasted_iota(jnp.int32, (tm, tm), 0)
               < lax.broadcasted_iota(jnp.int32, (tm, tm), 1))
    rank = jnp.dot(onehot.astype(BF16), jnp.where(earlier, 1.0, 0.0).astype(BF16),
                   preferred_element_type=F32) + run_ref[...]
    pos_ref[...] = jnp.concatenate(
        [jnp.sum(jnp.where(iota == idx, rank, 0.0), axis=0, keepdims=True) for idx in idxs],
        axis=0).astype(jnp.int32)
    run_ref[...] = run_ref[...] + jnp.sum(onehot, axis=1, keepdims=True)
    cnt_ref[...] = run_ref[...]


SC_CORES = 2
SC_SUBCORES = 16
SC_ROWS = 32


def _sc_scatter_rows(rows, dest_kt, n_slots):
    t, d = rows.shape
    n_workers = SC_CORES * SC_SUBCORES
    per_w = t // n_workers
    n_chunks = per_w // SC_ROWS
    assert per_w * n_workers == t and n_chunks * SC_ROWS == per_w and n_chunks % 2 == 0
    idx = dest_kt.reshape(TOP_K, n_workers, n_chunks, SC_ROWS).transpose(1, 2, 0, 3)
    idx = idx.reshape(n_workers * n_chunks * TOP_K, SC_ROWS)
    lists_per_w = n_chunks * TOP_K
    mesh = plsc.VectorSubcoreMesh(core_axis_name="c", subcore_axis_name="s")

    @functools.partial(
        pl.kernel, mesh=mesh,
        out_type=jax.ShapeDtypeStruct((n_slots, d), rows.dtype),
        scratch_types=[pltpu.VMEM((lists_per_w, SC_ROWS), jnp.int32),
                       pltpu.VMEM((SC_ROWS, d), rows.dtype),
                       pltpu.VMEM((SC_ROWS, d), rows.dtype)] + [pltpu.SemaphoreType.DMA] * 4,
        name="sc_scatter_rows",
    )
    def scatter_kernel(rows_hbm, idx_hbm, out_hbm, idx_v, buf_a, buf_b, ld_a, ld_b, st_a, st_b):
        wid = lax.axis_index("s") * SC_CORES + lax.axis_index("c")
        pltpu.sync_copy(idx_hbm.at[pl.ds(wid * lists_per_w, lists_per_w)], idx_v)

        def load(j, buf, sem):
            return pltpu.make_async_copy(rows_hbm.at[pl.ds(wid * per_w + j * SC_ROWS, SC_ROWS)], buf, sem)

        def scatter_all(j, buf, sem):
            copies = [pltpu.make_async_copy(buf, out_hbm.at[idx_v.at[j * TOP_K + kk]], sem)
                      for kk in range(TOP_K)]
            for cp in copies:
                cp.start()
            for cp in copies:
                cp.wait()

        load(0, buf_a, ld_a).start()

        @pl.loop(0, n_chunks, step=2)
        def _(j):
            load(j + 1, buf_b, ld_b).start()
            load(j, buf_a, ld_a).wait()
            scatter_all(j, buf_a, st_a)

            @pl.when(j + 2 < n_chunks)
            def _():
                load(j + 2, buf_a, ld_a).start()
            load(j + 1, buf_b, ld_b).wait()
            scatter_all(j + 1, buf_b, st_b)

    return scatter_kernel(rows, idx)


def _expert_kernel(blk_e_ref, n_used_ref, n_valid_ref, x_ref, wgu_ref, bgu_ref, wdn_ref, bdn_ref, o_ref,
                   wgu_bf, wdn_bf):
    j = pl.program_id(0)
    d = wdn_bf.shape[1]

    @pl.when(j < n_used_ref[0])
    def _():
        @pl.when((j == 0) | (blk_e_ref[j] != blk_e_ref[jnp.maximum(j - 1, 0)]))
        def _():
            wgu_bf[...] = wgu_ref[...].astype(BF16)
            wdn_bf[...] = wdn_ref[...].astype(BF16)

        row = lax.broadcasted_iota(jnp.int32, x_ref.shape, 0)
        packed = jnp.where(row < n_valid_ref[j], x_ref[...], jnp.uint32(0))
        xb = _unpack_bf16_pairs(packed).astype(BF16)
        gu = jnp.dot(xb, wgu_bf[...], preferred_element_type=F32) + bgu_ref[...]
        x_glu = jnp.minimum(gu[:, :d], SWIGLU_LIMIT)
        x_lin = jnp.clip(gu[:, d:], -SWIGLU_LIMIT, SWIGLU_LIMIT)
        act = x_glu * jax.nn.sigmoid(SWIGLU_ALPHA * x_glu) * (x_lin + 1.0)
        o_ref[...] = _pack_bf16_pairs(
            jnp.dot(act.astype(BF16), wdn_bf[...], preferred_element_type=F32) + bdn_ref[...])

    @pl.when(j >= n_used_ref[0])
    def _():
        o_ref[...] = jnp.zeros_like(o_ref)


def _experts(xs, blk_e, n_used, n_valid, wgu, bgu, wdn, bdn, layer):
    d = wdn.shape[-1]
    n_blocks = blk_e.shape[0]
    depth, ne = wgu.shape[:2]
    by_expert = lambda j, be, nu, nv: (layer, be[j], 0, 0)
    grid_spec = pltpu.PrefetchScalarGridSpec(
        num_scalar_prefetch=3,
        grid=(n_blocks,),
        in_specs=[pl.BlockSpec((MOE_BLOCK, d // 2), lambda j, be, nu, nv: (jnp.minimum(j, nu[0] - 1), 0)),
                  pl.BlockSpec((None, None, d, 2 * d), by_expert),
                  pl.BlockSpec((None, None, 1, 2 * d), by_expert),
                  pl.BlockSpec((None, None, d, d), by_expert),
                  pl.BlockSpec((None, None, 1, d), by_expert)],
        out_specs=pl.BlockSpec((MOE_BLOCK, d // 2), lambda j, be, nu, nv: (j, 0)),
        scratch_shapes=[pltpu.VMEM((d, 2 * d), BF16), pltpu.VMEM((d, d), BF16)],
    )
    return pl.pallas_call(
        _expert_kernel,
        out_shape=jax.ShapeDtypeStruct((n_blocks * MOE_BLOCK, d // 2), jnp.uint32),
        grid_spec=grid_spec,
        compiler_params=_cparams("arbitrary"),
        name="moe_experts",
    )(blk_e, n_used, n_valid, xs, wgu, bgu.reshape(depth, ne, 1, 2 * d), wdn, bdn.reshape(depth, ne, 1, d))


def _sc_gather_rows(table, idx):
    n_out = idx.shape[0]
    d = table.shape[1]
    n_workers = SC_CORES * SC_SUBCORES
    per_w = n_out // n_workers
    n_chunks = per_w // SC_ROWS
    assert per_w * n_workers == n_out and n_chunks * SC_ROWS == per_w and n_chunks % 2 == 0
    mesh = plsc.VectorSubcoreMesh(core_axis_name="c", subcore_axis_name="s")

    @functools.partial(
        pl.kernel, mesh=mesh,
        out_type=jax.ShapeDtypeStruct((n_out, d), table.dtype),
        scratch_types=[pltpu.VMEM((per_w,), jnp.int32),
                       pltpu.VMEM((SC_ROWS, d), table.dtype),
                       pltpu.VMEM((SC_ROWS, d), table.dtype),
                       pltpu.SemaphoreType.DMA, pltpu.SemaphoreType.DMA],
        name="sc_gather_rows",
    )
    def gather_kernel(table_hbm, idx_hbm, out_hbm, idx_v, rows_a, rows_b, sem_a, sem_b):
        base = (lax.axis_index("s") * SC_CORES + lax.axis_index("c")) * per_w
        pltpu.sync_copy(idx_hbm.at[pl.ds(base, per_w)], idx_v)

        def gather(j, buf, sem):
            rows = idx_v.at[pl.ds(j * SC_ROWS, SC_ROWS)]
            return pltpu.make_async_copy(table_hbm.at[rows], buf, sem)

        def write_back(j, buf):
            pltpu.sync_copy(buf, out_hbm.at[pl.ds(base + j * SC_ROWS, SC_ROWS)])

        gather(0, rows_a, sem_a).start()

        @pl.loop(0, n_chunks, step=2)
        def _(j):
            gather(j + 1, rows_b, sem_b).start()
            gather(j, rows_a, sem_a).wait()
            write_back(j, rows_a)

            @pl.when(j + 2 < n_chunks)
            def _():
                gather(j + 2, rows_a, sem_a).start()
            gather(j + 1, rows_b, sem_b).wait()
            write_back(j + 1, rows_b)

    return gather_kernel(table, idx)


def _combine_kernel(y0_ref, y1_ref, y2_ref, y3_ref, w_ref, x_ref, g2_ref, fg_ref, xo_ref, *, final):
    w = w_ref[...]
    y = w[:, 0:1] * _unpack_bf16_pairs(y0_ref[...])
    for kk, y_ref in enumerate((y1_ref, y2_ref, y3_ref), start=1):
        y = y + w[:, kk:kk + 1] * _unpack_bf16_pairs(y_ref[...])
    xn = x_ref[...] + g2_ref[...] * y
    if final:
        ms = jnp.mean(xn * xn, axis=-1, keepdims=True)
        xn = xn * lax.rsqrt(ms + EPS) * fg_ref[...]
    xo_ref[...] = xn


def _combine(dest, out_sorted, gates_t, x2, g2, final_g, s, final):
    t, d = x2.shape
    tm = 256
    tpb = s // tm
    sc_unit = 2 * SC_ROWS * SC_CORES * SC_SUBCORES // TOP_K
    chunkable = t % (COMBINE_CHUNKS * sc_unit) == 0 and t % (COMBINE_CHUNKS * tm) == 0
    n_chunks = COMBINE_CHUNKS if chunkable else 1
    tc = t // n_chunks
    nt = tc // tm
    for c in range(n_chunks):
        yg = _sc_gather_rows(out_sorted, dest[:, c * tc:(c + 1) * tc].reshape(TOP_K * tc))
        plane = lambda kk: pl.BlockSpec((tm, d // 2), lambda i: (kk * nt + i, 0))
        tile = lambda i, c=c: (c * nt + i, 0)
        x2 = pl.pallas_call(
            functools.partial(_combine_kernel, final=final),
            out_shape=jax.ShapeDtypeStruct((t, d), F32),
            grid=(nt,),
            in_specs=[plane(0), plane(1), plane(2), plane(3),
                      pl.BlockSpec((tm, TOP_K), tile),
                      pl.BlockSpec((tm, d), tile),
                      pl.BlockSpec((None, 1, d), lambda i, c=c: ((c * nt + i) // tpb, 0, 0)),
                      pl.BlockSpec((1, d), lambda i: (0, 0))],
            out_specs=pl.BlockSpec((tm, d), tile),
            input_output_aliases={5: 0},
            compiler_params=_cparams("arbitrary"),
            name="moe_combine",
        )(yg, yg, yg, yg, gates_t, x2, g2, final_g)
    return x2


def _moe_plan(top_e, pos, counts):
    t = top_e.shape[1]
    sizes = counts[:, 0].astype(jnp.int32)
    padded = (sizes + MOE_BLOCK - 1) // MOE_BLOCK * MOE_BLOCK
    pad_end = jnp.cumsum(padded)
    pad_start = pad_end - padded
    start_of = jnp.zeros_like(top_e)
    for e in range(N_EXPERTS):
        start_of = jnp.where(top_e == e, pad_start[e], start_of)
    dest_kt = start_of + pos
    n_blocks = -(-t * TOP_K // MOE_BLOCK) + N_EXPERTS
    blk_start = jnp.arange(n_blocks, dtype=jnp.int32) * MOE_BLOCK
    blk_e = jnp.minimum(jnp.sum((pad_end[None, :] <= blk_start[:, None]).astype(jnp.int32), axis=1),
                        N_EXPERTS - 1).astype(jnp.int32)
    n_used = (pad_end[-1] // MOE_BLOCK).astype(jnp.int32).reshape(1)
    filled_end = pad_start + sizes
    n_valid = jnp.zeros((n_blocks,), jnp.int32)
    for e in range(N_EXPERTS):
        n_valid = jnp.where(blk_e == e, jnp.clip(filled_end[e] - blk_start, 0, MOE_BLOCK), n_valid)
    return dest_kt, blk_e, n_used, n_valid, n_blocks


def _rope_tables(s, gain):
    n_rows = s // GRID_W
    row = jnp.repeat(jnp.arange(n_rows), GRID_W).astype(F32)
    col = (jnp.arange(s) % GRID_W).astype(F32)
    half = HEAD_DIM // 2
    inv = ROPE_THETA ** (-jnp.arange(0, half, 2, dtype=F32) / half)
    ang_r = row[:, None] * inv
    ang_c = col[:, None] * inv
    cos = jnp.concatenate([jnp.cos(ang_r)] * 2 + [jnp.cos(ang_c)] * 2, axis=-1)
    sin = jnp.concatenate([-jnp.sin(ang_r), jnp.sin(ang_r), -jnp.sin(ang_c), jnp.sin(ang_c)], axis=-1)
    gain = gain.astype(F32)
    return cos * gain[None, :], sin * gain[_PARTNER][None, :]


_q = HEAD_DIM // 4
_PARTNER = np.concatenate([np.arange(_q, 2 * _q), np.arange(0, _q),
                           np.arange(3 * _q, 4 * _q), np.arange(2 * _q, 3 * _q)])


def _partner_cols(w, n_heads):
    idx = (np.arange(n_heads)[:, None] * HEAD_DIM + _PARTNER[None, :]).reshape(-1)
    return w[:, idx]


def kernel(x, c, w_ada, b_ada, norm1_g, w_in, q_norm_g, k_norm_g, rel_bias, w_br_a, w_br_b, w_out,
           norm2_g, w_router, b_router, w_gate_up, b_gate_up, w_down, b_down, final_norm_g):
    bsz, s, d = x.shape
    depth = w_ada.shape[0]
    mod = _ada_mod(c, w_ada, b_ada)
    n_streams = N_STREAMS if bsz % N_STREAMS == 0 else 1
    hb = bsz // n_streams
    ht = hb * s
    streams = range(n_streams)
    xs2 = [x[i * hb:(i + 1) * hb].reshape(ht, d) for i in streams]
    q_off = 3 * A_WIDTH
    k_off = q_off + B_Q
    for l in range(depth):
        mods = [[mod[l, i * hb:(i + 1) * hb, j * d:(j + 1) * d].reshape(hb, 1, d) for j in range(N_MOD)]
                for i in streams]
        w = w_in[l]
        wq, wk = w[:, q_off:q_off + B_Q], w[:, k_off:k_off + B_KV]
        w_ext = jnp.concatenate([wq, _partner_cols(wq, B_HEADS), wk, _partner_cols(wk, B_KV_HEADS),
                                 w[:, k_off + B_KV:k_off + 2 * B_KV], w[:, :q_off],
                                 w[:, k_off + 2 * B_KV:]], axis=1).astype(BF16)
        tabs = [jnp.tile(tb, (1, 2)) for tb in
                _rope_tables(s, q_norm_g[l]) + _rope_tables(s, k_norm_g[l])]
        wa, wb, wo = w_br_a[l].astype(BF16), w_br_b[l].astype(BF16), w_out[l].astype(BF16)
        proj = [_projection(xs2[i], mods[i][0], mods[i][1], norm1_g[l].reshape(1, d), w_ext, tabs, hb, s)
                for i in streams]
        attn_a = [[_dilated_group(proj[i][g], rel_bias[:, g * A_HEADS:(g + 1) * A_HEADS], dil, hb, s)
                   for g, (_, dil) in enumerate(A_GROUPS)] for i in streams]
        attn_b = [_gqa_attention(proj[i][3], proj[i][4], proj[i][5], q_norm_g[l], k_norm_g[l])
                  .reshape(ht, B_Q) for i in streams]
        merged = [_merge_and_route([o for o, _ in attn_a[i]], [ls for _, ls in attn_a[i]], attn_b[i],
                                   proj[i][6], xs2[i], mods[i][2], wa, wb, wo, mods[i][3], mods[i][4],
                                   norm2_g[l].reshape(1, d), w_router[l], b_router[l], s)
                  for i in streams]
        xs2 = [m[0] for m in merged]
        routed = [m[1:] for m in merged]
        plans = [_moe_plan(routed[i][1], routed[i][3], routed[i][4]) for i in streams]
        slots = [_sc_scatter_rows(routed[i][0], plans[i][0].reshape(TOP_K * ht), plans[i][4] * MOE_BLOCK)
                 for i in streams]
        outs = [_experts(slots[i], plans[i][1], plans[i][2], plans[i][3],
                         w_gate_up, b_gate_up, w_down, b_down, l) for i in streams]
        xs2 = [_combine(plans[i][0], outs[i], routed[i][2].T, xs2[i], mods[i][5],
                        final_norm_g.reshape(1, d), s, final=(l == depth - 1)) for i in streams]
    return jnp.concatenate([xh.reshape(hb, s, d) for xh in xs2], axis=0)
```

```python
import functools
import math

import numpy as np
import jax
import jax.numpy as jnp
from jax import lax
from jax.experimental import pallas as pl
from jax.experimental.pallas import tpu as pltpu
from jax.experimental.pallas import tpu_sc as plsc

F32 = jnp.float32
BF16 = jnp.bfloat16

HEAD_DIM = 64
LANE = 128
PAIR = 2 * HEAD_DIM
A_GROUPS = ((128, 1), (512, 4), (2048, 16))
A_HEADS = 4
A_WIDTH = len(A_GROUPS) * A_HEADS * HEAD_DIM
A_OUT = A_HEADS * HEAD_DIM
A_GROUP_COLS = 3 * A_OUT
B_HEADS = 16
B_KV_HEADS = 4
B_GRP = B_HEADS // B_KV_HEADS
B_Q = B_HEADS * HEAD_DIM
B_KV = B_KV_HEADS * HEAD_DIM
GRID_W = 64
ROPE_THETA = 10000.0
REL_BUCKETS = 32
REL_MAX_DIST = 1024
N_EXPERTS = 32
TOP_K = 4
SWIGLU_LIMIT = 7.0
SWIGLU_ALPHA = 1.702
MOE_BLOCK = 512
COMBINE_CHUNKS = 4
MERGE_ROWS = 128
N_MOD = 6
N_STREAMS = 1
EPS = 1e-6
NEG_INF = -1e30
LOG2E = math.log2(math.e)
N_SIDE = 64
DIL_SUB = 2 * N_SIDE
DIL_NO_SHIFT_MAX = 60.0
DIL_STEP_ROWS = 512

VMEM_LIMIT = 56 * 1024 * 1024


def _pack_bf16_pairs(x):
    bits = pltpu.bitcast(x.astype(BF16).astype(F32), jnp.uint32)
    half = x.shape[1] // 2
    return (bits[:, :half] >> 16) | bits[:, half:]


def _unpack_bf16_pairs(u):
    return jnp.concatenate([pltpu.bitcast(u << 16, F32),
                            pltpu.bitcast(u & jnp.uint32(0xFFFF0000), F32)], axis=1)


def _cparams(*sem):
    return pltpu.CompilerParams(dimension_semantics=sem, vmem_limit_bytes=VMEM_LIMIT)


def _ada_kernel(c_ref, w_ref, b_ref, o_ref):
    c = c_ref[...]
    ca = (c * jax.nn.sigmoid(c)).astype(BF16)
    o_ref[...] = jnp.dot(ca, w_ref[...].astype(BF16), preferred_element_type=F32) + b_ref[...]


def _ada_mod(c, w_ada, b_ada):
    depth, d, n = w_ada.shape
    bsz = c.shape[0]
    tn = 1536
    return pl.pallas_call(
        _ada_kernel,
        out_shape=jax.ShapeDtypeStruct((depth, bsz, n), F32),
        grid=(depth, n // tn),
        in_specs=[pl.BlockSpec((bsz, d), lambda l, j: (0, 0)),
                  pl.BlockSpec((None, d, tn), lambda l, j: (l, 0, j)),
                  pl.BlockSpec((None, 1, tn), lambda l, j: (l, 0, j))],
        out_specs=pl.BlockSpec((None, bsz, tn), lambda l, j: (l, 0, j)),
        compiler_params=_cparams("arbitrary", "arbitrary"),
        name="ada_mod",
    )(c, w_ada, b_ada.reshape(depth, 1, n))


def _proj_kernel(x_ref, sh_ref, sc_ref, g_ref, w_ref, aq_ref, bq_ref, ak_ref, bk_ref, hsel_ref,
                 za0_ref, za1_ref, za2_ref, q_ref, k_ref, v_ref, gate_ref, nrm_ref, fold_ref):
    x = x_ref[...]
    ms = jnp.mean(x * x, axis=-1, keepdims=True)
    h = x * lax.rsqrt(ms + EPS) * g_ref[...]
    h = h * (1.0 + sc_ref[...]) + sh_ref[...]
    hb = h.astype(BF16)

    def mm(lo, hi):
        return jnp.dot(hb, w_ref[:, lo:hi], preferred_element_type=F32)

    tm = x.shape[0]
    o = 0
    zq = mm(o, o + B_Q); o += B_Q
    zqp = mm(o, o + B_Q); o += B_Q
    zk = mm(o, o + B_KV); o += B_KV
    zkp = mm(o, o + B_KV); o += B_KV
    zv = mm(o, o + B_KV); o += B_KV
    lane = lax.broadcasted_iota(jnp.int32, (tm, PAIR), 1)
    first = lane < HEAD_DIM

    def pair_norm_rope(z, zp, a, b):
        zz = z * z
        ss = jnp.where(first,
                       jnp.sum(jnp.where(first, zz, 0.0), axis=-1, keepdims=True),
                       jnp.sum(jnp.where(first, 0.0, zz), axis=-1, keepdims=True))
        return lax.rsqrt(ss * (1.0 / HEAD_DIM) + EPS) * (z * a + zp * b)

    aq, bq, ak, bk = aq_ref[...], bq_ref[...], ak_ref[...], bk_ref[...]
    for j in range(B_HEADS // 2):
        cols = slice(j * PAIR, (j + 1) * PAIR)
        q_ref[j] = (pair_norm_rope(zq[:, cols], zqp[:, cols], aq, bq)
                    * (LOG2E * HEAD_DIM ** -0.5)).astype(BF16)
    ones_col = jnp.where(lane == HEAD_DIM, 1.0, 0.0)
    for j in range(B_KV_HEADS // 2):
        cols = slice(j * PAIR, (j + 1) * PAIR)
        kk = pair_norm_rope(zk[:, cols], zkp[:, cols], ak, bk)
        kk_sw = pltpu.roll(kk, HEAD_DIM, axis=1)
        k_ref[2 * j] = jnp.where(first, kk, kk_sw).astype(BF16)
        k_ref[2 * j + 1] = jnp.where(first, kk_sw, kk).astype(BF16)
        vv = zv[:, cols]
        v_ref[2 * j] = jnp.where(first, vv, ones_col).astype(BF16)
        v_ref[2 * j + 1] = jnp.where(first, pltpu.roll(vv, HEAD_DIM, axis=1), ones_col).astype(BF16)
    @pl.when(pl.program_id(0) == 0)
    def _():
        nrm_ref[...] = jnp.zeros_like(nrm_ref)

    for g, (za_ref, (_, dil)) in enumerate(zip((za0_ref, za1_ref, za2_ref), A_GROUPS)):
        z = mm(o, o + A_GROUP_COLS)
        o += A_GROUP_COLS
        zqk = z[:, :2 * A_OUT]
        sq = jnp.dot((zqk * zqk).astype(BF16), hsel_ref[...], preferred_element_type=F32)
        nrm_ref[g:g + 1, :] = jnp.maximum(nrm_ref[g:g + 1, :], jnp.max(sq, axis=0, keepdims=True))
        if dil == 1:
            za_ref[...] = z.astype(BF16)
        else:
            for c in range(A_GROUP_COLS // LANE):
                fold_ref[c] = z[:, c * LANE:(c + 1) * LANE]
            for r in range(dil):
                for c in range(A_GROUP_COLS // LANE):
                    col = r * A_GROUP_COLS + c * LANE
                    za_ref[:, col:col + LANE] = (
                        fold_ref[c, pl.ds(r, tm // dil, stride=dil), :].astype(BF16))
    gate_ref[...] = jax.nn.sigmoid(mm(o, o + gate_ref.shape[-1])).astype(BF16)


def _projection(x2, sh, sc, g, w_ext, tabs, bsz, s):
    t, d = x2.shape
    tm = 256
    tpb = s // tm
    n_ext = w_ext.shape[1]
    n_gate = 2 * d
    row = lambda i: (i, 0)
    per_b = lambda i: (i // tpb, 0, 0)
    tab = lambda i: (i % tpb, 0)
    hm = lambda i: (i // tpb, 0, i % tpb, 0)
    out_shape = (
        [jax.ShapeDtypeStruct((bsz, s // dil, dil * A_GROUP_COLS), BF16) for _, dil in A_GROUPS]
        + [jax.ShapeDtypeStruct((bsz, B_HEADS // 2, s, PAIR), BF16),
           jax.ShapeDtypeStruct((bsz, B_KV_HEADS, s, PAIR), BF16),
           jax.ShapeDtypeStruct((bsz, B_KV_HEADS, s, PAIR), BF16),
           jax.ShapeDtypeStruct((t, n_gate), BF16),
           jax.ShapeDtypeStruct((len(A_GROUPS), LANE), F32)])
    head_sel = (jnp.arange(2 * A_OUT)[:, None] // HEAD_DIM == jnp.arange(LANE)[None, :]).astype(BF16)
    out_specs = (
        [pl.BlockSpec((None, tm // dil, dil * A_GROUP_COLS), lambda i: (i // tpb, i % tpb, 0))
         for _, dil in A_GROUPS]
        + [pl.BlockSpec((None, B_HEADS // 2, tm, PAIR), hm),
           pl.BlockSpec((None, B_KV_HEADS, tm, PAIR), hm),
           pl.BlockSpec((None, B_KV_HEADS, tm, PAIR), hm),
           pl.BlockSpec((tm, n_gate), row),
           pl.BlockSpec((len(A_GROUPS), LANE), lambda i: (0, 0))])
    return pl.pallas_call(
        _proj_kernel,
        out_shape=out_shape,
        grid=(t // tm,),
        in_specs=[pl.BlockSpec((tm, d), row),
                  pl.BlockSpec((None, 1, d), per_b),
                  pl.BlockSpec((None, 1, d), per_b),
                  pl.BlockSpec((1, d), lambda i: (0, 0)),
                  pl.BlockSpec((d, n_ext), lambda i: (0, 0))]
                 + [pl.BlockSpec((tm, PAIR), tab)] * 4
                 + [pl.BlockSpec((2 * A_OUT, LANE), lambda i: (0, 0))],
        out_specs=out_specs,
        scratch_shapes=[pltpu.VMEM((A_GROUP_COLS // LANE, tm, LANE), F32)],
        compiler_params=_cparams("arbitrary"),
        name="in_proj",
    )(x2, sh, sc, g, w_ext, *tabs, head_sel)


def _dilated_kernel(q_ref, kp_ref, kc_ref, kn_ref, vp_ref, vc_ref, vn_ref, bias_ref,
                    o_ref, lse_ref, *, tq, seq_len, shift):
    i = pl.program_id(2)
    k = jnp.concatenate([kp_ref[...], kc_ref[...], kn_ref[...]], axis=0)
    v = jnp.concatenate([vp_ref[...], vc_ref[...], vn_ref[...]], axis=0)
    nk = DIL_SUB + 2 * N_SIDE
    col = lax.broadcasted_iota(jnp.int32, (DIL_SUB, nk), 1)
    if not shift:
        lane = lax.broadcasted_iota(jnp.int32, (v.shape[0], PAIR), 1)
        first = lane < HEAD_DIM
        ones_col = jnp.where(lane == HEAD_DIM, 1.0, 0.0)
        v_ext = []
        for j in range(A_HEADS // 2):
            vp = v[:, j * PAIR:(j + 1) * PAIR].astype(F32)
            v_ext += [jnp.where(first, vp, ones_col).astype(BF16),
                      jnp.where(first, pltpu.roll(vp, HEAD_DIM, axis=1), ones_col).astype(BF16)]
    for sb in range(tq // DIL_SUB):
        rows = slice(sb * DIL_SUB, (sb + 1) * DIL_SUB)
        q = q_ref[rows, :]
        kb = k[sb * DIL_SUB:sb * DIL_SUB + nk]
        vb = v[sb * DIL_SUB:sb * DIL_SUB + nk]
        kpos = i * tq + sb * DIL_SUB - N_SIDE + col
        valid = (kpos >= 0) & (kpos < seq_len)
        for hh in range(A_HEADS):
            sl = slice(hh * HEAD_DIM, (hh + 1) * HEAD_DIM)
            sc = lax.dot_general(q[:, sl], kb[:, sl], (((1,), (1,)), ((), ())),
                                 preferred_element_type=F32)
            sc = sc * (HEAD_DIM ** -0.5) + bias_ref[hh]
            sc = jnp.where(valid, sc, NEG_INF)
            if shift:
                m = jnp.max(sc, axis=-1, keepdims=True)
                p = jnp.exp(sc - m)
                den = jnp.sum(p, axis=-1, keepdims=True)
                o = jnp.dot(p.astype(BF16), vb[:, sl], preferred_element_type=F32) / den
                lse = m + jnp.log(den)
            else:
                acc = jnp.dot(jnp.exp(sc).astype(BF16), v_ext[hh][sb * DIL_SUB:sb * DIL_SUB + nk],
                              preferred_element_type=F32)
                den = acc[:, HEAD_DIM:HEAD_DIM + 1]
                o = acc[:, :HEAD_DIM] / den
                lse = jnp.log(den)
            o_ref[rows, sl] = o
            lse_ref[rows, sl] = jnp.broadcast_to(lse, (DIL_SUB, HEAD_DIM))


def _t5_bucket(rel):
    nb = REL_BUCKETS // 2
    max_exact = nb // 2
    ret = jnp.where(rel > 0, nb, 0)
    n = jnp.abs(rel)
    nf = jnp.maximum(n, 1).astype(F32)
    large = max_exact + (jnp.log(nf / max_exact) / math.log(REL_MAX_DIST / max_exact)
                         * (nb - max_exact)).astype(jnp.int32)
    large = jnp.minimum(large, nb - 1)
    return ret + jnp.where(n < max_exact, n, large)


def _band_bias(rel_bias_g, dilation, tq):
    nk = tq + 2 * N_SIDE
    rel = jnp.arange(nk)[None, :] - N_SIDE - jnp.arange(tq)[:, None]
    bucket = _t5_bucket(rel * dilation)[None]
    bias = jnp.full((A_HEADS,) + rel.shape, NEG_INF, F32)
    for b in range(REL_BUCKETS):
        bias = jnp.where(bucket == b, rel_bias_g[b].astype(F32)[:, None, None], bias)
    return jnp.where((jnp.abs(rel) <= N_SIDE)[None], bias, NEG_INF)


def _dilated_group(za, sq_norms, rel_bias_g, dilation, bsz, s):
    ll = s // dilation
    tq = min(DIL_STEP_ROWS, ll)
    assert ll % tq == 0 and tq % DIL_SUB == 0
    nhalf = ll // N_SIDE
    per = tq // N_SIDE
    cur = lambda c: (lambda b, r, i: (b, i, 3 * r + c))
    prv = lambda c: (lambda b, r, i: (b, jnp.maximum(per * i - 1, 0), 3 * r + c))
    nxt = lambda c: (lambda b, r, i: (b, jnp.minimum(per * (i + 1), nhalf - 1), 3 * r + c))
    full = pl.BlockSpec((None, tq, A_OUT), cur(0))
    half = lambda f: pl.BlockSpec((None, N_SIDE, A_OUT), f)
    bias = _band_bias(rel_bias_g, dilation, DIL_SUB)
    out_sd = jax.ShapeDtypeStruct((bsz, ll, dilation * A_OUT), F32)
    out_spec = pl.BlockSpec((None, tq, A_OUT), lambda b, r, i: (b, i, r))

    def attend(shift, zv, bias):
        return pl.pallas_call(
            functools.partial(_dilated_kernel, tq=tq, seq_len=ll, shift=shift),
            out_shape=[out_sd, out_sd],
            grid=(bsz, dilation, ll // tq),
            in_specs=[full,
                      half(prv(1)), pl.BlockSpec((None, tq, A_OUT), cur(1)), half(nxt(1)),
                      half(prv(2)), pl.BlockSpec((None, tq, A_OUT), cur(2)), half(nxt(2)),
                      pl.BlockSpec(bias.shape, lambda b, r, i: (0, 0, 0))],
            out_specs=[out_spec, out_spec],
            compiler_params=_cparams("arbitrary", "arbitrary", "arbitrary"),
            name=f"dilated_attn_d{dilation}" + ("_rowmax" if shift else ""),
        )(zv, zv, zv, zv, zv, zv, zv, bias)

    q2 = jnp.max(sq_norms[:A_HEADS])
    k2 = jnp.max(sq_norms[A_HEADS:2 * A_HEADS])
    bound = 1.03 * jnp.sqrt(q2 * k2) * (HEAD_DIM ** -0.5) + jnp.max(jnp.abs(rel_bias_g))
    return lax.cond(bound <= DIL_NO_SHIFT_MAX, functools.partial(attend, False),
                    functools.partial(attend, True), za, bias)


GQA_NO_SHIFT_MAX_LOG2 = 80.0
GQA_KEY_CHUNK = 128


def _stacked_heads(q_ref):
    first = lax.broadcasted_iota(jnp.int32, q_ref.shape[1:], 1) < HEAD_DIM
    blocks = []
    for j in range(q_ref.shape[0]):
        qp = q_ref[j]
        blocks += [jnp.where(first, qp, jnp.zeros_like(qp)), jnp.where(first, jnp.zeros_like(qp), qp)]
    return jnp.concatenate(blocks, axis=0)


def _gqa_kernel_noshift(q_ref, k_ref, v_ref, o_ref):
    tq = q_ref.shape[1]
    q = _stacked_heads(q_ref)
    nt = (((1,), (1,)), ((), ()))
    acc = jnp.zeros((q.shape[0], PAIR), F32)
    for c in range(k_ref.shape[0] // GQA_KEY_CHUNK):
        rows = slice(c * GQA_KEY_CHUNK, (c + 1) * GQA_KEY_CHUNK)
        p = jnp.exp2(lax.dot_general(q, k_ref[rows, :], nt, preferred_element_type=F32))
        acc = acc + jnp.dot(p.astype(BF16), v_ref[rows, :], preferred_element_type=F32)
    o = acc[:, :HEAD_DIM] / acc[:, HEAD_DIM:HEAD_DIM + 1]
    for hh in range(q.shape[0] // tq):
        o_ref[:, hh * HEAD_DIM:(hh + 1) * HEAD_DIM] = o[hh * tq:(hh + 1) * tq].astype(BF16)


def _gqa_kernel_rowmax(q_ref, k_ref, v_ref, o_ref):
    tq = q_ref.shape[1]
    q = _stacked_heads(q_ref)
    k = k_ref[...]
    v = v_ref[:, :HEAD_DIM]
    for hh in range(q.shape[0] // tq):
        sc = lax.dot_general(q[hh * tq:(hh + 1) * tq], k, (((1,), (1,)), ((), ())),
                             preferred_element_type=F32)
        m = jnp.max(sc, axis=-1, keepdims=True)
        p = jnp.exp2(sc - m)
        den = jnp.sum(p, axis=-1, keepdims=True)
        o = jnp.dot(p.astype(BF16), v, preferred_element_type=F32) / den
        o_ref[:, hh * HEAD_DIM:(hh + 1) * HEAD_DIM] = o.astype(BF16)


def _gqa_call(body, name, tq, q, k, v):
    bsz, _, s, _ = q.shape
    return pl.pallas_call(
        body,
        out_shape=jax.ShapeDtypeStruct((bsz, s, B_Q), BF16),
        grid=(bsz, B_KV_HEADS, s // tq),
        in_specs=[pl.BlockSpec((None, B_GRP // 2, tq, PAIR), lambda b, h, i: (b, h, i, 0)),
                  pl.BlockSpec((None, None, s, PAIR), lambda b, h, i: (b, h, 0, 0)),
                  pl.BlockSpec((None, None, s, PAIR), lambda b, h, i: (b, h, 0, 0))],
        out_specs=pl.BlockSpec((None, tq, B_GRP * HEAD_DIM), lambda b, h, i: (b, i, h)),
        compiler_params=_cparams("arbitrary", "arbitrary", "arbitrary"),
        name=name,
    )(q, k, v)


def _gqa_attention(q, k, v, q_gain, k_gain):
    bound = (HEAD_DIM ** 0.5) * LOG2E * 1.01 * jnp.max(jnp.abs(q_gain)) * jnp.max(jnp.abs(k_gain))
    return lax.cond(bound <= GQA_NO_SHIFT_MAX_LOG2,
                    functools.partial(_gqa_call, _gqa_kernel_noshift, "gqa_attn", 1024),
                    functools.partial(_gqa_call, _gqa_kernel_rowmax, "gqa_attn_rowmax", 256),
                    q, k, v)


def _merge_kernel(o0_ref, o1_ref, o2_ref, l0_ref, l1_ref, l2_ref, ob_ref, gate_ref, x_ref, g1_ref,
                  wa_ref, wb_ref, wo_ref, sh_ref, sc_ref, g_ref, whi_ref, wlo_ref, b_ref,
                  xo_ref, h_ref, e_ref, w_ref, pos_ref, cnt_ref, run_ref, *unfold_refs):
    tm, d = x_ref.shape
    scratch = list(unfold_refs)

    def token_order(ref, dil):
        if dil == 1:
            return ref[...]
        buf = scratch.pop()
        for r in range(dil):
            for c in range(A_OUT // LANE):
                col = r * A_OUT + c * LANE
                buf[c, pl.ds(r, tm // dil, stride=dil), :] = ref[:, col:col + LANE]
        return jnp.concatenate([buf[c] for c in range(A_OUT // LANE)], axis=1)

    dils = [dil for _, dil in A_GROUPS]
    o0, o1, o2 = [token_order(r, dl) for r, dl in zip((o0_ref, o1_ref, o2_ref), dils)]
    l0, l1, l2 = [token_order(r, dl) for r, dl in zip((l0_ref, l1_ref, l2_ref), dils)]
    m = jnp.maximum(jnp.maximum(l0, l1), l2)
    e0, e1, e2 = jnp.exp(l0 - m), jnp.exp(l1 - m), jnp.exp(l2 - m)
    tot = e0 + e1 + e2
    oa = ((e0 / tot) * o0 + (e1 / tot) * o1 + (e2 / tot) * o2).astype(BF16)
    xn_rows = []
    for lo in range(0, tm, MERGE_ROWS):
        rows = slice(lo, lo + MERGE_ROWS)
        ya = jnp.dot(oa[rows], wa_ref[...], preferred_element_type=F32)
        yb = jnp.dot(ob_ref[rows, :], wb_ref[...], preferred_element_type=F32)
        merged = gate_ref[rows, :d].astype(F32) * ya + gate_ref[rows, d:].astype(F32) * yb
        y = jnp.dot(merged.astype(BF16), wo_ref[...], preferred_element_type=F32)
        xn_rows.append(x_ref[rows, :] + g1_ref[...] * y)
    xn = jnp.concatenate(xn_rows, axis=0)
    xo_ref[...] = xn
    _route_tile(xn, sh_ref, sc_ref, g_ref, whi_ref, wlo_ref, b_ref, h_ref, e_ref, w_ref, pos_ref, cnt_ref,
                run_ref)


def _merge_and_route(os_, ls_, ob, gate, x2, g1, wa, wb, wo, sh2, sc2, g2n, w_router, b_router, s):
    t, d = x2.shape
    tm = 256
    tpb = s // tm
    ne = w_router.shape[1]
    wt = w_router.T
    whi = wt.astype(BF16)
    wlo = (wt - whi.astype(F32)).astype(BF16)
    row = lambda i: (i, 0)
    const = lambda i: (0, 0)
    per_b = lambda i: (i // tpb, 0, 0)
    by_token = pl.BlockSpec((TOP_K, tm), lambda i: (0, i))
    a_specs = [pl.BlockSpec((None, tm // dil, dil * A_OUT), lambda i: (i // tpb, i % tpb, 0))
               for _, dil in A_GROUPS]
    n_unfold = 2 * sum(1 for _, dil in A_GROUPS if dil > 1)
    return pl.pallas_call(
        _merge_kernel,
        out_shape=[jax.ShapeDtypeStruct((t, d), F32),
                   jax.ShapeDtypeStruct((t, d // 2), jnp.uint32),
                   jax.ShapeDtypeStruct((TOP_K, t), jnp.int32),
                   jax.ShapeDtypeStruct((TOP_K, t), F32),
                   jax.ShapeDtypeStruct((TOP_K, t), jnp.int32),
                   jax.ShapeDtypeStruct((ne, 1), F32)],
        grid=(t // tm,),
        scratch_shapes=[pltpu.VMEM((ne, 1), F32)]
                       + [pltpu.VMEM((A_OUT // LANE, tm, LANE), F32)] * n_unfold,
        in_specs=a_specs * 2
                 + [pl.BlockSpec((tm, B_Q), row),
                    pl.BlockSpec((tm, 2 * d), row),
                    pl.BlockSpec((tm, d), row),
                    pl.BlockSpec((None, 1, d), per_b),
                    pl.BlockSpec(wa.shape, const),
                    pl.BlockSpec(wb.shape, const),
                    pl.BlockSpec(wo.shape, const),
                    pl.BlockSpec((None, 1, d), per_b),
                    pl.BlockSpec((None, 1, d), per_b),
                    pl.BlockSpec((1, d), const),
                    pl.BlockSpec((ne, d), const),
                    pl.BlockSpec((ne, d), const),
                    pl.BlockSpec((ne, 1), const)],
        out_specs=[pl.BlockSpec((tm, d), row),
                   pl.BlockSpec((tm, d // 2), row),
                   by_token, by_token, by_token,
                   pl.BlockSpec((ne, 1), const)],
        compiler_params=_cparams("arbitrary"),
        name="merge_route",
    )(*os_, *ls_, ob, gate, x2, g1, wa, wb, wo, sh2, sc2, g2n, whi, wlo, b_router.reshape(ne, 1))


def _route_tile(x, sh_ref, sc_ref, g_ref, whi_ref, wlo_ref, b_ref,
                h_ref, e_ref, w_ref, pos_ref, cnt_ref, run_ref):
    @pl.when(pl.program_id(0) == 0)
    def _():
        run_ref[...] = jnp.zeros_like(run_ref)

    ms = jnp.mean(x * x, axis=-1, keepdims=True)
    h = x * lax.rsqrt(ms + EPS) * g_ref[...]
    h = h * (1.0 + sc_ref[...]) + sh_ref[...]
    h_ref[...] = _pack_bf16_pairs(h)
    hhi = h.astype(BF16)
    hlo = (h - hhi.astype(F32)).astype(BF16)
    nt = (((1,), (1,)), ((), ()))
    dotf = lambda a, b: lax.dot_general(a, b, nt, preferred_element_type=F32)
    logits = dotf(whi_ref[...], hhi) + (dotf(whi_ref[...], hlo) + dotf(wlo_ref[...], hhi)) + b_ref[...]
    ne, tm = logits.shape
    iota = lax.broadcasted_iota(jnp.int32, (ne, tm), 0).astype(F32)
    vals, idxs = [], []
    cur = logits
    for _ in range(TOP_K):
        m = jnp.max(cur, axis=0, keepdims=True)
        idx = jnp.min(jnp.where(cur == m, iota, float(ne)), axis=0, keepdims=True)
        vals.append(m)
        idxs.append(idx)
        cur = jnp.where(iota == idx, -jnp.inf, cur)
    tv = jnp.concatenate(vals, axis=0)
    ex = jnp.exp(tv - tv[0:1])
    w_ref[...] = ex / jnp.sum(ex, axis=0, keepdims=True)
    e_ref[...] = jnp.concatenate(idxs, axis=0).astype(jnp.int32)
    onehot = jnp.zeros((ne, tm), F32)
    for idx in idxs:
        onehot = onehot + jnp.where(iota == idx, 1.0, 0.0)
    earlier = (lax.broadcasted_iota(jnp.int32, (tm, tm), 0)
               < lax.broadcasted_iota(jnp.int32, (tm, tm), 1))
    rank = jnp.dot(onehot.astype(BF16), jnp.where(earlier, 1.0, 0.0).astype(BF16),
                   preferred_element_type=F32) + run_ref[...]
    pos_ref[...] = jnp.concatenate(
        [jnp.sum(jnp.where(iota == idx, rank, 0.0), axis=0, keepdims=True) for idx in idxs],
        axis=0).astype(jnp.int32)
    run_ref[...] = run_ref[...] + jnp.sum(onehot, axis=1, keepdims=True)
    cnt_ref[...] = run_ref[...]


SC_CORES = 2
SC_SUBCORES = 16
SC_ROWS = 32


def _sc_scatter_rows(rows, dest_kt, n_slots):
    t, d = rows.shape
    n_workers = SC_CORES * SC_SUBCORES
    per_w = t // n_workers
    n_chunks = per_w // SC_ROWS
    assert per_w * n_workers == t and n_chunks * SC_ROWS == per_w and n_chunks % 2 == 0
    idx = dest_kt.reshape(TOP_K, n_workers, n_chunks, SC_ROWS).transpose(1, 2, 0, 3)
    idx = idx.reshape(n_workers * n_chunks * TOP_K, SC_ROWS)
    lists_per_w = n_chunks * TOP_K
    mesh = plsc.VectorSubcoreMesh(core_axis_name="c", subcore_axis_name="s")

    @functools.partial(
        pl.kernel, mesh=mesh,
        out_type=jax.ShapeDtypeStruct((n_slots, d), rows.dtype),
        scratch_types=[pltpu.VMEM((lists_per_w, SC_ROWS), jnp.int32),
                       pltpu.VMEM((SC_ROWS, d), rows.dtype),
                       pltpu.VMEM((SC_ROWS, d), rows.dtype)] + [pltpu.SemaphoreType.DMA] * 4,
        name="sc_scatter_rows",
    )
    def scatter_kernel(rows_hbm, idx_hbm, out_hbm, idx_v, buf_a, buf_b, ld_a, ld_b, st_a, st_b):
        wid = lax.axis_index("s") * SC_CORES + lax.axis_index("c")
        pltpu.sync_copy(idx_hbm.at[pl.ds(wid * lists_per_w, lists_per_w)], idx_v)

        def load(j, buf, sem):
            return pltpu.make_async_copy(rows_hbm.at[pl.ds(wid * per_w + j * SC_ROWS, SC_ROWS)], buf, sem)

        def scatter_all(j, buf, sem):
            copies = [pltpu.make_async_copy(buf, out_hbm.at[idx_v.at[j * TOP_K + kk]], sem)
                      for kk in range(TOP_K)]
            for cp in copies:
                cp.start()
            for cp in copies:
                cp.wait()

        load(0, buf_a, ld_a).start()

        @pl.loop(0, n_chunks, step=2)
        def _(j):
            load(j + 1, buf_b, ld_b).start()
            load(j, buf_a, ld_a).wait()
            scatter_all(j, buf_a, st_a)

            @pl.when(j + 2 < n_chunks)
            def _():
                load(j + 2, buf_a, ld_a).start()
            load(j + 1, buf_b, ld_b).wait()
            scatter_all(j + 1, buf_b, st_b)

    return scatter_kernel(rows, idx)


def _expert_kernel(blk_e_ref, n_used_ref, n_valid_ref, x_ref, wgu_ref, bgu_ref, wdn_ref, bdn_ref, o_ref,
                   wgu_bf, wdn_bf):
    j = pl.program_id(0)
    d = wdn_bf.shape[1]

    @pl.when(j < n_used_ref[0])
    def _():
        @pl.when((j == 0) | (blk_e_ref[j] != blk_e_ref[jnp.maximum(j - 1, 0)]))
        def _():
            wgu_bf[...] = wgu_ref[...].astype(BF16)
            wdn_bf[...] = wdn_ref[...].astype(BF16)

        row = lax.broadcasted_iota(jnp.int32, x_ref.shape, 0)
        packed = jnp.where(row < n_valid_ref[j], x_ref[...], jnp.uint32(0))
        xb = _unpack_bf16_pairs(packed).astype(BF16)
        gu = jnp.dot(xb, wgu_bf[...], preferred_element_type=F32) + bgu_ref[...]
        x_glu = jnp.minimum(gu[:, :d], SWIGLU_LIMIT)
        x_lin = jnp.clip(gu[:, d:], -SWIGLU_LIMIT, SWIGLU_LIMIT)
        act = x_glu * jax.nn.sigmoid(SWIGLU_ALPHA * x_glu) * (x_lin + 1.0)
        o_ref[...] = _pack_bf16_pairs(
            jnp.dot(act.astype(BF16), wdn_bf[...], preferred_element_type=F32) + bdn_ref[...])

    @pl.when(j >= n_used_ref[0])
    def _():
        o_ref[...] = jnp.zeros_like(o_ref)


def _experts(xs, blk_e, n_used, n_valid, wgu, bgu, wdn, bdn, layer):
    d = wdn.shape[-1]
    n_blocks = blk_e.shape[0]
    depth, ne = wgu.shape[:2]
    by_expert = lambda j, be, nu, nv: (layer, be[j], 0, 0)
    grid_spec = pltpu.PrefetchScalarGridSpec(
        num_scalar_prefetch=3,
        grid=(n_blocks,),
        in_specs=[pl.BlockSpec((MOE_BLOCK, d // 2), lambda j, be, nu, nv: (jnp.minimum(j, nu[0] - 1), 0)),
                  pl.BlockSpec((None, None, d, 2 * d), by_expert),
                  pl.BlockSpec((None, None, 1, 2 * d), by_expert),
                  pl.BlockSpec((None, None, d, d), by_expert),
                  pl.BlockSpec((None, None, 1, d), by_expert)],
        out_specs=pl.BlockSpec((MOE_BLOCK, d // 2), lambda j, be, nu, nv: (j, 0)),
        scratch_shapes=[pltpu.VMEM((d, 2 * d), BF16), pltpu.VMEM((d, d), BF16)],
    )
    return pl.pallas_call(
        _expert_kernel,
        out_shape=jax.ShapeDtypeStruct((n_blocks * MOE_BLOCK, d // 2), jnp.uint32),
        grid_spec=grid_spec,
        compiler_params=_cparams("arbitrary"),
        name="moe_experts",
    )(blk_e, n_used, n_valid, xs, wgu, bgu.reshape(depth, ne, 1, 2 * d), wdn, bdn.reshape(depth, ne, 1, d))


def _sc_gather_rows(table, idx):
    n_out = idx.shape[0]
    d = table.shape[1]
    n_workers = SC_CORES * SC_SUBCORES
    per_w = n_out // n_workers
    n_chunks = per_w // SC_ROWS
    assert per_w * n_workers == n_out and n_chunks * SC_ROWS == per_w and n_chunks % 2 == 0
    mesh = plsc.VectorSubcoreMesh(core_axis_name="c", subcore_axis_name="s")

    @functools.partial(
        pl.kernel, mesh=mesh,
        out_type=jax.ShapeDtypeStruct((n_out, d), table.dtype),
        scratch_types=[pltpu.VMEM((per_w,), jnp.int32),
                       pltpu.VMEM((SC_ROWS, d), table.dtype),
                       pltpu.VMEM((SC_ROWS, d), table.dtype),
                       pltpu.SemaphoreType.DMA, pltpu.SemaphoreType.DMA],
        name="sc_gather_rows",
    )
    def gather_kernel(table_hbm, idx_hbm, out_hbm, idx_v, rows_a, rows_b, sem_a, sem_b):
        base = (lax.axis_index("s") * SC_CORES + lax.axis_index("c")) * per_w
        pltpu.sync_copy(idx_hbm.at[pl.ds(base, per_w)], idx_v)

        def gather(j, buf, sem):
            rows = idx_v.at[pl.ds(j * SC_ROWS, SC_ROWS)]
            return pltpu.make_async_copy(table_hbm.at[rows], buf, sem)

        def write_back(j, buf):
            pltpu.sync_copy(buf, out_hbm.at[pl.ds(base + j * SC_ROWS, SC_ROWS)])

        gather(0, rows_a, sem_a).start()

        @pl.loop(0, n_chunks, step=2)
        def _(j):
            gather(j + 1, rows_b, sem_b).start()
            gather(j, rows_a, sem_a).wait()
            write_back(j, rows_a)

            @pl.when(j + 2 < n_chunks)
            def _():
                gather(j + 2, rows_a, sem_a).start()
            gather(j + 1, rows_b, sem_b).wait()
            write_back(j + 1, rows_b)

    return gather_kernel(table, idx)


def _combine_kernel(y0_ref, y1_ref, y2_ref, y3_ref, w_ref, x_ref, g2_ref, fg_ref, xo_ref, *, final):
    w = w_ref[...]
    y = w[:, 0:1] * _unpack_bf16_pairs(y0_ref[...])
    for kk, y_ref in enumerate((y1_ref, y2_ref, y3_ref), start=1):
        y = y + w[:, kk:kk + 1] * _unpack_bf16_pairs(y_ref[...])
    xn = x_ref[...] + g2_ref[...] * y
    if final:
        ms = jnp.mean(xn * xn, axis=-1, keepdims=True)
        xn = xn * lax.rsqrt(ms + EPS) * fg_ref[...]
    xo_ref[...] = xn


def _combine(dest, out_sorted, gates_t, x2, g2, final_g, s, final):
    t, d = x2.shape
    tm = 256
    tpb = s // tm
    sc_unit = 2 * SC_ROWS * SC_CORES * SC_SUBCORES // TOP_K
    chunkable = t % (COMBINE_CHUNKS * sc_unit) == 0 and t % (COMBINE_CHUNKS * tm) == 0
    n_chunks = COMBINE_CHUNKS if chunkable else 1
    tc = t // n_chunks
    nt = tc // tm
    for c in range(n_chunks):
        yg = _sc_gather_rows(out_sorted, dest[:, c * tc:(c + 1) * tc].reshape(TOP_K * tc))
        plane = lambda kk: pl.BlockSpec((tm, d // 2), lambda i: (kk * nt + i, 0))
        tile = lambda i, c=c: (c * nt + i, 0)
        x2 = pl.pallas_call(
            functools.partial(_combine_kernel, final=final),
            out_shape=jax.ShapeDtypeStruct((t, d), F32),
            grid=(nt,),
            in_specs=[plane(0), plane(1), plane(2), plane(3),
                      pl.BlockSpec((tm, TOP_K), tile),
                      pl.BlockSpec((tm, d), tile),
                      pl.BlockSpec((None, 1, d), lambda i, c=c: ((c * nt + i) // tpb, 0, 0)),
                      pl.BlockSpec((1, d), lambda i: (0, 0))],
            out_specs=pl.BlockSpec((tm, d), tile),
            input_output_aliases={5: 0},
            compiler_params=_cparams("arbitrary"),
            name="moe_combine",
        )(yg, yg, yg, yg, gates_t, x2, g2, final_g)
    return x2


def _moe_plan(top_e, pos, counts):
    t = top_e.shape[1]
    sizes = counts[:, 0].astype(jnp.int32)
    padded = (sizes + MOE_BLOCK - 1) // MOE_BLOCK * MOE_BLOCK
    pad_end = jnp.cumsum(padded)
    pad_start = pad_end - padded
    start_of = jnp.zeros_like(top_e)
    for e in range(N_EXPERTS):
        start_of = jnp.where(top_e == e, pad_start[e], start_of)
    dest_kt = start_of + pos
    n_blocks = -(-t * TOP_K // MOE_BLOCK) + N_EXPERTS
    blk_start = jnp.arange(n_blocks, dtype=jnp.int32) * MOE_BLOCK
    blk_e = jnp.minimum(jnp.sum((pad_end[None, :] <= blk_start[:, None]).astype(jnp.int32), axis=1),
                        N_EXPERTS - 1).astype(jnp.int32)
    n_used = (pad_end[-1] // MOE_BLOCK).astype(jnp.int32).reshape(1)
    filled_end = pad_start + sizes
    n_valid = jnp.zeros((n_blocks,), jnp.int32)
    for e in range(N_EXPERTS):
        n_valid = jnp.where(blk_e == e, jnp.clip(filled_end[e] - blk_start, 0, MOE_BLOCK), n_valid)
    return dest_kt, blk_e, n_used, n_valid, n_blocks


def _rope_tables(s, gain):
    n_rows = s // GRID_W
    row = jnp.repeat(jnp.arange(n_rows), GRID_W).astype(F32)
    col = (jnp.arange(s) % GRID_W).astype(F32)
    half = HEAD_DIM // 2
    inv = ROPE_THETA ** (-jnp.arange(0, half, 2, dtype=F32) / half)
    ang_r = row[:, None] * inv
    ang_c = col[:, None] * inv
    cos = jnp.concatenate([jnp.cos(ang_r)] * 2 + [jnp.cos(ang_c)] * 2, axis=-1)
    sin = jnp.concatenate([-jnp.sin(ang_r), jnp.sin(ang_r), -jnp.sin(ang_c), jnp.sin(ang_c)], axis=-1)
    gain = gain.astype(F32)
    return cos * gain[None, :], sin * gain[_PARTNER][None, :]


_q = HEAD_DIM // 4
_PARTNER = np.concatenate([np.arange(_q, 2 * _q), np.arange(0, _q),
                           np.arange(3 * _q, 4 * _q), np.arange(2 * _q, 3 * _q)])


def _partner_cols(w, n_heads):
    idx = (np.arange(n_heads)[:, None] * HEAD_DIM + _PARTNER[None, :]).reshape(-1)
    return w[:, idx]


def kernel(x, c, w_ada, b_ada, norm1_g, w_in, q_norm_g, k_norm_g, rel_bias, w_br_a, w_br_b, w_out,
           norm2_g, w_router, b_router, w_gate_up, b_gate_up, w_down, b_down, final_norm_g):
    bsz, s, d = x.shape
    depth = w_ada.shape[0]
    mod = _ada_mod(c, w_ada, b_ada)
    n_streams = N_STREAMS if bsz % N_STREAMS == 0 else 1
    hb = bsz // n_streams
    ht = hb * s
    streams = range(n_streams)
    xs2 = [x[i * hb:(i + 1) * hb].reshape(ht, d) for i in streams]
    q_off = 3 * A_WIDTH
    k_off = q_off + B_Q
    for l in range(depth):
        mods = [[mod[l, i * hb:(i + 1) * hb, j * d:(j + 1) * d].reshape(hb, 1, d) for j in range(N_MOD)]
                for i in streams]
        w = w_in[l]
        wq, wk = w[:, q_off:q_off + B_Q], w[:, k_off:k_off + B_KV]
        w_ext = jnp.concatenate([wq, _partner_cols(wq, B_HEADS), wk, _partner_cols(wk, B_KV_HEADS),
                                 w[:, k_off + B_KV:k_off + 2 * B_KV], w[:, :q_off],
                                 w[:, k_off + 2 * B_KV:]], axis=1).astype(BF16)
        tabs = [jnp.tile(tb, (1, 2)) for tb in
                _rope_tables(s, q_norm_g[l]) + _rope_tables(s, k_norm_g[l])]
        wa, wb, wo = w_br_a[l].astype(BF16), w_br_b[l].astype(BF16), w_out[l].astype(BF16)
        proj = [_projection(xs2[i], mods[i][0], mods[i][1], norm1_g[l].reshape(1, d), w_ext, tabs, hb, s)
                for i in streams]
        attn_a = [[_dilated_group(proj[i][g], proj[i][7][g], rel_bias[:, g * A_HEADS:(g + 1) * A_HEADS],
                                  dil, hb, s)
                   for g, (_, dil) in enumerate(A_GROUPS)] for i in streams]
        attn_b = [_gqa_attention(proj[i][3], proj[i][4], proj[i][5], q_norm_g[l], k_norm_g[l])
                  .reshape(ht, B_Q) for i in streams]
        merged = [_merge_and_route([o for o, _ in attn_a[i]], [ls for _, ls in attn_a[i]], attn_b[i],
                                   proj[i][6], xs2[i], mods[i][2], wa, wb, wo, mods[i][3], mods[i][4],
                                   norm2_g[l].reshape(1, d), w_router[l], b_router[l], s)
                  for i in streams]
        xs2 = [m[0] for m in merged]
        routed = [m[1:] for m in merged]
        plans = [_moe_plan(routed[i][1], routed[i][3], routed[i][4]) for i in streams]
        slots = [_sc_scatter_rows(routed[i][0], plans[i][0].reshape(TOP_K * ht), plans[i][4] * MOE_BLOCK)
                 for i in streams]
        outs = [_experts(slots[i], plans[i][1], plans[i][2], plans[i][3],
                         w_gate_up, b_gate_up, w_down, b_down, l) for i in streams]
        xs2 = [_combine(plans[i][0], outs[i], routed[i][2].T, xs2[i], mods[i][5],
                        final_norm_g.reshape(1, d), s, final=(l == depth - 1)) for i in streams]
    return jnp.concatenate([xh.reshape(hb, s, d) for xh in xs2], axis=0)
```

```python
import functools
import math

import numpy as np
import jax
import jax.numpy as jnp
from jax import lax
from jax.experimental import pallas as pl
from jax.experimental.pallas import tpu as pltpu
from jax.experimental.pallas import tpu_sc as plsc

F32 = jnp.float32
BF16 = jnp.bfloat16

HEAD_DIM = 64
LANE = 128
PAIR = 2 * HEAD_DIM
A_GROUPS = ((128, 1), (512, 4), (2048, 16))
A_HEADS = 4
A_WIDTH = len(A_GROUPS) * A_HEADS * HEAD_DIM
A_OUT = A_HEADS * HEAD_DIM
A_GROUP_COLS = 3 * A_OUT
B_HEADS = 16
B_KV_HEADS = 4
B_GRP = B_HEADS // B_KV_HEADS
B_Q = B_HEADS * HEAD_DIM
B_KV = B_KV_HEADS * HEAD_DIM
GRID_W = 64
ROPE_THETA = 10000.0
REL_BUCKETS = 32
REL_MAX_DIST = 1024
N_EXPERTS = 32
TOP_K = 4
SWIGLU_LIMIT = 7.0
SWIGLU_ALPHA = 1.702
MOE_BLOCK = 512
COMBINE_CHUNKS = 4
MERGE_ROWS = 128
N_MOD = 6
N_STREAMS = 1
EPS = 1e-6
NEG_INF = -1e30
LOG2E = math.log2(math.e)
N_SIDE = 64
DIL_SUB = 2 * N_SIDE
DIL_NO_SHIFT_MAX = 60.0
DIL_STEP_ROWS = 512

VMEM_LIMIT = 56 * 1024 * 1024


def _pack_bf16_pairs(x):
    bits = pltpu.bitcast(x.astype(BF16).astype(F32), jnp.uint32)
    half = x.shape[1] // 2
    return (bits[:, :half] >> 16) | bits[:, half:]


def _unpack_bf16_pairs(u):
    return jnp.concatenate([pltpu.bitcast(u << 16, F32),
                            pltpu.bitcast(u & jnp.uint32(0xFFFF0000), F32)], axis=1)


def _cparams(*sem):
    return pltpu.CompilerParams(dimension_semantics=sem, vmem_limit_bytes=VMEM_LIMIT)


def _ada_kernel(c_ref, w_ref, b_ref, o_ref):
    c = c_ref[...]
    ca = (c * jax.nn.sigmoid(c)).astype(BF16)
    o_ref[...] = jnp.dot(ca, w_ref[...].astype(BF16), preferred_element_type=F32) + b_ref[...]


def _ada_mod(c, w_ada, b_ada):
    depth, d, n = w_ada.shape
    bsz = c.shape[0]
    tn = 1536
    return pl.pallas_call(
        _ada_kernel,
        out_shape=jax.ShapeDtypeStruct((depth, bsz, n), F32),
        grid=(depth, n // tn),
        in_specs=[pl.BlockSpec((bsz, d), lambda l, j: (0, 0)),
                  pl.BlockSpec((None, d, tn), lambda l, j: (l, 0, j)),
                  pl.BlockSpec((None, 1, tn), lambda l, j: (l, 0, j))],
        out_specs=pl.BlockSpec((None, bsz, tn), lambda l, j: (l, 0, j)),
        compiler_params=_cparams("arbitrary", "arbitrary"),
        name="ada_mod",
    )(c, w_ada, b_ada.reshape(depth, 1, n))


def _proj_kernel(x_ref, sh_ref, sc_ref, g_ref, w_ref, aq_ref, bq_ref, ak_ref, bk_ref, hsel_ref,
                 za0_ref, za1_ref, za2_ref, q_ref, k_ref, v_ref, gate_ref, nrm_ref, fold_ref):
    x = x_ref[...]
    ms = jnp.mean(x * x, axis=-1, keepdims=True)
    h = x * lax.rsqrt(ms + EPS) * g_ref[...]
    h = h * (1.0 + sc_ref[...]) + sh_ref[...]
    hb = h.astype(BF16)

    def mm(lo, hi):
        return jnp.dot(hb, w_ref[:, lo:hi], preferred_element_type=F32)

    tm = x.shape[0]
    o = 0
    zq = mm(o, o + B_Q); o += B_Q
    zqp = mm(o, o + B_Q); o += B_Q
    zk = mm(o, o + B_KV); o += B_KV
    zkp = mm(o, o + B_KV); o += B_KV
    zv = mm(o, o + B_KV); o += B_KV
    lane = lax.broadcasted_iota(jnp.int32, (tm, PAIR), 1)
    first = lane < HEAD_DIM

    def pair_norm_rope(z, zp, a, b):
        zz = z * z
        ss = jnp.where(first,
                       jnp.sum(jnp.where(first, zz, 0.0), axis=-1, keepdims=True),
                       jnp.sum(jnp.where(first, 0.0, zz), axis=-1, keepdims=True))
        return lax.rsqrt(ss * (1.0 / HEAD_DIM) + EPS) * (z * a + zp * b)

    aq, bq, ak, bk = aq_ref[...], bq_ref[...], ak_ref[...], bk_ref[...]
    for j in range(B_HEADS // 2):
        cols = slice(j * PAIR, (j + 1) * PAIR)
        q_ref[j] = (pair_norm_rope(zq[:, cols], zqp[:, cols], aq, bq)
                    * (LOG2E * HEAD_DIM ** -0.5)).astype(BF16)
    ones_col = jnp.where(lane == HEAD_DIM, 1.0, 0.0)
    for j in range(B_KV_HEADS // 2):
        cols = slice(j * PAIR, (j + 1) * PAIR)
        kk = pair_norm_rope(zk[:, cols], zkp[:, cols], ak, bk)
        kk_sw = pltpu.roll(kk, HEAD_DIM, axis=1)
        k_ref[2 * j] = jnp.where(first, kk, kk_sw).astype(BF16)
        k_ref[2 * j + 1] = jnp.where(first, kk_sw, kk).astype(BF16)
        vv = zv[:, cols]
        v_ref[2 * j] = jnp.where(first, vv, ones_col).astype(BF16)
        v_ref[2 * j + 1] = jnp.where(first, pltpu.roll(vv, HEAD_DIM, axis=1), ones_col).astype(BF16)
    @pl.when(pl.program_id(0) == 0)
    def _():
        nrm_ref[...] = jnp.zeros_like(nrm_ref)

    for g, (za_ref, (_, dil)) in enumerate(zip((za0_ref, za1_ref, za2_ref), A_GROUPS)):
        z = mm(o, o + A_GROUP_COLS)
        o += A_GROUP_COLS
        zqk = z[:, :2 * A_OUT]
        sq = jnp.dot((zqk * zqk).astype(BF16), hsel_ref[...], preferred_element_type=F32)
        nrm_ref[g:g + 1, :] = jnp.maximum(nrm_ref[g:g + 1, :], jnp.max(sq, axis=0, keepdims=True))
        if dil == 1:
            za_ref[...] = z.astype(BF16)
        else:
            for c in range(A_GROUP_COLS // LANE):
                fold_ref[c] = z[:, c * LANE:(c + 1) * LANE]
            for r in range(dil):
                for c in range(A_GROUP_COLS // LANE):
                    col = r * A_GROUP_COLS + c * LANE
                    za_ref[:, col:col + LANE] = (
                        fold_ref[c, pl.ds(r, tm // dil, stride=dil), :].astype(BF16))
    gate_ref[...] = jax.nn.sigmoid(mm(o, o + gate_ref.shape[-1])).astype(BF16)


def _projection(x2, sh, sc, g, w_ext, tabs, bsz, s):
    t, d = x2.shape
    tm = 256
    tpb = s // tm
    n_ext = w_ext.shape[1]
    n_gate = 2 * d
    row = lambda i: (i, 0)
    per_b = lambda i: (i // tpb, 0, 0)
    tab = lambda i: (i % tpb, 0)
    hm = lambda i: (i // tpb, 0, i % tpb, 0)
    out_shape = (
        [jax.ShapeDtypeStruct((bsz, s // dil, dil * A_GROUP_COLS), BF16) for _, dil in A_GROUPS]
        + [jax.ShapeDtypeStruct((bsz, B_HEADS // 2, s, PAIR), BF16),
           jax.ShapeDtypeStruct((bsz, B_KV_HEADS, s, PAIR), BF16),
           jax.ShapeDtypeStruct((bsz, B_KV_HEADS, s, PAIR), BF16),
           jax.ShapeDtypeStruct((t, n_gate), BF16),
           jax.ShapeDtypeStruct((len(A_GROUPS), LANE), F32)])
    head_sel = (jnp.arange(2 * A_OUT)[:, None] // HEAD_DIM == jnp.arange(LANE)[None, :]).astype(BF16)
    out_specs = (
        [pl.BlockSpec((None, tm // dil, dil * A_GROUP_COLS), lambda i: (i // tpb, i % tpb, 0))
         for _, dil in A_GROUPS]
        + [pl.BlockSpec((None, B_HEADS // 2, tm, PAIR), hm),
           pl.BlockSpec((None, B_KV_HEADS, tm, PAIR), hm),
           pl.BlockSpec((None, B_KV_HEADS, tm, PAIR), hm),
           pl.BlockSpec((tm, n_gate), row),
           pl.BlockSpec((len(A_GROUPS), LANE), lambda i: (0, 0))])
    return pl.pallas_call(
        _proj_kernel,
        out_shape=out_shape,
        grid=(t // tm,),
        in_specs=[pl.BlockSpec((tm, d), row),
                  pl.BlockSpec((None, 1, d), per_b),
                  pl.BlockSpec((None, 1, d), per_b),
                  pl.BlockSpec((1, d), lambda i: (0, 0)),
                  pl.BlockSpec((d, n_ext), lambda i: (0, 0))]
                 + [pl.BlockSpec((tm, PAIR), tab)] * 4
                 + [pl.BlockSpec((2 * A_OUT, LANE), lambda i: (0, 0))],
        out_specs=out_specs,
        scratch_shapes=[pltpu.VMEM((A_GROUP_COLS // LANE, tm, LANE), F32)],
        compiler_params=_cparams("arbitrary"),
        name="in_proj",
    )(x2, sh, sc, g, w_ext, *tabs, head_sel)


def _dilated_kernel(prev_ref, cur_ref, next_ref, bias_ref, o_ref, lse_ref, *, tq, seq_len, shift):
    for res in range(cur_ref.shape[1] // A_GROUP_COLS):
        c0 = res * A_GROUP_COLS
        part = lambda ref, j: ref[:, c0 + j * A_OUT:c0 + (j + 1) * A_OUT]
        _dilated_subsequence(
            part(cur_ref, 0),
            jnp.concatenate([part(prev_ref, 1), part(cur_ref, 1), part(next_ref, 1)], axis=0),
            jnp.concatenate([part(prev_ref, 2), part(cur_ref, 2), part(next_ref, 2)], axis=0),
            bias_ref, o_ref, lse_ref, res * A_OUT, tq=tq, seq_len=seq_len, shift=shift)


def _dilated_subsequence(q_all, k, v, bias_ref, o_ref, lse_ref, out_col, *, tq, seq_len, shift):
    i = pl.program_id(2)
    nk = DIL_SUB + 2 * N_SIDE
    col = lax.broadcasted_iota(jnp.int32, (DIL_SUB, nk), 1)
    if not shift:
        lane = lax.broadcasted_iota(jnp.int32, (v.shape[0], PAIR), 1)
        first = lane < HEAD_DIM
        ones_col = jnp.where(lane == HEAD_DIM, 1.0, 0.0)
        v_ext = []
        for j in range(A_HEADS // 2):
            vp = v[:, j * PAIR:(j + 1) * PAIR].astype(F32)
            v_ext += [jnp.where(first, vp, ones_col).astype(BF16),
                      jnp.where(first, pltpu.roll(vp, HEAD_DIM, axis=1), ones_col).astype(BF16)]
    for sb in range(tq // DIL_SUB):
        rows = slice(sb * DIL_SUB, (sb + 1) * DIL_SUB)
        q = q_all[rows, :]
        kb = k[sb * DIL_SUB:sb * DIL_SUB + nk]
        vb = v[sb * DIL_SUB:sb * DIL_SUB + nk]
        kpos = i * tq + sb * DIL_SUB - N_SIDE + col
        valid = (kpos >= 0) & (kpos < seq_len)
        for hh in range(A_HEADS):
            sl = slice(hh * HEAD_DIM, (hh + 1) * HEAD_DIM)
            sc = lax.dot_general(q[:, sl], kb[:, sl], (((1,), (1,)), ((), ())),
                                 preferred_element_type=F32)
            sc = sc * (HEAD_DIM ** -0.5) + bias_ref[hh]
            sc = jnp.where(valid, sc, NEG_INF)
            if shift:
                m = jnp.max(sc, axis=-1, keepdims=True)
                p = jnp.exp(sc - m)
                den = jnp.sum(p, axis=-1, keepdims=True)
                o = jnp.dot(p.astype(BF16), vb[:, sl], preferred_element_type=F32) / den
                lse = m + jnp.log(den)
            else:
                acc = jnp.dot(jnp.exp(sc).astype(BF16), v_ext[hh][sb * DIL_SUB:sb * DIL_SUB + nk],
                              preferred_element_type=F32)
                den = acc[:, HEAD_DIM:HEAD_DIM + 1]
                o = acc[:, :HEAD_DIM] / den
                lse = jnp.log(den)
            out_sl = slice(out_col + hh * HEAD_DIM, out_col + (hh + 1) * HEAD_DIM)
            o_ref[rows, out_sl] = o
            lse_ref[rows, out_sl] = jnp.broadcast_to(lse, (DIL_SUB, HEAD_DIM))


def _t5_bucket(rel):
    nb = REL_BUCKETS // 2
    max_exact = nb // 2
    ret = jnp.where(rel > 0, nb, 0)
    n = jnp.abs(rel)
    nf = jnp.maximum(n, 1).astype(F32)
    large = max_exact + (jnp.log(nf / max_exact) / math.log(REL_MAX_DIST / max_exact)
                         * (nb - max_exact)).astype(jnp.int32)
    large = jnp.minimum(large, nb - 1)
    return ret + jnp.where(n < max_exact, n, large)


def _band_bias(rel_bias_g, dilation, tq):
    nk = tq + 2 * N_SIDE
    rel = jnp.arange(nk)[None, :] - N_SIDE - jnp.arange(tq)[:, None]
    bucket = _t5_bucket(rel * dilation)[None]
    bias = jnp.full((A_HEADS,) + rel.shape, NEG_INF, F32)
    for b in range(REL_BUCKETS):
        bias = jnp.where(bucket == b, rel_bias_g[b].astype(F32)[:, None, None], bias)
    return jnp.where((jnp.abs(rel) <= N_SIDE)[None], bias, NEG_INF)


def _dilated_group(za, sq_norms, rel_bias_g, dilation, bsz, s):
    ll = s // dilation
    tq = min(DIL_STEP_ROWS, ll)
    assert ll % tq == 0 and tq % DIL_SUB == 0
    n_res = max(1, min(dilation, DIL_STEP_ROWS // tq))
    assert dilation % n_res == 0
    nhalf = ll // N_SIDE
    per = tq // N_SIDE
    slab = n_res * A_GROUP_COLS
    cur = pl.BlockSpec((None, tq, slab), lambda b, r, i: (b, i, r))
    prv = pl.BlockSpec((None, N_SIDE, slab), lambda b, r, i: (b, jnp.maximum(per * i - 1, 0), r))
    nxt = pl.BlockSpec((None, N_SIDE, slab), lambda b, r, i: (b, jnp.minimum(per * (i + 1), nhalf - 1), r))
    bias = _band_bias(rel_bias_g, dilation, DIL_SUB)
    out_sd = jax.ShapeDtypeStruct((bsz, ll, dilation * A_OUT), F32)
    out_spec = pl.BlockSpec((None, tq, n_res * A_OUT), lambda b, r, i: (b, i, r))

    def attend(shift, zv, bias):
        return pl.pallas_call(
            functools.partial(_dilated_kernel, tq=tq, seq_len=ll, shift=shift),
            out_shape=[out_sd, out_sd],
            grid=(bsz, dilation // n_res, ll // tq),
            in_specs=[prv, cur, nxt, pl.BlockSpec(bias.shape, lambda b, r, i: (0, 0, 0))],
            out_specs=[out_spec, out_spec],
            compiler_params=_cparams("arbitrary", "arbitrary", "arbitrary"),
            name=f"dilated_attn_d{dilation}" + ("_rowmax" if shift else ""),
        )(zv, zv, zv, bias)

    q2 = jnp.max(sq_norms[:A_HEADS])
    k2 = jnp.max(sq_norms[A_HEADS:2 * A_HEADS])
    bound = 1.03 * jnp.sqrt(q2 * k2) * (HEAD_DIM ** -0.5) + jnp.max(jnp.abs(rel_bias_g))
    return lax.cond(bound <= DIL_NO_SHIFT_MAX, functools.partial(attend, False),
                    functools.partial(attend, True), za, bias)


GQA_NO_SHIFT_MAX_LOG2 = 80.0
GQA_KEY_CHUNK = 128


def _stacked_heads(q_ref):
    first = lax.broadcasted_iota(jnp.int32, q_ref.shape[1:], 1) < HEAD_DIM
    blocks = []
    for j in range(q_ref.shape[0]):
        qp = q_ref[j]
        blocks += [jnp.where(first, qp, jnp.zeros_like(qp)), jnp.where(first, jnp.zeros_like(qp), qp)]
    return jnp.concatenate(blocks, axis=0)


def _gqa_kernel_noshift(q_ref, k_ref, v_ref, o_ref):
    tq = q_ref.shape[1]
    q = _stacked_heads(q_ref)
    nt = (((1,), (1,)), ((), ()))
    acc = jnp.zeros((q.shape[0], PAIR), F32)
    for c in range(k_ref.shape[0] // GQA_KEY_CHUNK):
        rows = slice(c * GQA_KEY_CHUNK, (c + 1) * GQA_KEY_CHUNK)
        p = jnp.exp2(lax.dot_general(q, k_ref[rows, :], nt, preferred_element_type=F32))
        acc = acc + jnp.dot(p.astype(BF16), v_ref[rows, :], preferred_element_type=F32)
    o = acc[:, :HEAD_DIM] / acc[:, HEAD_DIM:HEAD_DIM + 1]
    for hh in range(q.shape[0] // tq):
        o_ref[:, hh * HEAD_DIM:(hh + 1) * HEAD_DIM] = o[hh * tq:(hh + 1) * tq].astype(BF16)


def _gqa_kernel_rowmax(q_ref, k_ref, v_ref, o_ref):
    tq = q_ref.shape[1]
    q = _stacked_heads(q_ref)
    k = k_ref[...]
    v = v_ref[:, :HEAD_DIM]
    for hh in range(q.shape[0] // tq):
        sc = lax.dot_general(q[hh * tq:(hh + 1) * tq], k, (((1,), (1,)), ((), ())),
                             preferred_element_type=F32)
        m = jnp.max(sc, axis=-1, keepdims=True)
        p = jnp.exp2(sc - m)
        den = jnp.sum(p, axis=-1, keepdims=True)
        o = jnp.dot(p.astype(BF16), v, preferred_element_type=F32) / den
        o_ref[:, hh * HEAD_DIM:(hh + 1) * HEAD_DIM] = o.astype(BF16)


def _gqa_call(body, name, tq, q, k, v):
    bsz, _, s, _ = q.shape
    return pl.pallas_call(
        body,
        out_shape=jax.ShapeDtypeStruct((bsz, s, B_Q), BF16),
        grid=(bsz, B_KV_HEADS, s // tq),
        in_specs=[pl.BlockSpec((None, B_GRP // 2, tq, PAIR), lambda b, h, i: (b, h, i, 0)),
                  pl.BlockSpec((None, None, s, PAIR), lambda b, h, i: (b, h, 0, 0)),
                  pl.BlockSpec((None, None, s, PAIR), lambda b, h, i: (b, h, 0, 0))],
        out_specs=pl.BlockSpec((None, tq, B_GRP * HEAD_DIM), lambda b, h, i: (b, i, h)),
        compiler_params=_cparams("arbitrary", "arbitrary", "arbitrary"),
        name=name,
    )(q, k, v)


def _gqa_attention(q, k, v, q_gain, k_gain):
    bound = (HEAD_DIM ** 0.5) * LOG2E * 1.01 * jnp.max(jnp.abs(q_gain)) * jnp.max(jnp.abs(k_gain))
    return lax.cond(bound <= GQA_NO_SHIFT_MAX_LOG2,
                    functools.partial(_gqa_call, _gqa_kernel_noshift, "gqa_attn", 1024),
                    functools.partial(_gqa_call, _gqa_kernel_rowmax, "gqa_attn_rowmax", 256),
                    q, k, v)


def _merge_kernel(o0_ref, o1_ref, o2_ref, l0_ref, l1_ref, l2_ref, ob_ref, gate_ref, x_ref, g1_ref,
                  wa_ref, wb_ref, wo_ref, sh_ref, sc_ref, g_ref, whi_ref, wlo_ref, b_ref,
                  xo_ref, h_ref, e_ref, w_ref, pos_ref, cnt_ref, run_ref, *unfold_refs):
    tm, d = x_ref.shape
    scratch = list(unfold_refs)

    def token_order(ref, dil):
        if dil == 1:
            return ref[...]
        buf = scratch.pop()
        for r in range(dil):
            for c in range(A_OUT // LANE):
                col = r * A_OUT + c * LANE
                buf[c, pl.ds(r, tm // dil, stride=dil), :] = ref[:, col:col + LANE]
        return jnp.concatenate([buf[c] for c in range(A_OUT // LANE)], axis=1)

    dils = [dil for _, dil in A_GROUPS]
    o0, o1, o2 = [token_order(r, dl) for r, dl in zip((o0_ref, o1_ref, o2_ref), dils)]
    l0, l1, l2 = [token_order(r, dl) for r, dl in zip((l0_ref, l1_ref, l2_ref), dils)]
    m = jnp.maximum(jnp.maximum(l0, l1), l2)
    e0, e1, e2 = jnp.exp(l0 - m), jnp.exp(l1 - m), jnp.exp(l2 - m)
    tot = e0 + e1 + e2
    oa = ((e0 / tot) * o0 + (e1 / tot) * o1 + (e2 / tot) * o2).astype(BF16)
    xn_rows = []
    for lo in range(0, tm, MERGE_ROWS):
        rows = slice(lo, lo + MERGE_ROWS)
        ya = jnp.dot(oa[rows], wa_ref[...], preferred_element_type=F32)
        yb = jnp.dot(ob_ref[rows, :], wb_ref[...], preferred_element_type=F32)
        merged = gate_ref[rows, :d].astype(F32) * ya + gate_ref[rows, d:].astype(F32) * yb
        y = jnp.dot(merged.astype(BF16), wo_ref[...], preferred_element_type=F32)
        xn_rows.append(x_ref[rows, :] + g1_ref[...] * y)
    xn = jnp.concatenate(xn_rows, axis=0)
    xo_ref[...] = xn
    _route_tile(xn, sh_ref, sc_ref, g_ref, whi_ref, wlo_ref, b_ref, h_ref, e_ref, w_ref, pos_ref, cnt_ref,
                run_ref)


def _merge_and_route(os_, ls_, ob, gate, x2, g1, wa, wb, wo, sh2, sc2, g2n, w_router, b_router, s):
    t, d = x2.shape
    tm = 256
    tpb = s // tm
    ne = w_router.shape[1]
    wt = w_router.T
    whi = wt.astype(BF16)
    wlo = (wt - whi.astype(F32)).astype(BF16)
    row = lambda i: (i, 0)
    const = lambda i: (0, 0)
    per_b = lambda i: (i // tpb, 0, 0)
    by_token = pl.BlockSpec((TOP_K, tm), lambda i: (0, i))
    a_specs = [pl.BlockSpec((None, tm // dil, dil * A_OUT), lambda i: (i // tpb, i % tpb, 0))
               for _, dil in A_GROUPS]
    n_unfold = 2 * sum(1 for _, dil in A_GROUPS if dil > 1)
    return pl.pallas_call(
        _merge_kernel,
        out_shape=[jax.ShapeDtypeStruct((t, d), F32),
                   jax.ShapeDtypeStruct((t, d // 2), jnp.uint32),
                   jax.ShapeDtypeStruct((TOP_K, t), jnp.int32),
                   jax.ShapeDtypeStruct((TOP_K, t), F32),
                   jax.ShapeDtypeStruct((TOP_K, t), jnp.int32),
                   jax.ShapeDtypeStruct((ne, 1), F32)],
        grid=(t // tm,),
        scratch_shapes=[pltpu.VMEM((ne, 1), F32)]
                       + [pltpu.VMEM((A_OUT // LANE, tm, LANE), F32)] * n_unfold,
        in_specs=a_specs * 2
                 + [pl.BlockSpec((tm, B_Q), row),
                    pl.BlockSpec((tm, 2 * d), row),
                    pl.BlockSpec((tm, d), row),
                    pl.BlockSpec((None, 1, d), per_b),
                    pl.BlockSpec(wa.shape, const),
                    pl.BlockSpec(wb.shape, const),
                    pl.BlockSpec(wo.shape, const),
                    pl.BlockSpec((None, 1, d), per_b),
                    pl.BlockSpec((None, 1, d), per_b),
                    pl.BlockSpec((1, d), const),
                    pl.BlockSpec((ne, d), const),
                    pl.BlockSpec((ne, d), const),
                    pl.BlockSpec((ne, 1), const)],
        out_specs=[pl.BlockSpec((tm, d), row),
                   pl.BlockSpec((tm, d // 2), row),
                   by_token, by_token, by_token,
                   pl.BlockSpec((ne, 1), const)],
        compiler_params=_cparams("arbitrary"),
        name="merge_route",
    )(*os_, *ls_, ob, gate, x2, g1, wa, wb, wo, sh2, sc2, g2n, whi, wlo, b_router.reshape(ne, 1))


def _route_tile(x, sh_ref, sc_ref, g_ref, whi_ref, wlo_ref, b_ref,
                h_ref, e_ref, w_ref, pos_ref, cnt_ref, run_ref):
    @pl.when(pl.program_id(0) == 0)
    def _():
        run_ref[...] = jnp.zeros_like(run_ref)

    ms = jnp.mean(x * x, axis=-1, keepdims=True)
    h = x * lax.rsqrt(ms + EPS) * g_ref[...]
    h = h * (1.0 + sc_ref[...]) + sh_ref[...]
    h_ref[...] = _pack_bf16_pairs(h)
    hhi = h.astype(BF16)
    hlo = (h - hhi.astype(F32)).astype(BF16)
    nt = (((1,), (1,)), ((), ()))
    dotf = lambda a, b: lax.dot_general(a, b, nt, preferred_element_type=F32)
    logits = dotf(whi_ref[...], hhi) + (dotf(whi_ref[...], hlo) + dotf(wlo_ref[...], hhi)) + b_ref[...]
    ne, tm = logits.shape
    iota = lax.broadcasted_iota(jnp.int32, (ne, tm), 0).astype(F32)
    vals, idxs = [], []
    cur = logits
    for _ in range(TOP_K):
        m = jnp.max(cur, axis=0, keepdims=True)
        idx = jnp.min(jnp.where(cur == m, iota, float(ne)), axis=0, keepdims=True)
        vals.append(m)
        idxs.append(idx)
        cur = jnp.where(iota == idx, -jnp.inf, cur)
    tv = jnp.concatenate(vals, axis=0)
    ex = jnp.exp(tv - tv[0:1])
    w_ref[...] = ex / jnp.sum(ex, axis=0, keepdims=True)
    e_ref[...] = jnp.concatenate(idxs, axis=0).astype(jnp.int32)
    onehot = jnp.zeros((ne, tm), F32)
    for idx in idxs:
        onehot = onehot + jnp.where(iota == idx, 1.0, 0.0)
    earlier = (lax.broadcasted_iota(jnp.int32, (tm, tm), 0)
               < lax.broadcasted_iota(jnp.int32, (tm, tm), 1))
    rank = jnp.dot(onehot.astype(BF16), jnp.where(earlier, 1.0, 0.0).astype(BF16),
                   preferred_element_type=F32) + run_ref[...]
    pos_ref[...] = jnp.concatenate(
        [jnp.sum(jnp.where(iota == idx, rank, 0.0), axis=0, keepdims=True) for idx in idxs],
        axis=0).astype(jnp.int32)
    run_ref[...] = run_ref[...] + jnp.sum(onehot, axis=1, keepdims=True)
    cnt_ref[...] = run_ref[...]


SC_CORES = 2
SC_SUBCORES = 16
SC_ROWS = 32


def _sc_scatter_rows(rows, dest_kt, n_slots):
    t, d = rows.shape
    n_workers = SC_CORES * SC_SUBCORES
    per_w = t // n_workers
    n_chunks = per_w // SC_ROWS
    assert per_w * n_workers == t and n_chunks * SC_ROWS == per_w and n_chunks % 2 == 0
    idx = dest_kt.reshape(TOP_K, n_workers, n_chunks, SC_ROWS).transpose(1, 2, 0, 3)
    idx = idx.reshape(n_workers * n_chunks * TOP_K, SC_ROWS)
    lists_per_w = n_chunks * TOP_K
    mesh = plsc.VectorSubcoreMesh(core_axis_name="c", subcore_axis_name="s")

    @functools.partial(
        pl.kernel, mesh=mesh,
        out_type=jax.ShapeDtypeStruct((n_slots, d), rows.dtype),
        scratch_types=[pltpu.VMEM((lists_per_w, SC_ROWS), jnp.int32),
                       pltpu.VMEM((SC_ROWS, d), rows.dtype),
                       pltpu.VMEM((SC_ROWS, d), rows.dtype)] + [pltpu.SemaphoreType.DMA] * 4,
        name="sc_scatter_rows",
    )
    def scatter_kernel(rows_hbm, idx_hbm, out_hbm, idx_v, buf_a, buf_b, ld_a, ld_b, st_a, st_b):
        wid = lax.axis_index("s") * SC_CORES + lax.axis_index("c")
        pltpu.sync_copy(idx_hbm.at[pl.ds(wid * lists_per_w, lists_per_w)], idx_v)

        def load(j, buf, sem):
            return pltpu.make_async_copy(rows_hbm.at[pl.ds(wid * per_w + j * SC_ROWS, SC_ROWS)], buf, sem)

        def scatter_all(j, buf, sem):
            copies = [pltpu.make_async_copy(buf, out_hbm.at[idx_v.at[j * TOP_K + kk]], sem)
                      for kk in range(TOP_K)]
            for cp in copies:
                cp.start()
            for cp in copies:
                cp.wait()

        load(0, buf_a, ld_a).start()

        @pl.loop(0, n_chunks, step=2)
        def _(j):
            load(j + 1, buf_b, ld_b).start()
            load(j, buf_a, ld_a).wait()
            scatter_all(j, buf_a, st_a)

            @pl.when(j + 2 < n_chunks)
            def _():
                load(j + 2, buf_a, ld_a).start()
            load(j + 1, buf_b, ld_b).wait()
            scatter_all(j + 1, buf_b, st_b)

    return scatter_kernel(rows, idx)


def _expert_kernel(blk_e_ref, n_used_ref, n_valid_ref, x_ref, wgu_ref, bgu_ref, wdn_ref, bdn_ref, o_ref,
                   wgu_bf, wdn_bf):
    j = pl.program_id(0)
    d = wdn_bf.shape[1]

    @pl.when(j < n_used_ref[0])
    def _():
        @pl.when((j == 0) | (blk_e_ref[j] != blk_e_ref[jnp.maximum(j - 1, 0)]))
        def _():
            wgu_bf[...] = wgu_ref[...].astype(BF16)
            wdn_bf[...] = wdn_ref[...].astype(BF16)

        row = lax.broadcasted_iota(jnp.int32, x_ref.shape, 0)
        packed = jnp.where(row < n_valid_ref[j], x_ref[...], jnp.uint32(0))
        xb = _unpack_bf16_pairs(packed).astype(BF16)
        gu = jnp.dot(xb, wgu_bf[...], preferred_element_type=F32) + bgu_ref[...]
        x_glu = jnp.minimum(gu[:, :d], SWIGLU_LIMIT)
        x_lin = jnp.clip(gu[:, d:], -SWIGLU_LIMIT, SWIGLU_LIMIT)
        act = x_glu * jax.nn.sigmoid(SWIGLU_ALPHA * x_glu) * (x_lin + 1.0)
        o_ref[...] = _pack_bf16_pairs(
            jnp.dot(act.astype(BF16), wdn_bf[...], preferred_element_type=F32) + bdn_ref[...])

    @pl.when(j >= n_used_ref[0])
    def _():
        o_ref[...] = jnp.zeros_like(o_ref)


def _experts(xs, blk_e, n_used, n_valid, wgu, bgu, wdn, bdn, layer):
    d = wdn.shape[-1]
    n_blocks = blk_e.shape[0]
    depth, ne = wgu.shape[:2]
    by_expert = lambda j, be, nu, nv: (layer, be[j], 0, 0)
    grid_spec = pltpu.PrefetchScalarGridSpec(
        num_scalar_prefetch=3,
        grid=(n_blocks,),
        in_specs=[pl.BlockSpec((MOE_BLOCK, d // 2), lambda j, be, nu, nv: (jnp.minimum(j, nu[0] - 1), 0)),
                  pl.BlockSpec((None, None, d, 2 * d), by_expert),
                  pl.BlockSpec((None, None, 1, 2 * d), by_expert),
                  pl.BlockSpec((None, None, d, d), by_expert),
                  pl.BlockSpec((None, None, 1, d), by_expert)],
        out_specs=pl.BlockSpec((MOE_BLOCK, d // 2), lambda j, be, nu, nv: (j, 0)),
        scratch_shapes=[pltpu.VMEM((d, 2 * d), BF16), pltpu.VMEM((d, d), BF16)],
    )
    return pl.pallas_call(
        _expert_kernel,
        out_shape=jax.ShapeDtypeStruct((n_blocks * MOE_BLOCK, d // 2), jnp.uint32),
        grid_spec=grid_spec,
        compiler_params=_cparams("arbitrary"),
        name="moe_experts",
    )(blk_e, n_used, n_valid, xs, wgu, bgu.reshape(depth, ne, 1, 2 * d), wdn, bdn.reshape(depth, ne, 1, d))


def _sc_gather_rows(table, idx):
    n_out = idx.shape[0]
    d = table.shape[1]
    n_workers = SC_CORES * SC_SUBCORES
    per_w = n_out // n_workers
    n_chunks = per_w // SC_ROWS
    assert per_w * n_workers == n_out and n_chunks * SC_ROWS == per_w and n_chunks % 2 == 0
    mesh = plsc.VectorSubcoreMesh(core_axis_name="c", subcore_axis_name="s")

    @functools.partial(
        pl.kernel, mesh=mesh,
        out_type=jax.ShapeDtypeStruct((n_out, d), table.dtype),
        scratch_types=[pltpu.VMEM((per_w,), jnp.int32),
                       pltpu.VMEM((SC_ROWS, d), table.dtype),
                       pltpu.VMEM((SC_ROWS, d), table.dtype),
                       pltpu.SemaphoreType.DMA, pltpu.SemaphoreType.DMA],
        name="sc_gather_rows",
    )
    def gather_kernel(table_hbm, idx_hbm, out_hbm, idx_v, rows_a, rows_b, sem_a, sem_b):
        base = (lax.axis_index("s") * SC_CORES + lax.axis_index("c")) * per_w
        pltpu.sync_copy(idx_hbm.at[pl.ds(base, per_w)], idx_v)

        def gather(j, buf, sem):
            rows = idx_v.at[pl.ds(j * SC_ROWS, SC_ROWS)]
            return pltpu.make_async_copy(table_hbm.at[rows], buf, sem)

        def write_back(j, buf):
            pltpu.sync_copy(buf, out_hbm.at[pl.ds(base + j * SC_ROWS, SC_ROWS)])

        gather(0, rows_a, sem_a).start()

        @pl.loop(0, n_chunks, step=2)
        def _(j):
            gather(j + 1, rows_b, sem_b).start()
            gather(j, rows_a, sem_a).wait()
            write_back(j, rows_a)

            @pl.when(j + 2 < n_chunks)
            def _():
                gather(j + 2, rows_a, sem_a).start()
            gather(j + 1, rows_b, sem_b).wait()
            write_back(j + 1, rows_b)

    return gather_kernel(table, idx)


def _combine_kernel(y0_ref, y1_ref, y2_ref, y3_ref, w_ref, x_ref, g2_ref, fg_ref, xo_ref, *, final):
    w = w_ref[...]
    y = w[:, 0:1] * _unpack_bf16_pairs(y0_ref[...])
    for kk, y_ref in enumerate((y1_ref, y2_ref, y3_ref), start=1):
        y = y + w[:, kk:kk + 1] * _unpack_bf16_pairs(y_ref[...])
    xn = x_ref[...] + g2_ref[...] * y
    if final:
        ms = jnp.mean(xn * xn, axis=-1, keepdims=True)
        xn = xn * lax.rsqrt(ms + EPS) * fg_ref[...]
    xo_ref[...] = xn


def _combine(dest, out_sorted, gates_t, x2, g2, final_g, s, final):
    t, d = x2.shape
    tm = 256
    tpb = s // tm
    sc_unit = 2 * SC_ROWS * SC_CORES * SC_SUBCORES // TOP_K
    chunkable = t % (COMBINE_CHUNKS * sc_unit) == 0 and t % (COMBINE_CHUNKS * tm) == 0
    n_chunks = COMBINE_CHUNKS if chunkable else 1
    tc = t // n_chunks
    nt = tc // tm
    for c in range(n_chunks):
        yg = _sc_gather_rows(out_sorted, dest[:, c * tc:(c + 1) * tc].reshape(TOP_K * tc))
        plane = lambda kk: pl.BlockSpec((tm, d // 2), lambda i: (kk * nt + i, 0))
        tile = lambda i, c=c: (c * nt + i, 0)
        x2 = pl.pallas_call(
            functools.partial(_combine_kernel, final=final),
            out_shape=jax.ShapeDtypeStruct((t, d), F32),
            grid=(nt,),
            in_specs=[plane(0), plane(1), plane(2), plane(3),
                      pl.BlockSpec((tm, TOP_K), tile),
                      pl.BlockSpec((tm, d), tile),
                      pl.BlockSpec((None, 1, d), lambda i, c=c: ((c * nt + i) // tpb, 0, 0)),
                      pl.BlockSpec((1, d), lambda i: (0, 0))],
            out_specs=pl.BlockSpec((tm, d), tile),
            input_output_aliases={5: 0},
            compiler_params=_cparams("arbitrary"),
            name="moe_combine",
        )(yg, yg, yg, yg, gates_t, x2, g2, final_g)
    return x2


def _moe_plan(top_e, pos, counts):
    t = top_e.shape[1]
    sizes = counts[:, 0].astype(jnp.int32)
    padded = (sizes + MOE_BLOCK - 1) // MOE_BLOCK * MOE_BLOCK
    pad_end = jnp.cumsum(padded)
    pad_start = pad_end - padded
    start_of = jnp.zeros_like(top_e)
    for e in range(N_EXPERTS):
        start_of = jnp.where(top_e == e, pad_start[e], start_of)
    dest_kt = start_of + pos
    n_blocks = -(-t * TOP_K // MOE_BLOCK) + N_EXPERTS
    blk_start = jnp.arange(n_blocks, dtype=jnp.int32) * MOE_BLOCK
    blk_e = jnp.minimum(jnp.sum((pad_end[None, :] <= blk_start[:, None]).astype(jnp.int32), axis=1),
                        N_EXPERTS - 1).astype(jnp.int32)
    n_used = (pad_end[-1] // MOE_BLOCK).astype(jnp.int32).reshape(1)
    filled_end = pad_start + sizes
    n_valid = jnp.zeros((n_blocks,), jnp.int32)
    for e in range(N_EXPERTS):
        n_valid = jnp.where(blk_e == e, jnp.clip(filled_end[e] - blk_start, 0, MOE_BLOCK), n_valid)
    return dest_kt, blk_e, n_used, n_valid, n_blocks


def _rope_tables(s, gain):
    n_rows = s // GRID_W
    row = jnp.repeat(jnp.arange(n_rows), GRID_W).astype(F32)
    col = (jnp.arange(s) % GRID_W).astype(F32)
    half = HEAD_DIM // 2
    inv = ROPE_THETA ** (-jnp.arange(0, half, 2, dtype=F32) / half)
    ang_r = row[:, None] * inv
    ang_c = col[:, None] * inv
    cos = jnp.concatenate([jnp.cos(ang_r)] * 2 + [jnp.cos(ang_c)] * 2, axis=-1)
    sin = jnp.concatenate([-jnp.sin(ang_r), jnp.sin(ang_r), -jnp.sin(ang_c), jnp.sin(ang_c)], axis=-1)
    gain = gain.astype(F32)
    return cos * gain[None, :], sin * gain[_PARTNER][None, :]


_q = HEAD_DIM // 4
_PARTNER = np.concatenate([np.arange(_q, 2 * _q), np.arange(0, _q),
                           np.arange(3 * _q, 4 * _q), np.arange(2 * _q, 3 * _q)])


def _partner_cols(w, n_heads):
    idx = (np.arange(n_heads)[:, None] * HEAD_DIM + _PARTNER[None, :]).reshape(-1)
    return w[:, idx]


def kernel(x, c, w_ada, b_ada, norm1_g, w_in, q_norm_g, k_norm_g, rel_bias, w_br_a, w_br_b, w_out,
           norm2_g, w_router, b_router, w_gate_up, b_gate_up, w_down, b_down, final_norm_g):
    bsz, s, d = x.shape
    depth = w_ada.shape[0]
    mod = _ada_mod(c, w_ada, b_ada)
    n_streams = N_STREAMS if bsz % N_STREAMS == 0 else 1
    hb = bsz // n_streams
    ht = hb * s
    streams = range(n_streams)
    xs2 = [x[i * hb:(i + 1) * hb].reshape(ht, d) for i in streams]
    q_off = 3 * A_WIDTH
    k_off = q_off + B_Q
    for l in range(depth):
        mods = [[mod[l, i * hb:(i + 1) * hb, j * d:(j + 1) * d].reshape(hb, 1, d) for j in range(N_MOD)]
                for i in streams]
        w = w_in[l]
        wq, wk = w[:, q_off:q_off + B_Q], w[:, k_off:k_off + B_KV]
        w_ext = jnp.concatenate([wq, _partner_cols(wq, B_HEADS), wk, _partner_cols(wk, B_KV_HEADS),
                                 w[:, k_off + B_KV:k_off + 2 * B_KV], w[:, :q_off],
                                 w[:, k_off + 2 * B_KV:]], axis=1).astype(BF16)
        tabs = [jnp.tile(tb, (1, 2)) for tb in
                _rope_tables(s, q_norm_g[l]) + _rope_tables(s, k_norm_g[l])]
        wa, wb, wo = w_br_a[l].astype(BF16), w_br_b[l].astype(BF16), w_out[l].astype(BF16)
        proj = [_projection(xs2[i], mods[i][0], mods[i][1], norm1_g[l].reshape(1, d), w_ext, tabs, hb, s)
                for i in streams]
        attn_a = [[_dilated_group(proj[i][g], proj[i][7][g], rel_bias[:, g * A_HEADS:(g + 1) * A_HEADS],
                                  dil, hb, s)
                   for g, (_, dil) in enumerate(A_GROUPS)] for i in streams]
        attn_b = [_gqa_attention(proj[i][3], proj[i][4], proj[i][5], q_norm_g[l], k_norm_g[l])
                  .reshape(ht, B_Q) for i in streams]
        merged = [_merge_and_route([o for o, _ in attn_a[i]], [ls for _, ls in attn_a[i]], attn_b[i],
                                   proj[i][6], xs2[i], mods[i][2], wa, wb, wo, mods[i][3], mods[i][4],
                                   norm2_g[l].reshape(1, d), w_router[l], b_router[l], s)
                  for i in streams]
        xs2 = [m[0] for m in merged]
        routed = [m[1:] for m in merged]
        plans = [_moe_plan(routed[i][1], routed[i][3], routed[i][4]) for i in streams]
        slots = [_sc_scatter_rows(routed[i][0], plans[i][0].reshape(TOP_K * ht), plans[i][4] * MOE_BLOCK)
                 for i in streams]
        outs = [_experts(slots[i], plans[i][1], plans[i][2], plans[i][3],
                         w_gate_up, b_gate_up, w_down, b_down, l) for i in streams]
        xs2 = [_combine(plans[i][0], outs[i], routed[i][2].T, xs2[i], mods[i][5],
                        final_norm_g.reshape(1, d), s, final=(l == depth - 1)) for i in streams]
    return jnp.concatenate([xh.reshape(hb, s, d) for xh in xs2], axis=0)
```

```python
import functools
import math

import numpy as np
import jax
import jax.numpy as jnp
from jax import lax
from jax.experimental import pallas as pl
from jax.experimental.pallas import tpu as pltpu
from jax.experimental.pallas import tpu_sc as plsc

F32 = jnp.float32
BF16 = jnp.bfloat16

HEAD_DIM = 64
LANE = 128
PAIR = 2 * HEAD_DIM
A_GROUPS = ((128, 1), (512, 4), (2048, 16))
A_HEADS = 4
A_WIDTH = len(A_GROUPS) * A_HEADS * HEAD_DIM
A_OUT = A_HEADS * HEAD_DIM
A_GROUP_COLS = 3 * A_OUT
B_HEADS = 16
B_KV_HEADS = 4
B_GRP = B_HEADS // B_KV_HEADS
B_Q = B_HEADS * HEAD_DIM
B_KV = B_KV_HEADS * HEAD_DIM
GRID_W = 64
ROPE_THETA = 10000.0
REL_BUCKETS = 32
REL_MAX_DIST = 1024
N_EXPERTS = 32
TOP_K = 4
SWIGLU_LIMIT = 7.0
SWIGLU_ALPHA = 1.702
MOE_BLOCK = 512
COMBINE_CHUNKS = 4
MERGE_ROWS = 128
N_MOD = 6
EPS = 1e-6
NEG_INF = -1e30
LOG2E = math.log2(math.e)
N_SIDE = 64
DIL_SUB = 2 * N_SIDE
DIL_NO_SHIFT_MAX = 60.0
DIL_STEP_ROWS = 512

VMEM_LIMIT = 56 * 1024 * 1024


def _pack_bf16_pairs(x):
    bits = pltpu.bitcast(x.astype(BF16).astype(F32), jnp.uint32)
    half = x.shape[1] // 2
    return (bits[:, :half] >> 16) | bits[:, half:]


def _unpack_bf16_pairs(u):
    return jnp.concatenate([pltpu.bitcast(u << 16, F32),
                            pltpu.bitcast(u & jnp.uint32(0xFFFF0000), F32)], axis=1)


def _cparams(*sem):
    return pltpu.CompilerParams(dimension_semantics=sem, vmem_limit_bytes=VMEM_LIMIT)


def _ada_kernel(c_ref, w_ref, b_ref, o_ref):
    c = c_ref[...]
    ca = (c * jax.nn.sigmoid(c)).astype(BF16)
    o_ref[...] = jnp.dot(ca, w_ref[...].astype(BF16), preferred_element_type=F32) + b_ref[...]


def _ada_mod(c, w_ada, b_ada):
    depth, d, n = w_ada.shape
    bsz = c.shape[0]
    tn = 1536
    return pl.pallas_call(
        _ada_kernel,
        out_shape=jax.ShapeDtypeStruct((depth, bsz, n), F32),
        grid=(depth, n // tn),
        in_specs=[pl.BlockSpec((bsz, d), lambda l, j: (0, 0)),
                  pl.BlockSpec((None, d, tn), lambda l, j: (l, 0, j)),
                  pl.BlockSpec((None, 1, tn), lambda l, j: (l, 0, j))],
        out_specs=pl.BlockSpec((None, bsz, tn), lambda l, j: (l, 0, j)),
        compiler_params=_cparams("arbitrary", "arbitrary"),
        name="ada_mod",
    )(c, w_ada, b_ada.reshape(depth, 1, n))


def _proj_kernel(x_ref, sh_ref, sc_ref, g_ref, w_ref, aq_ref, bq_ref, ak_ref, bk_ref, hsel_ref,
                 za0_ref, za1_ref, za2_ref, q_ref, k_ref, v_ref, gate_ref, nrm_ref, fold_ref):
    x = x_ref[...]
    ms = jnp.mean(x * x, axis=-1, keepdims=True)
    h = x * lax.rsqrt(ms + EPS) * g_ref[...]
    h = h * (1.0 + sc_ref[...]) + sh_ref[...]
    hb = h.astype(BF16)

    def mm(lo, hi):
        return jnp.dot(hb, w_ref[:, lo:hi], preferred_element_type=F32)

    tm = x.shape[0]
    o = 0
    zq = mm(o, o + B_Q); o += B_Q
    zqp = mm(o, o + B_Q); o += B_Q
    zk = mm(o, o + B_KV); o += B_KV
    zkp = mm(o, o + B_KV); o += B_KV
    zv = mm(o, o + B_KV); o += B_KV
    lane = lax.broadcasted_iota(jnp.int32, (tm, PAIR), 1)
    first = lane < HEAD_DIM

    def pair_norm_rope(z, zp, a, b):
        zz = z * z
        ss = jnp.where(first,
                       jnp.sum(jnp.where(first, zz, 0.0), axis=-1, keepdims=True),
                       jnp.sum(jnp.where(first, 0.0, zz), axis=-1, keepdims=True))
        return lax.rsqrt(ss * (1.0 / HEAD_DIM) + EPS) * (z * a + zp * b)

    aq, bq, ak, bk = aq_ref[...], bq_ref[...], ak_ref[...], bk_ref[...]
    for j in range(B_HEADS // 2):
        cols = slice(j * PAIR, (j + 1) * PAIR)
        q_ref[j] = (pair_norm_rope(zq[:, cols], zqp[:, cols], aq, bq)
                    * (LOG2E * HEAD_DIM ** -0.5)).astype(BF16)
    ones_col = jnp.where(lane == HEAD_DIM, 1.0, 0.0)
    for j in range(B_KV_HEADS // 2):
        cols = slice(j * PAIR, (j + 1) * PAIR)
        kk = pair_norm_rope(zk[:, cols], zkp[:, cols], ak, bk)
        kk_sw = pltpu.roll(kk, HEAD_DIM, axis=1)
        k_ref[2 * j] = jnp.where(first, kk, kk_sw).astype(BF16)
        k_ref[2 * j + 1] = jnp.where(first, kk_sw, kk).astype(BF16)
        vv = zv[:, cols]
        v_ref[2 * j] = jnp.where(first, vv, ones_col).astype(BF16)
        v_ref[2 * j + 1] = jnp.where(first, pltpu.roll(vv, HEAD_DIM, axis=1), ones_col).astype(BF16)
    @pl.when(pl.program_id(0) == 0)
    def _():
        nrm_ref[...] = jnp.zeros_like(nrm_ref)

    for g, (za_ref, (_, dil)) in enumerate(zip((za0_ref, za1_ref, za2_ref), A_GROUPS)):
        z = mm(o, o + A_GROUP_COLS)
        o += A_GROUP_COLS
        zqk = z[:, :2 * A_OUT]
        sq = jnp.dot((zqk * zqk).astype(BF16), hsel_ref[...], preferred_element_type=F32)
        nrm_ref[g:g + 1, :] = jnp.maximum(nrm_ref[g:g + 1, :], jnp.max(sq, axis=0, keepdims=True))
        if dil == 1:
            za_ref[...] = z.astype(BF16)
        else:
            for c in range(A_GROUP_COLS // LANE):
                fold_ref[c] = z[:, c * LANE:(c + 1) * LANE]
            for r in range(dil):
                for c in range(A_GROUP_COLS // LANE):
                    col = r * A_GROUP_COLS + c * LANE
                    za_ref[:, col:col + LANE] = (
                        fold_ref[c, pl.ds(r, tm // dil, stride=dil), :].astype(BF16))
    gate_ref[...] = jax.nn.sigmoid(mm(o, o + gate_ref.shape[-1])).astype(BF16)


def _projection(x2, sh, sc, g, w_ext, tabs, bsz, s):
    t, d = x2.shape
    tm = 256
    tpb = s // tm
    n_ext = w_ext.shape[1]
    n_gate = 2 * d
    row = lambda i: (i, 0)
    per_b = lambda i: (i // tpb, 0, 0)
    tab = lambda i: (i % tpb, 0)
    hm = lambda i: (i // tpb, 0, i % tpb, 0)
    out_shape = (
        [jax.ShapeDtypeStruct((bsz, s // dil, dil * A_GROUP_COLS), BF16) for _, dil in A_GROUPS]
        + [jax.ShapeDtypeStruct((bsz, B_HEADS // 2, s, PAIR), BF16),
           jax.ShapeDtypeStruct((bsz, B_KV_HEADS, s, PAIR), BF16),
           jax.ShapeDtypeStruct((bsz, B_KV_HEADS, s, PAIR), BF16),
           jax.ShapeDtypeStruct((t, n_gate), BF16),
           jax.ShapeDtypeStruct((len(A_GROUPS), LANE), F32)])
    head_sel = (jnp.arange(2 * A_OUT)[:, None] // HEAD_DIM == jnp.arange(LANE)[None, :]).astype(BF16)
    out_specs = (
        [pl.BlockSpec((None, tm // dil, dil * A_GROUP_COLS), lambda i: (i // tpb, i % tpb, 0))
         for _, dil in A_GROUPS]
        + [pl.BlockSpec((None, B_HEADS // 2, tm, PAIR), hm),
           pl.BlockSpec((None, B_KV_HEADS, tm, PAIR), hm),
           pl.BlockSpec((None, B_KV_HEADS, tm, PAIR), hm),
           pl.BlockSpec((tm, n_gate), row),
           pl.BlockSpec((len(A_GROUPS), LANE), lambda i: (0, 0))])
    return pl.pallas_call(
        _proj_kernel,
        out_shape=out_shape,
        grid=(t // tm,),
        in_specs=[pl.BlockSpec((tm, d), row),
                  pl.BlockSpec((None, 1, d), per_b),
                  pl.BlockSpec((None, 1, d), per_b),
                  pl.BlockSpec((1, d), lambda i: (0, 0)),
                  pl.BlockSpec((d, n_ext), lambda i: (0, 0))]
                 + [pl.BlockSpec((tm, PAIR), tab)] * 4
                 + [pl.BlockSpec((2 * A_OUT, LANE), lambda i: (0, 0))],
        out_specs=out_specs,
        scratch_shapes=[pltpu.VMEM((A_GROUP_COLS // LANE, tm, LANE), F32)],
        compiler_params=_cparams("arbitrary"),
        name="in_proj",
    )(x2, sh, sc, g, w_ext, *tabs, head_sel)


def _dilated_kernel(prev_ref, cur_ref, next_ref, bias_ref, o_ref, lse_ref, *, tq, seq_len, shift):
    for res in range(cur_ref.shape[1] // A_GROUP_COLS):
        c0 = res * A_GROUP_COLS
        part = lambda ref, j: ref[:, c0 + j * A_OUT:c0 + (j + 1) * A_OUT]
        _dilated_subsequence(
            part(cur_ref, 0),
            jnp.concatenate([part(prev_ref, 1), part(cur_ref, 1), part(next_ref, 1)], axis=0),
            jnp.concatenate([part(prev_ref, 2), part(cur_ref, 2), part(next_ref, 2)], axis=0),
            bias_ref, o_ref, lse_ref, res * A_OUT, tq=tq, seq_len=seq_len, shift=shift)


def _dilated_subsequence(q_all, k, v, bias_ref, o_ref, lse_ref, out_col, *, tq, seq_len, shift):
    i = pl.program_id(2)
    nk = DIL_SUB + 2 * N_SIDE
    col = lax.broadcasted_iota(jnp.int32, (DIL_SUB, nk), 1)
    if not shift:
        lane = lax.broadcasted_iota(jnp.int32, (v.shape[0], PAIR), 1)
        first = lane < HEAD_DIM
        ones_col = jnp.where(lane == HEAD_DIM, 1.0, 0.0)
        v_ext = []
        for j in range(A_HEADS // 2):
            vp = v[:, j * PAIR:(j + 1) * PAIR].astype(F32)
            v_ext += [jnp.where(first, vp, ones_col).astype(BF16),
                      jnp.where(first, pltpu.roll(vp, HEAD_DIM, axis=1), ones_col).astype(BF16)]
    for sb in range(tq // DIL_SUB):
        rows = slice(sb * DIL_SUB, (sb + 1) * DIL_SUB)
        q = q_all[rows, :]
        kb = k[sb * DIL_SUB:sb * DIL_SUB + nk]
        vb = v[sb * DIL_SUB:sb * DIL_SUB + nk]
        kpos = i * tq + sb * DIL_SUB - N_SIDE + col
        valid = (kpos >= 0) & (kpos < seq_len)
        for hh in range(A_HEADS):
            sl = slice(hh * HEAD_DIM, (hh + 1) * HEAD_DIM)
            sc = lax.dot_general(q[:, sl], kb[:, sl], (((1,), (1,)), ((), ())),
                                 preferred_element_type=F32)
            sc = sc * (HEAD_DIM ** -0.5) + bias_ref[hh]
            sc = jnp.where(valid, sc, NEG_INF)
            if shift:
                m = jnp.max(sc, axis=-1, keepdims=True)
                p = jnp.exp(sc - m)
                den = jnp.sum(p, axis=-1, keepdims=True)
                o = jnp.dot(p.astype(BF16), vb[:, sl], preferred_element_type=F32) / den
                lse = m + jnp.log(den)
            else:
                acc = jnp.dot(jnp.exp(sc).astype(BF16), v_ext[hh][sb * DIL_SUB:sb * DIL_SUB + nk],
                              preferred_element_type=F32)
                den = acc[:, HEAD_DIM:HEAD_DIM + 1]
                o = acc[:, :HEAD_DIM] / den
                lse = jnp.log(den)
            out_sl = slice(out_col + hh * HEAD_DIM, out_col + (hh + 1) * HEAD_DIM)
            o_ref[rows, out_sl] = o
            lse_ref[rows, out_sl] = jnp.broadcast_to(lse, (DIL_SUB, HEAD_DIM))


def _t5_bucket(rel):
    nb = REL_BUCKETS // 2
    max_exact = nb // 2
    ret = jnp.where(rel > 0, nb, 0)
    n = jnp.abs(rel)
    nf = jnp.maximum(n, 1).astype(F32)
    large = max_exact + (jnp.log(nf / max_exact) / math.log(REL_MAX_DIST / max_exact)
                         * (nb - max_exact)).astype(jnp.int32)
    large = jnp.minimum(large, nb - 1)
    return ret + jnp.where(n < max_exact, n, large)


def _band_bias(rel_bias_g, dilation, tq):
    nk = tq + 2 * N_SIDE
    rel = jnp.arange(nk)[None, :] - N_SIDE - jnp.arange(tq)[:, None]
    bucket = _t5_bucket(rel * dilation)[None]
    bias = jnp.full((A_HEADS,) + rel.shape, NEG_INF, F32)
    for b in range(REL_BUCKETS):
        bias = jnp.where(bucket == b, rel_bias_g[b].astype(F32)[:, None, None], bias)
    return jnp.where((jnp.abs(rel) <= N_SIDE)[None], bias, NEG_INF)


def _dilated_group(za, sq_norms, rel_bias_g, dilation, bsz, s):
    ll = s // dilation
    tq = min(DIL_STEP_ROWS, ll)
    assert ll % tq == 0 and tq % DIL_SUB == 0
    n_res = max(1, min(dilation, DIL_STEP_ROWS // tq))
    assert dilation % n_res == 0
    nhalf = ll // N_SIDE
    per = tq // N_SIDE
    slab = n_res * A_GROUP_COLS
    cur = pl.BlockSpec((None, tq, slab), lambda b, r, i: (b, i, r))
    prv = pl.BlockSpec((None, N_SIDE, slab), lambda b, r, i: (b, jnp.maximum(per * i - 1, 0), r))
    nxt = pl.BlockSpec((None, N_SIDE, slab), lambda b, r, i: (b, jnp.minimum(per * (i + 1), nhalf - 1), r))
    bias = _band_bias(rel_bias_g, dilation, DIL_SUB)
    out_sd = jax.ShapeDtypeStruct((bsz, ll, dilation * A_OUT), F32)
    out_spec = pl.BlockSpec((None, tq, n_res * A_OUT), lambda b, r, i: (b, i, r))

    def attend(shift, zv, bias):
        return pl.pallas_call(
            functools.partial(_dilated_kernel, tq=tq, seq_len=ll, shift=shift),
            out_shape=[out_sd, out_sd],
            grid=(bsz, dilation // n_res, ll // tq),
            in_specs=[prv, cur, nxt, pl.BlockSpec(bias.shape, lambda b, r, i: (0, 0, 0))],
            out_specs=[out_spec, out_spec],
            compiler_params=_cparams("arbitrary", "arbitrary", "arbitrary"),
            name=f"dilated_attn_d{dilation}" + ("_rowmax" if shift else ""),
        )(zv, zv, zv, bias)

    q2 = jnp.max(sq_norms[:A_HEADS])
    k2 = jnp.max(sq_norms[A_HEADS:2 * A_HEADS])
    bound = 1.03 * jnp.sqrt(q2 * k2) * (HEAD_DIM ** -0.5) + jnp.max(jnp.abs(rel_bias_g))
    return lax.cond(bound <= DIL_NO_SHIFT_MAX, functools.partial(attend, False),
                    functools.partial(attend, True), za, bias)


GQA_NO_SHIFT_MAX_LOG2 = 80.0
GQA_KEY_CHUNK = 128


def _stacked_heads(q_ref):
    first = lax.broadcasted_iota(jnp.int32, q_ref.shape[1:], 1) < HEAD_DIM
    blocks = []
    for j in range(q_ref.shape[0]):
        qp = q_ref[j]
        blocks += [jnp.where(first, qp, jnp.zeros_like(qp)), jnp.where(first, jnp.zeros_like(qp), qp)]
    return jnp.concatenate(blocks, axis=0)


def _gqa_kernel_noshift(q_ref, k_ref, v_ref, o_ref):
    tq = q_ref.shape[1]
    q = _stacked_heads(q_ref)
    nt = (((1,), (1,)), ((), ()))
    acc = jnp.zeros((q.shape[0], PAIR), F32)
    for c in range(k_ref.shape[0] // GQA_KEY_CHUNK):
        rows = slice(c * GQA_KEY_CHUNK, (c + 1) * GQA_KEY_CHUNK)
        p = jnp.exp2(lax.dot_general(q, k_ref[rows, :], nt, preferred_element_type=F32))
        acc = acc + jnp.dot(p.astype(BF16), v_ref[rows, :], preferred_element_type=F32)
    o = acc[:, :HEAD_DIM] / acc[:, HEAD_DIM:HEAD_DIM + 1]
    for hh in range(q.shape[0] // tq):
        o_ref[:, hh * HEAD_DIM:(hh + 1) * HEAD_DIM] = o[hh * tq:(hh + 1) * tq].astype(BF16)


def _gqa_kernel_rowmax(q_ref, k_ref, v_ref, o_ref):
    tq = q_ref.shape[1]
    q = _stacked_heads(q_ref)
    k = k_ref[...]
    v = v_ref[:, :HEAD_DIM]
    for hh in range(q.shape[0] // tq):
        sc = lax.dot_general(q[hh * tq:(hh + 1) * tq], k, (((1,), (1,)), ((), ())),
                             preferred_element_type=F32)
        m = jnp.max(sc, axis=-1, keepdims=True)
        p = jnp.exp2(sc - m)
        den = jnp.sum(p, axis=-1, keepdims=True)
        o = jnp.dot(p.astype(BF16), v, preferred_element_type=F32) / den
        o_ref[:, hh * HEAD_DIM:(hh + 1) * HEAD_DIM] = o.astype(BF16)


def _gqa_call(body, name, tq, q, k, v):
    bsz, _, s, _ = q.shape
    return pl.pallas_call(
        body,
        out_shape=jax.ShapeDtypeStruct((bsz, s, B_Q), BF16),
        grid=(bsz, B_KV_HEADS, s // tq),
        in_specs=[pl.BlockSpec((None, B_GRP // 2, tq, PAIR), lambda b, h, i: (b, h, i, 0)),
                  pl.BlockSpec((None, None, s, PAIR), lambda b, h, i: (b, h, 0, 0)),
                  pl.BlockSpec((None, None, s, PAIR), lambda b, h, i: (b, h, 0, 0))],
        out_specs=pl.BlockSpec((None, tq, B_GRP * HEAD_DIM), lambda b, h, i: (b, i, h)),
        compiler_params=_cparams("arbitrary", "arbitrary", "arbitrary"),
        name=name,
    )(q, k, v)


def _gqa_attention(q, k, v, q_gain, k_gain):
    bound = (HEAD_DIM ** 0.5) * LOG2E * 1.01 * jnp.max(jnp.abs(q_gain)) * jnp.max(jnp.abs(k_gain))
    return lax.cond(bound <= GQA_NO_SHIFT_MAX_LOG2,
                    functools.partial(_gqa_call, _gqa_kernel_noshift, "gqa_attn", 1024),
                    functools.partial(_gqa_call, _gqa_kernel_rowmax, "gqa_attn_rowmax", 256),
                    q, k, v)


def _merge_kernel(o0_ref, o1_ref, o2_ref, l0_ref, l1_ref, l2_ref, ob_ref, gate_ref, x_ref, g1_ref,
                  wa_ref, wb_ref, wo_ref, sh_ref, sc_ref, g_ref, whi_ref, wlo_ref, b_ref,
                  xo_ref, h_ref, e_ref, w_ref, pos_ref, cnt_ref, run_ref, *unfold_refs):
    tm, d = x_ref.shape
    scratch = list(unfold_refs)

    def token_order(ref, dil):
        if dil == 1:
            return ref[...]
        buf = scratch.pop()
        for r in range(dil):
            for c in range(A_OUT // LANE):
                col = r * A_OUT + c * LANE
                buf[c, pl.ds(r, tm // dil, stride=dil), :] = ref[:, col:col + LANE]
        return jnp.concatenate([buf[c] for c in range(A_OUT // LANE)], axis=1)

    dils = [dil for _, dil in A_GROUPS]
    o0, o1, o2 = [token_order(r, dl) for r, dl in zip((o0_ref, o1_ref, o2_ref), dils)]
    l0, l1, l2 = [token_order(r, dl) for r, dl in zip((l0_ref, l1_ref, l2_ref), dils)]
    m = jnp.maximum(jnp.maximum(l0, l1), l2)
    e0, e1, e2 = jnp.exp(l0 - m), jnp.exp(l1 - m), jnp.exp(l2 - m)
    tot = e0 + e1 + e2
    oa = ((e0 / tot) * o0 + (e1 / tot) * o1 + (e2 / tot) * o2).astype(BF16)
    xn_rows = []
    for lo in range(0, tm, MERGE_ROWS):
        rows = slice(lo, lo + MERGE_ROWS)
        ya = jnp.dot(oa[rows], wa_ref[...], preferred_element_type=F32)
        yb = jnp.dot(ob_ref[rows, :], wb_ref[...], preferred_element_type=F32)
        merged = gate_ref[rows, :d].astype(F32) * ya + gate_ref[rows, d:].astype(F32) * yb
        y = jnp.dot(merged.astype(BF16), wo_ref[...], preferred_element_type=F32)
        xn_rows.append(x_ref[rows, :] + g1_ref[...] * y)
    xn = jnp.concatenate(xn_rows, axis=0)
    xo_ref[...] = xn
    _route_tile(xn, sh_ref, sc_ref, g_ref, whi_ref, wlo_ref, b_ref, h_ref, e_ref, w_ref, pos_ref, cnt_ref,
                run_ref)


def _merge_and_route(os_, ls_, ob, gate, x2, g1, wa, wb, wo, sh2, sc2, g2n, w_router, b_router, s):
    t, d = x2.shape
    tm = 256
    tpb = s // tm
    ne = w_router.shape[1]
    wt = w_router.T
    whi = wt.astype(BF16)
    wlo = (wt - whi.astype(F32)).astype(BF16)
    row = lambda i: (i, 0)
    const = lambda i: (0, 0)
    per_b = lambda i: (i // tpb, 0, 0)
    by_token = pl.BlockSpec((TOP_K, tm), lambda i: (0, i))
    a_specs = [pl.BlockSpec((None, tm // dil, dil * A_OUT), lambda i: (i // tpb, i % tpb, 0))
               for _, dil in A_GROUPS]
    n_unfold = 2 * sum(1 for _, dil in A_GROUPS if dil > 1)
    return pl.pallas_call(
        _merge_kernel,
        out_shape=[jax.ShapeDtypeStruct((t, d), F32),
                   jax.ShapeDtypeStruct((t, d // 2), jnp.uint32),
                   jax.ShapeDtypeStruct((TOP_K, t), jnp.int32),
                   jax.ShapeDtypeStruct((TOP_K, t), F32),
                   jax.ShapeDtypeStruct((TOP_K, t), jnp.int32),
                   jax.ShapeDtypeStruct((ne, 1), F32)],
        grid=(t // tm,),
        scratch_shapes=[pltpu.VMEM((ne, 1), F32)]
                       + [pltpu.VMEM((A_OUT // LANE, tm, LANE), F32)] * n_unfold,
        in_specs=a_specs * 2
                 + [pl.BlockSpec((tm, B_Q), row),
                    pl.BlockSpec((tm, 2 * d), row),
                    pl.BlockSpec((tm, d), row),
                    pl.BlockSpec((None, 1, d), per_b),
                    pl.BlockSpec(wa.shape, const),
                    pl.BlockSpec(wb.shape, const),
                    pl.BlockSpec(wo.shape, const),
                    pl.BlockSpec((None, 1, d), per_b),
                    pl.BlockSpec((None, 1, d), per_b),
                    pl.BlockSpec((1, d), const),
                    pl.BlockSpec((ne, d), const),
                    pl.BlockSpec((ne, d), const),
                    pl.BlockSpec((ne, 1), const)],
        out_specs=[pl.BlockSpec((tm, d), row),
                   pl.BlockSpec((tm, d // 2), row),
                   by_token, by_token, by_token,
                   pl.BlockSpec((ne, 1), const)],
        compiler_params=_cparams("arbitrary"),
        name="merge_route",
    )(*os_, *ls_, ob, gate, x2, g1, wa, wb, wo, sh2, sc2, g2n, whi, wlo, b_router.reshape(ne, 1))


def _route_tile(x, sh_ref, sc_ref, g_ref, whi_ref, wlo_ref, b_ref,
                h_ref, e_ref, w_ref, pos_ref, cnt_ref, run_ref):
    @pl.when(pl.program_id(0) == 0)
    def _():
        run_ref[...] = jnp.zeros_like(run_ref)

    ms = jnp.mean(x * x, axis=-1, keepdims=True)
    h = x * lax.rsqrt(ms + EPS) * g_ref[...]
    h = h * (1.0 + sc_ref[...]) + sh_ref[...]
    h_ref[...] = _pack_bf16_pairs(h)
    hhi = h.astype(BF16)
    hlo = (h - hhi.astype(F32)).astype(BF16)
    nt = (((1,), (1,)), ((), ()))
    dotf = lambda a, b: lax.dot_general(a, b, nt, preferred_element_type=F32)
    logits = dotf(whi_ref[...], hhi) + (dotf(whi_ref[...], hlo) + dotf(wlo_ref[...], hhi)) + b_ref[...]
    ne, tm = logits.shape
    iota = lax.broadcasted_iota(jnp.int32, (ne, tm), 0).astype(F32)
    vals, idxs = [], []
    cur = logits
    for _ in range(TOP_K):
        m = jnp.max(cur, axis=0, keepdims=True)
        idx = jnp.min(jnp.where(cur == m, iota, float(ne)), axis=0, keepdims=True)
        vals.append(m)
        idxs.append(idx)
        cur = jnp.where(iota == idx, -jnp.inf, cur)
    tv = jnp.concatenate(vals, axis=0)
    ex = jnp.exp(tv - tv[0:1])
    w_ref[...] = ex / jnp.sum(ex, axis=0, keepdims=True)
    e_ref[...] = jnp.concatenate(idxs, axis=0).astype(jnp.int32)
    onehot = jnp.zeros((ne, tm), F32)
    for idx in idxs:
        onehot = onehot + jnp.where(iota == idx, 1.0, 0.0)
    earlier = (lax.broadcasted_iota(jnp.int32, (tm, tm), 0)
               < lax.broadcasted_iota(jnp.int32, (tm, tm), 1))
    rank = jnp.dot(onehot.astype(BF16), jnp.where(earlier, 1.0, 0.0).astype(BF16),
                   preferred_element_type=F32) + run_ref[...]
    pos_ref[...] = jnp.concatenate(
        [jnp.sum(jnp.where(iota == idx, rank, 0.0), axis=0, keepdims=True) for idx in idxs],
        axis=0).astype(jnp.int32)
    run_ref[...] = run_ref[...] + jnp.sum(onehot, axis=1, keepdims=True)
    cnt_ref[...] = run_ref[...]


SC_CORES = 2
SC_SUBCORES = 16
SC_ROWS = 64


def _sc_scatter_rows(rows, dest_kt, n_slots):
    t, d = rows.shape
    n_workers = SC_CORES * SC_SUBCORES
    per_w = t // n_workers
    n_chunks = per_w // SC_ROWS
    assert per_w * n_workers == t and n_chunks * SC_ROWS == per_w and n_chunks % 2 == 0
    idx = dest_kt.reshape(TOP_K, n_workers, n_chunks, SC_ROWS).transpose(1, 2, 0, 3)
    idx = idx.reshape(n_workers * n_chunks * TOP_K, SC_ROWS)
    lists_per_w = n_chunks * TOP_K
    mesh = plsc.VectorSubcoreMesh(core_axis_name="c", subcore_axis_name="s")

    @functools.partial(
        pl.kernel, mesh=mesh,
        out_type=jax.ShapeDtypeStruct((n_slots, d), rows.dtype),
        scratch_types=[pltpu.VMEM((lists_per_w, SC_ROWS), jnp.int32),
                       pltpu.VMEM((SC_ROWS, d), rows.dtype),
                       pltpu.VMEM((SC_ROWS, d), rows.dtype)] + [pltpu.SemaphoreType.DMA] * 4,
        name="sc_scatter_rows",
    )
    def scatter_kernel(rows_hbm, idx_hbm, out_hbm, idx_v, buf_a, buf_b, ld_a, ld_b, st_a, st_b):
        wid = lax.axis_index("s") * SC_CORES + lax.axis_index("c")
        pltpu.sync_copy(idx_hbm.at[pl.ds(wid * lists_per_w, lists_per_w)], idx_v)

        def load(j, buf, sem):
            return pltpu.make_async_copy(rows_hbm.at[pl.ds(wid * per_w + j * SC_ROWS, SC_ROWS)], buf, sem)

        def scatter_all(j, buf, sem):
            copies = [pltpu.make_async_copy(buf, out_hbm.at[idx_v.at[j * TOP_K + kk]], sem)
                      for kk in range(TOP_K)]
            for cp in copies:
                cp.start()
            for cp in copies:
                cp.wait()

        load(0, buf_a, ld_a).start()

        @pl.loop(0, n_chunks, step=2)
        def _(j):
            load(j + 1, buf_b, ld_b).start()
            load(j, buf_a, ld_a).wait()
            scatter_all(j, buf_a, st_a)

            @pl.when(j + 2 < n_chunks)
            def _():
                load(j + 2, buf_a, ld_a).start()
            load(j + 1, buf_b, ld_b).wait()
            scatter_all(j + 1, buf_b, st_b)

    return scatter_kernel(rows, idx)


def _expert_kernel(blk_e_ref, n_used_ref, n_valid_ref, x_ref, wgu_ref, bgu_ref, wdn_ref, bdn_ref, o_ref,
                   wgu_bf, wdn_bf):
    j = pl.program_id(0)
    d = wdn_bf.shape[1]

    @pl.when(j < n_used_ref[0])
    def _():
        @pl.when((j == 0) | (blk_e_ref[j] != blk_e_ref[jnp.maximum(j - 1, 0)]))
        def _():
            wgu_bf[...] = wgu_ref[...].astype(BF16)
            wdn_bf[...] = wdn_ref[...].astype(BF16)

        row = lax.broadcasted_iota(jnp.int32, x_ref.shape, 0)
        packed = jnp.where(row < n_valid_ref[j], x_ref[...], jnp.uint32(0))
        xb = _unpack_bf16_pairs(packed).astype(BF16)
        gu = jnp.dot(xb, wgu_bf[...], preferred_element_type=F32) + bgu_ref[...]
        x_glu = jnp.minimum(gu[:, :d], SWIGLU_LIMIT)
        x_lin = jnp.clip(gu[:, d:], -SWIGLU_LIMIT, SWIGLU_LIMIT)
        act = x_glu * jax.nn.sigmoid(SWIGLU_ALPHA * x_glu) * (x_lin + 1.0)
        o_ref[...] = _pack_bf16_pairs(
            jnp.dot(act.astype(BF16), wdn_bf[...], preferred_element_type=F32) + bdn_ref[...])

    @pl.when(j >= n_used_ref[0])
    def _():
        o_ref[...] = jnp.zeros_like(o_ref)


def _experts(xs, blk_e, n_used, n_valid, wgu, bgu, wdn, bdn, layer):
    d = wdn.shape[-1]
    n_blocks = blk_e.shape[0]
    depth, ne = wgu.shape[:2]
    by_expert = lambda j, be, nu, nv: (layer, be[j], 0, 0)
    grid_spec = pltpu.PrefetchScalarGridSpec(
        num_scalar_prefetch=3,
        grid=(n_blocks,),
        in_specs=[pl.BlockSpec((MOE_BLOCK, d // 2), lambda j, be, nu, nv: (jnp.minimum(j, nu[0] - 1), 0)),
                  pl.BlockSpec((None, None, d, 2 * d), by_expert),
                  pl.BlockSpec((None, None, 1, 2 * d), by_expert),
                  pl.BlockSpec((None, None, d, d), by_expert),
                  pl.BlockSpec((None, None, 1, d), by_expert)],
        out_specs=pl.BlockSpec((MOE_BLOCK, d // 2), lambda j, be, nu, nv: (j, 0)),
        scratch_shapes=[pltpu.VMEM((d, 2 * d), BF16), pltpu.VMEM((d, d), BF16)],
    )
    return pl.pallas_call(
        _expert_kernel,
        out_shape=jax.ShapeDtypeStruct((n_blocks * MOE_BLOCK, d // 2), jnp.uint32),
        grid_spec=grid_spec,
        compiler_params=_cparams("arbitrary"),
        name="moe_experts",
    )(blk_e, n_used, n_valid, xs, wgu, bgu.reshape(depth, ne, 1, 2 * d), wdn, bdn.reshape(depth, ne, 1, d))


def _sc_gather_rows(table, idx):
    n_out = idx.shape[0]
    d = table.shape[1]
    n_workers = SC_CORES * SC_SUBCORES
    per_w = n_out // n_workers
    n_chunks = per_w // SC_ROWS
    assert per_w * n_workers == n_out and n_chunks * SC_ROWS == per_w and n_chunks % 2 == 0
    mesh = plsc.VectorSubcoreMesh(core_axis_name="c", subcore_axis_name="s")

    @functools.partial(
        pl.kernel, mesh=mesh,
        out_type=jax.ShapeDtypeStruct((n_out, d), table.dtype),
        scratch_types=[pltpu.VMEM((per_w,), jnp.int32),
                       pltpu.VMEM((SC_ROWS, d), table.dtype),
                       pltpu.VMEM((SC_ROWS, d), table.dtype),
                       pltpu.SemaphoreType.DMA, pltpu.SemaphoreType.DMA],
        name="sc_gather_rows",
    )
    def gather_kernel(table_hbm, idx_hbm, out_hbm, idx_v, rows_a, rows_b, sem_a, sem_b):
        base = (lax.axis_index("s") * SC_CORES + lax.axis_index("c")) * per_w
        pltpu.sync_copy(idx_hbm.at[pl.ds(base, per_w)], idx_v)

        def gather(j, buf, sem):
            rows = idx_v.at[pl.ds(j * SC_ROWS, SC_ROWS)]
            return pltpu.make_async_copy(table_hbm.at[rows], buf, sem)

        def write_back(j, buf):
            pltpu.sync_copy(buf, out_hbm.at[pl.ds(base + j * SC_ROWS, SC_ROWS)])

        gather(0, rows_a, sem_a).start()

        @pl.loop(0, n_chunks, step=2)
        def _(j):
            gather(j + 1, rows_b, sem_b).start()
            gather(j, rows_a, sem_a).wait()
            write_back(j, rows_a)

            @pl.when(j + 2 < n_chunks)
            def _():
                gather(j + 2, rows_a, sem_a).start()
            gather(j + 1, rows_b, sem_b).wait()
            write_back(j + 1, rows_b)

    return gather_kernel(table, idx)


def _combine_kernel(y0_ref, y1_ref, y2_ref, y3_ref, w_ref, x_ref, g2_ref, fg_ref, xo_ref, *, final):
    w = w_ref[...]
    y = w[:, 0:1] * _unpack_bf16_pairs(y0_ref[...])
    for kk, y_ref in enumerate((y1_ref, y2_ref, y3_ref), start=1):
        y = y + w[:, kk:kk + 1] * _unpack_bf16_pairs(y_ref[...])
    xn = x_ref[...] + g2_ref[...] * y
    if final:
        ms = jnp.mean(xn * xn, axis=-1, keepdims=True)
        xn = xn * lax.rsqrt(ms + EPS) * fg_ref[...]
    xo_ref[...] = xn


def _combine(dest, out_sorted, gates_t, x2, g2, final_g, s, final):
    t, d = x2.shape
    tm = 256
    tpb = s // tm
    sc_unit = 2 * SC_ROWS * SC_CORES * SC_SUBCORES // TOP_K
    chunkable = t % (COMBINE_CHUNKS * sc_unit) == 0 and t % (COMBINE_CHUNKS * tm) == 0
    n_chunks = COMBINE_CHUNKS if chunkable else 1
    tc = t // n_chunks
    nt = tc // tm
    for c in range(n_chunks):
        yg = _sc_gather_rows(out_sorted, dest[:, c * tc:(c + 1) * tc].reshape(TOP_K * tc))
        plane = lambda kk: pl.BlockSpec((tm, d // 2), lambda i: (kk * nt + i, 0))
        tile = lambda i, c=c: (c * nt + i, 0)
        x2 = pl.pallas_call(
            functools.partial(_combine_kernel, final=final),
            out_shape=jax.ShapeDtypeStruct((t, d), F32),
            grid=(nt,),
            in_specs=[plane(0), plane(1), plane(2), plane(3),
                      pl.BlockSpec((tm, TOP_K), tile),
                      pl.BlockSpec((tm, d), tile),
                      pl.BlockSpec((None, 1, d), lambda i, c=c: ((c * nt + i) // tpb, 0, 0)),
                      pl.BlockSpec((1, d), lambda i: (0, 0))],
            out_specs=pl.BlockSpec((tm, d), tile),
            input_output_aliases={5: 0},
            compiler_params=_cparams("arbitrary"),
            name="moe_combine",
        )(yg, yg, yg, yg, gates_t, x2, g2, final_g)
    return x2


def _moe_plan(top_e, pos, counts):
    t = top_e.shape[1]
    sizes = counts[:, 0].astype(jnp.int32)
    padded = (sizes + MOE_BLOCK - 1) // MOE_BLOCK * MOE_BLOCK
    pad_end = jnp.cumsum(padded)
    pad_start = pad_end - padded
    start_of = jnp.zeros_like(top_e)
    for e in range(N_EXPERTS):
        start_of = jnp.where(top_e == e, pad_start[e], start_of)
    dest_kt = start_of + pos
    n_blocks = -(-t * TOP_K // MOE_BLOCK) + N_EXPERTS
    blk_start = jnp.arange(n_blocks, dtype=jnp.int32) * MOE_BLOCK
    blk_e = jnp.minimum(jnp.sum((pad_end[None, :] <= blk_start[:, None]).astype(jnp.int32), axis=1),
                        N_EXPERTS - 1).astype(jnp.int32)
    n_used = (pad_end[-1] // MOE_BLOCK).astype(jnp.int32).reshape(1)
    filled_end = pad_start + sizes
    n_valid = jnp.zeros((n_blocks,), jnp.int32)
    for e in range(N_EXPERTS):
        n_valid = jnp.where(blk_e == e, jnp.clip(filled_end[e] - blk_start, 0, MOE_BLOCK), n_valid)
    return dest_kt, blk_e, n_used, n_valid, n_blocks


def _rope_tables(s, gain):
    n_rows = s // GRID_W
    row = jnp.repeat(jnp.arange(n_rows), GRID_W).astype(F32)
    col = (jnp.arange(s) % GRID_W).astype(F32)
    half = HEAD_DIM // 2
    inv = ROPE_THETA ** (-jnp.arange(0, half, 2, dtype=F32) / half)
    ang_r = row[:, None] * inv
    ang_c = col[:, None] * inv
    cos = jnp.concatenate([jnp.cos(ang_r)] * 2 + [jnp.cos(ang_c)] * 2, axis=-1)
    sin = jnp.concatenate([-jnp.sin(ang_r), jnp.sin(ang_r), -jnp.sin(ang_c), jnp.sin(ang_c)], axis=-1)
    gain = gain.astype(F32)
    return cos * gain[None, :], sin * gain[_PARTNER][None, :]


_q = HEAD_DIM // 4
_PARTNER = np.concatenate([np.arange(_q, 2 * _q), np.arange(0, _q),
                           np.arange(3 * _q, 4 * _q), np.arange(2 * _q, 3 * _q)])


def _partner_cols(w, n_heads):
    idx = (np.arange(n_heads)[:, None] * HEAD_DIM + _PARTNER[None, :]).reshape(-1)
    return w[:, idx]


def kernel(x, c, w_ada, b_ada, norm1_g, w_in, q_norm_g, k_norm_g, rel_bias, w_br_a, w_br_b, w_out,
           norm2_g, w_router, b_router, w_gate_up, b_gate_up, w_down, b_down, final_norm_g):
    bsz, s, d = x.shape
    depth = w_ada.shape[0]
    t = bsz * s
    mod = _ada_mod(c, w_ada, b_ada)
    x2 = x.reshape(t, d)
    q_off = 3 * A_WIDTH
    k_off = q_off + B_Q
    for l in range(depth):
        sh1, sc1, g1, sh2, sc2, g2 = [mod[l, :, j * d:(j + 1) * d].reshape(bsz, 1, d)
                                      for j in range(N_MOD)]
        w = w_in[l]
        wq, wk = w[:, q_off:q_off + B_Q], w[:, k_off:k_off + B_KV]
        w_ext = jnp.concatenate([wq, _partner_cols(wq, B_HEADS), wk, _partner_cols(wk, B_KV_HEADS),
                                 w[:, k_off + B_KV:k_off + 2 * B_KV], w[:, :q_off],
                                 w[:, k_off + 2 * B_KV:]], axis=1).astype(BF16)
        tabs = [jnp.tile(tb, (1, 2)) for tb in
                _rope_tables(s, q_norm_g[l]) + _rope_tables(s, k_norm_g[l])]
        wa, wb, wo = w_br_a[l].astype(BF16), w_br_b[l].astype(BF16), w_out[l].astype(BF16)
        za0, za1, za2, q, k, v, gate, sq_norms = _projection(
            x2, sh1, sc1, norm1_g[l].reshape(1, d), w_ext, tabs, bsz, s)
        attn_a = [_dilated_group(za, sq_norms[g], rel_bias[:, g * A_HEADS:(g + 1) * A_HEADS], dil, bsz, s)
                  for g, (za, (_, dil)) in enumerate(zip((za0, za1, za2), A_GROUPS))]
        attn_b = _gqa_attention(q, k, v, q_norm_g[l], k_norm_g[l]).reshape(t, B_Q)
        x2, h2, top_e, gates, pos, counts = _merge_and_route(
            [o for o, _ in attn_a], [lse for _, lse in attn_a], attn_b, gate, x2, g1, wa, wb, wo,
            sh2, sc2, norm2_g[l].reshape(1, d), w_router[l], b_router[l], s)
        dest, blk_e, n_used, n_valid, n_blocks = _moe_plan(top_e, pos, counts)
        slots = _sc_scatter_rows(h2, dest.reshape(TOP_K * t), n_blocks * MOE_BLOCK)
        outs = _experts(slots, blk_e, n_used, n_valid, w_gate_up, b_gate_up, w_down, b_down, l)
        x2 = _combine(dest, outs, gates.T, x2, g2, final_norm_g.reshape(1, d), s,
                      final=(l == depth - 1))
    return x2.reshape(bsz, s, d)
```

```python
import functools
import math

import numpy as np
import jax
import jax.numpy as jnp
from jax import lax
from jax.experimental import pallas as pl
from jax.experimental.pallas import tpu as pltpu
from jax.experimental.pallas import tpu_sc as plsc

F32 = jnp.float32
BF16 = jnp.bfloat16

HEAD_DIM = 64
LANE = 128
PAIR = 2 * HEAD_DIM
A_GROUPS = ((128, 1), (512, 4), (2048, 16))
A_HEADS = 4
A_WIDTH = len(A_GROUPS) * A_HEADS * HEAD_DIM
A_OUT = A_HEADS * HEAD_DIM
A_GROUP_COLS = 3 * A_OUT
B_HEADS = 16
B_KV_HEADS = 4
B_GRP = B_HEADS // B_KV_HEADS
B_Q = B_HEADS * HEAD_DIM
B_KV = B_KV_HEADS * HEAD_DIM
GRID_W = 64
ROPE_THETA = 10000.0
REL_BUCKETS = 32
REL_MAX_DIST = 1024
N_EXPERTS = 32
TOP_K = 4
SWIGLU_LIMIT = 7.0
SWIGLU_ALPHA = 1.702
MOE_BLOCK = 512
COMBINE_CHUNKS = 4
MERGE_ROWS = 128
N_MOD = 6
EPS = 1e-6
NEG_INF = -1e30
LOG2E = math.log2(math.e)
N_SIDE = 64
DIL_SUB = 2 * N_SIDE
DIL_NO_SHIFT_MAX = 60.0
DIL_STEP_ROWS = 512

VMEM_LIMIT = 56 * 1024 * 1024


def _pack_bf16_pairs(x):
    bits = pltpu.bitcast(x.astype(BF16).astype(F32), jnp.uint32)
    half = x.shape[1] // 2
    return (bits[:, :half] >> 16) | bits[:, half:]


def _unpack_bf16_pairs(u):
    return jnp.concatenate([pltpu.bitcast(u << 16, F32),
                            pltpu.bitcast(u & jnp.uint32(0xFFFF0000), F32)], axis=1)


def _cparams(*sem):
    return pltpu.CompilerParams(dimension_semantics=sem, vmem_limit_bytes=VMEM_LIMIT)


def _ada_kernel(c_ref, w_ref, b_ref, o_ref):
    c = c_ref[...]
    ca = (c * jax.nn.sigmoid(c)).astype(BF16)
    o_ref[...] = jnp.dot(ca, w_ref[...].astype(BF16), preferred_element_type=F32) + b_ref[...]


def _ada_mod(c, w_ada, b_ada):
    depth, d, n = w_ada.shape
    bsz = c.shape[0]
    tn = 1536
    return pl.pallas_call(
        _ada_kernel,
        out_shape=jax.ShapeDtypeStruct((depth, bsz, n), F32),
        grid=(depth, n // tn),
        in_specs=[pl.BlockSpec((bsz, d), lambda l, j: (0, 0)),
                  pl.BlockSpec((None, d, tn), lambda l, j: (l, 0, j)),
                  pl.BlockSpec((None, 1, tn), lambda l, j: (l, 0, j))],
        out_specs=pl.BlockSpec((None, bsz, tn), lambda l, j: (l, 0, j)),
        compiler_params=_cparams("arbitrary", "arbitrary"),
        name="ada_mod",
    )(c, w_ada, b_ada.reshape(depth, 1, n))


def _proj_kernel(x_ref, sh_ref, sc_ref, g_ref, w_ref, aq_ref, bq_ref, ak_ref, bk_ref, hsel_ref,
                 za0_ref, za1_ref, za2_ref, q_ref, k_ref, v_ref, gate_ref, nrm_ref, fold_ref):
    @pl.when(pl.program_id(0) == 0)
    def _():
        nrm_ref[...] = jnp.zeros_like(nrm_ref)

    x = x_ref[...]
    ms = jnp.mean(x * x, axis=-1, keepdims=True)
    h = x * lax.rsqrt(ms + EPS) * g_ref[...]
    h = h * (1.0 + sc_ref[...]) + sh_ref[...]
    hb = h.astype(BF16)

    def mm(lo, hi):
        return jnp.dot(hb, w_ref[:, lo:hi], preferred_element_type=F32)

    tm = x.shape[0]
    o = 0
    zq = mm(o, o + B_Q); o += B_Q
    zqp = mm(o, o + B_Q); o += B_Q
    zk = mm(o, o + B_KV); o += B_KV
    zkp = mm(o, o + B_KV); o += B_KV
    zv = mm(o, o + B_KV); o += B_KV
    lane = lax.broadcasted_iota(jnp.int32, (tm, PAIR), 1)
    first = lane < HEAD_DIM

    def pair_norm_rope(z, zp, a, b):
        zz = z * z
        ss = jnp.where(first,
                       jnp.sum(jnp.where(first, zz, 0.0), axis=-1, keepdims=True),
                       jnp.sum(jnp.where(first, 0.0, zz), axis=-1, keepdims=True))
        return lax.rsqrt(ss * (1.0 / HEAD_DIM) + EPS) * (z * a + zp * b)

    aq, bq, ak, bk = aq_ref[...], bq_ref[...], ak_ref[...], bk_ref[...]
    for j in range(B_HEADS // 2):
        cols = slice(j * PAIR, (j + 1) * PAIR)
        q_ref[j] = (pair_norm_rope(zq[:, cols], zqp[:, cols], aq, bq)
                    * (LOG2E * HEAD_DIM ** -0.5)).astype(BF16)
    ones_col = jnp.where(lane == HEAD_DIM, 1.0, 0.0)
    for j in range(B_KV_HEADS // 2):
        cols = slice(j * PAIR, (j + 1) * PAIR)
        kk = pair_norm_rope(zk[:, cols], zkp[:, cols], ak, bk)
        kk_sw = pltpu.roll(kk, HEAD_DIM, axis=1)
        k_ref[2 * j] = jnp.where(first, kk, kk_sw).astype(BF16)
        k_ref[2 * j + 1] = jnp.where(first, kk_sw, kk).astype(BF16)
        vv = zv[:, cols]
        v_ref[2 * j] = jnp.where(first, vv, ones_col).astype(BF16)
        v_ref[2 * j + 1] = jnp.where(first, pltpu.roll(vv, HEAD_DIM, axis=1), ones_col).astype(BF16)
    for g, (za_ref, (_, dil)) in enumerate(zip((za0_ref, za1_ref, za2_ref), A_GROUPS)):
        z = mm(o, o + A_GROUP_COLS)
        o += A_GROUP_COLS
        zqk = z[:, :2 * A_OUT]
        sq = jnp.dot((zqk * zqk).astype(BF16), hsel_ref[...], preferred_element_type=F32)
        nrm_ref[g:g + 1, :] = jnp.maximum(nrm_ref[g:g + 1, :], jnp.max(sq, axis=0, keepdims=True))
        if dil == 1:
            za_ref[...] = z.astype(BF16)
        else:
            for c in range(A_GROUP_COLS // LANE):
                fold_ref[c] = z[:, c * LANE:(c + 1) * LANE]
            for r in range(dil):
                for c in range(A_GROUP_COLS // LANE):
                    col = r * A_GROUP_COLS + c * LANE
                    za_ref[:, col:col + LANE] = (
                        fold_ref[c, pl.ds(r, tm // dil, stride=dil), :].astype(BF16))
    gate_ref[...] = jax.nn.sigmoid(mm(o, o + gate_ref.shape[-1])).astype(BF16)


def _projection(x2, sh, sc, g, w_ext, tabs, bsz, s):
    t, d = x2.shape
    tm = 256
    tpb = s // tm
    n_ext = w_ext.shape[1]
    n_gate = 2 * d
    row = lambda i: (i, 0)
    per_b = lambda i: (i // tpb, 0, 0)
    tab = lambda i: (i % tpb, 0)
    hm = lambda i: (i // tpb, 0, i % tpb, 0)
    out_shape = (
        [jax.ShapeDtypeStruct((bsz, s // dil, dil * A_GROUP_COLS), BF16) for _, dil in A_GROUPS]
        + [jax.ShapeDtypeStruct((bsz, B_HEADS // 2, s, PAIR), BF16),
           jax.ShapeDtypeStruct((bsz, B_KV_HEADS, s, PAIR), BF16),
           jax.ShapeDtypeStruct((bsz, B_KV_HEADS, s, PAIR), BF16),
           jax.ShapeDtypeStruct((t, n_gate), BF16),
           jax.ShapeDtypeStruct((len(A_GROUPS), LANE), F32)])
    head_sel = (jnp.arange(2 * A_OUT)[:, None] // HEAD_DIM == jnp.arange(LANE)[None, :]).astype(BF16)
    out_specs = (
        [pl.BlockSpec((None, tm // dil, dil * A_GROUP_COLS), lambda i: (i // tpb, i % tpb, 0))
         for _, dil in A_GROUPS]
        + [pl.BlockSpec((None, B_HEADS // 2, tm, PAIR), hm),
           pl.BlockSpec((None, B_KV_HEADS, tm, PAIR), hm),
           pl.BlockSpec((None, B_KV_HEADS, tm, PAIR), hm),
           pl.BlockSpec((tm, n_gate), row),
           pl.BlockSpec((len(A_GROUPS), LANE), lambda i: (0, 0))])
    return pl.pallas_call(
        _proj_kernel,
        out_shape=out_shape,
        grid=(t // tm,),
        in_specs=[pl.BlockSpec((tm, d), row),
                  pl.BlockSpec((None, 1, d), per_b),
                  pl.BlockSpec((None, 1, d), per_b),
                  pl.BlockSpec((1, d), lambda i: (0, 0)),
                  pl.BlockSpec((d, n_ext), lambda i: (0, 0))]
                 + [pl.BlockSpec((tm, PAIR), tab)] * 4
                 + [pl.BlockSpec((2 * A_OUT, LANE), lambda i: (0, 0))],
        out_specs=out_specs,
        scratch_shapes=[pltpu.VMEM((A_GROUP_COLS // LANE, tm, LANE), F32)],
        compiler_params=_cparams("arbitrary"),
        name="in_proj",
    )(x2, sh, sc, g, w_ext, *tabs, head_sel)


def _dilated_kernel(prev_ref, cur_ref, next_ref, bias_ref, o_ref, lse_ref, *, tq, seq_len, shift):
    for res in range(cur_ref.shape[1] // A_GROUP_COLS):
        c0 = res * A_GROUP_COLS
        part = lambda ref, j: ref[:, c0 + j * A_OUT:c0 + (j + 1) * A_OUT]
        _dilated_subsequence(
            part(cur_ref, 0),
            jnp.concatenate([part(prev_ref, 1), part(cur_ref, 1), part(next_ref, 1)], axis=0),
            jnp.concatenate([part(prev_ref, 2), part(cur_ref, 2), part(next_ref, 2)], axis=0),
            bias_ref, o_ref, lse_ref, res * A_OUT, tq=tq, seq_len=seq_len, shift=shift)


def _dilated_subsequence(q_all, k, v, bias_ref, o_ref, lse_ref, out_col, *, tq, seq_len, shift):
    i = pl.program_id(2)
    nk = DIL_SUB + 2 * N_SIDE
    col = lax.broadcasted_iota(jnp.int32, (DIL_SUB, nk), 1)
    if not shift:
        lane = lax.broadcasted_iota(jnp.int32, (v.shape[0], PAIR), 1)
        first = lane < HEAD_DIM
        ones_col = jnp.where(lane == HEAD_DIM, 1.0, 0.0)
        v_ext = []
        for j in range(A_HEADS // 2):
            vp = v[:, j * PAIR:(j + 1) * PAIR].astype(F32)
            v_ext += [jnp.where(first, vp, ones_col).astype(BF16),
                      jnp.where(first, pltpu.roll(vp, HEAD_DIM, axis=1), ones_col).astype(BF16)]
    for sb in range(tq // DIL_SUB):
        rows = slice(sb * DIL_SUB, (sb + 1) * DIL_SUB)
        q = q_all[rows, :]
        kb = k[sb * DIL_SUB:sb * DIL_SUB + nk]
        vb = v[sb * DIL_SUB:sb * DIL_SUB + nk]
        kpos = i * tq + sb * DIL_SUB - N_SIDE + col
        valid = (kpos >= 0) & (kpos < seq_len)
        for hh in range(A_HEADS):
            sl = slice(hh * HEAD_DIM, (hh + 1) * HEAD_DIM)
            sc = lax.dot_general(q[:, sl], kb[:, sl], (((1,), (1,)), ((), ())),
                                 preferred_element_type=F32)
            sc = sc * (HEAD_DIM ** -0.5) + bias_ref[hh]
            sc = jnp.where(valid, sc, NEG_INF)
            if shift:
                m = jnp.max(sc, axis=-1, keepdims=True)
                p = jnp.exp(sc - m)
                den = jnp.sum(p, axis=-1, keepdims=True)
                o = jnp.dot(p.astype(BF16), vb[:, sl], preferred_element_type=F32) / den
                lse = m + jnp.log(den)
            else:
                acc = jnp.dot(jnp.exp(sc).astype(BF16), v_ext[hh][sb * DIL_SUB:sb * DIL_SUB + nk],
                              preferred_element_type=F32)
                den = acc[:, HEAD_DIM:HEAD_DIM + 1]
                o = acc[:, :HEAD_DIM] / den
                lse = jnp.log(den)
            out_sl = slice(out_col + hh * HEAD_DIM, out_col + (hh + 1) * HEAD_DIM)
            o_ref[rows, out_sl] = o
            lse_ref[rows, out_sl] = jnp.broadcast_to(lse, (DIL_SUB, HEAD_DIM))


def _t5_bucket(rel):
    nb = REL_BUCKETS // 2
    max_exact = nb // 2
    ret = jnp.where(rel > 0, nb, 0)
    n = jnp.abs(rel)
    nf = jnp.maximum(n, 1).astype(F32)
    large = max_exact + (jnp.log(nf / max_exact) / math.log(REL_MAX_DIST / max_exact)
                         * (nb - max_exact)).astype(jnp.int32)
    large = jnp.minimum(large, nb - 1)
    return ret + jnp.where(n < max_exact, n, large)


def _band_bias(rel_bias_g, dilation, tq):
    nk = tq + 2 * N_SIDE
    rel = jnp.arange(nk)[None, :] - N_SIDE - jnp.arange(tq)[:, None]
    bucket = _t5_bucket(rel * dilation)[None]
    bias = jnp.full((A_HEADS,) + rel.shape, NEG_INF, F32)
    for b in range(REL_BUCKETS):
        bias = jnp.where(bucket == b, rel_bias_g[b].astype(F32)[:, None, None], bias)
    return jnp.where((jnp.abs(rel) <= N_SIDE)[None], bias, NEG_INF)


def _dilated_group(za, sq_norms, rel_bias_g, dilation, bsz, s):
    ll = s // dilation
    tq = min(DIL_STEP_ROWS, ll)
    assert ll % tq == 0 and tq % DIL_SUB == 0
    n_res = max(1, min(dilation, DIL_STEP_ROWS // tq))
    assert dilation % n_res == 0
    nhalf = ll // N_SIDE
    per = tq // N_SIDE
    slab = n_res * A_GROUP_COLS
    cur = pl.BlockSpec((None, tq, slab), lambda b, r, i: (b, i, r))
    prv = pl.BlockSpec((None, N_SIDE, slab), lambda b, r, i: (b, jnp.maximum(per * i - 1, 0), r))
    nxt = pl.BlockSpec((None, N_SIDE, slab), lambda b, r, i: (b, jnp.minimum(per * (i + 1), nhalf - 1), r))
    bias = _band_bias(rel_bias_g, dilation, DIL_SUB)
    out_sd = jax.ShapeDtypeStruct((bsz, ll, dilation * A_OUT), F32)
    out_spec = pl.BlockSpec((None, tq, n_res * A_OUT), lambda b, r, i: (b, i, r))

    def attend(shift, zv, bias):
        return pl.pallas_call(
            functools.partial(_dilated_kernel, tq=tq, seq_len=ll, shift=shift),
            out_shape=[out_sd, out_sd],
            grid=(bsz, dilation // n_res, ll // tq),
            in_specs=[prv, cur, nxt, pl.BlockSpec(bias.shape, lambda b, r, i: (0, 0, 0))],
            out_specs=[out_spec, out_spec],
            compiler_params=_cparams("arbitrary", "arbitrary", "arbitrary"),
            name=f"dilated_attn_d{dilation}" + ("_rowmax" if shift else ""),
        )(zv, zv, zv, bias)

    q2 = jnp.max(sq_norms[:A_HEADS])
    k2 = jnp.max(sq_norms[A_HEADS:2 * A_HEADS])
    bound = 1.03 * jnp.sqrt(q2 * k2) * (HEAD_DIM ** -0.5) + jnp.max(jnp.abs(rel_bias_g))
    return lax.cond(bound <= DIL_NO_SHIFT_MAX, functools.partial(attend, False),
                    functools.partial(attend, True), za, bias)


GQA_NO_SHIFT_MAX_LOG2 = 80.0
GQA_KEY_CHUNK = 128


def _stacked_heads(q_ref):
    first = lax.broadcasted_iota(jnp.int32, q_ref.shape[1:], 1) < HEAD_DIM
    blocks = []
    for j in range(q_ref.shape[0]):
        qp = q_ref[j]
        blocks += [jnp.where(first, qp, jnp.zeros_like(qp)), jnp.where(first, jnp.zeros_like(qp), qp)]
    return jnp.concatenate(blocks, axis=0)


def _gqa_kernel_noshift(q_ref, k_ref, v_ref, o_ref):
    tq = q_ref.shape[1]
    q = _stacked_heads(q_ref)
    nt = (((1,), (1,)), ((), ()))
    acc = jnp.zeros((q.shape[0], PAIR), F32)
    for c in range(k_ref.shape[0] // GQA_KEY_CHUNK):
        rows = slice(c * GQA_KEY_CHUNK, (c + 1) * GQA_KEY_CHUNK)
        p = jnp.exp2(lax.dot_general(q, k_ref[rows, :], nt, preferred_element_type=F32))
        acc = acc + jnp.dot(p.astype(BF16), v_ref[rows, :], preferred_element_type=F32)
    o = acc[:, :HEAD_DIM] / acc[:, HEAD_DIM:HEAD_DIM + 1]
    for hh in range(q.shape[0] // tq):
        o_ref[:, hh * HEAD_DIM:(hh + 1) * HEAD_DIM] = o[hh * tq:(hh + 1) * tq].astype(BF16)


def _gqa_kernel_rowmax(q_ref, k_ref, v_ref, o_ref):
    tq = q_ref.shape[1]
    q = _stacked_heads(q_ref)
    k = k_ref[...]
    v = v_ref[:, :HEAD_DIM]
    for hh in range(q.shape[0] // tq):
        sc = lax.dot_general(q[hh * tq:(hh + 1) * tq], k, (((1,), (1,)), ((), ())),
                             preferred_element_type=F32)
        m = jnp.max(sc, axis=-1, keepdims=True)
        p = jnp.exp2(sc - m)
        den = jnp.sum(p, axis=-1, keepdims=True)
        o = jnp.dot(p.astype(BF16), v, preferred_element_type=F32) / den
        o_ref[:, hh * HEAD_DIM:(hh + 1) * HEAD_DIM] = o.astype(BF16)


def _gqa_call(body, name, tq, q, k, v):
    bsz, _, s, _ = q.shape
    return pl.pallas_call(
        body,
        out_shape=jax.ShapeDtypeStruct((bsz, s, B_Q), BF16),
        grid=(bsz, B_KV_HEADS, s // tq),
        in_specs=[pl.BlockSpec((None, B_GRP // 2, tq, PAIR), lambda b, h, i: (b, h, i, 0)),
                  pl.BlockSpec((None, None, s, PAIR), lambda b, h, i: (b, h, 0, 0)),
                  pl.BlockSpec((None, None, s, PAIR), lambda b, h, i: (b, h, 0, 0))],
        out_specs=pl.BlockSpec((None, tq, B_GRP * HEAD_DIM), lambda b, h, i: (b, i, h)),
        compiler_params=_cparams("arbitrary", "arbitrary", "arbitrary"),
        name=name,
    )(q, k, v)


def _gqa_attention(q, k, v, q_gain, k_gain):
    bound = (HEAD_DIM ** 0.5) * LOG2E * 1.01 * jnp.max(jnp.abs(q_gain)) * jnp.max(jnp.abs(k_gain))
    return lax.cond(bound <= GQA_NO_SHIFT_MAX_LOG2,
                    functools.partial(_gqa_call, _gqa_kernel_noshift, "gqa_attn", 1024),
                    functools.partial(_gqa_call, _gqa_kernel_rowmax, "gqa_attn_rowmax", 256),
                    q, k, v)


def _merge_kernel(o0_ref, o1_ref, o2_ref, l0_ref, l1_ref, l2_ref, ob_ref, gate_ref, x_ref, g1_ref,
                  wa_ref, wb_ref, wo_ref, sh_ref, sc_ref, g_ref, whi_ref, wlo_ref, b_ref,
                  xo_ref, h_ref, e_ref, w_ref, pos_ref, cnt_ref, run_ref, *unfold_refs):
    @pl.when(pl.program_id(0) == 0)
    def _():
        run_ref[...] = jnp.zeros_like(run_ref)

    tm, d = x_ref.shape
    scratch = list(unfold_refs)

    def token_order(ref, dil):
        if dil == 1:
            return ref[...]
        buf = scratch.pop()
        for r in range(dil):
            for c in range(A_OUT // LANE):
                col = r * A_OUT + c * LANE
                buf[c, pl.ds(r, tm // dil, stride=dil), :] = ref[:, col:col + LANE]
        return jnp.concatenate([buf[c] for c in range(A_OUT // LANE)], axis=1)

    dils = [dil for _, dil in A_GROUPS]
    o0, o1, o2 = [token_order(r, dl) for r, dl in zip((o0_ref, o1_ref, o2_ref), dils)]
    l0, l1, l2 = [token_order(r, dl) for r, dl in zip((l0_ref, l1_ref, l2_ref), dils)]
    m = jnp.maximum(jnp.maximum(l0, l1), l2)
    e0, e1, e2 = jnp.exp(l0 - m), jnp.exp(l1 - m), jnp.exp(l2 - m)
    tot = e0 + e1 + e2
    oa = ((e0 / tot) * o0 + (e1 / tot) * o1 + (e2 / tot) * o2).astype(BF16)
    xn_rows = []
    for lo in range(0, tm, MERGE_ROWS):
        rows = slice(lo, lo + MERGE_ROWS)
        ya = jnp.dot(oa[rows], wa_ref[...], preferred_element_type=F32)
        yb = jnp.dot(ob_ref[rows, :], wb_ref[...], preferred_element_type=F32)
        merged = gate_ref[rows, :d].astype(F32) * ya + gate_ref[rows, d:].astype(F32) * yb
        y = jnp.dot(merged.astype(BF16), wo_ref[...], preferred_element_type=F32)
        xn_rows.append(x_ref[rows, :] + g1_ref[...] * y)
    xn = jnp.concatenate(xn_rows, axis=0)
    xo_ref[...] = xn
    _route_tile(xn, sh_ref, sc_ref, g_ref, whi_ref, wlo_ref, b_ref, h_ref, e_ref, w_ref, pos_ref, cnt_ref,
                run_ref)


def _merge_and_route(os_, ls_, ob, gate, x2, g1, wa, wb, wo, sh2, sc2, g2n, w_router, b_router, s):
    t, d = x2.shape
    tm = 256
    tpb = s // tm
    ne = w_router.shape[1]
    wt = w_router.T
    whi = wt.astype(BF16)
    wlo = (wt - whi.astype(F32)).astype(BF16)
    row = lambda i: (i, 0)
    const = lambda i: (0, 0)
    per_b = lambda i: (i // tpb, 0, 0)
    by_token = pl.BlockSpec((TOP_K, tm), lambda i: (0, i))
    a_specs = [pl.BlockSpec((None, tm // dil, dil * A_OUT), lambda i: (i // tpb, i % tpb, 0))
               for _, dil in A_GROUPS]
    n_unfold = 2 * sum(1 for _, dil in A_GROUPS if dil > 1)
    return pl.pallas_call(
        _merge_kernel,
        out_shape=[jax.ShapeDtypeStruct((t, d), F32),
                   jax.ShapeDtypeStruct((t, d // 2), jnp.uint32),
                   jax.ShapeDtypeStruct((TOP_K, t), jnp.int32),
                   jax.ShapeDtypeStruct((TOP_K, t), F32),
                   jax.ShapeDtypeStruct((TOP_K, t), jnp.int32),
                   jax.ShapeDtypeStruct((ne, 1), F32)],
        grid=(t // tm,),
        scratch_shapes=[pltpu.VMEM((ne, 1), F32)]
                       + [pltpu.VMEM((A_OUT // LANE, tm, LANE), F32)] * n_unfold,
        in_specs=a_specs * 2
                 + [pl.BlockSpec((tm, B_Q), row),
                    pl.BlockSpec((tm, 2 * d), row),
                    pl.BlockSpec((tm, d), row),
                    pl.BlockSpec((None, 1, d), per_b),
                    pl.BlockSpec(wa.shape, const),
                    pl.BlockSpec(wb.shape, const),
                    pl.BlockSpec(wo.shape, const),
                    pl.BlockSpec((None, 1, d), per_b),
                    pl.BlockSpec((None, 1, d), per_b),
                    pl.BlockSpec((1, d), const),
                    pl.BlockSpec((ne, d), const),
                    pl.BlockSpec((ne, d), const),
                    pl.BlockSpec((ne, 1), const)],
        out_specs=[pl.BlockSpec((tm, d), row),
                   pl.BlockSpec((tm, d // 2), row),
                   by_token, by_token, by_token,
                   pl.BlockSpec((ne, 1), const)],
        compiler_params=_cparams("arbitrary"),
        name="merge_route",
    )(*os_, *ls_, ob, gate, x2, g1, wa, wb, wo, sh2, sc2, g2n, whi, wlo, b_router.reshape(ne, 1))


def _route_tile(x, sh_ref, sc_ref, g_ref, whi_ref, wlo_ref, b_ref,
                h_ref, e_ref, w_ref, pos_ref, cnt_ref, run_ref):
    ms = jnp.mean(x * x, axis=-1, keepdims=True)
    h = x * lax.rsqrt(ms + EPS) * g_ref[...]
    h = h * (1.0 + sc_ref[...]) + sh_ref[...]
    h_ref[...] = _pack_bf16_pairs(h)
    hhi = h.astype(BF16)
    hlo = (h - hhi.astype(F32)).astype(BF16)
    nt = (((1,), (1,)), ((), ()))
    dotf = lambda a, b: lax.dot_general(a, b, nt, preferred_element_type=F32)
    logits = dotf(whi_ref[...], hhi) + (dotf(whi_ref[...], hlo) + dotf(wlo_ref[...], hhi)) + b_ref[...]
    ne, tm = logits.shape
    iota = lax.broadcasted_iota(jnp.int32, (ne, tm), 0).astype(F32)
    vals, idxs = [], []
    cur = logits
    for _ in range(TOP_K):
        m = jnp.max(cur, axis=0, keepdims=True)
        idx = jnp.min(jnp.where(cur == m, iota, float(ne)), axis=0, keepdims=True)
        vals.append(m)
        idxs.append(idx)
        cur = jnp.where(iota == idx, -jnp.inf, cur)
    tv = jnp.concatenate(vals, axis=0)
    ex = jnp.exp(tv - tv[0:1])
    w_ref[...] = ex / jnp.sum(ex, axis=0, keepdims=True)
    e_ref[...] = jnp.concatenate(idxs, axis=0).astype(jnp.int32)
    onehot = jnp.zeros((ne, tm), F32)
    for idx in idxs:
        onehot = onehot + jnp.where(iota == idx, 1.0, 0.0)
    earlier = (lax.broadcasted_iota(jnp.int32, (tm, tm), 0)
               < lax.broadcasted_iota(jnp.int32, (tm, tm), 1))
    rank = jnp.dot(onehot.astype(BF16), jnp.where(earlier, 1.0, 0.0).astype(BF16),
                   preferred_element_type=F32) + run_ref[...]
    pos_ref[...] = jnp.concatenate(
        [jnp.sum(jnp.where(iota == idx, rank, 0.0), axis=0, keepdims=True) for idx in idxs],
        axis=0).astype(jnp.int32)
    run_ref[...] = run_ref[...] + jnp.sum(onehot, axis=1, keepdims=True)
    cnt_ref[...] = run_ref[...]


SC_CORES = 2
SC_SUBCORES = 16
SC_ROWS = 64


def _sc_scatter_rows(rows, dest_kt, n_slots):
    t, d = rows.shape
    n_workers = SC_CORES * SC_SUBCORES
    per_w = t // n_workers
    n_chunks = per_w // SC_ROWS
    assert per_w * n_workers == t and n_chunks * SC_ROWS == per_w and n_chunks % 2 == 0
    idx = dest_kt.reshape(TOP_K, n_workers, n_chunks, SC_ROWS).transpose(1, 2, 0, 3)
    idx = idx.reshape(n_workers * n_chunks * TOP_K, SC_ROWS)
    lists_per_w = n_chunks * TOP_K
    mesh = plsc.VectorSubcoreMesh(core_axis_name="c", subcore_axis_name="s")

    @functools.partial(
        pl.kernel, mesh=mesh,
        out_type=jax.ShapeDtypeStruct((n_slots, d), rows.dtype),
        scratch_types=[pltpu.VMEM((lists_per_w, SC_ROWS), jnp.int32),
                       pltpu.VMEM((SC_ROWS, d), rows.dtype),
                       pltpu.VMEM((SC_ROWS, d), rows.dtype)] + [pltpu.SemaphoreType.DMA] * 4,
        name="sc_scatter_rows",
    )
    def scatter_kernel(rows_hbm, idx_hbm, out_hbm, idx_v, buf_a, buf_b, ld_a, ld_b, st_a, st_b):
        wid = lax.axis_index("s") * SC_CORES + lax.axis_index("c")
        pltpu.sync_copy(idx_hbm.at[pl.ds(wid * lists_per_w, lists_per_w)], idx_v)

        def load(j, buf, sem):
            return pltpu.make_async_copy(rows_hbm.at[pl.ds(wid * per_w + j * SC_ROWS, SC_ROWS)], buf, sem)

        def scatter_all(j, buf, sem):
            copies = [pltpu.make_async_copy(buf, out_hbm.at[idx_v.at[j * TOP_K + kk]], sem)
                      for kk in range(TOP_K)]
            for cp in copies:
                cp.start()
            for cp in copies:
                cp.wait()

        load(0, buf_a, ld_a).start()

        @pl.loop(0, n_chunks, step=2)
        def _(j):
            load(j + 1, buf_b, ld_b).start()
            load(j, buf_a, ld_a).wait()
            scatter_all(j, buf_a, st_a)

            @pl.when(j + 2 < n_chunks)
            def _():
                load(j + 2, buf_a, ld_a).start()
            load(j + 1, buf_b, ld_b).wait()
            scatter_all(j + 1, buf_b, st_b)

    return scatter_kernel(rows, idx)


def _expert_kernel(blk_e_ref, n_used_ref, n_valid_ref, x_ref, wgu_ref, bgu_ref, wdn_ref, bdn_ref, o_ref,
                   wgu_bf, wdn_bf):
    j = pl.program_id(0)
    d = wdn_bf.shape[1]

    @pl.when(j < n_used_ref[0])
    def _():
        @pl.when((j == 0) | (blk_e_ref[j] != blk_e_ref[jnp.maximum(j - 1, 0)]))
        def _():
            wgu_bf[...] = wgu_ref[...].astype(BF16)
            wdn_bf[...] = wdn_ref[...].astype(BF16)

        row = lax.broadcasted_iota(jnp.int32, x_ref.shape, 0)
        packed = jnp.where(row < n_valid_ref[j], x_ref[...], jnp.uint32(0))
        xb = _unpack_bf16_pairs(packed).astype(BF16)
        gu = jnp.dot(xb, wgu_bf[...], preferred_element_type=F32) + bgu_ref[...]
        x_glu = jnp.minimum(gu[:, :d], SWIGLU_LIMIT)
        x_lin = jnp.clip(gu[:, d:], -SWIGLU_LIMIT, SWIGLU_LIMIT)
        act = x_glu * jax.nn.sigmoid(SWIGLU_ALPHA * x_glu) * (x_lin + 1.0)
        o_ref[...] = _pack_bf16_pairs(
            jnp.dot(act.astype(BF16), wdn_bf[...], preferred_element_type=F32) + bdn_ref[...])

    @pl.when(j >= n_used_ref[0])
    def _():
        o_ref[...] = jnp.zeros_like(o_ref)


def _experts(xs, blk_e, n_used, n_valid, wgu, bgu, wdn, bdn, layer):
    d = wdn.shape[-1]
    n_blocks = blk_e.shape[0]
    depth, ne = wgu.shape[:2]
    by_expert = lambda j, be, nu, nv: (layer, be[j], 0, 0)
    grid_spec = pltpu.PrefetchScalarGridSpec(
        num_scalar_prefetch=3,
        grid=(n_blocks,),
        in_specs=[pl.BlockSpec((MOE_BLOCK, d // 2), lambda j, be, nu, nv: (jnp.minimum(j, nu[0] - 1), 0)),
                  pl.BlockSpec((None, None, d, 2 * d), by_expert),
                  pl.BlockSpec((None, None, 1, 2 * d), by_expert),
                  pl.BlockSpec((None, None, d, d), by_expert),
                  pl.BlockSpec((None, None, 1, d), by_expert)],
        out_specs=pl.BlockSpec((MOE_BLOCK, d // 2), lambda j, be, nu, nv: (j, 0)),
        scratch_shapes=[pltpu.VMEM((d, 2 * d), BF16), pltpu.VMEM((d, d), BF16)],
    )
    return pl.pallas_call(
        _expert_kernel,
        out_shape=jax.ShapeDtypeStruct((n_blocks * MOE_BLOCK, d // 2), jnp.uint32),
        grid_spec=grid_spec,
        compiler_params=_cparams("arbitrary"),
        name="moe_experts",
    )(blk_e, n_used, n_valid, xs, wgu, bgu.reshape(depth, ne, 1, 2 * d), wdn, bdn.reshape(depth, ne, 1, d))


def _sc_gather_rows(table, idx):
    n_out = idx.shape[0]
    d = table.shape[1]
    n_workers = SC_CORES * SC_SUBCORES
    per_w = n_out // n_workers
    n_chunks = per_w // SC_ROWS
    assert per_w * n_workers == n_out and n_chunks * SC_ROWS == per_w and n_chunks % 2 == 0
    mesh = plsc.VectorSubcoreMesh(core_axis_name="c", subcore_axis_name="s")

    @functools.partial(
        pl.kernel, mesh=mesh,
        out_type=jax.ShapeDtypeStruct((n_out, d), table.dtype),
        scratch_types=[pltpu.VMEM((per_w,), jnp.int32),
                       pltpu.VMEM((SC_ROWS, d), table.dtype),
                       pltpu.VMEM((SC_ROWS, d), table.dtype),
                       pltpu.SemaphoreType.DMA, pltpu.SemaphoreType.DMA],
        name="sc_gather_rows",
    )
    def gather_kernel(table_hbm, idx_hbm, out_hbm, idx_v, rows_a, rows_b, sem_a, sem_b):
        base = (lax.axis_index("s") * SC_CORES + lax.axis_index("c")) * per_w
        pltpu.sync_copy(idx_hbm.at[pl.ds(base, per_w)], idx_v)

        def gather(j, buf, sem):
            rows = idx_v.at[pl.ds(j * SC_ROWS, SC_ROWS)]
            return pltpu.make_async_copy(table_hbm.at[rows], buf, sem)

        def write_back(j, buf):
            pltpu.sync_copy(buf, out_hbm.at[pl.ds(base + j * SC_ROWS, SC_ROWS)])

        gather(0, rows_a, sem_a).start()

        @pl.loop(0, n_chunks, step=2)
        def _(j):
            gather(j + 1, rows_b, sem_b).start()
            gather(j, rows_a, sem_a).wait()
            write_back(j, rows_a)

            @pl.when(j + 2 < n_chunks)
            def _():
                gather(j + 2, rows_a, sem_a).start()
            gather(j + 1, rows_b, sem_b).wait()
            write_back(j + 1, rows_b)

    return gather_kernel(table, idx)


def _combine_kernel(y0_ref, y1_ref, y2_ref, y3_ref, w_ref, x_ref, g2_ref, fg_ref, xo_ref, *, final):
    w = w_ref[...]
    y = w[:, 0:1] * _unpack_bf16_pairs(y0_ref[...])
    for kk, y_ref in enumerate((y1_ref, y2_ref, y3_ref), start=1):
        y = y + w[:, kk:kk + 1] * _unpack_bf16_pairs(y_ref[...])
    xn = x_ref[...] + g2_ref[...] * y
    if final:
        ms = jnp.mean(xn * xn, axis=-1, keepdims=True)
        xn = xn * lax.rsqrt(ms + EPS) * fg_ref[...]
    xo_ref[...] = xn


def _combine(dest, out_sorted, gates_t, x2, g2, final_g, s, final):
    t, d = x2.shape
    tm = 256
    tpb = s // tm
    sc_unit = 2 * SC_ROWS * SC_CORES * SC_SUBCORES // TOP_K
    chunkable = t % (COMBINE_CHUNKS * sc_unit) == 0 and t % (COMBINE_CHUNKS * tm) == 0
    n_chunks = COMBINE_CHUNKS if chunkable else 1
    tc = t // n_chunks
    nt = tc // tm
    for c in range(n_chunks):
        yg = _sc_gather_rows(out_sorted, dest[:, c * tc:(c + 1) * tc].reshape(TOP_K * tc))
        plane = lambda kk: pl.BlockSpec((tm, d // 2), lambda i: (kk * nt + i, 0))
        tile = lambda i, c=c: (c * nt + i, 0)
        x2 = pl.pallas_call(
            functools.partial(_combine_kernel, final=final),
            out_shape=jax.ShapeDtypeStruct((t, d), F32),
            grid=(nt,),
            in_specs=[plane(0), plane(1), plane(2), plane(3),
                      pl.BlockSpec((tm, TOP_K), tile),
                      pl.BlockSpec((tm, d), tile),
                      pl.BlockSpec((None, 1, d), lambda i, c=c: ((c * nt + i) // tpb, 0, 0)),
                      pl.BlockSpec((1, d), lambda i: (0, 0))],
            out_specs=pl.BlockSpec((tm, d), tile),
            input_output_aliases={5: 0},
            compiler_params=_cparams("arbitrary"),
            name="moe_combine",
        )(yg, yg, yg, yg, gates_t, x2, g2, final_g)
    return x2


def _moe_plan(top_e, pos, counts):
    t = top_e.shape[1]
    sizes = counts[:, 0].astype(jnp.int32)
    padded = (sizes + MOE_BLOCK - 1) // MOE_BLOCK * MOE_BLOCK
    pad_end = jnp.cumsum(padded)
    pad_start = pad_end - padded
    start_of = jnp.zeros_like(top_e)
    for e in range(N_EXPERTS):
        start_of = jnp.where(top_e == e, pad_start[e], start_of)
    dest_kt = start_of + pos
    n_blocks = -(-t * TOP_K // MOE_BLOCK) + N_EXPERTS
    blk_start = jnp.arange(n_blocks, dtype=jnp.int32) * MOE_BLOCK
    blk_e = jnp.minimum(jnp.sum((pad_end[None, :] <= blk_start[:, None]).astype(jnp.int32), axis=1),
                        N_EXPERTS - 1).astype(jnp.int32)
    n_used = (pad_end[-1] // MOE_BLOCK).astype(jnp.int32).reshape(1)
    filled_end = pad_start + sizes
    n_valid = jnp.zeros((n_blocks,), jnp.int32)
    for e in range(N_EXPERTS):
        n_valid = jnp.where(blk_e == e, jnp.clip(filled_end[e] - blk_start, 0, MOE_BLOCK), n_valid)
    return dest_kt, blk_e, n_used, n_valid, n_blocks


def _rope_tables(s, gain):
    n_rows = s // GRID_W
    row = jnp.repeat(jnp.arange(n_rows), GRID_W).astype(F32)
    col = (jnp.arange(s) % GRID_W).astype(F32)
    half = HEAD_DIM // 2
    inv = ROPE_THETA ** (-jnp.arange(0, half, 2, dtype=F32) / half)
    ang_r = row[:, None] * inv
    ang_c = col[:, None] * inv
    cos = jnp.concatenate([jnp.cos(ang_r)] * 2 + [jnp.cos(ang_c)] * 2, axis=-1)
    sin = jnp.concatenate([-jnp.sin(ang_r), jnp.sin(ang_r), -jnp.sin(ang_c), jnp.sin(ang_c)], axis=-1)
    gain = gain.astype(F32)
    return cos * gain[None, :], sin * gain[_PARTNER][None, :]


_q = HEAD_DIM // 4
_PARTNER = np.concatenate([np.arange(_q, 2 * _q), np.arange(0, _q),
                           np.arange(3 * _q, 4 * _q), np.arange(2 * _q, 3 * _q)])


def _partner_cols(w, n_heads):
    idx = (np.arange(n_heads)[:, None] * HEAD_DIM + _PARTNER[None, :]).reshape(-1)
    return w[:, idx]


def kernel(x, c, w_ada, b_ada, norm1_g, w_in, q_norm_g, k_norm_g, rel_bias, w_br_a, w_br_b, w_out,
           norm2_g, w_router, b_router, w_gate_up, b_gate_up, w_down, b_down, final_norm_g):
    bsz, s, d = x.shape
    depth = w_ada.shape[0]
    t = bsz * s
    mod = _ada_mod(c, w_ada, b_ada)
    x2 = x.reshape(t, d)
    q_off = 3 * A_WIDTH
    k_off = q_off + B_Q
    for l in range(depth):
        sh1, sc1, g1, sh2, sc2, g2 = [mod[l, :, j * d:(j + 1) * d].reshape(bsz, 1, d)
                                      for j in range(N_MOD)]
        w = w_in[l]
        wq, wk = w[:, q_off:q_off + B_Q], w[:, k_off:k_off + B_KV]
        w_ext = jnp.concatenate([wq, _partner_cols(wq, B_HEADS), wk, _partner_cols(wk, B_KV_HEADS),
                                 w[:, k_off + B_KV:k_off + 2 * B_KV], w[:, :q_off],
                                 w[:, k_off + 2 * B_KV:]], axis=1).astype(BF16)
        tabs = [jnp.tile(tb, (1, 2)) for tb in
                _rope_tables(s, q_norm_g[l]) + _rope_tables(s, k_norm_g[l])]
        wa, wb, wo = w_br_a[l].astype(BF16), w_br_b[l].astype(BF16), w_out[l].astype(BF16)
        za0, za1, za2, q, k, v, gate, sq_norms = _projection(
            x2, sh1, sc1, norm1_g[l].reshape(1, d), w_ext, tabs, bsz, s)
        attn_a = [_dilated_group(za, sq_norms[g], rel_bias[:, g * A_HEADS:(g + 1) * A_HEADS], dil, bsz, s)
                  for g, (za, (_, dil)) in enumerate(zip((za0, za1, za2), A_GROUPS))]
        attn_b = _gqa_attention(q, k, v, q_norm_g[l], k_norm_g[l]).reshape(t, B_Q)
        x2, h2, top_e, gates, pos, counts = _merge_and_route(
            [o for o, _ in attn_a], [lse for _, lse in attn_a], attn_b, gate, x2, g1, wa, wb, wo,
            sh2, sc2, norm2_g[l].reshape(1, d), w_router[l], b_router[l], s)
        dest, blk_e, n_used, n_valid, n_blocks = _moe_plan(top_e, pos, counts)
        slots = _sc_scatter_rows(h2, dest.reshape(TOP_K * t), n_blocks * MOE_BLOCK)
        outs = _experts(slots, blk_e, n_used, n_valid, w_gate_up, b_gate_up, w_down, b_down, l)
        x2 = _combine(dest, outs, gates.T, x2, g2, final_norm_g.reshape(1, d), s,
                      final=(l == depth - 1))
    return x2.reshape(bsz, s, d)
```

```python
import functools
import math

import numpy as np
import jax
import jax.numpy as jnp
from jax import lax
from jax.experimental import pallas as pl
from jax.experimental.pallas import tpu as pltpu
from jax.experimental.pallas import tpu_sc as plsc

F32 = jnp.float32
BF16 = jnp.bfloat16

HEAD_DIM = 64
LANE = 128
PAIR = 2 * HEAD_DIM
A_GROUPS = ((128, 1), (512, 4), (2048, 16))
A_HEADS = 4
A_WIDTH = len(A_GROUPS) * A_HEADS * HEAD_DIM
A_OUT = A_HEADS * HEAD_DIM
A_GROUP_COLS = 3 * A_OUT
B_HEADS = 16
B_KV_HEADS = 4
B_GRP = B_HEADS // B_KV_HEADS
B_Q = B_HEADS * HEAD_DIM
B_KV = B_KV_HEADS * HEAD_DIM
GRID_W = 64
ROPE_THETA = 10000.0
REL_BUCKETS = 32
REL_MAX_DIST = 1024
N_EXPERTS = 32
TOP_K = 4
SWIGLU_LIMIT = 7.0
SWIGLU_ALPHA = 1.702
MOE_BLOCK = 512
COMBINE_CHUNKS = 4
MERGE_ROWS = 128
N_MOD = 6
EPS = 1e-6
NEG_INF = -1e30
LOG2E = math.log2(math.e)
N_SIDE = 64
DIL_SUB = 2 * N_SIDE
DIL_NO_SHIFT_MAX = 60.0
DIL_STEP_ROWS = 512

VMEM_LIMIT = 56 * 1024 * 1024


def _pack_bf16_pairs(x):
    bits = pltpu.bitcast(x.astype(BF16).astype(F32), jnp.uint32)
    half = x.shape[1] // 2
    return (bits[:, :half] >> 16) | bits[:, half:]


def _unpack_bf16_pairs(u):
    return jnp.concatenate([pltpu.bitcast(u << 16, F32),
                            pltpu.bitcast(u & jnp.uint32(0xFFFF0000), F32)], axis=1)


def _cparams(*sem):
    return pltpu.CompilerParams(dimension_semantics=sem, vmem_limit_bytes=VMEM_LIMIT)


def _ada_kernel(c_ref, w_ref, b_ref, o_ref):
    c = c_ref[...]
    ca = (c * jax.nn.sigmoid(c)).astype(BF16)
    o_ref[...] = jnp.dot(ca, w_ref[...].astype(BF16), preferred_element_type=F32) + b_ref[...]


def _ada_mod(c, w_ada, b_ada):
    depth, d, n = w_ada.shape
    bsz = c.shape[0]
    tn = 1536
    return pl.pallas_call(
        _ada_kernel,
        out_shape=jax.ShapeDtypeStruct((depth, bsz, n), F32),
        grid=(depth, n // tn),
        in_specs=[pl.BlockSpec((bsz, d), lambda l, j: (0, 0)),
                  pl.BlockSpec((None, d, tn), lambda l, j: (l, 0, j)),
                  pl.BlockSpec((None, 1, tn), lambda l, j: (l, 0, j))],
        out_specs=pl.BlockSpec((None, bsz, tn), lambda l, j: (l, 0, j)),
        compiler_params=_cparams("arbitrary", "arbitrary"),
        name="ada_mod",
    )(c, w_ada, b_ada.reshape(depth, 1, n))


def _proj_kernel(x_ref, sh_ref, sc_ref, g_ref, w_ref, aq_ref, bq_ref, ak_ref, bk_ref, hsel_ref,
                 za0_ref, za1_ref, za2_ref, q_ref, k_ref, v_ref, gate_ref, nrm_ref, fold_ref):
    @pl.when(pl.program_id(0) == 0)
    def _():
        nrm_ref[...] = jnp.zeros_like(nrm_ref)

    x = x_ref[...]
    ms = jnp.mean(x * x, axis=-1, keepdims=True)
    h = x * lax.rsqrt(ms + EPS) * g_ref[...]
    h = h * (1.0 + sc_ref[...]) + sh_ref[...]
    hb = h.astype(BF16)

    def mm(lo, hi):
        return jnp.dot(hb, w_ref[:, lo:hi], preferred_element_type=F32)

    tm = x.shape[0]
    o = 0
    zq = mm(o, o + B_Q); o += B_Q
    zqp = mm(o, o + B_Q); o += B_Q
    zk = mm(o, o + B_KV); o += B_KV
    zkp = mm(o, o + B_KV); o += B_KV
    zv = mm(o, o + B_KV); o += B_KV
    lane = lax.broadcasted_iota(jnp.int32, (tm, PAIR), 1)
    first = lane < HEAD_DIM

    def pair_norm_rope(z, zp, a, b):
        zz = z * z
        ss = jnp.where(first,
                       jnp.sum(jnp.where(first, zz, 0.0), axis=-1, keepdims=True),
                       jnp.sum(jnp.where(first, 0.0, zz), axis=-1, keepdims=True))
        return lax.rsqrt(ss * (1.0 / HEAD_DIM) + EPS) * (z * a + zp * b)

    aq, bq, ak, bk = aq_ref[...], bq_ref[...], ak_ref[...], bk_ref[...]
    for j in range(B_HEADS // 2):
        cols = slice(j * PAIR, (j + 1) * PAIR)
        q_ref[j] = (pair_norm_rope(zq[:, cols], zqp[:, cols], aq, bq)
                    * (LOG2E * HEAD_DIM ** -0.5)).astype(BF16)
    ones_col = jnp.where(lane == HEAD_DIM, 1.0, 0.0)
    for j in range(B_KV_HEADS // 2):
        cols = slice(j * PAIR, (j + 1) * PAIR)
        kk = pair_norm_rope(zk[:, cols], zkp[:, cols], ak, bk)
        kk_sw = pltpu.roll(kk, HEAD_DIM, axis=1)
        k_ref[2 * j] = jnp.where(first, kk, kk_sw).astype(BF16)
        k_ref[2 * j + 1] = jnp.where(first, kk_sw, kk).astype(BF16)
        vv = zv[:, cols]
        v_ref[2 * j] = jnp.where(first, vv, ones_col).astype(BF16)
        v_ref[2 * j + 1] = jnp.where(first, pltpu.roll(vv, HEAD_DIM, axis=1), ones_col).astype(BF16)
    for g, (za_ref, (_, dil)) in enumerate(zip((za0_ref, za1_ref, za2_ref), A_GROUPS)):
        z = mm(o, o + A_GROUP_COLS)
        o += A_GROUP_COLS
        zqk = z[:, :2 * A_OUT]
        sq = jnp.dot((zqk * zqk).astype(BF16), hsel_ref[...], preferred_element_type=F32)
        nrm_ref[g:g + 1, :] = jnp.maximum(nrm_ref[g:g + 1, :], jnp.max(sq, axis=0, keepdims=True))
        if dil == 1:
            za_ref[...] = z.astype(BF16)
        else:
            for c in range(A_GROUP_COLS // LANE):
                fold_ref[c] = z[:, c * LANE:(c + 1) * LANE]
            for r in range(dil):
                for c in range(A_GROUP_COLS // LANE):
                    col = r * A_GROUP_COLS + c * LANE
                    za_ref[:, col:col + LANE] = (
                        fold_ref[c, pl.ds(r, tm // dil, stride=dil), :].astype(BF16))
    gate_ref[...] = jax.nn.sigmoid(mm(o, o + gate_ref.shape[-1])).astype(BF16)


def _projection(x2, sh, sc, g, w_ext, tabs, bsz, s):
    t, d = x2.shape
    tm = 512
    tpb = s // tm
    n_ext = w_ext.shape[1]
    n_gate = 2 * d
    row = lambda i: (i, 0)
    per_b = lambda i: (i // tpb, 0, 0)
    tab = lambda i: (i % tpb, 0)
    hm = lambda i: (i // tpb, 0, i % tpb, 0)
    out_shape = (
        [jax.ShapeDtypeStruct((bsz, s // dil, dil * A_GROUP_COLS), BF16) for _, dil in A_GROUPS]
        + [jax.ShapeDtypeStruct((bsz, B_HEADS // 2, s, PAIR), BF16),
           jax.ShapeDtypeStruct((bsz, B_KV_HEADS, s, PAIR), BF16),
           jax.ShapeDtypeStruct((bsz, B_KV_HEADS, s, PAIR), BF16),
           jax.ShapeDtypeStruct((t, n_gate), BF16),
           jax.ShapeDtypeStruct((len(A_GROUPS), LANE), F32)])
    head_sel = (jnp.arange(2 * A_OUT)[:, None] // HEAD_DIM == jnp.arange(LANE)[None, :]).astype(BF16)
    out_specs = (
        [pl.BlockSpec((None, tm // dil, dil * A_GROUP_COLS), lambda i: (i // tpb, i % tpb, 0))
         for _, dil in A_GROUPS]
        + [pl.BlockSpec((None, B_HEADS // 2, tm, PAIR), hm),
           pl.BlockSpec((None, B_KV_HEADS, tm, PAIR), hm),
           pl.BlockSpec((None, B_KV_HEADS, tm, PAIR), hm),
           pl.BlockSpec((tm, n_gate), row),
           pl.BlockSpec((len(A_GROUPS), LANE), lambda i: (0, 0))])
    return pl.pallas_call(
        _proj_kernel,
        out_shape=out_shape,
        grid=(t // tm,),
        in_specs=[pl.BlockSpec((tm, d), row),
                  pl.BlockSpec((None, 1, d), per_b),
                  pl.BlockSpec((None, 1, d), per_b),
                  pl.BlockSpec((1, d), lambda i: (0, 0)),
                  pl.BlockSpec((d, n_ext), lambda i: (0, 0), pipeline_mode=pl.Buffered(1))]
                 + [pl.BlockSpec((tm, PAIR), tab)] * 4
                 + [pl.BlockSpec((2 * A_OUT, LANE), lambda i: (0, 0))],
        out_specs=out_specs,
        scratch_shapes=[pltpu.VMEM((A_GROUP_COLS // LANE, tm, LANE), F32)],
        compiler_params=_cparams("arbitrary"),
        name="in_proj",
    )(x2, sh, sc, g, w_ext, *tabs, head_sel)


def _dilated_kernel(prev_ref, cur_ref, next_ref, bias_ref, o_ref, lse_ref, *, tq, seq_len, shift):
    for res in range(cur_ref.shape[1] // A_GROUP_COLS):
        c0 = res * A_GROUP_COLS
        part = lambda ref, j: ref[:, c0 + j * A_OUT:c0 + (j + 1) * A_OUT]
        _dilated_subsequence(
            part(cur_ref, 0),
            jnp.concatenate([part(prev_ref, 1), part(cur_ref, 1), part(next_ref, 1)], axis=0),
            jnp.concatenate([part(prev_ref, 2), part(cur_ref, 2), part(next_ref, 2)], axis=0),
            bias_ref, o_ref, lse_ref, res * A_OUT, tq=tq, seq_len=seq_len, shift=shift)


def _dilated_subsequence(q_all, k, v, bias_ref, o_ref, lse_ref, out_col, *, tq, seq_len, shift):
    i = pl.program_id(2)
    nk = DIL_SUB + 2 * N_SIDE
    col = lax.broadcasted_iota(jnp.int32, (DIL_SUB, nk), 1)
    if not shift:
        lane = lax.broadcasted_iota(jnp.int32, (v.shape[0], PAIR), 1)
        first = lane < HEAD_DIM
        ones_col = jnp.where(lane == HEAD_DIM, 1.0, 0.0)
        v_ext = []
        for j in range(A_HEADS // 2):
            vp = v[:, j * PAIR:(j + 1) * PAIR].astype(F32)
            v_ext += [jnp.where(first, vp, ones_col).astype(BF16),
                      jnp.where(first, pltpu.roll(vp, HEAD_DIM, axis=1), ones_col).astype(BF16)]
    for sb in range(tq // DIL_SUB):
        rows = slice(sb * DIL_SUB, (sb + 1) * DIL_SUB)
        q = q_all[rows, :]
        kb = k[sb * DIL_SUB:sb * DIL_SUB + nk]
        vb = v[sb * DIL_SUB:sb * DIL_SUB + nk]
        kpos = i * tq + sb * DIL_SUB - N_SIDE + col
        valid = (kpos >= 0) & (kpos < seq_len)
        for hh in range(A_HEADS):
            sl = slice(hh * HEAD_DIM, (hh + 1) * HEAD_DIM)
            sc = lax.dot_general(q[:, sl], kb[:, sl], (((1,), (1,)), ((), ())),
                                 preferred_element_type=F32)
            sc = sc * (HEAD_DIM ** -0.5) + bias_ref[hh]
            sc = jnp.where(valid, sc, NEG_INF)
            if shift:
                m = jnp.max(sc, axis=-1, keepdims=True)
                p = jnp.exp(sc - m)
                den = jnp.sum(p, axis=-1, keepdims=True)
                o = jnp.dot(p.astype(BF16), vb[:, sl], preferred_element_type=F32) / den
                lse = m + jnp.log(den)
            else:
                acc = jnp.dot(jnp.exp(sc).astype(BF16), v_ext[hh][sb * DIL_SUB:sb * DIL_SUB + nk],
                              preferred_element_type=F32)
                den = acc[:, HEAD_DIM:HEAD_DIM + 1]
                o = acc[:, :HEAD_DIM] / den
                lse = jnp.log(den)
            out_sl = slice(out_col + hh * HEAD_DIM, out_col + (hh + 1) * HEAD_DIM)
            o_ref[rows, out_sl] = o
            lse_ref[rows, out_sl] = jnp.broadcast_to(lse, (DIL_SUB, HEAD_DIM))


def _t5_bucket(rel):
    nb = REL_BUCKETS // 2
    max_exact = nb // 2
    ret = jnp.where(rel > 0, nb, 0)
    n = jnp.abs(rel)
    nf = jnp.maximum(n, 1).astype(F32)
    large = max_exact + (jnp.log(nf / max_exact) / math.log(REL_MAX_DIST / max_exact)
                         * (nb - max_exact)).astype(jnp.int32)
    large = jnp.minimum(large, nb - 1)
    return ret + jnp.where(n < max_exact, n, large)


def _band_bias(rel_bias_g, dilation, tq):
    nk = tq + 2 * N_SIDE
    rel = jnp.arange(nk)[None, :] - N_SIDE - jnp.arange(tq)[:, None]
    bucket = _t5_bucket(rel * dilation)[None]
    bias = jnp.full((A_HEADS,) + rel.shape, NEG_INF, F32)
    for b in range(REL_BUCKETS):
        bias = jnp.where(bucket == b, rel_bias_g[b].astype(F32)[:, None, None], bias)
    return jnp.where((jnp.abs(rel) <= N_SIDE)[None], bias, NEG_INF)


def _dilated_group(za, sq_norms, rel_bias_g, dilation, bsz, s):
    ll = s // dilation
    tq = min(DIL_STEP_ROWS, ll)
    assert ll % tq == 0 and tq % DIL_SUB == 0
    n_res = max(1, min(dilation, DIL_STEP_ROWS // tq))
    assert dilation % n_res == 0
    nhalf = ll // N_SIDE
    per = tq // N_SIDE
    slab = n_res * A_GROUP_COLS
    cur = pl.BlockSpec((None, tq, slab), lambda b, r, i: (b, i, r))
    prv = pl.BlockSpec((None, N_SIDE, slab), lambda b, r, i: (b, jnp.maximum(per * i - 1, 0), r))
    nxt = pl.BlockSpec((None, N_SIDE, slab), lambda b, r, i: (b, jnp.minimum(per * (i + 1), nhalf - 1), r))
    bias = _band_bias(rel_bias_g, dilation, DIL_SUB)
    out_sd = jax.ShapeDtypeStruct((bsz, ll, dilation * A_OUT), F32)
    out_spec = pl.BlockSpec((None, tq, n_res * A_OUT), lambda b, r, i: (b, i, r))

    def attend(shift, zv, bias):
        return pl.pallas_call(
            functools.partial(_dilated_kernel, tq=tq, seq_len=ll, shift=shift),
            out_shape=[out_sd, out_sd],
            grid=(bsz, dilation // n_res, ll // tq),
            in_specs=[prv, cur, nxt, pl.BlockSpec(bias.shape, lambda b, r, i: (0, 0, 0))],
            out_specs=[out_spec, out_spec],
            compiler_params=_cparams("arbitrary", "arbitrary", "arbitrary"),
            name=f"dilated_attn_d{dilation}" + ("_rowmax" if shift else ""),
        )(zv, zv, zv, bias)

    q2 = jnp.max(sq_norms[:A_HEADS])
    k2 = jnp.max(sq_norms[A_HEADS:2 * A_HEADS])
    bound = 1.03 * jnp.sqrt(q2 * k2) * (HEAD_DIM ** -0.5) + jnp.max(jnp.abs(rel_bias_g))
    return lax.cond(bound <= DIL_NO_SHIFT_MAX, functools.partial(attend, False),
                    functools.partial(attend, True), za, bias)


GQA_NO_SHIFT_MAX_LOG2 = 80.0
GQA_KEY_CHUNK = 128


def _stacked_heads(q_ref):
    first = lax.broadcasted_iota(jnp.int32, q_ref.shape[1:], 1) < HEAD_DIM
    blocks = []
    for j in range(q_ref.shape[0]):
        qp = q_ref[j]
        blocks += [jnp.where(first, qp, jnp.zeros_like(qp)), jnp.where(first, jnp.zeros_like(qp), qp)]
    return jnp.concatenate(blocks, axis=0)


def _gqa_kernel_noshift(q_ref, k_ref, v_ref, o_ref):
    tq = q_ref.shape[1]
    q = _stacked_heads(q_ref)
    nt = (((1,), (1,)), ((), ()))
    acc = jnp.zeros((q.shape[0], PAIR), F32)
    for c in range(k_ref.shape[0] // GQA_KEY_CHUNK):
        rows = slice(c * GQA_KEY_CHUNK, (c + 1) * GQA_KEY_CHUNK)
        p = jnp.exp2(lax.dot_general(q, k_ref[rows, :], nt, preferred_element_type=F32))
        acc = acc + jnp.dot(p.astype(BF16), v_ref[rows, :], preferred_element_type=F32)
    o = acc[:, :HEAD_DIM] / acc[:, HEAD_DIM:HEAD_DIM + 1]
    for hh in range(q.shape[0] // tq):
        o_ref[:, hh * HEAD_DIM:(hh + 1) * HEAD_DIM] = o[hh * tq:(hh + 1) * tq].astype(BF16)


def _gqa_kernel_rowmax(q_ref, k_ref, v_ref, o_ref):
    tq = q_ref.shape[1]
    q = _stacked_heads(q_ref)
    k = k_ref[...]
    v = v_ref[:, :HEAD_DIM]
    for hh in range(q.shape[0] // tq):
        sc = lax.dot_general(q[hh * tq:(hh + 1) * tq], k, (((1,), (1,)), ((), ())),
                             preferred_element_type=F32)
        m = jnp.max(sc, axis=-1, keepdims=True)
        p = jnp.exp2(sc - m)
        den = jnp.sum(p, axis=-1, keepdims=True)
        o = jnp.dot(p.astype(BF16), v, preferred_element_type=F32) / den
        o_ref[:, hh * HEAD_DIM:(hh + 1) * HEAD_DIM] = o.astype(BF16)


def _gqa_call(body, name, tq, q, k, v):
    bsz, _, s, _ = q.shape
    return pl.pallas_call(
        body,
        out_shape=jax.ShapeDtypeStruct((bsz, s, B_Q), BF16),
        grid=(bsz, B_KV_HEADS, s // tq),
        in_specs=[pl.BlockSpec((None, B_GRP // 2, tq, PAIR), lambda b, h, i: (b, h, i, 0)),
                  pl.BlockSpec((None, None, s, PAIR), lambda b, h, i: (b, h, 0, 0)),
                  pl.BlockSpec((None, None, s, PAIR), lambda b, h, i: (b, h, 0, 0))],
        out_specs=pl.BlockSpec((None, tq, B_GRP * HEAD_DIM), lambda b, h, i: (b, i, h)),
        compiler_params=_cparams("arbitrary", "arbitrary", "arbitrary"),
        name=name,
    )(q, k, v)


def _gqa_attention(q, k, v, q_gain, k_gain):
    bound = (HEAD_DIM ** 0.5) * LOG2E * 1.01 * jnp.max(jnp.abs(q_gain)) * jnp.max(jnp.abs(k_gain))
    return lax.cond(bound <= GQA_NO_SHIFT_MAX_LOG2,
                    functools.partial(_gqa_call, _gqa_kernel_noshift, "gqa_attn", 1024),
                    functools.partial(_gqa_call, _gqa_kernel_rowmax, "gqa_attn_rowmax", 256),
                    q, k, v)


def _merge_kernel(o0_ref, o1_ref, o2_ref, l0_ref, l1_ref, l2_ref, ob_ref, gate_ref, x_ref, g1_ref,
                  wa_ref, wb_ref, wo_ref, sh_ref, sc_ref, g_ref, whi_ref, wlo_ref, b_ref,
                  xo_ref, h_ref, e_ref, w_ref, pos_ref, cnt_ref, run_ref, *unfold_refs):
    @pl.when(pl.program_id(0) == 0)
    def _():
        run_ref[...] = jnp.zeros_like(run_ref)

    tm, d = x_ref.shape
    scratch = list(unfold_refs)

    def token_order(ref, dil):
        if dil == 1:
            return ref[...]
        buf = scratch.pop()
        for r in range(dil):
            for c in range(A_OUT // LANE):
                col = r * A_OUT + c * LANE
                buf[c, pl.ds(r, tm // dil, stride=dil), :] = ref[:, col:col + LANE]
        return jnp.concatenate([buf[c] for c in range(A_OUT // LANE)], axis=1)

    dils = [dil for _, dil in A_GROUPS]
    o0, o1, o2 = [token_order(r, dl) for r, dl in zip((o0_ref, o1_ref, o2_ref), dils)]
    l0, l1, l2 = [token_order(r, dl) for r, dl in zip((l0_ref, l1_ref, l2_ref), dils)]
    m = jnp.maximum(jnp.maximum(l0, l1), l2)
    e0, e1, e2 = jnp.exp(l0 - m), jnp.exp(l1 - m), jnp.exp(l2 - m)
    tot = e0 + e1 + e2
    oa = ((e0 / tot) * o0 + (e1 / tot) * o1 + (e2 / tot) * o2).astype(BF16)
    xn_rows = []
    for lo in range(0, tm, MERGE_ROWS):
        rows = slice(lo, lo + MERGE_ROWS)
        ya = jnp.dot(oa[rows], wa_ref[...], preferred_element_type=F32)
        yb = jnp.dot(ob_ref[rows, :], wb_ref[...], preferred_element_type=F32)
        merged = gate_ref[rows, :d].astype(F32) * ya + gate_ref[rows, d:].astype(F32) * yb
        y = jnp.dot(merged.astype(BF16), wo_ref[...], preferred_element_type=F32)
        xn_rows.append(x_ref[rows, :] + g1_ref[...] * y)
    xn = jnp.concatenate(xn_rows, axis=0)
    xo_ref[...] = xn
    _route_tile(xn, sh_ref, sc_ref, g_ref, whi_ref, wlo_ref, b_ref, h_ref, e_ref, w_ref, pos_ref, cnt_ref,
                run_ref)


def _merge_and_route(os_, ls_, ob, gate, x2, g1, wa, wb, wo, sh2, sc2, g2n, w_router, b_router, s):
    t, d = x2.shape
    tm = 256
    tpb = s // tm
    ne = w_router.shape[1]
    wt = w_router.T
    whi = wt.astype(BF16)
    wlo = (wt - whi.astype(F32)).astype(BF16)
    row = lambda i: (i, 0)
    const = lambda i: (0, 0)
    per_b = lambda i: (i // tpb, 0, 0)
    by_token = pl.BlockSpec((TOP_K, tm), lambda i: (0, i))
    a_specs = [pl.BlockSpec((None, tm // dil, dil * A_OUT), lambda i: (i // tpb, i % tpb, 0))
               for _, dil in A_GROUPS]
    n_unfold = 2 * sum(1 for _, dil in A_GROUPS if dil > 1)
    return pl.pallas_call(
        _merge_kernel,
        out_shape=[jax.ShapeDtypeStruct((t, d), F32),
                   jax.ShapeDtypeStruct((t, d // 2), jnp.uint32),
                   jax.ShapeDtypeStruct((TOP_K, t), jnp.int32),
                   jax.ShapeDtypeStruct((TOP_K, t), F32),
                   jax.ShapeDtypeStruct((TOP_K, t), jnp.int32),
                   jax.ShapeDtypeStruct((ne, 1), F32)],
        grid=(t // tm,),
        scratch_shapes=[pltpu.VMEM((ne, 1), F32)]
                       + [pltpu.VMEM((A_OUT // LANE, tm, LANE), F32)] * n_unfold,
        in_specs=a_specs * 2
                 + [pl.BlockSpec((tm, B_Q), row),
                    pl.BlockSpec((tm, 2 * d), row),
                    pl.BlockSpec((tm, d), row),
                    pl.BlockSpec((None, 1, d), per_b),
                    pl.BlockSpec(wa.shape, const),
                    pl.BlockSpec(wb.shape, const),
                    pl.BlockSpec(wo.shape, const),
                    pl.BlockSpec((None, 1, d), per_b),
                    pl.BlockSpec((None, 1, d), per_b),
                    pl.BlockSpec((1, d), const),
                    pl.BlockSpec((ne, d), const),
                    pl.BlockSpec((ne, d), const),
                    pl.BlockSpec((ne, 1), const)],
        out_specs=[pl.BlockSpec((tm, d), row),
                   pl.BlockSpec((tm, d // 2), row),
                   by_token, by_token, by_token,
                   pl.BlockSpec((ne, 1), const)],
        compiler_params=_cparams("arbitrary"),
        name="merge_route",
    )(*os_, *ls_, ob, gate, x2, g1, wa, wb, wo, sh2, sc2, g2n, whi, wlo, b_router.reshape(ne, 1))


def _route_tile(x, sh_ref, sc_ref, g_ref, whi_ref, wlo_ref, b_ref,
                h_ref, e_ref, w_ref, pos_ref, cnt_ref, run_ref):
    ms = jnp.mean(x * x, axis=-1, keepdims=True)
    h = x * lax.rsqrt(ms + EPS) * g_ref[...]
    h = h * (1.0 + sc_ref[...]) + sh_ref[...]
    h_ref[...] = _pack_bf16_pairs(h)
    hhi = h.astype(BF16)
    hlo = (h - hhi.astype(F32)).astype(BF16)
    nt = (((1,), (1,)), ((), ()))
    dotf = lambda a, b: lax.dot_general(a, b, nt, preferred_element_type=F32)
    logits = dotf(whi_ref[...], hhi) + (dotf(whi_ref[...], hlo) + dotf(wlo_ref[...], hhi)) + b_ref[...]
    ne, tm = logits.shape
    iota = lax.broadcasted_iota(jnp.int32, (ne, tm), 0).astype(F32)
    vals, idxs = [], []
    cur = logits
    for _ in range(TOP_K):
        m = jnp.max(cur, axis=0, keepdims=True)
        idx = jnp.min(jnp.where(cur == m, iota, float(ne)), axis=0, keepdims=True)
        vals.append(m)
        idxs.append(idx)
        cur = jnp.where(iota == idx, -jnp.inf, cur)
    tv = jnp.concatenate(vals, axis=0)
    ex = jnp.exp(tv - tv[0:1])
    w_ref[...] = ex / jnp.sum(ex, axis=0, keepdims=True)
    e_ref[...] = jnp.concatenate(idxs, axis=0).astype(jnp.int32)
    onehot = jnp.zeros((ne, tm), F32)
    for idx in idxs:
        onehot = onehot + jnp.where(iota == idx, 1.0, 0.0)
    earlier = (lax.broadcasted_iota(jnp.int32, (tm, tm), 0)
               < lax.broadcasted_iota(jnp.int32, (tm, tm), 1))
    rank = jnp.dot(onehot.astype(BF16), jnp.where(earlier, 1.0, 0.0).astype(BF16),
                   preferred_element_type=F32) + run_ref[...]
    pos_ref[...] = jnp.concatenate(
        [jnp.sum(jnp.where(iota == idx, rank, 0.0), axis=0, keepdims=True) for idx in idxs],
        axis=0).astype(jnp.int32)
    run_ref[...] = run_ref[...] + jnp.sum(onehot, axis=1, keepdims=True)
    cnt_ref[...] = run_ref[...]


SC_CORES = 2
SC_SUBCORES = 16
SC_ROWS = 64


def _sc_scatter_rows(rows, dest_kt, n_slots):
    t, d = rows.shape
    n_workers = SC_CORES * SC_SUBCORES
    per_w = t // n_workers
    n_chunks = per_w // SC_ROWS
    assert per_w * n_workers == t and n_chunks * SC_ROWS == per_w and n_chunks % 2 == 0
    idx = dest_kt.reshape(TOP_K, n_workers, n_chunks, SC_ROWS).transpose(1, 2, 0, 3)
    idx = idx.reshape(n_workers * n_chunks * TOP_K, SC_ROWS)
    lists_per_w = n_chunks * TOP_K
    mesh = plsc.VectorSubcoreMesh(core_axis_name="c", subcore_axis_name="s")

    @functools.partial(
        pl.kernel, mesh=mesh,
        out_type=jax.ShapeDtypeStruct((n_slots, d), rows.dtype),
        scratch_types=[pltpu.VMEM((lists_per_w, SC_ROWS), jnp.int32),
                       pltpu.VMEM((SC_ROWS, d), rows.dtype),
                       pltpu.VMEM((SC_ROWS, d), rows.dtype)] + [pltpu.SemaphoreType.DMA] * 4,
        name="sc_scatter_rows",
    )
    def scatter_kernel(rows_hbm, idx_hbm, out_hbm, idx_v, buf_a, buf_b, ld_a, ld_b, st_a, st_b):
        wid = lax.axis_index("s") * SC_CORES + lax.axis_index("c")
        pltpu.sync_copy(idx_hbm.at[pl.ds(wid * lists_per_w, lists_per_w)], idx_v)

        def load(j, buf, sem):
            return pltpu.make_async_copy(rows_hbm.at[pl.ds(wid * per_w + j * SC_ROWS, SC_ROWS)], buf, sem)

        def scatter_all(j, buf, sem):
            copies = [pltpu.make_async_copy(buf, out_hbm.at[idx_v.at[j * TOP_K + kk]], sem)
                      for kk in range(TOP_K)]
            for cp in copies:
                cp.start()
            for cp in copies:
                cp.wait()

        load(0, buf_a, ld_a).start()

        @pl.loop(0, n_chunks, step=2)
        def _(j):
            load(j + 1, buf_b, ld_b).start()
            load(j, buf_a, ld_a).wait()
            scatter_all(j, buf_a, st_a)

            @pl.when(j + 2 < n_chunks)
            def _():
                load(j + 2, buf_a, ld_a).start()
            load(j + 1, buf_b, ld_b).wait()
            scatter_all(j + 1, buf_b, st_b)

    return scatter_kernel(rows, idx)


def _expert_kernel(blk_e_ref, n_used_ref, n_valid_ref, x_ref, wgu_ref, bgu_ref, wdn_ref, bdn_ref, o_ref,
                   wgu_bf, wdn_bf):
    j = pl.program_id(0)
    d = wdn_bf.shape[1]

    @pl.when(j < n_used_ref[0])
    def _():
        @pl.when((j == 0) | (blk_e_ref[j] != blk_e_ref[jnp.maximum(j - 1, 0)]))
        def _():
            wgu_bf[...] = wgu_ref[...].astype(BF16)
            wdn_bf[...] = wdn_ref[...].astype(BF16)

        row = lax.broadcasted_iota(jnp.int32, x_ref.shape, 0)
        packed = jnp.where(row < n_valid_ref[j], x_ref[...], jnp.uint32(0))
        xb = _unpack_bf16_pairs(packed).astype(BF16)
        gu = jnp.dot(xb, wgu_bf[...], preferred_element_type=F32) + bgu_ref[...]
        x_glu = jnp.minimum(gu[:, :d], SWIGLU_LIMIT)
        x_lin = jnp.clip(gu[:, d:], -SWIGLU_LIMIT, SWIGLU_LIMIT)
        act = x_glu * jax.nn.sigmoid(SWIGLU_ALPHA * x_glu) * (x_lin + 1.0)
        o_ref[...] = _pack_bf16_pairs(
            jnp.dot(act.astype(BF16), wdn_bf[...], preferred_element_type=F32) + bdn_ref[...])

    @pl.when(j >= n_used_ref[0])
    def _():
        o_ref[...] = jnp.zeros_like(o_ref)


def _experts(xs, blk_e, n_used, n_valid, wgu, bgu, wdn, bdn, layer):
    d = wdn.shape[-1]
    n_blocks = blk_e.shape[0]
    depth, ne = wgu.shape[:2]
    by_expert = lambda j, be, nu, nv: (layer, be[j], 0, 0)
    grid_spec = pltpu.PrefetchScalarGridSpec(
        num_scalar_prefetch=3,
        grid=(n_blocks,),
        in_specs=[pl.BlockSpec((MOE_BLOCK, d // 2), lambda j, be, nu, nv: (jnp.minimum(j, nu[0] - 1), 0)),
                  pl.BlockSpec((None, None, d, 2 * d), by_expert),
                  pl.BlockSpec((None, None, 1, 2 * d), by_expert),
                  pl.BlockSpec((None, None, d, d), by_expert),
                  pl.BlockSpec((None, None, 1, d), by_expert)],
        out_specs=pl.BlockSpec((MOE_BLOCK, d // 2), lambda j, be, nu, nv: (j, 0)),
        scratch_shapes=[pltpu.VMEM((d, 2 * d), BF16), pltpu.VMEM((d, d), BF16)],
    )
    return pl.pallas_call(
        _expert_kernel,
        out_shape=jax.ShapeDtypeStruct((n_blocks * MOE_BLOCK, d // 2), jnp.uint32),
        grid_spec=grid_spec,
        compiler_params=_cparams("arbitrary"),
        name="moe_experts",
    )(blk_e, n_used, n_valid, xs, wgu, bgu.reshape(depth, ne, 1, 2 * d), wdn, bdn.reshape(depth, ne, 1, d))


def _sc_gather_rows(table, idx):
    n_out = idx.shape[0]
    d = table.shape[1]
    n_workers = SC_CORES * SC_SUBCORES
    per_w = n_out // n_workers
    n_chunks = per_w // SC_ROWS
    assert per_w * n_workers == n_out and n_chunks * SC_ROWS == per_w and n_chunks % 2 == 0
    mesh = plsc.VectorSubcoreMesh(core_axis_name="c", subcore_axis_name="s")

    @functools.partial(
        pl.kernel, mesh=mesh,
        out_type=jax.ShapeDtypeStruct((n_out, d), table.dtype),
        scratch_types=[pltpu.VMEM((per_w,), jnp.int32),
                       pltpu.VMEM((SC_ROWS, d), table.dtype),
                       pltpu.VMEM((SC_ROWS, d), table.dtype),
                       pltpu.SemaphoreType.DMA, pltpu.SemaphoreType.DMA],
        name="sc_gather_rows",
    )
    def gather_kernel(table_hbm, idx_hbm, out_hbm, idx_v, rows_a, rows_b, sem_a, sem_b):
        base = (lax.axis_index("s") * SC_CORES + lax.axis_index("c")) * per_w
        pltpu.sync_copy(idx_hbm.at[pl.ds(base, per_w)], idx_v)

        def gather(j, buf, sem):
            rows = idx_v.at[pl.ds(j * SC_ROWS, SC_ROWS)]
            return pltpu.make_async_copy(table_hbm.at[rows], buf, sem)

        def write_back(j, buf):
            pltpu.sync_copy(buf, out_hbm.at[pl.ds(base + j * SC_ROWS, SC_ROWS)])

        gather(0, rows_a, sem_a).start()

        @pl.loop(0, n_chunks, step=2)
        def _(j):
            gather(j + 1, rows_b, sem_b).start()
            gather(j, rows_a, sem_a).wait()
            write_back(j, rows_a)

            @pl.when(j + 2 < n_chunks)
            def _():
                gather(j + 2, rows_a, sem_a).start()
            gather(j + 1, rows_b, sem_b).wait()
            write_back(j + 1, rows_b)

    return gather_kernel(table, idx)


def _combine_kernel(y0_ref, y1_ref, y2_ref, y3_ref, w_ref, x_ref, g2_ref, fg_ref, xo_ref, *, final):
    w = w_ref[...]
    y = w[:, 0:1] * _unpack_bf16_pairs(y0_ref[...])
    for kk, y_ref in enumerate((y1_ref, y2_ref, y3_ref), start=1):
        y = y + w[:, kk:kk + 1] * _unpack_bf16_pairs(y_ref[...])
    xn = x_ref[...] + g2_ref[...] * y
    if final:
        ms = jnp.mean(xn * xn, axis=-1, keepdims=True)
        xn = xn * lax.rsqrt(ms + EPS) * fg_ref[...]
    xo_ref[...] = xn


def _combine(dest, out_sorted, gates_t, x2, g2, final_g, s, final):
    t, d = x2.shape
    tm = 256
    tpb = s // tm
    sc_unit = 2 * SC_ROWS * SC_CORES * SC_SUBCORES // TOP_K
    chunkable = t % (COMBINE_CHUNKS * sc_unit) == 0 and t % (COMBINE_CHUNKS * tm) == 0
    n_chunks = COMBINE_CHUNKS if chunkable else 1
    tc = t // n_chunks
    nt = tc // tm
    for c in range(n_chunks):
        yg = _sc_gather_rows(out_sorted, dest[:, c * tc:(c + 1) * tc].reshape(TOP_K * tc))
        plane = lambda kk: pl.BlockSpec((tm, d // 2), lambda i: (kk * nt + i, 0))
        tile = lambda i, c=c: (c * nt + i, 0)
        x2 = pl.pallas_call(
            functools.partial(_combine_kernel, final=final),
            out_shape=jax.ShapeDtypeStruct((t, d), F32),
            grid=(nt,),
            in_specs=[plane(0), plane(1), plane(2), plane(3),
                      pl.BlockSpec((tm, TOP_K), tile),
                      pl.BlockSpec((tm, d), tile),
                      pl.BlockSpec((None, 1, d), lambda i, c=c: ((c * nt + i) // tpb, 0, 0)),
                      pl.BlockSpec((1, d), lambda i: (0, 0))],
            out_specs=pl.BlockSpec((tm, d), tile),
            input_output_aliases={5: 0},
            compiler_params=_cparams("arbitrary"),
            name="moe_combine",
        )(yg, yg, yg, yg, gates_t, x2, g2, final_g)
    return x2


def _moe_plan(top_e, pos, counts):
    t = top_e.shape[1]
    sizes = counts[:, 0].astype(jnp.int32)
    padded = (sizes + MOE_BLOCK - 1) // MOE_BLOCK * MOE_BLOCK
    pad_end = jnp.cumsum(padded)
    pad_start = pad_end - padded
    start_of = jnp.zeros_like(top_e)
    for e in range(N_EXPERTS):
        start_of = jnp.where(top_e == e, pad_start[e], start_of)
    dest_kt = start_of + pos
    n_blocks = -(-t * TOP_K // MOE_BLOCK) + N_EXPERTS
    blk_start = jnp.arange(n_blocks, dtype=jnp.int32) * MOE_BLOCK
    blk_e = jnp.minimum(jnp.sum((pad_end[None, :] <= blk_start[:, None]).astype(jnp.int32), axis=1),
                        N_EXPERTS - 1).astype(jnp.int32)
    n_used = (pad_end[-1] // MOE_BLOCK).astype(jnp.int32).reshape(1)
    filled_end = pad_start + sizes
    n_valid = jnp.zeros((n_blocks,), jnp.int32)
    for e in range(N_EXPERTS):
        n_valid = jnp.where(blk_e == e, jnp.clip(filled_end[e] - blk_start, 0, MOE_BLOCK), n_valid)
    return dest_kt, blk_e, n_used, n_valid, n_blocks


def _rope_tables(s, gain):
    n_rows = s // GRID_W
    row = jnp.repeat(jnp.arange(n_rows), GRID_W).astype(F32)
    col = (jnp.arange(s) % GRID_W).astype(F32)
    half = HEAD_DIM // 2
    inv = ROPE_THETA ** (-jnp.arange(0, half, 2, dtype=F32) / half)
    ang_r = row[:, None] * inv
    ang_c = col[:, None] * inv
    cos = jnp.concatenate([jnp.cos(ang_r)] * 2 + [jnp.cos(ang_c)] * 2, axis=-1)
    sin = jnp.concatenate([-jnp.sin(ang_r), jnp.sin(ang_r), -jnp.sin(ang_c), jnp.sin(ang_c)], axis=-1)
    gain = gain.astype(F32)
    return cos * gain[None, :], sin * gain[_PARTNER][None, :]


_q = HEAD_DIM // 4
_PARTNER = np.concatenate([np.arange(_q, 2 * _q), np.arange(0, _q),
                           np.arange(3 * _q, 4 * _q), np.arange(2 * _q, 3 * _q)])


def _partner_cols(w, n_heads):
    idx = (np.arange(n_heads)[:, None] * HEAD_DIM + _PARTNER[None, :]).reshape(-1)
    return w[:, idx]


def kernel(x, c, w_ada, b_ada, norm1_g, w_in, q_norm_g, k_norm_g, rel_bias, w_br_a, w_br_b, w_out,
           norm2_g, w_router, b_router, w_gate_up, b_gate_up, w_down, b_down, final_norm_g):
    bsz, s, d = x.shape
    depth = w_ada.shape[0]
    t = bsz * s
    mod = _ada_mod(c, w_ada, b_ada)
    x2 = x.reshape(t, d)
    q_off = 3 * A_WIDTH
    k_off = q_off + B_Q
    for l in range(depth):
        sh1, sc1, g1, sh2, sc2, g2 = [mod[l, :, j * d:(j + 1) * d].reshape(bsz, 1, d)
                                      for j in range(N_MOD)]
        w = w_in[l]
        wq, wk = w[:, q_off:q_off + B_Q], w[:, k_off:k_off + B_KV]
        w_ext = jnp.concatenate([wq, _partner_cols(wq, B_HEADS), wk, _partner_cols(wk, B_KV_HEADS),
                                 w[:, k_off + B_KV:k_off + 2 * B_KV], w[:, :q_off],
                                 w[:, k_off + 2 * B_KV:]], axis=1).astype(BF16)
        tabs = [jnp.tile(tb, (1, 2)) for tb in
                _rope_tables(s, q_norm_g[l]) + _rope_tables(s, k_norm_g[l])]
        wa, wb, wo = w_br_a[l].astype(BF16), w_br_b[l].astype(BF16), w_out[l].astype(BF16)
        za0, za1, za2, q, k, v, gate, sq_norms = _projection(
            x2, sh1, sc1, norm1_g[l].reshape(1, d), w_ext, tabs, bsz, s)
        attn_a = [_dilated_group(za, sq_norms[g], rel_bias[:, g * A_HEADS:(g + 1) * A_HEADS], dil, bsz, s)
                  for g, (za, (_, dil)) in enumerate(zip((za0, za1, za2), A_GROUPS))]
        attn_b = _gqa_attention(q, k, v, q_norm_g[l], k_norm_g[l]).reshape(t, B_Q)
        x2, h2, top_e, gates, pos, counts = _merge_and_route(
            [o for o, _ in attn_a], [lse for _, lse in attn_a], attn_b, gate, x2, g1, wa, wb, wo,
            sh2, sc2, norm2_g[l].reshape(1, d), w_router[l], b_router[l], s)
        dest, blk_e, n_used, n_valid, n_blocks = _moe_plan(top_e, pos, counts)
        slots = _sc_scatter_rows(h2, dest.reshape(TOP_K * t), n_blocks * MOE_BLOCK)
        outs = _experts(slots, blk_e, n_used, n_valid, w_gate_up, b_gate_up, w_down, b_down, l)
        x2 = _combine(dest, outs, gates.T, x2, g2, final_norm_g.reshape(1, d), s,
                      final=(l == depth - 1))
    return x2.reshape(bsz, s, d)
```

```python
import functools
import math

import numpy as np
import jax
import jax.numpy as jnp
from jax import lax
from jax.experimental import pallas as pl
from jax.experimental.pallas import tpu as pltpu
from jax.experimental.pallas import tpu_sc as plsc

F32 = jnp.float32
BF16 = jnp.bfloat16

HEAD_DIM = 64
LANE = 128
PAIR = 2 * HEAD_DIM
A_GROUPS = ((128, 1), (512, 4), (2048, 16))
A_HEADS = 4
A_WIDTH = len(A_GROUPS) * A_HEADS * HEAD_DIM
A_OUT = A_HEADS * HEAD_DIM
A_GROUP_COLS = 3 * A_OUT
B_HEADS = 16
B_KV_HEADS = 4
B_GRP = B_HEADS // B_KV_HEADS
B_Q = B_HEADS * HEAD_DIM
B_KV = B_KV_HEADS * HEAD_DIM
GRID_W = 64
ROPE_THETA = 10000.0
REL_BUCKETS = 32
REL_MAX_DIST = 1024
N_EXPERTS = 32
TOP_K = 4
SWIGLU_LIMIT = 7.0
SWIGLU_ALPHA = 1.702
MOE_BLOCK = 512
COMBINE_CHUNKS = 4
MERGE_ROWS = 128
N_MOD = 6
EPS = 1e-6
NEG_INF = -1e30
LOG2E = math.log2(math.e)
N_SIDE = 64
DIL_SUB = 2 * N_SIDE
DIL_NO_SHIFT_MAX = 60.0
DIL_STEP_ROWS = 512

VMEM_LIMIT = 56 * 1024 * 1024


def _pack_bf16_pairs(x):
    bits = pltpu.bitcast(x.astype(BF16).astype(F32), jnp.uint32)
    half = x.shape[1] // 2
    return (bits[:, :half] >> 16) | bits[:, half:]


def _unpack_bf16_pairs(u):
    return jnp.concatenate([pltpu.bitcast(u << 16, F32),
                            pltpu.bitcast(u & jnp.uint32(0xFFFF0000), F32)], axis=1)


def _cparams(*sem):
    return pltpu.CompilerParams(dimension_semantics=sem, vmem_limit_bytes=VMEM_LIMIT)


def _ada_kernel(c_ref, w_ref, b_ref, o_ref):
    c = c_ref[...]
    ca = (c * jax.nn.sigmoid(c)).astype(BF16)
    o_ref[...] = jnp.dot(ca, w_ref[...].astype(BF16), preferred_element_type=F32) + b_ref[...]


def _ada_mod(c, w_ada, b_ada):
    depth, d, n = w_ada.shape
    bsz = c.shape[0]
    tn = 1536
    return pl.pallas_call(
        _ada_kernel,
        out_shape=jax.ShapeDtypeStruct((depth, bsz, n), F32),
        grid=(depth, n // tn),
        in_specs=[pl.BlockSpec((bsz, d), lambda l, j: (0, 0)),
                  pl.BlockSpec((None, d, tn), lambda l, j: (l, 0, j)),
                  pl.BlockSpec((None, 1, tn), lambda l, j: (l, 0, j))],
        out_specs=pl.BlockSpec((None, bsz, tn), lambda l, j: (l, 0, j)),
        compiler_params=_cparams("arbitrary", "arbitrary"),
        name="ada_mod",
    )(c, w_ada, b_ada.reshape(depth, 1, n))


def _proj_kernel(x_ref, sh_ref, sc_ref, g_ref, w_ref, aq_ref, bq_ref, ak_ref, bk_ref, hsel_ref,
                 za0_ref, za1_ref, za2_ref, q_ref, k_ref, v_ref, gate_ref, nrm_ref, fold_ref):
    @pl.when(pl.program_id(0) == 0)
    def _():
        nrm_ref[...] = jnp.zeros_like(nrm_ref)

    x = x_ref[...]
    ms = jnp.mean(x * x, axis=-1, keepdims=True)
    h = x * lax.rsqrt(ms + EPS) * g_ref[...]
    h = h * (1.0 + sc_ref[...]) + sh_ref[...]
    hb = h.astype(BF16)

    def mm(lo, hi):
        return jnp.dot(hb, w_ref[:, lo:hi], preferred_element_type=F32)

    tm = x.shape[0]
    o = 0
    zq = mm(o, o + B_Q); o += B_Q
    zqp = mm(o, o + B_Q); o += B_Q
    zk = mm(o, o + B_KV); o += B_KV
    zkp = mm(o, o + B_KV); o += B_KV
    zv = mm(o, o + B_KV); o += B_KV
    lane = lax.broadcasted_iota(jnp.int32, (tm, PAIR), 1)
    first = lane < HEAD_DIM

    def pair_norm_rope(z, zp, a, b):
        zz = z * z
        ss = jnp.where(first,
                       jnp.sum(jnp.where(first, zz, 0.0), axis=-1, keepdims=True),
                       jnp.sum(jnp.where(first, 0.0, zz), axis=-1, keepdims=True))
        return lax.rsqrt(ss * (1.0 / HEAD_DIM) + EPS) * (z * a + zp * b)

    aq, bq, ak, bk = aq_ref[...], bq_ref[...], ak_ref[...], bk_ref[...]
    for j in range(B_HEADS // 2):
        cols = slice(j * PAIR, (j + 1) * PAIR)
        q_ref[j] = (pair_norm_rope(zq[:, cols], zqp[:, cols], aq, bq)
                    * (LOG2E * HEAD_DIM ** -0.5)).astype(BF16)
    ones_col = jnp.where(lane == HEAD_DIM, 1.0, 0.0)
    for j in range(B_KV_HEADS // 2):
        cols = slice(j * PAIR, (j + 1) * PAIR)
        kk = pair_norm_rope(zk[:, cols], zkp[:, cols], ak, bk)
        kk_sw = pltpu.roll(kk, HEAD_DIM, axis=1)
        k_ref[2 * j] = jnp.where(first, kk, kk_sw).astype(BF16)
        k_ref[2 * j + 1] = jnp.where(first, kk_sw, kk).astype(BF16)
        vv = zv[:, cols]
        v_ref[2 * j] = jnp.where(first, vv, ones_col).astype(BF16)
        v_ref[2 * j + 1] = jnp.where(first, pltpu.roll(vv, HEAD_DIM, axis=1), ones_col).astype(BF16)
    for g, (za_ref, (_, dil)) in enumerate(zip((za0_ref, za1_ref, za2_ref), A_GROUPS)):
        z = mm(o, o + A_GROUP_COLS)
        o += A_GROUP_COLS
        zqk = z[:, :2 * A_OUT]
        sq = jnp.dot((zqk * zqk).astype(BF16), hsel_ref[...], preferred_element_type=F32)
        nrm_ref[g:g + 1, :] = jnp.maximum(nrm_ref[g:g + 1, :], jnp.max(sq, axis=0, keepdims=True))
        if dil == 1:
            za_ref[...] = z.astype(BF16)
        else:
            for c in range(A_GROUP_COLS // LANE):
                fold_ref[c] = z[:, c * LANE:(c + 1) * LANE]
            for r in range(dil):
                for c in range(A_GROUP_COLS // LANE):
                    col = r * A_GROUP_COLS + c * LANE
                    za_ref[:, col:col + LANE] = (
                        fold_ref[c, pl.ds(r, tm // dil, stride=dil), :].astype(BF16))
    gate_ref[...] = jax.nn.sigmoid(mm(o, o + gate_ref.shape[-1])).astype(BF16)


def _projection(x2, sh, sc, g, w_ext, tabs, bsz, s):
    t, d = x2.shape
    tm = 512
    tpb = s // tm
    n_ext = w_ext.shape[1]
    n_gate = 2 * d
    row = lambda i: (i, 0)
    per_b = lambda i: (i // tpb, 0, 0)
    tab = lambda i: (i % tpb, 0)
    hm = lambda i: (i // tpb, 0, i % tpb, 0)
    out_shape = (
        [jax.ShapeDtypeStruct((bsz, s // dil, dil * A_GROUP_COLS), BF16) for _, dil in A_GROUPS]
        + [jax.ShapeDtypeStruct((bsz, B_HEADS // 2, s, PAIR), BF16),
           jax.ShapeDtypeStruct((bsz, B_KV_HEADS, s, PAIR), BF16),
           jax.ShapeDtypeStruct((bsz, B_KV_HEADS, s, PAIR), BF16),
           jax.ShapeDtypeStruct((t, n_gate), BF16),
           jax.ShapeDtypeStruct((len(A_GROUPS), LANE), F32)])
    head_sel = (jnp.arange(2 * A_OUT)[:, None] // HEAD_DIM == jnp.arange(LANE)[None, :]).astype(BF16)
    out_specs = (
        [pl.BlockSpec((None, tm // dil, dil * A_GROUP_COLS), lambda i: (i // tpb, i % tpb, 0))
         for _, dil in A_GROUPS]
        + [pl.BlockSpec((None, B_HEADS // 2, tm, PAIR), hm),
           pl.BlockSpec((None, B_KV_HEADS, tm, PAIR), hm),
           pl.BlockSpec((None, B_KV_HEADS, tm, PAIR), hm),
           pl.BlockSpec((tm, n_gate), row),
           pl.BlockSpec((len(A_GROUPS), LANE), lambda i: (0, 0))])
    return pl.pallas_call(
        _proj_kernel,
        out_shape=out_shape,
        grid=(t // tm,),
        in_specs=[pl.BlockSpec((tm, d), row),
                  pl.BlockSpec((None, 1, d), per_b),
                  pl.BlockSpec((None, 1, d), per_b),
                  pl.BlockSpec((1, d), lambda i: (0, 0)),
                  pl.BlockSpec((d, n_ext), lambda i: (0, 0), pipeline_mode=pl.Buffered(1))]
                 + [pl.BlockSpec((tm, PAIR), tab)] * 4
                 + [pl.BlockSpec((2 * A_OUT, LANE), lambda i: (0, 0))],
        out_specs=out_specs,
        scratch_shapes=[pltpu.VMEM((A_GROUP_COLS // LANE, tm, LANE), F32)],
        compiler_params=_cparams("arbitrary"),
        name="in_proj",
    )(x2, sh, sc, g, w_ext, *tabs, head_sel)


def _dilated_kernel(prev_ref, cur_ref, next_ref, bias_ref, o_ref, lse_ref, *, tq, seq_len, shift):
    for res in range(cur_ref.shape[1] // A_GROUP_COLS):
        c0 = res * A_GROUP_COLS
        part = lambda ref, j: ref[:, c0 + j * A_OUT:c0 + (j + 1) * A_OUT]
        _dilated_subsequence(
            part(cur_ref, 0),
            jnp.concatenate([part(prev_ref, 1), part(cur_ref, 1), part(next_ref, 1)], axis=0),
            jnp.concatenate([part(prev_ref, 2), part(cur_ref, 2), part(next_ref, 2)], axis=0),
            bias_ref, o_ref, lse_ref, res * A_OUT, tq=tq, seq_len=seq_len, shift=shift)


def _dilated_subsequence(q_all, k, v, bias_ref, o_ref, lse_ref, out_col, *, tq, seq_len, shift):
    i = pl.program_id(2)
    nk = DIL_SUB + 2 * N_SIDE
    col = lax.broadcasted_iota(jnp.int32, (DIL_SUB, nk), 1)
    if not shift:
        lane = lax.broadcasted_iota(jnp.int32, (v.shape[0], PAIR), 1)
        first = lane < HEAD_DIM
        ones_col = jnp.where(lane == HEAD_DIM, 1.0, 0.0)
        v_ext = []
        for j in range(A_HEADS // 2):
            vp = v[:, j * PAIR:(j + 1) * PAIR].astype(F32)
            v_ext += [jnp.where(first, vp, ones_col).astype(BF16),
                      jnp.where(first, pltpu.roll(vp, HEAD_DIM, axis=1), ones_col).astype(BF16)]
    for sb in range(tq // DIL_SUB):
        rows = slice(sb * DIL_SUB, (sb + 1) * DIL_SUB)
        q = q_all[rows, :]
        kb = k[sb * DIL_SUB:sb * DIL_SUB + nk]
        vb = v[sb * DIL_SUB:sb * DIL_SUB + nk]
        kpos = i * tq + sb * DIL_SUB - N_SIDE + col
        valid = (kpos >= 0) & (kpos < seq_len)
        for hh in range(A_HEADS):
            sl = slice(hh * HEAD_DIM, (hh + 1) * HEAD_DIM)
            sc = lax.dot_general(q[:, sl], kb[:, sl], (((1,), (1,)), ((), ())),
                                 preferred_element_type=F32)
            sc = sc * (HEAD_DIM ** -0.5) + bias_ref[hh]
            sc = jnp.where(valid, sc, NEG_INF)
            if shift:
                m = jnp.max(sc, axis=-1, keepdims=True)
                p = jnp.exp(sc - m)
                den = jnp.sum(p, axis=-1, keepdims=True)
                o = jnp.dot(p.astype(BF16), vb[:, sl], preferred_element_type=F32) / den
                lse = m + jnp.log(den)
            else:
                acc = jnp.dot(jnp.exp(sc).astype(BF16), v_ext[hh][sb * DIL_SUB:sb * DIL_SUB + nk],
                              preferred_element_type=F32)
                den = acc[:, HEAD_DIM:HEAD_DIM + 1]
                o = acc[:, :HEAD_DIM] / den
                lse = jnp.log(den)
            out_sl = slice(out_col + hh * HEAD_DIM, out_col + (hh + 1) * HEAD_DIM)
            o_ref[rows, out_sl] = o
            lse_ref[rows, out_sl] = jnp.broadcast_to(lse, (DIL_SUB, HEAD_DIM))


def _t5_bucket(rel):
    nb = REL_BUCKETS // 2
    max_exact = nb // 2
    ret = jnp.where(rel > 0, nb, 0)
    n = jnp.abs(rel)
    nf = jnp.maximum(n, 1).astype(F32)
    large = max_exact + (jnp.log(nf / max_exact) / math.log(REL_MAX_DIST / max_exact)
                         * (nb - max_exact)).astype(jnp.int32)
    large = jnp.minimum(large, nb - 1)
    return ret + jnp.where(n < max_exact, n, large)


def _band_bias(rel_bias_g, dilation, tq):
    nk = tq + 2 * N_SIDE
    rel = jnp.arange(nk)[None, :] - N_SIDE - jnp.arange(tq)[:, None]
    bucket = _t5_bucket(rel * dilation)[None]
    bias = jnp.full((A_HEADS,) + rel.shape, NEG_INF, F32)
    for b in range(REL_BUCKETS):
        bias = jnp.where(bucket == b, rel_bias_g[b].astype(F32)[:, None, None], bias)
    return jnp.where((jnp.abs(rel) <= N_SIDE)[None], bias, NEG_INF)


def _dilated_group(za, sq_norms, rel_bias_g, dilation, bsz, s):
    ll = s // dilation
    tq = min(DIL_STEP_ROWS, ll)
    assert ll % tq == 0 and tq % DIL_SUB == 0
    n_res = max(1, min(dilation, DIL_STEP_ROWS // tq))
    assert dilation % n_res == 0
    nhalf = ll // N_SIDE
    per = tq // N_SIDE
    slab = n_res * A_GROUP_COLS
    cur = pl.BlockSpec((None, tq, slab), lambda b, r, i: (b, i, r))
    prv = pl.BlockSpec((None, N_SIDE, slab), lambda b, r, i: (b, jnp.maximum(per * i - 1, 0), r))
    nxt = pl.BlockSpec((None, N_SIDE, slab), lambda b, r, i: (b, jnp.minimum(per * (i + 1), nhalf - 1), r))
    bias = _band_bias(rel_bias_g, dilation, DIL_SUB)
    out_sd = jax.ShapeDtypeStruct((bsz, ll, dilation * A_OUT), F32)
    out_spec = pl.BlockSpec((None, tq, n_res * A_OUT), lambda b, r, i: (b, i, r))

    def attend(shift, zv, bias):
        return pl.pallas_call(
            functools.partial(_dilated_kernel, tq=tq, seq_len=ll, shift=shift),
            out_shape=[out_sd, out_sd],
            grid=(bsz, dilation // n_res, ll // tq),
            in_specs=[prv, cur, nxt, pl.BlockSpec(bias.shape, lambda b, r, i: (0, 0, 0))],
            out_specs=[out_spec, out_spec],
            compiler_params=_cparams("arbitrary", "arbitrary", "arbitrary"),
            name=f"dilated_attn_d{dilation}" + ("_rowmax" if shift else ""),
        )(zv, zv, zv, bias)

    q2 = jnp.max(sq_norms[:A_HEADS])
    k2 = jnp.max(sq_norms[A_HEADS:2 * A_HEADS])
    bound = 1.03 * jnp.sqrt(q2 * k2) * (HEAD_DIM ** -0.5) + jnp.max(jnp.abs(rel_bias_g))
    return lax.cond(bound <= DIL_NO_SHIFT_MAX, functools.partial(attend, False),
                    functools.partial(attend, True), za, bias)


GQA_NO_SHIFT_MAX_LOG2 = 80.0
GQA_KEY_CHUNK = 128


def _stacked_heads(q_ref):
    first = lax.broadcasted_iota(jnp.int32, q_ref.shape[1:], 1) < HEAD_DIM
    blocks = []
    for j in range(q_ref.shape[0]):
        qp = q_ref[j]
        blocks += [jnp.where(first, qp, jnp.zeros_like(qp)), jnp.where(first, jnp.zeros_like(qp), qp)]
    return jnp.concatenate(blocks, axis=0)


def _gqa_kernel_noshift(q_ref, k_ref, v_ref, o_ref):
    tq = q_ref.shape[1]
    q = _stacked_heads(q_ref)
    nt = (((1,), (1,)), ((), ()))
    acc = jnp.zeros((q.shape[0], PAIR), F32)
    for c in range(k_ref.shape[0] // GQA_KEY_CHUNK):
        rows = slice(c * GQA_KEY_CHUNK, (c + 1) * GQA_KEY_CHUNK)
        p = jnp.exp2(lax.dot_general(q, k_ref[rows, :], nt, preferred_element_type=F32))
        acc = acc + jnp.dot(p.astype(BF16), v_ref[rows, :], preferred_element_type=F32)
    o = acc[:, :HEAD_DIM] / acc[:, HEAD_DIM:HEAD_DIM + 1]
    for hh in range(q.shape[0] // tq):
        o_ref[:, hh * HEAD_DIM:(hh + 1) * HEAD_DIM] = o[hh * tq:(hh + 1) * tq].astype(BF16)


def _gqa_kernel_rowmax(q_ref, k_ref, v_ref, o_ref):
    tq = q_ref.shape[1]
    q = _stacked_heads(q_ref)
    k = k_ref[...]
    v = v_ref[:, :HEAD_DIM]
    for hh in range(q.shape[0] // tq):
        sc = lax.dot_general(q[hh * tq:(hh + 1) * tq], k, (((1,), (1,)), ((), ())),
                             preferred_element_type=F32)
        m = jnp.max(sc, axis=-1, keepdims=True)
        p = jnp.exp2(sc - m)
        den = jnp.sum(p, axis=-1, keepdims=True)
        o = jnp.dot(p.astype(BF16), v, preferred_element_type=F32) / den
        o_ref[:, hh * HEAD_DIM:(hh + 1) * HEAD_DIM] = o.astype(BF16)


def _gqa_call(body, name, tq, q, k, v):
    bsz, _, s, _ = q.shape
    return pl.pallas_call(
        body,
        out_shape=jax.ShapeDtypeStruct((bsz, s, B_Q), BF16),
        grid=(bsz, B_KV_HEADS, s // tq),
        in_specs=[pl.BlockSpec((None, B_GRP // 2, tq, PAIR), lambda b, h, i: (b, h, i, 0)),
                  pl.BlockSpec((None, None, s, PAIR), lambda b, h, i: (b, h, 0, 0)),
                  pl.BlockSpec((None, None, s, PAIR), lambda b, h, i: (b, h, 0, 0))],
        out_specs=pl.BlockSpec((None, tq, B_GRP * HEAD_DIM), lambda b, h, i: (b, i, h)),
        compiler_params=_cparams("arbitrary", "arbitrary", "arbitrary"),
        name=name,
    )(q, k, v)


def _gqa_attention(q, k, v, q_gain, k_gain):
    bound = (HEAD_DIM ** 0.5) * LOG2E * 1.01 * jnp.max(jnp.abs(q_gain)) * jnp.max(jnp.abs(k_gain))
    return lax.cond(bound <= GQA_NO_SHIFT_MAX_LOG2,
                    functools.partial(_gqa_call, _gqa_kernel_noshift, "gqa_attn", 1024),
                    functools.partial(_gqa_call, _gqa_kernel_rowmax, "gqa_attn_rowmax", 256),
                    q, k, v)


def _merge_kernel(o0_ref, o1_ref, o2_ref, l0_ref, l1_ref, l2_ref, ob_ref, gate_ref, x_ref, g1_ref,
                  wa_ref, wb_ref, wo_ref, sh_ref, sc_ref, g_ref, whi_ref, wlo_ref, b_ref,
                  xo_ref, h_ref, e_ref, w_ref, pos_ref, cnt_ref, run_ref, *unfold_refs):
    @pl.when(pl.program_id(0) == 0)
    def _():
        run_ref[...] = jnp.zeros_like(run_ref)

    tm, d = x_ref.shape
    scratch = list(unfold_refs)

    def token_order(ref, dil):
        if dil == 1:
            return ref[...]
        buf = scratch.pop()
        for r in range(dil):
            for c in range(A_OUT // LANE):
                col = r * A_OUT + c * LANE
                buf[c, pl.ds(r, tm // dil, stride=dil), :] = ref[:, col:col + LANE]
        return jnp.concatenate([buf[c] for c in range(A_OUT // LANE)], axis=1)

    dils = [dil for _, dil in A_GROUPS]
    o0, o1, o2 = [token_order(r, dl) for r, dl in zip((o0_ref, o1_ref, o2_ref), dils)]
    l0, l1, l2 = [token_order(r, dl) for r, dl in zip((l0_ref, l1_ref, l2_ref), dils)]
    m = jnp.maximum(jnp.maximum(l0, l1), l2)
    e0, e1, e2 = jnp.exp(l0 - m), jnp.exp(l1 - m), jnp.exp(l2 - m)
    tot = e0 + e1 + e2
    oa = ((e0 / tot) * o0 + (e1 / tot) * o1 + (e2 / tot) * o2).astype(BF16)
    xn_rows = []
    for lo in range(0, tm, MERGE_ROWS):
        rows = slice(lo, lo + MERGE_ROWS)
        ya = jnp.dot(oa[rows], wa_ref[...], preferred_element_type=F32)
        yb = jnp.dot(ob_ref[rows, :], wb_ref[...], preferred_element_type=F32)
        merged = gate_ref[rows, :d].astype(F32) * ya + gate_ref[rows, d:].astype(F32) * yb
        y = jnp.dot(merged.astype(BF16), wo_ref[...], preferred_element_type=F32)
        xn_rows.append(x_ref[rows, :] + g1_ref[...] * y)
    xn = jnp.concatenate(xn_rows, axis=0)
    xo_ref[...] = xn
    _route_tile(xn, sh_ref, sc_ref, g_ref, whi_ref, wlo_ref, b_ref, h_ref, e_ref, w_ref, pos_ref, cnt_ref,
                run_ref)


def _merge_and_route(os_, ls_, ob, gate, x2, g1, wa, wb, wo, sh2, sc2, g2n, w_router, b_router, s):
    t, d = x2.shape
    tm = 512
    tpb = s // tm
    ne = w_router.shape[1]
    wt = w_router.T
    whi = wt.astype(BF16)
    wlo = (wt - whi.astype(F32)).astype(BF16)
    row = lambda i: (i, 0)
    const = lambda i: (0, 0)
    per_b = lambda i: (i // tpb, 0, 0)
    by_token = pl.BlockSpec((TOP_K, tm), lambda i: (0, i))
    a_specs = [pl.BlockSpec((None, tm // dil, dil * A_OUT), lambda i: (i // tpb, i % tpb, 0))
               for _, dil in A_GROUPS]
    n_unfold = 2 * sum(1 for _, dil in A_GROUPS if dil > 1)
    return pl.pallas_call(
        _merge_kernel,
        out_shape=[jax.ShapeDtypeStruct((t, d), F32),
                   jax.ShapeDtypeStruct((t, d // 2), jnp.uint32),
                   jax.ShapeDtypeStruct((TOP_K, t), jnp.int32),
                   jax.ShapeDtypeStruct((TOP_K, t), F32),
                   jax.ShapeDtypeStruct((TOP_K, t), jnp.int32),
                   jax.ShapeDtypeStruct((ne, 1), F32)],
        grid=(t // tm,),
        scratch_shapes=[pltpu.VMEM((ne, 1), F32)]
                       + [pltpu.VMEM((A_OUT // LANE, tm, LANE), F32)] * n_unfold,
        in_specs=a_specs * 2
                 + [pl.BlockSpec((tm, B_Q), row),
                    pl.BlockSpec((tm, 2 * d), row),
                    pl.BlockSpec((tm, d), row),
                    pl.BlockSpec((None, 1, d), per_b),
                    pl.BlockSpec(wa.shape, const),
                    pl.BlockSpec(wb.shape, const),
                    pl.BlockSpec(wo.shape, const),
                    pl.BlockSpec((None, 1, d), per_b),
                    pl.BlockSpec((None, 1, d), per_b),
                    pl.BlockSpec((1, d), const),
                    pl.BlockSpec((ne, d), const),
                    pl.BlockSpec((ne, d), const),
                    pl.BlockSpec((ne, 1), const)],
        out_specs=[pl.BlockSpec((tm, d), row),
                   pl.BlockSpec((tm, d // 2), row),
                   by_token, by_token, by_token,
                   pl.BlockSpec((ne, 1), const)],
        compiler_params=_cparams("arbitrary"),
        name="merge_route",
    )(*os_, *ls_, ob, gate, x2, g1, wa, wb, wo, sh2, sc2, g2n, whi, wlo, b_router.reshape(ne, 1))


def _route_tile(x, sh_ref, sc_ref, g_ref, whi_ref, wlo_ref, b_ref,
                h_ref, e_ref, w_ref, pos_ref, cnt_ref, run_ref):
    ms = jnp.mean(x * x, axis=-1, keepdims=True)
    h = x * lax.rsqrt(ms + EPS) * g_ref[...]
    h = h * (1.0 + sc_ref[...]) + sh_ref[...]
    h_ref[...] = _pack_bf16_pairs(h)
    hhi = h.astype(BF16)
    hlo = (h - hhi.astype(F32)).astype(BF16)
    nt = (((1,), (1,)), ((), ()))
    dotf = lambda a, b: lax.dot_general(a, b, nt, preferred_element_type=F32)
    logits = dotf(whi_ref[...], hhi) + (dotf(whi_ref[...], hlo) + dotf(wlo_ref[...], hhi)) + b_ref[...]
    ne, tm = logits.shape
    iota = lax.broadcasted_iota(jnp.int32, (ne, tm), 0).astype(F32)
    vals, idxs = [], []
    cur = logits
    for _ in range(TOP_K):
        m = jnp.max(cur, axis=0, keepdims=True)
        idx = jnp.min(jnp.where(cur == m, iota, float(ne)), axis=0, keepdims=True)
        vals.append(m)
        idxs.append(idx)
        cur = jnp.where(iota == idx, -jnp.inf, cur)
    tv = jnp.concatenate(vals, axis=0)
    ex = jnp.exp(tv - tv[0:1])
    w_ref[...] = ex / jnp.sum(ex, axis=0, keepdims=True)
    e_ref[...] = jnp.concatenate(idxs, axis=0).astype(jnp.int32)
    onehot = jnp.zeros((ne, tm), F32)
    for idx in idxs:
        onehot = onehot + jnp.where(iota == idx, 1.0, 0.0)
    earlier = (lax.broadcasted_iota(jnp.int32, (tm, tm), 0)
               < lax.broadcasted_iota(jnp.int32, (tm, tm), 1))
    rank = jnp.dot(onehot.astype(BF16), jnp.where(earlier, 1.0, 0.0).astype(BF16),
                   preferred_element_type=F32) + run_ref[...]
    pos_ref[...] = jnp.concatenate(
        [jnp.sum(jnp.where(iota == idx, rank, 0.0), axis=0, keepdims=True) for idx in idxs],
        axis=0).astype(jnp.int32)
    run_ref[...] = run_ref[...] + jnp.sum(onehot, axis=1, keepdims=True)
    cnt_ref[...] = run_ref[...]


SC_CORES = 2
SC_SUBCORES = 16
SC_ROWS = 64


def _sc_scatter_rows(rows, dest_kt, n_slots):
    t, d = rows.shape
    n_workers = SC_CORES * SC_SUBCORES
    per_w = t // n_workers
    n_chunks = per_w // SC_ROWS
    assert per_w * n_workers == t and n_chunks * SC_ROWS == per_w and n_chunks % 2 == 0
    idx = dest_kt.reshape(TOP_K, n_workers, n_chunks, SC_ROWS).transpose(1, 2, 0, 3)
    idx = idx.reshape(n_workers * n_chunks * TOP_K, SC_ROWS)
    lists_per_w = n_chunks * TOP_K
    mesh = plsc.VectorSubcoreMesh(core_axis_name="c", subcore_axis_name="s")

    @functools.partial(
        pl.kernel, mesh=mesh,
        out_type=jax.ShapeDtypeStruct((n_slots, d), rows.dtype),
        scratch_types=[pltpu.VMEM((lists_per_w, SC_ROWS), jnp.int32),
                       pltpu.VMEM((SC_ROWS, d), rows.dtype),
                       pltpu.VMEM((SC_ROWS, d), rows.dtype)] + [pltpu.SemaphoreType.DMA] * 4,
        name="sc_scatter_rows",
    )
    def scatter_kernel(rows_hbm, idx_hbm, out_hbm, idx_v, buf_a, buf_b, ld_a, ld_b, st_a, st_b):
        wid = lax.axis_index("s") * SC_CORES + lax.axis_index("c")
        pltpu.sync_copy(idx_hbm.at[pl.ds(wid * lists_per_w, lists_per_w)], idx_v)

        def load(j, buf, sem):
            return pltpu.make_async_copy(rows_hbm.at[pl.ds(wid * per_w + j * SC_ROWS, SC_ROWS)], buf, sem)

        def scatter_all(j, buf, sem):
            copies = [pltpu.make_async_copy(buf, out_hbm.at[idx_v.at[j * TOP_K + kk]], sem)
                      for kk in range(TOP_K)]
            for cp in copies:
                cp.start()
            for cp in copies:
                cp.wait()

        load(0, buf_a, ld_a).start()

        @pl.loop(0, n_chunks, step=2)
        def _(j):
            load(j + 1, buf_b, ld_b).start()
            load(j, buf_a, ld_a).wait()
            scatter_all(j, buf_a, st_a)

            @pl.when(j + 2 < n_chunks)
            def _():
                load(j + 2, buf_a, ld_a).start()
            load(j + 1, buf_b, ld_b).wait()
            scatter_all(j + 1, buf_b, st_b)

    return scatter_kernel(rows, idx)


def _expert_kernel(blk_e_ref, n_used_ref, n_valid_ref, x_ref, wgu_ref, bgu_ref, wdn_ref, bdn_ref, o_ref,
                   wgu_bf, wdn_bf):
    j = pl.program_id(0)
    d = wdn_bf.shape[1]

    @pl.when(j < n_used_ref[0])
    def _():
        @pl.when((j == 0) | (blk_e_ref[j] != blk_e_ref[jnp.maximum(j - 1, 0)]))
        def _():
            wgu_bf[...] = wgu_ref[...].astype(BF16)
            wdn_bf[...] = wdn_ref[...].astype(BF16)

        row = lax.broadcasted_iota(jnp.int32, x_ref.shape, 0)
        packed = jnp.where(row < n_valid_ref[j], x_ref[...], jnp.uint32(0))
        xb = _unpack_bf16_pairs(packed).astype(BF16)
        gu = jnp.dot(xb, wgu_bf[...], preferred_element_type=F32) + bgu_ref[...]
        x_glu = jnp.minimum(gu[:, :d], SWIGLU_LIMIT)
        x_lin = jnp.clip(gu[:, d:], -SWIGLU_LIMIT, SWIGLU_LIMIT)
        act = x_glu * jax.nn.sigmoid(SWIGLU_ALPHA * x_glu) * (x_lin + 1.0)
        o_ref[...] = _pack_bf16_pairs(
            jnp.dot(act.astype(BF16), wdn_bf[...], preferred_element_type=F32) + bdn_ref[...])

    @pl.when(j >= n_used_ref[0])
    def _():
        o_ref[...] = jnp.zeros_like(o_ref)


def _experts(xs, blk_e, n_used, n_valid, wgu, bgu, wdn, bdn, layer):
    d = wdn.shape[-1]
    n_blocks = blk_e.shape[0]
    depth, ne = wgu.shape[:2]
    by_expert = lambda j, be, nu, nv: (layer, be[j], 0, 0)
    grid_spec = pltpu.PrefetchScalarGridSpec(
        num_scalar_prefetch=3,
        grid=(n_blocks,),
        in_specs=[pl.BlockSpec((MOE_BLOCK, d // 2), lambda j, be, nu, nv: (jnp.minimum(j, nu[0] - 1), 0)),
                  pl.BlockSpec((None, None, d, 2 * d), by_expert),
                  pl.BlockSpec((None, None, 1, 2 * d), by_expert),
                  pl.BlockSpec((None, None, d, d), by_expert),
                  pl.BlockSpec((None, None, 1, d), by_expert)],
        out_specs=pl.BlockSpec((MOE_BLOCK, d // 2), lambda j, be, nu, nv: (j, 0)),
        scratch_shapes=[pltpu.VMEM((d, 2 * d), BF16), pltpu.VMEM((d, d), BF16)],
    )
    return pl.pallas_call(
        _expert_kernel,
        out_shape=jax.ShapeDtypeStruct((n_blocks * MOE_BLOCK, d // 2), jnp.uint32),
        grid_spec=grid_spec,
        compiler_params=_cparams("arbitrary"),
        name="moe_experts",
    )(blk_e, n_used, n_valid, xs, wgu, bgu.reshape(depth, ne, 1, 2 * d), wdn, bdn.reshape(depth, ne, 1, d))


def _sc_gather_rows(table, idx):
    n_out = idx.shape[0]
    d = table.shape[1]
    n_workers = SC_CORES * SC_SUBCORES
    per_w = n_out // n_workers
    n_chunks = per_w // SC_ROWS
    assert per_w * n_workers == n_out and n_chunks * SC_ROWS == per_w and n_chunks % 2 == 0
    mesh = plsc.VectorSubcoreMesh(core_axis_name="c", subcore_axis_name="s")

    @functools.partial(
        pl.kernel, mesh=mesh,
        out_type=jax.ShapeDtypeStruct((n_out, d), table.dtype),
        scratch_types=[pltpu.VMEM((per_w,), jnp.int32),
                       pltpu.VMEM((SC_ROWS, d), table.dtype),
                       pltpu.VMEM((SC_ROWS, d), table.dtype),
                       pltpu.SemaphoreType.DMA, pltpu.SemaphoreType.DMA],
        name="sc_gather_rows",
    )
    def gather_kernel(table_hbm, idx_hbm, out_hbm, idx_v, rows_a, rows_b, sem_a, sem_b):
        base = (lax.axis_index("s") * SC_CORES + lax.axis_index("c")) * per_w
        pltpu.sync_copy(idx_hbm.at[pl.ds(base, per_w)], idx_v)

        def gather(j, buf, sem):
            rows = idx_v.at[pl.ds(j * SC_ROWS, SC_ROWS)]
            return pltpu.make_async_copy(table_hbm.at[rows], buf, sem)

        def write_back(j, buf):
            pltpu.sync_copy(buf, out_hbm.at[pl.ds(base + j * SC_ROWS, SC_ROWS)])

        gather(0, rows_a, sem_a).start()

        @pl.loop(0, n_chunks, step=2)
        def _(j):
            gather(j + 1, rows_b, sem_b).start()
            gather(j, rows_a, sem_a).wait()
            write_back(j, rows_a)

            @pl.when(j + 2 < n_chunks)
            def _():
                gather(j + 2, rows_a, sem_a).start()
            gather(j + 1, rows_b, sem_b).wait()
            write_back(j + 1, rows_b)

    return gather_kernel(table, idx)


def _combine_kernel(y0_ref, y1_ref, y2_ref, y3_ref, w_ref, x_ref, g2_ref, fg_ref, xo_ref, *, final):
    w = w_ref[...]
    y = w[:, 0:1] * _unpack_bf16_pairs(y0_ref[...])
    for kk, y_ref in enumerate((y1_ref, y2_ref, y3_ref), start=1):
        y = y + w[:, kk:kk + 1] * _unpack_bf16_pairs(y_ref[...])
    xn = x_ref[...] + g2_ref[...] * y
    if final:
        ms = jnp.mean(xn * xn, axis=-1, keepdims=True)
        xn = xn * lax.rsqrt(ms + EPS) * fg_ref[...]
    xo_ref[...] = xn


def _combine(dest, out_sorted, gates_t, x2, g2, final_g, s, final):
    t, d = x2.shape
    tm = 256
    tpb = s // tm
    sc_unit = 2 * SC_ROWS * SC_CORES * SC_SUBCORES // TOP_K
    chunkable = t % (COMBINE_CHUNKS * sc_unit) == 0 and t % (COMBINE_CHUNKS * tm) == 0
    n_chunks = COMBINE_CHUNKS if chunkable else 1
    tc = t // n_chunks
    nt = tc // tm
    for c in range(n_chunks):
        yg = _sc_gather_rows(out_sorted, dest[:, c * tc:(c + 1) * tc].reshape(TOP_K * tc))
        plane = lambda kk: pl.BlockSpec((tm, d // 2), lambda i: (kk * nt + i, 0))
        tile = lambda i, c=c: (c * nt + i, 0)
        x2 = pl.pallas_call(
            functools.partial(_combine_kernel, final=final),
            out_shape=jax.ShapeDtypeStruct((t, d), F32),
            grid=(nt,),
            in_specs=[plane(0), plane(1), plane(2), plane(3),
                      pl.BlockSpec((tm, TOP_K), tile),
                      pl.BlockSpec((tm, d), tile),
                      pl.BlockSpec((None, 1, d), lambda i, c=c: ((c * nt + i) // tpb, 0, 0)),
                      pl.BlockSpec((1, d), lambda i: (0, 0))],
            out_specs=pl.BlockSpec((tm, d), tile),
            input_output_aliases={5: 0},
            compiler_params=_cparams("arbitrary"),
            name="moe_combine",
        )(yg, yg, yg, yg, gates_t, x2, g2, final_g)
    return x2


def _moe_plan(top_e, pos, counts):
    t = top_e.shape[1]
    sizes = counts[:, 0].astype(jnp.int32)
    padded = (sizes + MOE_BLOCK - 1) // MOE_BLOCK * MOE_BLOCK
    pad_end = jnp.cumsum(padded)
    pad_start = pad_end - padded
    start_of = jnp.zeros_like(top_e)
    for e in range(N_EXPERTS):
        start_of = jnp.where(top_e == e, pad_start[e], start_of)
    dest_kt = start_of + pos
    n_blocks = -(-t * TOP_K // MOE_BLOCK) + N_EXPERTS
    blk_start = jnp.arange(n_blocks, dtype=jnp.int32) * MOE_BLOCK
    blk_e = jnp.minimum(jnp.sum((pad_end[None, :] <= blk_start[:, None]).astype(jnp.int32), axis=1),
                        N_EXPERTS - 1).astype(jnp.int32)
    n_used = (pad_end[-1] // MOE_BLOCK).astype(jnp.int32).reshape(1)
    filled_end = pad_start + sizes
    n_valid = jnp.zeros((n_blocks,), jnp.int32)
    for e in range(N_EXPERTS):
        n_valid = jnp.where(blk_e == e, jnp.clip(filled_end[e] - blk_start, 0, MOE_BLOCK), n_valid)
    return dest_kt, blk_e, n_used, n_valid, n_blocks


def _rope_tables(s, gain):
    n_rows = s // GRID_W
    row = jnp.repeat(jnp.arange(n_rows), GRID_W).astype(F32)
    col = (jnp.arange(s) % GRID_W).astype(F32)
    half = HEAD_DIM // 2
    inv = ROPE_THETA ** (-jnp.arange(0, half, 2, dtype=F32) / half)
    ang_r = row[:, None] * inv
    ang_c = col[:, None] * inv
    cos = jnp.concatenate([jnp.cos(ang_r)] * 2 + [jnp.cos(ang_c)] * 2, axis=-1)
    sin = jnp.concatenate([-jnp.sin(ang_r), jnp.sin(ang_r), -jnp.sin(ang_c), jnp.sin(ang_c)], axis=-1)
    gain = gain.astype(F32)
    return cos * gain[None, :], sin * gain[_PARTNER][None, :]


_q = HEAD_DIM // 4
_PARTNER = np.concatenate([np.arange(_q, 2 * _q), np.arange(0, _q),
                           np.arange(3 * _q, 4 * _q), np.arange(2 * _q, 3 * _q)])


def _partner_cols(w, n_heads):
    idx = (np.arange(n_heads)[:, None] * HEAD_DIM + _PARTNER[None, :]).reshape(-1)
    return w[:, idx]


def kernel(x, c, w_ada, b_ada, norm1_g, w_in, q_norm_g, k_norm_g, rel_bias, w_br_a, w_br_b, w_out,
           norm2_g, w_router, b_router, w_gate_up, b_gate_up, w_down, b_down, final_norm_g):
    bsz, s, d = x.shape
    depth = w_ada.shape[0]
    t = bsz * s
    mod = _ada_mod(c, w_ada, b_ada)
    x2 = x.reshape(t, d)
    q_off = 3 * A_WIDTH
    k_off = q_off + B_Q
    for l in range(depth):
        sh1, sc1, g1, sh2, sc2, g2 = [mod[l, :, j * d:(j + 1) * d].reshape(bsz, 1, d)
                                      for j in range(N_MOD)]
        w = w_in[l]
        wq, wk = w[:, q_off:q_off + B_Q], w[:, k_off:k_off + B_KV]
        w_ext = jnp.concatenate([wq, _partner_cols(wq, B_HEADS), wk, _partner_cols(wk, B_KV_HEADS),
                                 w[:, k_off + B_KV:k_off + 2 * B_KV], w[:, :q_off],
                                 w[:, k_off + 2 * B_KV:]], axis=1).astype(BF16)
        tabs = [jnp.tile(tb, (1, 2)) for tb in
                _rope_tables(s, q_norm_g[l]) + _rope_tables(s, k_norm_g[l])]
        wa, wb, wo = w_br_a[l].astype(BF16), w_br_b[l].astype(BF16), w_out[l].astype(BF16)
        za0, za1, za2, q, k, v, gate, sq_norms = _projection(
            x2, sh1, sc1, norm1_g[l].reshape(1, d), w_ext, tabs, bsz, s)
        attn_a = [_dilated_group(za, sq_norms[g], rel_bias[:, g * A_HEADS:(g + 1) * A_HEADS], dil, bsz, s)
                  for g, (za, (_, dil)) in enumerate(zip((za0, za1, za2), A_GROUPS))]
        attn_b = _gqa_attention(q, k, v, q_norm_g[l], k_norm_g[l]).reshape(t, B_Q)
        x2, h2, top_e, gates, pos, counts = _merge_and_route(
            [o for o, _ in attn_a], [lse for _, lse in attn_a], attn_b, gate, x2, g1, wa, wb, wo,
            sh2, sc2, norm2_g[l].reshape(1, d), w_router[l], b_router[l], s)
        dest, blk_e, n_used, n_valid, n_blocks = _moe_plan(top_e, pos, counts)
        slots = _sc_scatter_rows(h2, dest.reshape(TOP_K * t), n_blocks * MOE_BLOCK)
        outs = _experts(slots, blk_e, n_used, n_valid, w_gate_up, b_gate_up, w_down, b_down, l)
        x2 = _combine(dest, outs, gates.T, x2, g2, final_norm_g.reshape(1, d), s,
                      final=(l == depth - 1))
    return x2.reshape(bsz, s, d)
```
